```python
import math
import jax, jax.numpy as jnp
from jax import lax
import numpy as np


D_MODEL = 1024
BATCH = 8
SEQ = 2048
DEPTH = 1
DEC_BATCH = 128
DEC_SEQ = 4
PAST_LEN = 16384
PAGE_SIZE = 128

D_SSM = D_MODEL
SSM_HEAD_DIM = 64
SSM_HEADS = D_SSM // SSM_HEAD_DIM
SSM_GROUPS = 2
SSM_STATE = 128
SSM_CONV = 4
SSM_CHUNK = 128
D_XBC = D_SSM + 2 * SSM_GROUPS * SSM_STATE
D_POOL = D_MODEL
POOL_WINDOWS = (2, 4, 8, 16)
POOL_GROUPS = len(POOL_WINDOWS)
POOL_GROUP_DIM = D_POOL // POOL_GROUPS
POOL_HIST = max(POOL_WINDOWS) - 1
D_MIX = D_SSM + D_POOL
D_IN_PROJ = D_SSM + D_XBC + SSM_HEADS + D_POOL
N_MEM = 256
MEM_HEADS = 4
MEM_HEAD_DIM = D_MODEL // MEM_HEADS
D_FF = 2816
FFN_CONV = 3
EPS = 1e-6

kernel_name = 'ssd_multiscale_pool_hybrid_decoder_step'


def rmsnorm(x, g):
    xf = x.astype(jnp.float32)
    y = xf * lax.rsqrt(jnp.mean(xf * xf, axis=-1, keepdims=True) + EPS)
    return (y * g.astype(jnp.float32)).astype(x.dtype)


def gated_group_rmsnorm(y, z, g):
    shp = y.shape
    t = (y * jax.nn.silu(z)).astype(jnp.float32).reshape(shp[:-1] + (SSM_GROUPS, shp[-1] // SSM_GROUPS))
    t = t * lax.rsqrt(jnp.mean(t * t, axis=-1, keepdims=True) + EPS)
    return (t.reshape(shp) * g.astype(jnp.float32)).astype(y.dtype)


def causal_dwconv(prev, x, w, b):
    width = w.shape[0]
    seqlen = x.shape[1]
    ext = jnp.concatenate([prev, x], axis=1)
    y = b + sum(ext[:, k:k + seqlen] * w[k] for k in range(width))
    return y, ext[:, ext.shape[1] - (width - 1):]


def ssd_chunked(x, dt, a, b_in, c_in, h0):
    bsz, seqlen, n_heads, hd = x.shape
    n_groups, n_state = b_in.shape[2], b_in.shape[3]
    rep = n_heads // n_groups
    q = min(SSM_CHUNK, seqlen)
    nc = -(-seqlen // q)
    pad = nc * q - seqlen
    f32 = jnp.float32

    def chunked(t):
        t = jnp.pad(t.astype(f32), [(0, 0), (0, pad)] + [(0, 0)] * (t.ndim - 2))
        return t.reshape((bsz, nc, q) + t.shape[2:])

    xdt = chunked(x * dt[..., None]).reshape(bsz, nc, q, n_groups, rep, hd)
    da = chunked(dt * a).reshape(bsz, nc, q, n_groups, rep)
    bc = chunked(b_in)
    cc = chunked(c_in)
    acs = jnp.cumsum(da, axis=2)
    seg = acs[:, :, :, None] - acs[:, :, None, :]
    causal = jnp.tril(jnp.ones((q, q), dtype=bool))[None, None, :, :, None, None]
    decay = jnp.where(causal, jnp.exp(jnp.where(causal, seg, 0.0)), 0.0)
    cb = jnp.einsum('bclgn,bcsgn->bclsg', cc, bc)
    y_diag = jnp.einsum('bclsg,bclsgr,bcsgrp->bclgrp', cb, decay, xdt)
    decay_end = jnp.exp(acs[:, :, -1:] - acs)
    chunk_states = jnp.einsum('bcsgn,bcsgr,bcsgrp->bcgrpn', bc, decay_end, xdt)
    chunk_decay = jnp.exp(acs[:, :, -1])

    def step(h, inp):
        s, dcy = inp
        return h * dcy[..., None, None] + s, h

    h_init = h0.astype(f32).reshape(bsz, n_groups, rep, hd, n_state)
    h_final, h_prev = lax.scan(step, h_init, (jnp.moveaxis(chunk_states, 1, 0), jnp.moveaxis(chunk_decay, 1, 0)))
    h_prev = jnp.moveaxis(h_prev, 0, 1)
    y_off = jnp.einsum('bclgn,bcgrpn,bclgr->bclgrp', cc, h_prev, jnp.exp(acs))
    y = (y_diag + y_off).reshape(bsz, nc * q, n_heads, hd)[:, :seqlen]
    return y.astype(x.dtype), h_final.reshape(bsz, n_heads, hd, n_state).astype(h0.dtype)


def multiscale_pool(prev, v, start):
    seqlen = v.shape[1]
    ext = jnp.concatenate([prev, v], axis=1)
    cs = jnp.pad(jnp.cumsum(ext.astype(jnp.float32), axis=1), [(0, 0), (1, 0), (0, 0)])
    pos = start + jnp.arange(seqlen)
    e = POOL_HIST + 1
    outs = []
    for gi, w in enumerate(POOL_WINDOWS):
        lo, hi = gi * POOL_GROUP_DIM, (gi + 1) * POOL_GROUP_DIM
        win = cs[:, e:e + seqlen, lo:hi] - cs[:, e - w:e - w + seqlen, lo:hi]
        cnt = jnp.minimum(pos + 1, w).astype(jnp.float32)[None, :, None]
        outs.append(win / cnt)
    pooled = jnp.concatenate(outs, axis=-1) - v.astype(jnp.float32)
    return pooled.astype(v.dtype), ext[:, ext.shape[1] - POOL_HIST:]


def memory_kv(mem, norm_memkv, w_mk, w_mv):
    bsz = mem.shape[0]
    hm = rmsnorm(mem, norm_memkv)
    k = (hm @ w_mk).reshape(bsz, N_MEM, MEM_HEADS, MEM_HEAD_DIM)
    v = (hm @ w_mv).reshape(bsz, N_MEM, MEM_HEADS, MEM_HEAD_DIM)
    return k, v


def hybrid_layer(x, mem_k, mem_v, ssm_h, conv_buf, pool_buf, ffn_buf, start,
                 norm_mix, w_in, ssm_conv_w, ssm_conv_b, ssm_dt_bias, ssm_a_log, ssm_d, ssm_norm,
                 w_pool, pool_scale, w_out, norm_mem, w_mq, w_mo,
                 norm_ffn, w_up, ffn_conv_w, ffn_conv_b, w_down):
    bsz, seqlen, _ = x.shape
    f32 = jnp.float32
    h = rmsnorm(x, norm_mix)
    proj = h @ w_in
    z, xbc, dt_raw, v_pool = jnp.split(proj, [D_SSM, D_SSM + D_XBC, D_SSM + D_XBC + SSM_HEADS], axis=-1)
    xbc, conv_new = causal_dwconv(conv_buf, xbc, ssm_conv_w, ssm_conv_b)
    xbc = jax.nn.silu(xbc)
    xs, b_ssm, c_ssm = jnp.split(xbc, [D_SSM, D_SSM + SSM_GROUPS * SSM_STATE], axis=-1)
    xs = xs.reshape(bsz, seqlen, SSM_HEADS, SSM_HEAD_DIM)
    dt = jax.nn.softplus((dt_raw + ssm_dt_bias).astype(f32))
    a = -jnp.exp(ssm_a_log.astype(f32))
    y, ssm_new = ssd_chunked(xs, dt, a,
                             b_ssm.reshape(bsz, seqlen, SSM_GROUPS, SSM_STATE),
                             c_ssm.reshape(bsz, seqlen, SSM_GROUPS, SSM_STATE), ssm_h)
    y = y + xs * ssm_d[:, None]
    y = gated_group_rmsnorm(y.reshape(bsz, seqlen, D_SSM), z, ssm_norm)
    pooled, pool_new = multiscale_pool(pool_buf, v_pool, start)
    pooled = jnp.einsum('blgc,gcd->blgd', pooled.reshape(bsz, seqlen, POOL_GROUPS, POOL_GROUP_DIM), w_pool)
    pooled = pooled.reshape(bsz, seqlen, D_POOL) * pool_scale
    x = x + jnp.concatenate([y, pooled], axis=-1) @ w_out
    h = rmsnorm(x, norm_mem)
    qm = (h @ w_mq).reshape(bsz, seqlen, MEM_HEADS, MEM_HEAD_DIM)
    s = jnp.einsum('blhd,bmhd->bhlm', qm, mem_k).astype(f32) * (MEM_HEAD_DIM ** -0.5)
    pr = jax.nn.softmax(s, axis=-1).astype(x.dtype)
    o = jnp.einsum('bhlm,bmhd->blhd', pr, mem_v).reshape(bsz, seqlen, D_MODEL)
    x = x + o @ w_mo
    h = rmsnorm(x, norm_ffn)
    u = h @ w_up
    u, ffn_new = causal_dwconv(ffn_buf, u, ffn_conv_w, ffn_conv_b)
    g, val = jnp.split(u, [D_FF], axis=-1)
    x = x + (jax.nn.silu(g) * val) @ w_down
    return x, ssm_new, conv_new, pool_new, ffn_new


def setup_inputs(seed: int = 0) -> dict:
    key = jax.random.key(seed)
    ks = iter(jax.random.split(key, 48))
    f32 = jnp.float32

    def nrm(shape, scale):
        return jax.random.normal(next(ks), shape, f32) * scale

    def gain(shape):
        return 1.0 + nrm(shape, 0.02)

    dt0 = jnp.exp(jax.random.uniform(next(ks), (DEPTH, SSM_HEADS), f32)
                  * (math.log(0.1) - math.log(0.001)) + math.log(0.001))
    dt_bias = dt0 + jnp.log(-jnp.expm1(-dt0))
    a_log = jnp.log(jax.random.uniform(next(ks), (DEPTH, SSM_HEADS), f32, 1.0, 16.0))
    return {
        'x_prompt': nrm((BATCH, SEQ, D_MODEL), 1.0),
        'x_sample': nrm((DEC_BATCH, DEC_SEQ, D_MODEL), 1.0),
        'mem_prompt': nrm((BATCH, N_MEM, D_MODEL), 1.0),
        'state_ssm': nrm((DEPTH, DEC_BATCH, SSM_HEADS, SSM_HEAD_DIM, SSM_STATE), 0.1),
        'state_ssm_conv': nrm((DEPTH, DEC_BATCH, SSM_CONV - 1, D_XBC), 1.0),
        'state_pool': nrm((DEPTH, DEC_BATCH, POOL_HIST, D_POOL), 1.0),
        'state_ffn_conv': nrm((DEPTH, DEC_BATCH, FFN_CONV - 1, 2 * D_FF), 1.0),
        'cache_mem_k': nrm((DEPTH, DEC_BATCH, N_MEM, MEM_HEADS, MEM_HEAD_DIM), 1.0),
        'cache_mem_v': nrm((DEPTH, DEC_BATCH, N_MEM, MEM_HEADS, MEM_HEAD_DIM), 1.0),
        'norm_mix': gain((DEPTH, D_MODEL)),
        'w_in': nrm((DEPTH, D_MODEL, D_IN_PROJ), D_MODEL ** -0.5),
        'ssm_conv_w': nrm((DEPTH, SSM_CONV, D_XBC), 0.5),
        'ssm_conv_b': nrm((DEPTH, D_XBC), 0.02),
        'ssm_dt_bias': dt_bias,
        'ssm_a_log': a_log,
        'ssm_d': 1.0 + nrm((DEPTH, SSM_HEADS), 0.1),
        'ssm_norm': gain((DEPTH, D_SSM)),
        'w_pool': nrm((DEPTH, POOL_GROUPS, POOL_GROUP_DIM, POOL_GROUP_DIM), POOL_GROUP_DIM ** -0.5),
        'pool_scale': 1.0 + nrm((DEPTH, D_POOL), 0.1),
        'w_out': nrm((DEPTH, D_MIX, D_MODEL), D_MIX ** -0.5),
        'norm_mem': gain((DEPTH, D_MODEL)),
        'norm_memkv': gain((DEPTH, D_MODEL)),
        'w_mq': nrm((DEPTH, D_MODEL, D_MODEL), D_MODEL ** -0.5),
        'w_mk': nrm((DEPTH, D_MODEL, D_MODEL), D_MODEL ** -0.5),
        'w_mv': nrm((DEPTH, D_MODEL, D_MODEL), D_MODEL ** -0.5),
        'w_mo': nrm((DEPTH, D_MODEL, D_MODEL), D_MODEL ** -0.5),
        'norm_ffn': gain((DEPTH, D_MODEL)),
        'w_up': nrm((DEPTH, D_MODEL, 2 * D_FF), D_MODEL ** -0.5),
        'ffn_conv_w': nrm((DEPTH, FFN_CONV, 2 * D_FF), 0.5),
        'ffn_conv_b': nrm((DEPTH, 2 * D_FF), 0.02),
        'w_down': nrm((DEPTH, D_FF, D_MODEL), D_FF ** -0.5),
        'final_norm': gain((D_MODEL,)),
    }


def reference(x_prompt, x_sample, mem_prompt, state_ssm, state_ssm_conv, state_pool, state_ffn_conv,
              cache_mem_k, cache_mem_v,
              norm_mix, w_in, ssm_conv_w, ssm_conv_b, ssm_dt_bias, ssm_a_log, ssm_d, ssm_norm,
              w_pool, pool_scale, w_out, norm_mem, norm_memkv, w_mq, w_mk, w_mv, w_mo,
              norm_ffn, w_up, ffn_conv_w, ffn_conv_b, w_down, final_norm):
    layer_params = [norm_mix, w_in, ssm_conv_w, ssm_conv_b, ssm_dt_bias, ssm_a_log, ssm_d, ssm_norm,
                    w_pool, pool_scale, w_out, norm_mem, w_mq, w_mo,
                    norm_ffn, w_up, ffn_conv_w, ffn_conv_b, w_down]
    dtp = x_prompt.dtype
    hp, hs = x_prompt, x_sample
    ssm_p, ssm_s, conv_p, conv_s, pool_p, pool_s, ffn_p, ffn_s, mk_p, mv_p = ([] for _ in range(10))
    for i in range(DEPTH):
        lp = [p[i] for p in layer_params]
        mem_k, mem_v = memory_kv(mem_prompt, norm_memkv[i], w_mk[i], w_mv[i])
        hp, s1, s2, s3, s4 = hybrid_layer(
            hp, mem_k, mem_v,
            jnp.zeros((BATCH, SSM_HEADS, SSM_HEAD_DIM, SSM_STATE), dtp),
            jnp.zeros((BATCH, SSM_CONV - 1, D_XBC), dtp),
            jnp.zeros((BATCH, POOL_HIST, D_POOL), dtp),
            jnp.zeros((BATCH, FFN_CONV - 1, 2 * D_FF), dtp),
            0, *lp)
        ssm_p.append(s1); conv_p.append(s2); pool_p.append(s3); ffn_p.append(s4)
        mk_p.append(mem_k); mv_p.append(mem_v)
        hs, t1, t2, t3, t4 = hybrid_layer(
            hs, cache_mem_k[i], cache_mem_v[i], state_ssm[i], state_ssm_conv[i], state_pool[i],
            state_ffn_conv[i], PAST_LEN, *lp)
        ssm_s.append(t1); conv_s.append(t2); pool_s.append(t3); ffn_s.append(t4)
    y_prompt = rmsnorm(hp, final_norm)
    y_sample = rmsnorm(hs, final_norm)
    return (y_prompt, y_sample,
            jnp.stack(ssm_p), jnp.stack(ssm_s),
            jnp.stack(conv_p), jnp.stack(conv_s),
            jnp.stack(pool_p), jnp.stack(pool_s),
            jnp.stack(ffn_p), jnp.stack(ffn_s),
            jnp.stack(mk_p), jnp.stack(mv_p))
```

```python
import functools

import jax
import jax.numpy as jnp
from jax import lax
from jax.experimental import pallas as pl
from jax.experimental.pallas import tpu as pltpu

f32 = jnp.float32
bf16 = jnp.bfloat16

D_MODEL = 1024
SSM_HEADS = 16
SSM_HEAD_DIM = 64
SSM_STATE = 128
SSM_GROUPS = 2
SSM_CHUNK = 128
D_SSM = 1024
D_BC = SSM_GROUPS * SSM_STATE
D_XBC = D_SSM + 2 * D_BC
SSM_CONV = 4
D_POOL = 1024
POOL_WINDOWS = (2, 4, 8, 16)
POOL_GROUP_DIM = 256
POOL_HIST = 15
N_MEM = 256
MEM_HEADS = 4
MEM_HEAD_DIM = 256
D_FF = 2816
FFN_CONV = 3
EPS = 1e-6
PAST_LEN = 16384

LANES = 128
SUBLANES = 8
MXU_DIM = 256
VMEM_LIMIT_BYTES = 56 * 1024 * 1024

COL_Z = 0
COL_XBC = D_SSM
COL_VP = COL_XBC + D_XBC
COL_DT = COL_VP + D_POOL
D_PROJ = COL_DT + LANES

CONV_HIST_ROWS = SUBLANES
POOL_HIST_ROWS = 2 * SUBLANES
FF_CHUNK = MXU_DIM
N_FF_CHUNKS = D_FF // FF_CHUNK


def _silu(v):
    return v * (1.0 / (1.0 + jnp.exp(-v)))


def _softplus(v):
    return jnp.maximum(v, 0.0) + jnp.log1p(jnp.exp(-jnp.abs(v)))


def _rmsnorm(x, g):
    ms = jnp.mean(x * x, axis=-1, keepdims=True)
    return x * lax.rsqrt(ms + EPS) * g


def _dot(a, b):
    return jnp.dot(a, b, preferred_element_type=f32)


def _dot_nt(a, b):
    return lax.dot_general(a, b, (((1,), (1,)), ((), ())), preferred_element_type=f32)


def _dot_tn(a, b):
    return lax.dot_general(a, b, (((0,), (0,)), ((), ())), preferred_element_type=f32)


def _split3(v):
    p1 = v.astype(bf16)
    r1 = v - p1.astype(f32)
    p2 = r1.astype(bf16)
    r2 = r1 - p2.astype(f32)
    return p1, p2, r2.astype(bf16)


def _const_spec(shape):
    return pl.BlockSpec(shape, lambda *_: (0,) * len(shape))


def _memkv_kernel(mem_ref, g_ref, wk_ref, wv_ref, k_ref, v_ref):
    h = _rmsnorm(mem_ref[0], g_ref[...]).astype(bf16)
    k_ref[0] = _dot(h, wk_ref[...])
    v_ref[0] = _dot(h, wv_ref[...])


def _memkv(mem, g, wk, wv):
    b = mem.shape[0]
    blk = pl.BlockSpec((1, N_MEM, D_MODEL), lambda i: (i, 0, 0))
    return pl.pallas_call(
        _memkv_kernel,
        grid=(b,),
        in_specs=[blk, _const_spec((1, D_MODEL)), _const_spec((D_MODEL, D_MODEL)), _const_spec((D_MODEL, D_MODEL))],
        out_specs=[blk, blk],
        out_shape=[jax.ShapeDtypeStruct((b, N_MEM, D_MODEL), f32)] * 2,
        compiler_params=pltpu.CompilerParams(dimension_semantics=("arbitrary",), vmem_limit_bytes=VMEM_LIMIT_BYTES),
        name="memkv",
    )(mem, g, wk, wv)


def _ssd_chunk(r0, dt_scr, xs_scr, b_scr, c_scr, y_scr, xd_scr, hst_scr, dtb_ref, a_ref, dexp_ref):
    q = SSM_CHUNK
    rows = pl.ds(r0, q)
    row_i = lax.broadcasted_iota(jnp.int32, (q, q), 0)
    col_i = lax.broadcasted_iota(jnp.int32, (q, q), 1)
    causal = col_i <= row_i
    lo = col_i < SSM_HEAD_DIM
    tril = jnp.where(causal, 1.0, 0.0).astype(bf16)

    dt = _softplus(dt_scr[rows, :] + dtb_ref[...])
    da = dt * a_ref[...]
    p1, p2, p3 = _split3(da)
    acs = _dot(tril, p1) + _dot(tril, p2) + _dot(tril, p3)
    acs_t = acs.T

    for g in range(SSM_GROUPS):
        bg = b_scr[rows, g * SSM_STATE:(g + 1) * SSM_STATE]
        cg = c_scr[rows, g * SSM_STATE:(g + 1) * SSM_STATE]
        bg_b = bg.astype(bf16)
        cb = jnp.where(causal, _dot_nt(cg.astype(bf16), bg_b), 0.0)
        cdec_rows = []
        pairs_per_group = SSM_HEADS // SSM_GROUPS // 2
        for pp in range(pairs_per_group):
            pr = g * pairs_per_group + pp
            lanes = slice(pr * LANES, (pr + 1) * LANES)
            lhs, dtb, dend, cdec = [], [], [], []
            for hh in (2 * pr, 2 * pr + 1):
                colb = jnp.broadcast_to(acs[:, hh:hh + 1], (q, q))
                seg = jnp.where(causal, colb - acs_t[hh:hh + 1, :], 0.0)
                lhs.append((jnp.exp(seg) * cb).astype(bf16))
                lhs.append((cg * jnp.exp(colb)).astype(bf16))
                last = colb[q - 1:q, :]
                dend.append(jnp.exp(last - colb))
                cdec.append(jnp.exp(last))
                dtb.append(jnp.broadcast_to(dt[:, hh:hh + 1], (q, q)))
            xs_pair = xs_scr[rows, lanes]
            xdt = xs_pair * jnp.where(lo, dtb[0], dtb[1])
            xd_scr[:, lanes] = (xdt * jnp.where(lo, dend[0], dend[1])).astype(bf16)
            hst_pair = hst_scr[:, lanes]
            rhs = jnp.concatenate([
                jnp.where(lo, xdt, 0.0).astype(bf16), jnp.where(lo, hst_pair, 0.0).astype(bf16),
                jnp.where(lo, 0.0, xdt).astype(bf16), jnp.where(lo, 0.0, hst_pair).astype(bf16)], axis=0)
            y_pair = _dot(jnp.concatenate(lhs, axis=1), rhs)
            y_scr[rows, lanes] = y_pair + xs_pair * dexp_ref[:, lanes]
            cdec_rows.append(jnp.where(lo[:1], cdec[0], cdec[1]))
        gl = slice(g * (D_SSM // SSM_GROUPS), (g + 1) * (D_SSM // SSM_GROUPS))
        upd = _dot_tn(bg_b, xd_scr[:, gl])
        hst_scr[:, gl] = hst_scr[:, gl] * jnp.concatenate(cdec_rows, axis=1) + upd


def _mixer_kernel(x_ref, nmix_ref, win_ref, cw_ref, cb_ref, dtb_ref, a_ref, dexp_ref, snorm_ref, wpool_ref,
                  pscale_ref, wout_ref,
                  x1_ref, ssm_ref, conv_ref, pool_ref,
                  h_scr, z_scr, xbc_scr, vp_scr, dt_scr, xs_scr, b_scr, c_scr, y_scr, xd_scr, hst_scr,
                  pooled_scr, cat_scr, *, tile):
    t = pl.program_id(1)
    last_t = pl.num_programs(1) - 1
    ch = CONV_HIST_ROWS
    ph = POOL_HIST_ROWS

    @pl.when(t == 0)
    def _():
        xbc_scr[:, 0:ch, :] = jnp.zeros((D_XBC // LANES, ch, LANES), f32)
        vp_scr[:, 0:ph, :] = jnp.zeros((D_POOL // LANES, ph, LANES), f32)
        hst_scr[...] = jnp.zeros_like(hst_scr)

    x = x_ref[0]
    h_scr[...] = _rmsnorm(x, nmix_ref[...]).astype(bf16)
    z_scr[...] = _dot(h_scr[...], win_ref[:, COL_Z:COL_XBC])
    xbc = _dot(h_scr[...], win_ref[:, COL_XBC:COL_VP])
    for j in range(D_XBC // LANES):
        xbc_scr[j, ch:ch + tile, :] = xbc[:, j * LANES:(j + 1) * LANES]
    vp = _dot(h_scr[...], win_ref[:, COL_VP:COL_DT])
    for j in range(D_POOL // LANES):
        vp_scr[j, ph:ph + tile, :] = vp[:, j * LANES:(j + 1) * LANES]
    dt_scr[...] = _dot(h_scr[...], win_ref[:, COL_DT:D_PROJ])

    for j in range(D_XBC // LANES):
        cl = slice(j * LANES, (j + 1) * LANES)
        acc = cb_ref[:, cl] + cw_ref[0:1, cl] * xbc_scr[j, pl.ds(ch - 3, tile), :]
        for k in range(1, SSM_CONV):
            acc = acc + cw_ref[k:k + 1, cl] * xbc_scr[j, pl.ds(ch - 3 + k, tile), :]
        act = _silu(acc)
        if j < D_SSM // LANES:
            xs_scr[:, cl] = act
        elif j < (D_SSM + D_BC) // LANES:
            b_scr[:, j * LANES - D_SSM:(j + 1) * LANES - D_SSM] = act
        else:
            c_scr[:, j * LANES - D_SSM - D_BC:(j + 1) * LANES - D_SSM - D_BC] = act

        @pl.when(t == last_t)
        def _():
            conv_ref[0, :, cl] = xbc_scr[j, pl.ds(ch + tile - (SSM_CONV - 1), SSM_CONV - 1), :]

        xbc_scr[j, 0:ch, :] = xbc_scr[j, tile:tile + ch, :]

    pos1 = lax.broadcasted_iota(jnp.int32, (tile, LANES), 0) + (t * tile + 1)
    for j in range(D_POOL // LANES):
        cl = slice(j * LANES, (j + 1) * LANES)
        w = POOL_WINDOWS[j * LANES // POOL_GROUP_DIM]
        cur = vp_scr[j, ph:ph + tile, :]
        s = cur
        for k in range(1, w):
            s = s + vp_scr[j, pl.ds(ph - k, tile), :]
        cnt = jnp.minimum(pos1, w).astype(f32)
        pooled_scr[:, cl] = (s / cnt - cur).astype(bf16)

        @pl.when(t == last_t)
        def _():
            pool_ref[0, :, cl] = vp_scr[j, pl.ds(ph + tile - POOL_HIST, POOL_HIST), :]

        vp_scr[j, 0:ph, :] = vp_scr[j, tile:tile + ph, :]
    for gi in range(len(POOL_WINDOWS)):
        gl = slice(gi * POOL_GROUP_DIM, (gi + 1) * POOL_GROUP_DIM)
        pg = _dot(pooled_scr[:, gl], wpool_ref[gi]) * pscale_ref[:, gl]
        cat_scr[:, D_SSM + gi * POOL_GROUP_DIM:D_SSM + (gi + 1) * POOL_GROUP_DIM] = pg.astype(bf16)

    for c in range(tile // SSM_CHUNK):
        _ssd_chunk(c * SSM_CHUNK, dt_scr, xs_scr, b_scr, c_scr, y_scr, xd_scr, hst_scr, dtb_ref, a_ref, dexp_ref)

    gw = D_SSM // SSM_GROUPS
    for g in range(SSM_GROUPS):
        gl = slice(g * gw, (g + 1) * gw)
        tg = y_scr[:, gl] * _silu(z_scr[:, gl])
        ms = jnp.mean(tg * tg, axis=-1, keepdims=True)
        cat_scr[:, gl] = (tg * lax.rsqrt(ms + EPS) * snorm_ref[:, gl]).astype(bf16)

    x1_ref[0] = x + _dot(cat_scr[...], wout_ref[...])

    @pl.when(t == last_t)
    def _():
        for pr in range(D_SSM // LANES):
            ssm_ref[0, pr * LANES:(pr + 1) * LANES, :] = hst_scr[:, pr * LANES:(pr + 1) * LANES].T


def _mixer_prompt(x, nmix, win, cw, cb, dtb, a_row, dexp, snorm, wpool, pscale, wout, tile):
    b, seq, _ = x.shape
    nt = seq // tile
    xblk = pl.BlockSpec((1, tile, D_MODEL), lambda i, j: (i, j, 0))
    scratch = [
        pltpu.VMEM((tile, D_MODEL), bf16),
        pltpu.VMEM((tile, D_SSM), f32),
        pltpu.VMEM((D_XBC // LANES, CONV_HIST_ROWS + tile, LANES), f32),
        pltpu.VMEM((D_POOL // LANES, POOL_HIST_ROWS + tile, LANES), f32),
        pltpu.VMEM((tile, LANES), f32),
        pltpu.VMEM((tile, D_SSM), f32),
        pltpu.VMEM((tile, D_BC), f32),
        pltpu.VMEM((tile, D_BC), f32),
        pltpu.VMEM((tile, D_SSM), f32),
        pltpu.VMEM((SSM_CHUNK, D_SSM), bf16),
        pltpu.VMEM((SSM_STATE, D_SSM), f32),
        pltpu.VMEM((tile, D_POOL), bf16),
        pltpu.VMEM((tile, D_SSM + D_POOL), bf16),
    ]
    return pl.pallas_call(
        functools.partial(_mixer_kernel, tile=tile),
        grid=(b, nt),
        in_specs=[xblk, _const_spec((1, D_MODEL)), _const_spec((D_MODEL, D_PROJ)), _const_spec((SSM_CONV, D_XBC)),
                  _const_spec((1, D_XBC)), _const_spec((1, LANES)), _const_spec((1, LANES)), _const_spec((1, D_SSM)),
                  _const_spec((1, D_SSM)), _const_spec((len(POOL_WINDOWS), POOL_GROUP_DIM, POOL_GROUP_DIM)),
                  _const_spec((1, D_POOL)), _const_spec((D_SSM + D_POOL, D_MODEL))],
        out_specs=[xblk,
                   pl.BlockSpec((1, D_SSM, SSM_STATE), lambda i, j: (i, 0, 0)),
                   pl.BlockSpec((1, SSM_CONV - 1, D_XBC), lambda i, j: (i, 0, 0)),
                   pl.BlockSpec((1, POOL_HIST, D_POOL), lambda i, j: (i, 0, 0))],
        out_shape=[jax.ShapeDtypeStruct((b, seq, D_MODEL), f32),
                   jax.ShapeDtypeStruct((b, D_SSM, SSM_STATE), f32),
                   jax.ShapeDtypeStruct((b, SSM_CONV - 1, D_XBC), f32),
                   jax.ShapeDtypeStruct((b, POOL_HIST, D_POOL), f32)],
        scratch_shapes=scratch,
        compiler_params=pltpu.CompilerParams(dimension_semantics=("arbitrary", "arbitrary"),
                                             vmem_limit_bytes=VMEM_LIMIT_BYTES),
        name="mixer_prompt",
    )(x, nmix, win, cw, cb, dtb, a_row, dexp, snorm, wpool, pscale, wout)


def _attn_kernel(x_ref, g_ref, wq_ref, k_ref, v_ref, wo_ref, o_ref, q_scr, ao_scr):
    x = x_ref[0]
    h = _rmsnorm(x, g_ref[...]).astype(bf16)
    q_scr[...] = (_dot(h, wq_ref[...]) * (MEM_HEAD_DIM ** -0.5)).astype(bf16)
    for hd in range(MEM_HEADS):
        hl = slice(hd * MEM_HEAD_DIM, (hd + 1) * MEM_HEAD_DIM)
        s = _dot_nt(q_scr[:, hl], k_ref[0, :, hl].astype(bf16))
        p = jnp.exp(s - jnp.max(s, axis=-1, keepdims=True))
        p = (p / jnp.sum(p, axis=-1, keepdims=True)).astype(bf16)
        ao_scr[:, hl] = _dot(p, v_ref[0, :, hl].astype(bf16)).astype(bf16)
    o_ref[0] = x + _dot(ao_scr[...], wo_ref[...])


def _attn_prompt(x, g, wq, mem_k, mem_v, wo, tile):
    b, seq, _ = x.shape
    xblk = pl.BlockSpec((1, tile, D_MODEL), lambda i, j: (i, j, 0))
    kvblk = pl.BlockSpec((1, N_MEM, D_MODEL), lambda i, j: (i, 0, 0))
    return pl.pallas_call(
        _attn_kernel,
        grid=(b, seq // tile),
        in_specs=[xblk, _const_spec((1, D_MODEL)), _const_spec((D_MODEL, D_MODEL)), kvblk, kvblk,
                  _const_spec((D_MODEL, D_MODEL))],
        out_specs=xblk,
        out_shape=jax.ShapeDtypeStruct((b, seq, D_MODEL), f32),
        scratch_shapes=[pltpu.VMEM((tile, D_MODEL), bf16), pltpu.VMEM((tile, D_MODEL), bf16)],
        compiler_params=pltpu.CompilerParams(dimension_semantics=("arbitrary", "arbitrary"),
                                             vmem_limit_bytes=VMEM_LIMIT_BYTES),
        name="attn_prompt",
    )(x, g, wq, mem_k, mem_v, wo)


def _ffn_kernel(x_ref, g_ref, wup_ref, cw_ref, cb_ref, wdn_ref, gfin_ref, y_ref, st_ref, h_scr, u_scr, acc_scr, *, tile):
    t = pl.program_id(1)
    last_t = pl.num_programs(1) - 1
    ch = CONV_HIST_ROWS
    tpc = FF_CHUNK // LANES

    @pl.when(t == 0)
    def _():
        u_scr[:, 0:ch, :] = jnp.zeros((2 * D_FF // LANES, ch, LANES), f32)

    x = x_ref[0]
    h_scr[...] = _rmsnorm(x, g_ref[...]).astype(bf16)
    acc_scr[...] = x

    def conv_chunk(c):
        u = _dot(h_scr[...], wup_ref[c])
        outs = []
        for i in range(tpc):
            ti = c * tpc + i
            cl = slice(i * LANES, (i + 1) * LANES)
            u_scr[ti, ch:ch + tile, :] = u[:, cl]
            w = cw_ref[c]
            acc = cb_ref[c][:, cl] + w[0:1, cl] * u_scr[ti, pl.ds(ch - 2, tile), :]
            acc = acc + w[1:2, cl] * u_scr[ti, pl.ds(ch - 1, tile), :]
            acc = acc + w[2:3, cl] * u[:, cl]
            outs.append(acc)
            u_scr[ti, 0:ch, :] = u_scr[ti, tile:tile + ch, :]
        return jnp.concatenate(outs, axis=1)

    def body(j, carry):
        gate = conv_chunk(j)
        val = conv_chunk(j + N_FF_CHUNKS)
        act = (_silu(gate) * val).astype(bf16)
        acc_scr[...] += _dot(act, wdn_ref[j])
        return carry

    lax.fori_loop(0, N_FF_CHUNKS, body, 0)
    y_ref[0] = _rmsnorm(acc_scr[...], gfin_ref[...])

    @pl.when(t == last_t)
    def _():
        for ti in range(2 * D_FF // LANES):
            st_ref[0, :, ti * LANES:(ti + 1) * LANES] = u_scr[ti, pl.ds(ch - (FFN_CONV - 1), FFN_CONV - 1), :]


def _ffn_prompt(x, g, wup_r, cw_r, cb_r, wdn_r, gfin, tile):
    b, seq, _ = x.shape
    xblk = pl.BlockSpec((1, tile, D_MODEL), lambda i, j: (i, j, 0))
    nc = 2 * N_FF_CHUNKS
    return pl.pallas_call(
        functools.partial(_ffn_kernel, tile=tile),
        grid=(b, seq // tile),
        in_specs=[xblk, _const_spec((1, D_MODEL)), _const_spec((nc, D_MODEL, FF_CHUNK)),
                  _const_spec((nc, FFN_CONV, FF_CHUNK)), _const_spec((nc, 1, FF_CHUNK)),
                  _const_spec((N_FF_CHUNKS, FF_CHUNK, D_MODEL)), _const_spec((1, D_MODEL))],
        out_specs=[xblk, pl.BlockSpec((1, FFN_CONV - 1, 2 * D_FF), lambda i, j: (i, 0, 0))],
        out_shape=[jax.ShapeDtypeStruct((b, seq, D_MODEL), f32),
                   jax.ShapeDtypeStruct((b, FFN_CONV - 1, 2 * D_FF), f32)],
        scratch_shapes=[pltpu.VMEM((tile, D_MODEL), bf16),
                        pltpu.VMEM((2 * D_FF // LANES, CONV_HIST_ROWS + tile, LANES), f32),
                        pltpu.VMEM((tile, D_MODEL), f32)],
        compiler_params=pltpu.CompilerParams(dimension_semantics=("arbitrary", "arbitrary"),
                                             vmem_limit_bytes=VMEM_LIMIT_BYTES),
        name="ffn_prompt",
    )(x, g, wup_r, cw_r, cb_r, wdn_r, gfin)


def _prep_weights(p):
    i = 0
    w_in = p["w_in"][i]
    dt_cols = jnp.pad(w_in[:, D_SSM + D_XBC:D_SSM + D_XBC + SSM_HEADS], ((0, 0), (0, LANES - SSM_HEADS)))
    win = jnp.concatenate([w_in[:, :D_SSM + D_XBC], w_in[:, D_SSM + D_XBC + SSM_HEADS:], dt_cols], axis=1).astype(bf16)
    pad_h = (0, LANES - SSM_HEADS)
    nc = 2 * N_FF_CHUNKS
    return dict(
        nmix=p["norm_mix"][i][None], win=win, cw=p["ssm_conv_w"][i], cb=p["ssm_conv_b"][i][None],
        dtb=jnp.pad(p["ssm_dt_bias"][i], pad_h)[None],
        a_row=jnp.pad(-jnp.exp(p["ssm_a_log"][i].astype(f32)), pad_h)[None],
        dexp=jnp.repeat(p["ssm_d"][i], SSM_HEAD_DIM)[None], snorm=p["ssm_norm"][i][None],
        wpool=p["w_pool"][i].astype(bf16), pscale=p["pool_scale"][i][None], wout=p["w_out"][i].astype(bf16),
        nmem=p["norm_mem"][i][None], nmemkv=p["norm_memkv"][i][None],
        wq=p["w_mq"][i].astype(bf16), wk=p["w_mk"][i].astype(bf16), wv=p["w_mv"][i].astype(bf16),
        wo=p["w_mo"][i].astype(bf16),
        nffn=p["norm_ffn"][i][None],
        wup=p["w_up"][i].astype(bf16).reshape(D_MODEL, nc, FF_CHUNK).transpose(1, 0, 2),
        fcw=p["ffn_conv_w"][i].reshape(FFN_CONV, nc, FF_CHUNK).transpose(1, 0, 2),
        fcb=p["ffn_conv_b"][i].reshape(nc, 1, FF_CHUNK),
        wdn=p["w_down"][i].astype(bf16).reshape(N_FF_CHUNKS, FF_CHUNK, D_MODEL),
        gfin=p["final_norm"][None],
    )


def _prompt_path(x_prompt, mem_prompt, w, tile):
    b = x_prompt.shape[0]
    mem_k, mem_v = _memkv(mem_prompt, w["nmemkv"], w["wk"], w["wv"])
    x1, ssm, conv, pool = _mixer_prompt(x_prompt, w["nmix"], w["win"], w["cw"], w["cb"], w["dtb"], w["a_row"],
                                        w["dexp"], w["snorm"], w["wpool"], w["pscale"], w["wout"], tile)
    x2 = _attn_prompt(x1, w["nmem"], w["wq"], mem_k, mem_v, w["wo"], tile)
    y, ffn = _ffn_prompt(x2, w["nffn"], w["wup"], w["fcw"], w["fcb"], w["wdn"], w["gfin"], tile)
    return (y, ssm.reshape(1, b, SSM_HEADS, SSM_HEAD_DIM, SSM_STATE), conv[None], pool[None], ffn[None],
            mem_k.reshape(1, b, N_MEM, MEM_HEADS, MEM_HEAD_DIM), mem_v.reshape(1, b, N_MEM, MEM_HEADS, MEM_HEAD_DIM))


PROMPT_TILE = 256


DEC_SEQ = 4
S_SEQ_BLOCK = 32
S_SSD_BLOCK = 8
S_ATT_BLOCK = 4
ROW_GROUP = 16


def _expand_heads(v, lo):
    r = v.shape[0]
    tiles = []
    for pr in range(SSM_HEADS // 2):
        a = jnp.broadcast_to(v[:, 2 * pr:2 * pr + 1], (r, LANES))
        b = jnp.broadcast_to(v[:, 2 * pr + 1:2 * pr + 2], (r, LANES))
        tiles.append(jnp.where(lo, a, b))
    return jnp.concatenate(tiles, axis=1)


def _time_major(ref, width):
    return jnp.concatenate([ref[:, l * width:(l + 1) * width] for l in range(DEC_SEQ)], axis=0)


def _smix_in_kernel(x_ref, sconv_ref, spool_ref, nmix_ref, win_ref, cw_ref, cb_ref, dtb_ref, a_ref, dexp_ref,
                    wpool_ref, pscale_ref,
                    z_ref, ypart_ref, eacs_ref, xd_ref, c_ref, b_ref, dec_ref, pout_ref, conv_ref, pool_ref,
                    h_scr, pooled_scr):
    nb = x_ref.shape[0]
    lo = lax.broadcasted_iota(jnp.int32, (nb, LANES), 1) < SSM_HEAD_DIM
    h_scr[...] = _rmsnorm(_time_major(x_ref, D_MODEL), nmix_ref[...]).astype(bf16)
    z = _dot(h_scr[...], win_ref[:, COL_Z:COL_XBC])
    for l in range(DEC_SEQ):
        z_ref[:, l * D_SSM:(l + 1) * D_SSM] = z[l * nb:(l + 1) * nb]
    xbc = _dot(h_scr[...], win_ref[:, COL_XBC:COL_VP])
    vp = _dot(h_scr[...], win_ref[:, COL_VP:COL_DT])
    dtr = _dot(h_scr[...], win_ref[:, COL_DT:D_PROJ])

    def conv_slot(i):
        if i < SSM_CONV - 1:
            return sconv_ref[:, i * D_XBC:(i + 1) * D_XBC]
        return xbc[(i - SSM_CONV + 1) * nb:(i - SSM_CONV + 2) * nb]

    def pool_slot(i, cl):
        if i < POOL_HIST:
            return spool_ref[:, i * D_POOL + cl.start:i * D_POOL + cl.stop]
        return vp[(i - POOL_HIST) * nb:(i - POOL_HIST + 1) * nb, cl]

    xs, bm, cm, dt, acs = [], [], [], [], []
    for l in range(DEC_SEQ):
        acc = cb_ref[...] + cw_ref[0:1, :] * conv_slot(l)
        for k in range(1, SSM_CONV):
            acc = acc + cw_ref[k:k + 1, :] * conv_slot(l + k)
        act = _silu(acc)
        xs.append(act[:, :D_SSM])
        bm.append(act[:, D_SSM:D_SSM + D_BC])
        cm.append(act[:, D_SSM + D_BC:])
        b_ref[:, l * D_BC:(l + 1) * D_BC] = bm[l]
        c_ref[:, l * D_BC:(l + 1) * D_BC] = cm[l]
        dt.append(_softplus(dtr[l * nb:(l + 1) * nb] + dtb_ref[...]))
        da = dt[l] * a_ref[...]
        acs.append(da if l == 0 else acs[l - 1] + da)
        for gi, w in enumerate(POOL_WINDOWS):
            gl = slice(gi * POOL_GROUP_DIM, (gi + 1) * POOL_GROUP_DIM)
            s = pool_slot(POOL_HIST + l, gl)
            for k in range(1, w):
                s = s + pool_slot(POOL_HIST + l - k, gl)
            cnt = float(min(PAST_LEN + l + 1, w))
            pooled_scr[l * nb:(l + 1) * nb, gl] = (s / cnt - pool_slot(POOL_HIST + l, gl)).astype(bf16)
    for i in range(SSM_CONV - 1):
        conv_ref[:, i * D_XBC:(i + 1) * D_XBC] = conv_slot(DEC_SEQ + i)
    for i in range(POOL_HIST):
        pool_ref[:, i * D_POOL:(i + 1) * D_POOL] = pool_slot(DEC_SEQ + i, slice(0, D_POOL))
    for gi in range(len(POOL_WINDOWS)):
        gl = slice(gi * POOL_GROUP_DIM, (gi + 1) * POOL_GROUP_DIM)
        pg = (_dot(pooled_scr[:, gl], wpool_ref[gi]) * pscale_ref[:, gl]).astype(bf16)
        for l in range(DEC_SEQ):
            pout_ref[:, l * D_POOL + gi * POOL_GROUP_DIM:l * D_POOL + (gi + 1) * POOL_GROUP_DIM] = pg[l * nb:(l + 1) * nb]

    xdt = [xs[l] * _expand_heads(dt[l], lo) for l in range(DEC_SEQ)]
    gw = D_SSM // SSM_GROUPS
    for l in range(DEC_SEQ):
        y = xs[l] * dexp_ref[...]
        for s in range(l + 1):
            decay = _expand_heads(jnp.exp(acs[l] - acs[s]), lo)
            cbs = [jnp.sum(cm[l][:, g * SSM_STATE:(g + 1) * SSM_STATE] * bm[s][:, g * SSM_STATE:(g + 1) * SSM_STATE],
                           axis=-1, keepdims=True) for g in range(SSM_GROUPS)]
            coef = jnp.concatenate([decay[:, g * gw:(g + 1) * gw] * cbs[g] for g in range(SSM_GROUPS)], axis=1)
            y = y + coef * xdt[s]
        ypart_ref[:, l * D_SSM:(l + 1) * D_SSM] = y
        eacs_ref[:, l * D_SSM:(l + 1) * D_SSM] = _expand_heads(jnp.exp(acs[l]), lo)
        xd_ref[:, l * D_SSM:(l + 1) * D_SSM] = xdt[l] * _expand_heads(jnp.exp(acs[DEC_SEQ - 1] - acs[l]), lo)
    dec_ref[...] = jnp.exp(acs[DEC_SEQ - 1])


def _smix_in(x, sconv, spool, w):
    ns = x.shape[0]
    nb = S_SEQ_BLOCK
    rows = lambda width: pl.BlockSpec((nb, width), lambda i: (i, 0))
    outs = [(DEC_SEQ * D_SSM, f32), (DEC_SEQ * D_SSM, f32), (DEC_SEQ * D_SSM, f32), (DEC_SEQ * D_SSM, f32),
            (DEC_SEQ * D_BC, f32), (DEC_SEQ * D_BC, f32), (LANES, f32), (DEC_SEQ * D_POOL, bf16),
            ((SSM_CONV - 1) * D_XBC, f32), (POOL_HIST * D_POOL, f32)]
    return pl.pallas_call(
        _smix_in_kernel,
        grid=(ns // nb,),
        in_specs=[rows(DEC_SEQ * D_MODEL), rows((SSM_CONV - 1) * D_XBC), rows(POOL_HIST * D_POOL),
                  _const_spec((1, D_MODEL)), _const_spec((D_MODEL, D_PROJ)), _const_spec((SSM_CONV, D_XBC)),
                  _const_spec((1, D_XBC)), _const_spec((1, LANES)), _const_spec((1, LANES)), _const_spec((1, D_SSM)),
                  _const_spec((len(POOL_WINDOWS), POOL_GROUP_DIM, POOL_GROUP_DIM)), _const_spec((1, D_POOL))],
        out_specs=[rows(wd) for wd, _ in outs],
        out_shape=[jax.ShapeDtypeStruct((ns, wd), dt) for wd, dt in outs],
        scratch_shapes=[pltpu.VMEM((DEC_SEQ * nb, D_MODEL), bf16), pltpu.VMEM((DEC_SEQ * nb, D_POOL), bf16)],
        compiler_params=pltpu.CompilerParams(dimension_semantics=("arbitrary",), vmem_limit_bytes=VMEM_LIMIT_BYTES),
        name="smix_in",
    )(x, sconv, spool, w["nmix"], w["win"], w["cw"], w["cb"], w["dtb"], w["a_row"], w["dexp"], w["wpool"],
      w["pscale"])


def _sssd_kernel(dec_ref, c_ref, b_ref, xd_ref, st_ref, yoff_ref, stn_ref):
    blk = pl.program_id(0)
    gw = D_SSM // SSM_GROUPS
    row_seq = lax.broadcasted_iota(jnp.int32, (ROW_GROUP, gw), 0) // DEC_SEQ
    hpg = SSM_HEADS // SSM_GROUPS
    for m in range(S_SSD_BLOCK * DEC_SEQ // ROW_GROUP):
        rows = pl.ds(m * ROW_GROUP, ROW_GROUP)
        for g in range(SSM_GROUPS):
            gl = slice(g * gw, (g + 1) * gw)
            c16 = c_ref[rows, g * SSM_STATE:(g + 1) * SSM_STATE].astype(bf16)
            b16 = b_ref[rows, g * SSM_STATE:(g + 1) * SSM_STATE].astype(bf16)
            xd16 = xd_ref[rows, gl]
            yo = jnp.zeros((ROW_GROUP, gw), f32)
            for jj in range(ROW_GROUP // DEC_SEQ):
                j = m * (ROW_GROUP // DEC_SEQ) + jj
                mine = row_seq == jj
                h0 = st_ref[j, gl, :]
                yo = jnp.where(mine, _dot_nt(c16, h0.astype(bf16)), yo)
                upd = _dot_tn(jnp.where(mine, xd16, 0.0).astype(bf16), b16)
                for hh in range(hpg):
                    hr = slice(hh * SSM_HEAD_DIM, (hh + 1) * SSM_HEAD_DIM)
                    d = dec_ref[(blk * S_SSD_BLOCK + j) * SSM_HEADS + g * hpg + hh]
                    stn_ref[j, g * gw + hh * SSM_HEAD_DIM:g * gw + (hh + 1) * SSM_HEAD_DIM, :] = h0[hr] * d + upd[hr]
            yoff_ref[rows, gl] = yo


def _sssd(dec_flat, cmat, bmat, xd, state):
    ns = state.shape[0]
    nb = S_SSD_BLOCK
    rows = lambda width: pl.BlockSpec((nb * DEC_SEQ, width), lambda i: (i, 0))
    stblk = pl.BlockSpec((nb, D_SSM, SSM_STATE), lambda i: (i, 0, 0))
    return pl.pallas_call(
        _sssd_kernel,
        grid=(ns // nb,),
        in_specs=[pl.BlockSpec(memory_space=pltpu.SMEM), rows(D_BC), rows(D_BC), rows(D_SSM), stblk],
        out_specs=[rows(D_SSM), stblk],
        out_shape=[jax.ShapeDtypeStruct((ns * DEC_SEQ, D_SSM), f32), jax.ShapeDtypeStruct(state.shape, f32)],
        compiler_params=pltpu.CompilerParams(dimension_semantics=("arbitrary",), vmem_limit_bytes=VMEM_LIMIT_BYTES),
        name="sssd",
    )(dec_flat, cmat, bmat, xd, state)


def _smix_out_kernel(x_ref, ypart_ref, yoff_ref, eacs_ref, z_ref, pout_ref, snorm_ref, wout_ref, nmem_ref, wq_ref,
                     x1_ref, q_ref, cat_scr):
    y = ypart_ref[...] + yoff_ref[...] * eacs_ref[...]
    t = y * _silu(z_ref[...])
    gw = D_SSM // SSM_GROUPS
    for g in range(SSM_GROUPS):
        gl = slice(g * gw, (g + 1) * gw)
        tg = t[:, gl]
        ms = jnp.mean(tg * tg, axis=-1, keepdims=True)
        cat_scr[:, gl] = (tg * lax.rsqrt(ms + EPS) * snorm_ref[:, gl]).astype(bf16)
    cat_scr[:, D_SSM:] = pout_ref[...]
    x1 = x_ref[...] + _dot(cat_scr[...], wout_ref[...])
    x1_ref[...] = x1
    h = _rmsnorm(x1, nmem_ref[...]).astype(bf16)
    q_ref[...] = (_dot(h, wq_ref[...]) * (MEM_HEAD_DIM ** -0.5)).astype(bf16)


def _smix_out(x, ypart, yoff, eacs, z, pout, w):
    n = x.shape[0]
    rb = 128
    rows = lambda width: pl.BlockSpec((rb, width), lambda i: (i, 0))
    return pl.pallas_call(
        _smix_out_kernel,
        grid=(n // rb,),
        in_specs=[rows(D_MODEL), rows(D_SSM), rows(D_SSM), rows(D_SSM), rows(D_SSM), rows(D_POOL),
                  _const_spec((1, D_SSM)), _const_spec((D_SSM + D_POOL, D_MODEL)), _const_spec((1, D_MODEL)),
                  _const_spec((D_MODEL, D_MODEL))],
        out_specs=[rows(D_MODEL), rows(D_MODEL)],
        out_shape=[jax.ShapeDtypeStruct((n, D_MODEL), f32), jax.ShapeDtypeStruct((n, D_MODEL), bf16)],
        scratch_shapes=[pltpu.VMEM((rb, D_SSM + D_POOL), bf16)],
        compiler_params=pltpu.CompilerParams(dimension_semantics=("arbitrary",), vmem_limit_bytes=VMEM_LIMIT_BYTES),
        name="smix_out",
    )(x, ypart, yoff, eacs, z, pout, w["snorm"], w["wout"], w["nmem"], w["wq"])


def _sattn_kernel(q_ref, k_ref, v_ref, o_ref):
    row_seq = lax.broadcasted_iota(jnp.int32, (ROW_GROUP, MEM_HEAD_DIM), 0) // DEC_SEQ
    for hd in range(MEM_HEADS):
        hl = slice(hd * MEM_HEAD_DIM, (hd + 1) * MEM_HEAD_DIM)
        qh = q_ref[:, hl]
        out = jnp.zeros((ROW_GROUP, MEM_HEAD_DIM), f32)
        for jj in range(S_ATT_BLOCK):
            s = _dot_nt(qh, k_ref[jj, :, hl].astype(bf16))
            p = jnp.exp(s - jnp.max(s, axis=-1, keepdims=True))
            p = (p / jnp.sum(p, axis=-1, keepdims=True)).astype(bf16)
            out = jnp.where(row_seq == jj, _dot(p, v_ref[jj, :, hl].astype(bf16)), out)
        o_ref[:, hl] = out.astype(bf16)


def _sattn(q, mem_k, mem_v):
    ns = mem_k.shape[0]
    nb = S_ATT_BLOCK
    qblk = pl.BlockSpec((nb * DEC_SEQ, D_MODEL), lambda i: (i, 0))
    kvblk = pl.BlockSpec((nb, N_MEM, D_MODEL), lambda i: (i, 0, 0))
    return pl.pallas_call(
        _sattn_kernel,
        grid=(ns // nb,),
        in_specs=[qblk, kvblk, kvblk],
        out_specs=qblk,
        out_shape=jax.ShapeDtypeStruct((ns * DEC_SEQ, D_MODEL), bf16),
        compiler_params=pltpu.CompilerParams(dimension_semantics=("arbitrary",), vmem_limit_bytes=VMEM_LIMIT_BYTES),
        name="sattn",
    )(q, mem_k, mem_v)


def _sffn_kernel(x1_ref, ao_ref, sffn_ref, wo_ref, g_ref, wup_ref, cw_ref, cb_ref, wdn_ref, gfin_ref,
                 y_ref, st_ref, h_scr, acc_scr):
    nb = x1_ref.shape[0]
    x2 = _time_major(x1_ref, D_MODEL) + _dot(_time_major(ao_ref, D_MODEL), wo_ref[...])
    h_scr[...] = _rmsnorm(x2, g_ref[...]).astype(bf16)
    acc_scr[...] = x2

    def conv_chunk(c):
        u = _dot(h_scr[...], wup_ref[c])
        slots = [sffn_ref[:, i * 2 * D_FF + c * FF_CHUNK:i * 2 * D_FF + (c + 1) * FF_CHUNK]
                 for i in range(FFN_CONV - 1)]
        slots += [u[l * nb:(l + 1) * nb] for l in range(DEC_SEQ)]
        for i in range(FFN_CONV - 1):
            st_ref[:, i * 2 * D_FF + c * FF_CHUNK:i * 2 * D_FF + (c + 1) * FF_CHUNK] = slots[DEC_SEQ + i]
        w = cw_ref[c]
        outs = []
        for l in range(DEC_SEQ):
            acc = cb_ref[c] + w[0:1] * slots[l]
            for k in range(1, FFN_CONV):
                acc = acc + w[k:k + 1] * slots[l + k]
            outs.append(acc)
        return jnp.concatenate(outs, axis=0)

    for j in range(N_FF_CHUNKS):
        act = (_silu(conv_chunk(j)) * conv_chunk(j + N_FF_CHUNKS)).astype(bf16)
        acc_scr[...] += _dot(act, wdn_ref[j])
    y = _rmsnorm(acc_scr[...], gfin_ref[...])
    for l in range(DEC_SEQ):
        y_ref[:, l * D_MODEL:(l + 1) * D_MODEL] = y[l * nb:(l + 1) * nb]


def _sffn(x1, ao, sffn, w):
    ns = x1.shape[0]
    nb = S_SEQ_BLOCK
    nc = 2 * N_FF_CHUNKS
    rows = lambda width: pl.BlockSpec((nb, width), lambda i: (i, 0))
    return pl.pallas_call(
        _sffn_kernel,
        grid=(ns // nb,),
        in_specs=[rows(DEC_SEQ * D_MODEL), rows(DEC_SEQ * D_MODEL), rows((FFN_CONV - 1) * 2 * D_FF),
                  _const_spec((D_MODEL, D_MODEL)), _const_spec((1, D_MODEL)), _const_spec((nc, D_MODEL, FF_CHUNK)),
                  _const_spec((nc, FFN_CONV, FF_CHUNK)), _const_spec((nc, 1, FF_CHUNK)),
                  _const_spec((N_FF_CHUNKS, FF_CHUNK, D_MODEL)), _const_spec((1, D_MODEL))],
        out_specs=[rows(DEC_SEQ * D_MODEL), rows((FFN_CONV - 1) * 2 * D_FF)],
        out_shape=[jax.ShapeDtypeStruct((ns, DEC_SEQ * D_MODEL), f32),
                   jax.ShapeDtypeStruct((ns, (FFN_CONV - 1) * 2 * D_FF), f32)],
        scratch_shapes=[pltpu.VMEM((DEC_SEQ * nb, D_MODEL), bf16), pltpu.VMEM((DEC_SEQ * nb, D_MODEL), f32)],
        compiler_params=pltpu.CompilerParams(dimension_semantics=("arbitrary",), vmem_limit_bytes=VMEM_LIMIT_BYTES),
        name="sffn",
    )(x1, ao, sffn, w["wo"], w["nffn"], w["wup"], w["fcw"], w["fcb"], w["wdn"], w["gfin"])


def _sample_path(x_sample, state_ssm, state_ssm_conv, state_pool, state_ffn_conv, cache_mem_k, cache_mem_v, w):
    ns = x_sample.shape[0]
    x = x_sample.reshape(ns, DEC_SEQ * D_MODEL)
    z, ypart, eacs, xd, cmat, bmat, dec, pout, conv_new, pool_new = _smix_in(
        x, state_ssm_conv[0].reshape(ns, -1), state_pool[0].reshape(ns, -1), w)
    tok = lambda a, width: a.reshape(ns * DEC_SEQ, width)
    yoff, ssm_new = _sssd(dec[:, :SSM_HEADS].reshape(-1), tok(cmat, D_BC), tok(bmat, D_BC), tok(xd, D_SSM),
                          state_ssm[0].reshape(ns, D_SSM, SSM_STATE))
    x1, q = _smix_out(tok(x, D_MODEL), tok(ypart, D_SSM), yoff, tok(eacs, D_SSM), tok(z, D_SSM), tok(pout, D_POOL), w)
    ao = _sattn(q, cache_mem_k[0].reshape(ns, N_MEM, D_MODEL), cache_mem_v[0].reshape(ns, N_MEM, D_MODEL))
    y, ffn_new = _sffn(x1.reshape(ns, -1), ao.reshape(ns, -1), state_ffn_conv[0].reshape(ns, -1), w)
    return (y.reshape(ns, DEC_SEQ, D_MODEL), ssm_new.reshape(1, ns, SSM_HEADS, SSM_HEAD_DIM, SSM_STATE),
            conv_new.reshape(1, ns, SSM_CONV - 1, D_XBC), pool_new.reshape(1, ns, POOL_HIST, D_POOL),
            ffn_new.reshape(1, ns, FFN_CONV - 1, 2 * D_FF))


def kernel(x_prompt, x_sample, mem_prompt, state_ssm, state_ssm_conv, state_pool, state_ffn_conv, cache_mem_k, cache_mem_v, norm_mix, w_in, ssm_conv_w, ssm_conv_b, ssm_dt_bias, ssm_a_log, ssm_d, ssm_norm, w_pool, pool_scale, w_out, norm_mem, norm_memkv, w_mq, w_mk, w_mv, w_mo, norm_ffn, w_up, ffn_conv_w, ffn_conv_b, w_down, final_norm):
    params = dict(norm_mix=norm_mix, w_in=w_in, ssm_conv_w=ssm_conv_w, ssm_conv_b=ssm_conv_b, ssm_dt_bias=ssm_dt_bias,
                  ssm_a_log=ssm_a_log, ssm_d=ssm_d, ssm_norm=ssm_norm, w_pool=w_pool, pool_scale=pool_scale,
                  w_out=w_out, norm_mem=norm_mem, norm_memkv=norm_memkv, w_mq=w_mq, w_mk=w_mk, w_mv=w_mv, w_mo=w_mo,
                  norm_ffn=norm_ffn, w_up=w_up, ffn_conv_w=ffn_conv_w, ffn_conv_b=ffn_conv_b, w_down=w_down,
                  final_norm=final_norm)
    w = _prep_weights(params)
    yp, ssm_p, conv_p, pool_p, ffn_p, mk_p, mv_p = _prompt_path(x_prompt, mem_prompt, w, PROMPT_TILE)
    ys, ssm_s, conv_s, pool_s, ffn_s = _sample_path(x_sample, state_ssm, state_ssm_conv, state_pool, state_ffn_conv,
                                                    cache_mem_k, cache_mem_v, w)
    return yp, ys, ssm_p, ssm_s, conv_p, conv_s, pool_p, pool_s, ffn_p, ffn_s, mk_p, mv_p
```

```python
import functools

import jax
import jax.numpy as jnp
from jax import lax
from jax.experimental import pallas as pl
from jax.experimental.pallas import tpu as pltpu

f32 = jnp.float32
bf16 = jnp.bfloat16

D_MODEL = 1024
SSM_HEADS = 16
SSM_HEAD_DIM = 64
SSM_STATE = 128
SSM_GROUPS = 2
SSM_CHUNK = 128
D_SSM = 1024
D_BC = SSM_GROUPS * SSM_STATE
D_XBC = D_SSM + 2 * D_BC
SSM_CONV = 4
D_POOL = 1024
POOL_WINDOWS = (2, 4, 8, 16)
POOL_GROUP_DIM = 256
POOL_HIST = 15
N_MEM = 256
MEM_HEADS = 4
MEM_HEAD_DIM = 256
D_FF = 2816
FFN_CONV = 3
EPS = 1e-6
PAST_LEN = 16384

LANES = 128
SUBLANES = 8
MXU_DIM = 256
VMEM_LIMIT_BYTES = 56 * 1024 * 1024

COL_Z = 0
COL_XBC = D_SSM
COL_VP = COL_XBC + D_XBC
COL_DT = COL_VP + D_POOL
D_PROJ = COL_DT + LANES

CONV_HIST_ROWS = SUBLANES
POOL_HIST_ROWS = 2 * SUBLANES
FF_CHUNK = MXU_DIM
N_FF_CHUNKS = D_FF // FF_CHUNK


def _silu(v):
    return v * (1.0 / (1.0 + jnp.exp(-v)))


def _softplus(v):
    return jnp.maximum(v, 0.0) + jnp.log1p(jnp.exp(-jnp.abs(v)))


def _rmsnorm(x, g):
    ms = jnp.mean(x * x, axis=-1, keepdims=True)
    return x * lax.rsqrt(ms + EPS) * g


def _dot(a, b):
    return jnp.dot(a, b, preferred_element_type=f32)


def _dot_nt(a, b):
    return lax.dot_general(a, b, (((1,), (1,)), ((), ())), preferred_element_type=f32)


def _dot_tn(a, b):
    return lax.dot_general(a, b, (((0,), (0,)), ((), ())), preferred_element_type=f32)


def _split3(v):
    p1 = v.astype(bf16)
    r1 = v - p1.astype(f32)
    p2 = r1.astype(bf16)
    r2 = r1 - p2.astype(f32)
    return p1, p2, r2.astype(bf16)


def _const_spec(shape):
    return pl.BlockSpec(shape, lambda *_: (0,) * len(shape))


def _memkv_kernel(mem_ref, g_ref, wk_ref, wv_ref, k_ref, v_ref):
    h = _rmsnorm(mem_ref[0], g_ref[...]).astype(bf16)
    k_ref[0] = _dot(h, wk_ref[...])
    v_ref[0] = _dot(h, wv_ref[...])


def _memkv(mem, g, wk, wv):
    b = mem.shape[0]
    blk = pl.BlockSpec((1, N_MEM, D_MODEL), lambda i: (i, 0, 0))
    return pl.pallas_call(
        _memkv_kernel,
        grid=(b,),
        in_specs=[blk, _const_spec((1, D_MODEL)), _const_spec((D_MODEL, D_MODEL)), _const_spec((D_MODEL, D_MODEL))],
        out_specs=[blk, blk],
        out_shape=[jax.ShapeDtypeStruct((b, N_MEM, D_MODEL), f32)] * 2,
        compiler_params=pltpu.CompilerParams(dimension_semantics=("arbitrary",), vmem_limit_bytes=VMEM_LIMIT_BYTES),
        name="memkv",
    )(mem, g, wk, wv)


def _ssd_chunk(r0, dt_scr, xs_scr, b_scr, c_scr, y_scr, xd_scr, hst_scr, dtb_ref, a_ref, dexp_ref):
    q = SSM_CHUNK
    rows = pl.ds(r0, q)
    row_i = lax.broadcasted_iota(jnp.int32, (q, q), 0)
    col_i = lax.broadcasted_iota(jnp.int32, (q, q), 1)
    causal = col_i <= row_i
    lo = col_i < SSM_HEAD_DIM
    tril = jnp.where(causal, 1.0, 0.0).astype(bf16)

    dt = _softplus(dt_scr[rows, :] + dtb_ref[...])
    da = dt * a_ref[...]
    p1, p2, p3 = _split3(da)
    acs = _dot(tril, p1) + _dot(tril, p2) + _dot(tril, p3)
    acs_t = acs.T

    for g in range(SSM_GROUPS):
        bg = b_scr[rows, g * SSM_STATE:(g + 1) * SSM_STATE]
        cg = c_scr[rows, g * SSM_STATE:(g + 1) * SSM_STATE]
        bg_b = bg.astype(bf16)
        cb = jnp.where(causal, _dot_nt(cg.astype(bf16), bg_b), 0.0)
        cdec_rows = []
        pairs_per_group = SSM_HEADS // SSM_GROUPS // 2
        for pp in range(pairs_per_group):
            pr = g * pairs_per_group + pp
            lanes = slice(pr * LANES, (pr + 1) * LANES)
            lhs, dtb, dend, cdec = [], [], [], []
            for hh in (2 * pr, 2 * pr + 1):
                colb = jnp.broadcast_to(acs[:, hh:hh + 1], (q, q))
                seg = jnp.where(causal, colb - acs_t[hh:hh + 1, :], 0.0)
                lhs.append((jnp.exp(seg) * cb).astype(bf16))
                lhs.append((cg * jnp.exp(colb)).astype(bf16))
                last = colb[q - 1:q, :]
                dend.append(jnp.exp(last - colb))
                cdec.append(jnp.exp(last))
                dtb.append(jnp.broadcast_to(dt[:, hh:hh + 1], (q, q)))
            xs_pair = xs_scr[rows, lanes]
            xdt = xs_pair * jnp.where(lo, dtb[0], dtb[1])
            xd_scr[:, lanes] = (xdt * jnp.where(lo, dend[0], dend[1])).astype(bf16)
            hst_pair = hst_scr[:, lanes]
            rhs = jnp.concatenate([
                jnp.where(lo, xdt, 0.0).astype(bf16), jnp.where(lo, hst_pair, 0.0).astype(bf16),
                jnp.where(lo, 0.0, xdt).astype(bf16), jnp.where(lo, 0.0, hst_pair).astype(bf16)], axis=0)
            y_pair = _dot(jnp.concatenate(lhs, axis=1), rhs)
            y_scr[rows, lanes] = y_pair + xs_pair * dexp_ref[:, lanes]
            cdec_rows.append(jnp.where(lo[:1], cdec[0], cdec[1]))
        gl = slice(g * (D_SSM // SSM_GROUPS), (g + 1) * (D_SSM // SSM_GROUPS))
        upd = _dot_tn(bg_b, xd_scr[:, gl])
        hst_scr[:, gl] = hst_scr[:, gl] * jnp.concatenate(cdec_rows, axis=1) + upd


def _mixer_kernel(x_ref, nmix_ref, win_ref, cw_ref, cb_ref, dtb_ref, a_ref, dexp_ref, snorm_ref, wpool_ref,
                  pscale_ref, wout_ref,
                  x1_ref, ssm_ref, conv_ref, pool_ref,
                  h_scr, z_scr, xbc_scr, vp_scr, dt_scr, xs_scr, b_scr, c_scr, y_scr, xd_scr, hst_scr,
                  pooled_scr, cat_scr, *, tile):
    t = pl.program_id(1)
    last_t = pl.num_programs(1) - 1
    ch = CONV_HIST_ROWS
    ph = POOL_HIST_ROWS

    @pl.when(t == 0)
    def _():
        xbc_scr[:, 0:ch, :] = jnp.zeros((D_XBC // LANES, ch, LANES), f32)
        vp_scr[:, 0:ph, :] = jnp.zeros((D_POOL // LANES, ph, LANES), f32)
        hst_scr[...] = jnp.zeros_like(hst_scr)

    x = x_ref[0]
    h_scr[...] = _rmsnorm(x, nmix_ref[...]).astype(bf16)
    z_scr[...] = _dot(h_scr[...], win_ref[:, COL_Z:COL_XBC])
    xbc = _dot(h_scr[...], win_ref[:, COL_XBC:COL_VP])
    for j in range(D_XBC // LANES):
        xbc_scr[j, ch:ch + tile, :] = xbc[:, j * LANES:(j + 1) * LANES]
    vp = _dot(h_scr[...], win_ref[:, COL_VP:COL_DT])
    for j in range(D_POOL // LANES):
        vp_scr[j, ph:ph + tile, :] = vp[:, j * LANES:(j + 1) * LANES]
    dt_scr[...] = _dot(h_scr[...], win_ref[:, COL_DT:D_PROJ])

    for j in range(D_XBC // LANES):
        cl = slice(j * LANES, (j + 1) * LANES)
        acc = cb_ref[:, cl] + cw_ref[0:1, cl] * xbc_scr[j, pl.ds(ch - 3, tile), :]
        for k in range(1, SSM_CONV):
            acc = acc + cw_ref[k:k + 1, cl] * xbc_scr[j, pl.ds(ch - 3 + k, tile), :]
        act = _silu(acc)
        if j < D_SSM // LANES:
            xs_scr[:, cl] = act
        elif j < (D_SSM + D_BC) // LANES:
            b_scr[:, j * LANES - D_SSM:(j + 1) * LANES - D_SSM] = act
        else:
            c_scr[:, j * LANES - D_SSM - D_BC:(j + 1) * LANES - D_SSM - D_BC] = act

        @pl.when(t == last_t)
        def _():
            conv_ref[0, :, cl] = xbc_scr[j, pl.ds(ch + tile - (SSM_CONV - 1), SSM_CONV - 1), :]

        xbc_scr[j, 0:ch, :] = xbc_scr[j, tile:tile + ch, :]

    pos1 = lax.broadcasted_iota(jnp.int32, (tile, LANES), 0) + (t * tile + 1)
    for j in range(D_POOL // LANES):
        cl = slice(j * LANES, (j + 1) * LANES)
        w = POOL_WINDOWS[j * LANES // POOL_GROUP_DIM]
        cur = vp_scr[j, ph:ph + tile, :]
        s = cur
        for k in range(1, w):
            s = s + vp_scr[j, pl.ds(ph - k, tile), :]
        cnt = jnp.minimum(pos1, w).astype(f32)
        pooled_scr[:, cl] = (s / cnt - cur).astype(bf16)

        @pl.when(t == last_t)
        def _():
            pool_ref[0, :, cl] = vp_scr[j, pl.ds(ph + tile - POOL_HIST, POOL_HIST), :]

        vp_scr[j, 0:ph, :] = vp_scr[j, tile:tile + ph, :]
    for gi in range(len(POOL_WINDOWS)):
        gl = slice(gi * POOL_GROUP_DIM, (gi + 1) * POOL_GROUP_DIM)
        pg = _dot(pooled_scr[:, gl], wpool_ref[gi]) * pscale_ref[:, gl]
        cat_scr[:, D_SSM + gi * POOL_GROUP_DIM:D_SSM + (gi + 1) * POOL_GROUP_DIM] = pg.astype(bf16)

    for c in range(tile // SSM_CHUNK):
        _ssd_chunk(c * SSM_CHUNK, dt_scr, xs_scr, b_scr, c_scr, y_scr, xd_scr, hst_scr, dtb_ref, a_ref, dexp_ref)

    gw = D_SSM // SSM_GROUPS
    for g in range(SSM_GROUPS):
        gl = slice(g * gw, (g + 1) * gw)
        tg = y_scr[:, gl] * _silu(z_scr[:, gl])
        ms = jnp.mean(tg * tg, axis=-1, keepdims=True)
        cat_scr[:, gl] = (tg * lax.rsqrt(ms + EPS) * snorm_ref[:, gl]).astype(bf16)

    x1_ref[0] = x + _dot(cat_scr[...], wout_ref[...])

    @pl.when(t == last_t)
    def _():
        for pr in range(D_SSM // LANES):
            ssm_ref[0, pr * LANES:(pr + 1) * LANES, :] = hst_scr[:, pr * LANES:(pr + 1) * LANES].T


def _mixer_prompt(x, nmix, win, cw, cb, dtb, a_row, dexp, snorm, wpool, pscale, wout, tile):
    b, seq, _ = x.shape
    nt = seq // tile
    xblk = pl.BlockSpec((1, tile, D_MODEL), lambda i, j: (i, j, 0))
    scratch = [
        pltpu.VMEM((tile, D_MODEL), bf16),
        pltpu.VMEM((tile, D_SSM), f32),
        pltpu.VMEM((D_XBC // LANES, CONV_HIST_ROWS + tile, LANES), f32),
        pltpu.VMEM((D_POOL // LANES, POOL_HIST_ROWS + tile, LANES), f32),
        pltpu.VMEM((tile, LANES), f32),
        pltpu.VMEM((tile, D_SSM), f32),
        pltpu.VMEM((tile, D_BC), f32),
        pltpu.VMEM((tile, D_BC), f32),
        pltpu.VMEM((tile, D_SSM), f32),
        pltpu.VMEM((SSM_CHUNK, D_SSM), bf16),
        pltpu.VMEM((SSM_STATE, D_SSM), f32),
        pltpu.VMEM((tile, D_POOL), bf16),
        pltpu.VMEM((tile, D_SSM + D_POOL), bf16),
    ]
    return pl.pallas_call(
        functools.partial(_mixer_kernel, tile=tile),
        grid=(b, nt),
        in_specs=[xblk, _const_spec((1, D_MODEL)), _const_spec((D_MODEL, D_PROJ)), _const_spec((SSM_CONV, D_XBC)),
                  _const_spec((1, D_XBC)), _const_spec((1, LANES)), _const_spec((1, LANES)), _const_spec((1, D_SSM)),
                  _const_spec((1, D_SSM)), _const_spec((len(POOL_WINDOWS), POOL_GROUP_DIM, POOL_GROUP_DIM)),
                  _const_spec((1, D_POOL)), _const_spec((D_SSM + D_POOL, D_MODEL))],
        out_specs=[xblk,
                   pl.BlockSpec((1, D_SSM, SSM_STATE), lambda i, j: (i, 0, 0)),
                   pl.BlockSpec((1, SSM_CONV - 1, D_XBC), lambda i, j: (i, 0, 0)),
                   pl.BlockSpec((1, POOL_HIST, D_POOL), lambda i, j: (i, 0, 0))],
        out_shape=[jax.ShapeDtypeStruct((b, seq, D_MODEL), f32),
                   jax.ShapeDtypeStruct((b, D_SSM, SSM_STATE), f32),
                   jax.ShapeDtypeStruct((b, SSM_CONV - 1, D_XBC), f32),
                   jax.ShapeDtypeStruct((b, POOL_HIST, D_POOL), f32)],
        scratch_shapes=scratch,
        compiler_params=pltpu.CompilerParams(dimension_semantics=("arbitrary", "arbitrary"),
                                             vmem_limit_bytes=VMEM_LIMIT_BYTES),
        name="mixer_prompt",
    )(x, nmix, win, cw, cb, dtb, a_row, dexp, snorm, wpool, pscale, wout)


def _attn_kernel(x_ref, g_ref, wq_ref, k_ref, v_ref, wo_ref, o_ref, q_scr, ao_scr):
    x = x_ref[0]
    h = _rmsnorm(x, g_ref[...]).astype(bf16)
    q_scr[...] = (_dot(h, wq_ref[...]) * (MEM_HEAD_DIM ** -0.5)).astype(bf16)
    for hd in range(MEM_HEADS):
        hl = slice(hd * MEM_HEAD_DIM, (hd + 1) * MEM_HEAD_DIM)
        s = _dot_nt(q_scr[:, hl], k_ref[0, :, hl].astype(bf16))
        p = jnp.exp(s - jnp.max(s, axis=-1, keepdims=True))
        p = (p / jnp.sum(p, axis=-1, keepdims=True)).astype(bf16)
        ao_scr[:, hl] = _dot(p, v_ref[0, :, hl].astype(bf16)).astype(bf16)
    o_ref[0] = x + _dot(ao_scr[...], wo_ref[...])


def _attn_prompt(x, g, wq, mem_k, mem_v, wo, tile):
    b, seq, _ = x.shape
    xblk = pl.BlockSpec((1, tile, D_MODEL), lambda i, j: (i, j, 0))
    kvblk = pl.BlockSpec((1, N_MEM, D_MODEL), lambda i, j: (i, 0, 0))
    return pl.pallas_call(
        _attn_kernel,
        grid=(b, seq // tile),
        in_specs=[xblk, _const_spec((1, D_MODEL)), _const_spec((D_MODEL, D_MODEL)), kvblk, kvblk,
                  _const_spec((D_MODEL, D_MODEL))],
        out_specs=xblk,
        out_shape=jax.ShapeDtypeStruct((b, seq, D_MODEL), f32),
        scratch_shapes=[pltpu.VMEM((tile, D_MODEL), bf16), pltpu.VMEM((tile, D_MODEL), bf16)],
        compiler_params=pltpu.CompilerParams(dimension_semantics=("arbitrary", "arbitrary"),
                                             vmem_limit_bytes=VMEM_LIMIT_BYTES),
        name="attn_prompt",
    )(x, g, wq, mem_k, mem_v, wo)


def _ff_block_col(blk):
    return (blk % 2) * D_FF + (blk // 2) * FF_CHUNK


def _ffn_kernel(x_ref, g_ref, wup_ref, cw_ref, cb_ref, wdn_ref, gfin_ref, y_ref, st_ref, h_scr, u_scr, act_scr, *, tile):
    t = pl.program_id(1)
    last_t = pl.num_programs(1) - 1
    ch = CONV_HIST_ROWS
    tpc = FF_CHUNK // LANES
    ntile = 2 * D_FF // LANES

    @pl.when(t == 0)
    def _():
        u_scr[:, 0:ch, :] = jnp.zeros((ntile, ch, LANES), f32)

    x = x_ref[0]
    h_scr[...] = _rmsnorm(x, g_ref[...]).astype(bf16)
    u = _dot(h_scr[...], wup_ref[...])
    for ti in range(ntile):
        u_scr[ti, ch:ch + tile, :] = u[:, ti * LANES:(ti + 1) * LANES]

    def conv(ti):
        cl = slice(ti * LANES, (ti + 1) * LANES)
        acc = cb_ref[:, cl] + cw_ref[0:1, cl] * u_scr[ti, pl.ds(ch - 2, tile), :]
        acc = acc + cw_ref[1:2, cl] * u_scr[ti, pl.ds(ch - 1, tile), :]
        return acc + cw_ref[2:3, cl] * u_scr[ti, ch:ch + tile, :]

    for j in range(N_FF_CHUNKS):
        for i in range(tpc):
            gate = conv(2 * j * tpc + i)
            val = conv((2 * j + 1) * tpc + i)
            act_scr[:, (j * tpc + i) * LANES:(j * tpc + i + 1) * LANES] = (_silu(gate) * val).astype(bf16)
    for ti in range(ntile):
        u_scr[ti, 0:ch, :] = u_scr[ti, tile:tile + ch, :]
    y_ref[0] = _rmsnorm(x + _dot(act_scr[...], wdn_ref[...]), gfin_ref[...])

    @pl.when(t == last_t)
    def _():
        for ti in range(ntile):
            nat = _ff_block_col(ti // tpc) + (ti % tpc) * LANES
            st_ref[0, :, nat:nat + LANES] = u_scr[ti, pl.ds(ch - (FFN_CONV - 1), FFN_CONV - 1), :]


def _ffn_prompt(x, g, wup, cw, cb, wdn, gfin, tile):
    b, seq, _ = x.shape
    xblk = pl.BlockSpec((1, tile, D_MODEL), lambda i, j: (i, j, 0))
    return pl.pallas_call(
        functools.partial(_ffn_kernel, tile=tile),
        grid=(b, seq // tile),
        in_specs=[xblk, _const_spec((1, D_MODEL)), _const_spec((D_MODEL, 2 * D_FF)),
                  _const_spec((FFN_CONV, 2 * D_FF)), _const_spec((1, 2 * D_FF)),
                  _const_spec((D_FF, D_MODEL)), _const_spec((1, D_MODEL))],
        out_specs=[xblk, pl.BlockSpec((1, FFN_CONV - 1, 2 * D_FF), lambda i, j: (i, 0, 0))],
        out_shape=[jax.ShapeDtypeStruct((b, seq, D_MODEL), f32),
                   jax.ShapeDtypeStruct((b, FFN_CONV - 1, 2 * D_FF), f32)],
        scratch_shapes=[pltpu.VMEM((tile, D_MODEL), bf16),
                        pltpu.VMEM((2 * D_FF // LANES, CONV_HIST_ROWS + tile, LANES), f32),
                        pltpu.VMEM((tile, D_FF), bf16)],
        compiler_params=pltpu.CompilerParams(dimension_semantics=("arbitrary", "arbitrary"),
                                             vmem_limit_bytes=VMEM_LIMIT_BYTES),
        name="ffn_prompt",
    )(x, g, wup, cw, cb, wdn, gfin)


def _ff_interleave(a):
    lead = a.shape[:-1]
    a = a.reshape(lead + (2, N_FF_CHUNKS, FF_CHUNK))
    return jnp.swapaxes(a, -3, -2).reshape(lead + (2 * D_FF,))


def _prep_weights(p):
    i = 0
    w_in = p["w_in"][i]
    dt_cols = jnp.pad(w_in[:, D_SSM + D_XBC:D_SSM + D_XBC + SSM_HEADS], ((0, 0), (0, LANES - SSM_HEADS)))
    win = jnp.concatenate([w_in[:, :D_SSM + D_XBC], w_in[:, D_SSM + D_XBC + SSM_HEADS:], dt_cols], axis=1).astype(bf16)
    pad_h = (0, LANES - SSM_HEADS)
    return dict(
        nmix=p["norm_mix"][i][None], win=win, cw=p["ssm_conv_w"][i], cb=p["ssm_conv_b"][i][None],
        dtb=jnp.pad(p["ssm_dt_bias"][i], pad_h)[None],
        a_row=jnp.pad(-jnp.exp(p["ssm_a_log"][i].astype(f32)), pad_h)[None],
        dexp=jnp.repeat(p["ssm_d"][i], SSM_HEAD_DIM)[None], snorm=p["ssm_norm"][i][None],
        wpool=p["w_pool"][i].astype(bf16), pscale=p["pool_scale"][i][None], wout=p["w_out"][i].astype(bf16),
        nmem=p["norm_mem"][i][None], nmemkv=p["norm_memkv"][i][None],
        wq=p["w_mq"][i].astype(bf16), wk=p["w_mk"][i].astype(bf16), wv=p["w_mv"][i].astype(bf16),
        wo=p["w_mo"][i].astype(bf16),
        nffn=p["norm_ffn"][i][None],
        wup=_ff_interleave(p["w_up"][i].astype(bf16)), fcw=_ff_interleave(p["ffn_conv_w"][i]),
        fcb=_ff_interleave(p["ffn_conv_b"][i][None]), wdn=p["w_down"][i].astype(bf16),
        gfin=p["final_norm"][None],
    )


def _prompt_path(x_prompt, mem_prompt, w, tile):
    b = x_prompt.shape[0]
    mem_k, mem_v = _memkv(mem_prompt, w["nmemkv"], w["wk"], w["wv"])
    x1, ssm, conv, pool = _mixer_prompt(x_prompt, w["nmix"], w["win"], w["cw"], w["cb"], w["dtb"], w["a_row"],
                                        w["dexp"], w["snorm"], w["wpool"], w["pscale"], w["wout"], tile)
    x2 = _attn_prompt(x1, w["nmem"], w["wq"], mem_k, mem_v, w["wo"], tile)
    y, ffn = _ffn_prompt(x2, w["nffn"], w["wup"], w["fcw"], w["fcb"], w["wdn"], w["gfin"], tile)
    return (y, ssm.reshape(1, b, SSM_HEADS, SSM_HEAD_DIM, SSM_STATE), conv[None], pool[None], ffn[None],
            mem_k.reshape(1, b, N_MEM, MEM_HEADS, MEM_HEAD_DIM), mem_v.reshape(1, b, N_MEM, MEM_HEADS, MEM_HEAD_DIM))


PROMPT_TILE = 256


DEC_SEQ = 4
S_SEQ_BLOCK = 32
S_SSD_BLOCK = 8
S_ATT_BLOCK = 4
ROW_GROUP = 16


def _expand_heads(v, lo):
    r = v.shape[0]
    tiles = []
    for pr in range(SSM_HEADS // 2):
        a = jnp.broadcast_to(v[:, 2 * pr:2 * pr + 1], (r, LANES))
        b = jnp.broadcast_to(v[:, 2 * pr + 1:2 * pr + 2], (r, LANES))
        tiles.append(jnp.where(lo, a, b))
    return jnp.concatenate(tiles, axis=1)


def _time_major(ref, width):
    return jnp.concatenate([ref[:, l * width:(l + 1) * width] for l in range(DEC_SEQ)], axis=0)


def _smix_in_kernel(x_ref, sconv_ref, spool_ref, nmix_ref, win_ref, cw_ref, cb_ref, dtb_ref, a_ref, dexp_ref,
                    wpool_ref, pscale_ref,
                    z_ref, ypart_ref, eacs_ref, xd_ref, c_ref, b_ref, dec_ref, pout_ref, conv_ref, pool_ref,
                    h_scr, pooled_scr):
    nb = x_ref.shape[0]
    lo = lax.broadcasted_iota(jnp.int32, (nb, LANES), 1) < SSM_HEAD_DIM
    h_scr[...] = _rmsnorm(_time_major(x_ref, D_MODEL), nmix_ref[...]).astype(bf16)
    z = _dot(h_scr[...], win_ref[:, COL_Z:COL_XBC])
    for l in range(DEC_SEQ):
        z_ref[:, l * D_SSM:(l + 1) * D_SSM] = z[l * nb:(l + 1) * nb]
    xbc = _dot(h_scr[...], win_ref[:, COL_XBC:COL_VP])
    vp = _dot(h_scr[...], win_ref[:, COL_VP:COL_DT])
    dtr = _dot(h_scr[...], win_ref[:, COL_DT:D_PROJ])

    def conv_slot(i):
        if i < SSM_CONV - 1:
            return sconv_ref[:, i * D_XBC:(i + 1) * D_XBC]
        return xbc[(i - SSM_CONV + 1) * nb:(i - SSM_CONV + 2) * nb]

    def pool_slot(i, cl):
        if i < POOL_HIST:
            return spool_ref[:, i * D_POOL + cl.start:i * D_POOL + cl.stop]
        return vp[(i - POOL_HIST) * nb:(i - POOL_HIST + 1) * nb, cl]

    xs, bm, cm, dt, acs = [], [], [], [], []
    for l in range(DEC_SEQ):
        acc = cb_ref[...] + cw_ref[0:1, :] * conv_slot(l)
        for k in range(1, SSM_CONV):
            acc = acc + cw_ref[k:k + 1, :] * conv_slot(l + k)
        act = _silu(acc)
        xs.append(act[:, :D_SSM])
        bm.append(act[:, D_SSM:D_SSM + D_BC])
        cm.append(act[:, D_SSM + D_BC:])
        b_ref[:, l * D_BC:(l + 1) * D_BC] = bm[l]
        c_ref[:, l * D_BC:(l + 1) * D_BC] = cm[l]
        dt.append(_softplus(dtr[l * nb:(l + 1) * nb] + dtb_ref[...]))
        da = dt[l] * a_ref[...]
        acs.append(da if l == 0 else acs[l - 1] + da)
        for gi, w in enumerate(POOL_WINDOWS):
            gl = slice(gi * POOL_GROUP_DIM, (gi + 1) * POOL_GROUP_DIM)
            s = pool_slot(POOL_HIST + l, gl)
            for k in range(1, w):
                s = s + pool_slot(POOL_HIST + l - k, gl)
            cnt = float(min(PAST_LEN + l + 1, w))
            pooled_scr[l * nb:(l + 1) * nb, gl] = (s / cnt - pool_slot(POOL_HIST + l, gl)).astype(bf16)
    for i in range(SSM_CONV - 1):
        conv_ref[:, i * D_XBC:(i + 1) * D_XBC] = conv_slot(DEC_SEQ + i)
    for i in range(POOL_HIST):
        pool_ref[:, i * D_POOL:(i + 1) * D_POOL] = pool_slot(DEC_SEQ + i, slice(0, D_POOL))
    for gi in range(len(POOL_WINDOWS)):
        gl = slice(gi * POOL_GROUP_DIM, (gi + 1) * POOL_GROUP_DIM)
        pg = (_dot(pooled_scr[:, gl], wpool_ref[gi]) * pscale_ref[:, gl]).astype(bf16)
        for l in range(DEC_SEQ):
            pout_ref[:, l * D_POOL + gi * POOL_GROUP_DIM:l * D_POOL + (gi + 1) * POOL_GROUP_DIM] = pg[l * nb:(l + 1) * nb]

    xdt = [xs[l] * _expand_heads(dt[l], lo) for l in range(DEC_SEQ)]
    gw = D_SSM // SSM_GROUPS
    for l in range(DEC_SEQ):
        y = xs[l] * dexp_ref[...]
        for s in range(l + 1):
            decay = _expand_heads(jnp.exp(acs[l] - acs[s]), lo)
            cbs = [jnp.sum(cm[l][:, g * SSM_STATE:(g + 1) * SSM_STATE] * bm[s][:, g * SSM_STATE:(g + 1) * SSM_STATE],
                           axis=-1, keepdims=True) for g in range(SSM_GROUPS)]
            coef = jnp.concatenate([decay[:, g * gw:(g + 1) * gw] * cbs[g] for g in range(SSM_GROUPS)], axis=1)
            y = y + coef * xdt[s]
        ypart_ref[:, l * D_SSM:(l + 1) * D_SSM] = y
        eacs_ref[:, l * D_SSM:(l + 1) * D_SSM] = _expand_heads(jnp.exp(acs[l]), lo)
        xd_ref[:, l * D_SSM:(l + 1) * D_SSM] = xdt[l] * _expand_heads(jnp.exp(acs[DEC_SEQ - 1] - acs[l]), lo)
    dec_ref[...] = jnp.exp(acs[DEC_SEQ - 1])


def _smix_in(x, sconv, spool, w):
    ns = x.shape[0]
    nb = S_SEQ_BLOCK
    rows = lambda width: pl.BlockSpec((nb, width), lambda i: (i, 0))
    outs = [(DEC_SEQ * D_SSM, f32), (DEC_SEQ * D_SSM, f32), (DEC_SEQ * D_SSM, f32), (DEC_SEQ * D_SSM, f32),
            (DEC_SEQ * D_BC, f32), (DEC_SEQ * D_BC, f32), (LANES, f32), (DEC_SEQ * D_POOL, bf16),
            ((SSM_CONV - 1) * D_XBC, f32), (POOL_HIST * D_POOL, f32)]
    return pl.pallas_call(
        _smix_in_kernel,
        grid=(ns // nb,),
        in_specs=[rows(DEC_SEQ * D_MODEL), rows((SSM_CONV - 1) * D_XBC), rows(POOL_HIST * D_POOL),
                  _const_spec((1, D_MODEL)), _const_spec((D_MODEL, D_PROJ)), _const_spec((SSM_CONV, D_XBC)),
                  _const_spec((1, D_XBC)), _const_spec((1, LANES)), _const_spec((1, LANES)), _const_spec((1, D_SSM)),
                  _const_spec((len(POOL_WINDOWS), POOL_GROUP_DIM, POOL_GROUP_DIM)), _const_spec((1, D_POOL))],
        out_specs=[rows(wd) for wd, _ in outs],
        out_shape=[jax.ShapeDtypeStruct((ns, wd), dt) for wd, dt in outs],
        scratch_shapes=[pltpu.VMEM((DEC_SEQ * nb, D_MODEL), bf16), pltpu.VMEM((DEC_SEQ * nb, D_POOL), bf16)],
        compiler_params=pltpu.CompilerParams(dimension_semantics=("arbitrary",), vmem_limit_bytes=VMEM_LIMIT_BYTES),
        name="smix_in",
    )(x, sconv, spool, w["nmix"], w["win"], w["cw"], w["cb"], w["dtb"], w["a_row"], w["dexp"], w["wpool"],
      w["pscale"])


def _sssd_kernel(dec_ref, c_ref, b_ref, xd_ref, st_ref, yoff_ref, stn_ref):
    blk = pl.program_id(0)
    gw = D_SSM // SSM_GROUPS
    row_seq = lax.broadcasted_iota(jnp.int32, (ROW_GROUP, gw), 0) // DEC_SEQ
    hpg = SSM_HEADS // SSM_GROUPS
    for m in range(S_SSD_BLOCK * DEC_SEQ // ROW_GROUP):
        rows = pl.ds(m * ROW_GROUP, ROW_GROUP)
        for g in range(SSM_GROUPS):
            gl = slice(g * gw, (g + 1) * gw)
            c16 = c_ref[rows, g * SSM_STATE:(g + 1) * SSM_STATE].astype(bf16)
            b16 = b_ref[rows, g * SSM_STATE:(g + 1) * SSM_STATE].astype(bf16)
            xd16 = xd_ref[rows, gl]
            yo = jnp.zeros((ROW_GROUP, gw), f32)
            for jj in range(ROW_GROUP // DEC_SEQ):
                j = m * (ROW_GROUP // DEC_SEQ) + jj
                mine = row_seq == jj
                h0 = st_ref[j, gl, :]
                yo = jnp.where(mine, _dot_nt(c16, h0.astype(bf16)), yo)
                upd = _dot_tn(jnp.where(mine, xd16, 0.0).astype(bf16), b16)
                for hh in range(hpg):
                    hr = slice(hh * SSM_HEAD_DIM, (hh + 1) * SSM_HEAD_DIM)
                    d = dec_ref[(blk * S_SSD_BLOCK + j) * SSM_HEADS + g * hpg + hh]
                    stn_ref[j, g * gw + hh * SSM_HEAD_DIM:g * gw + (hh + 1) * SSM_HEAD_DIM, :] = h0[hr] * d + upd[hr]
            yoff_ref[rows, gl] = yo


def _sssd(dec_flat, cmat, bmat, xd, state):
    ns = state.shape[0]
    nb = S_SSD_BLOCK
    rows = lambda width: pl.BlockSpec((nb * DEC_SEQ, width), lambda i: (i, 0))
    stblk = pl.BlockSpec((nb, D_SSM, SSM_STATE), lambda i: (i, 0, 0))
    return pl.pallas_call(
        _sssd_kernel,
        grid=(ns // nb,),
        in_specs=[pl.BlockSpec(memory_space=pltpu.SMEM), rows(D_BC), rows(D_BC), rows(D_SSM), stblk],
        out_specs=[rows(D_SSM), stblk],
        out_shape=[jax.ShapeDtypeStruct((ns * DEC_SEQ, D_SSM), f32), jax.ShapeDtypeStruct(state.shape, f32)],
        compiler_params=pltpu.CompilerParams(dimension_semantics=("arbitrary",), vmem_limit_bytes=VMEM_LIMIT_BYTES),
        name="sssd",
    )(dec_flat, cmat, bmat, xd, state)


def _smix_out_kernel(x_ref, ypart_ref, yoff_ref, eacs_ref, z_ref, pout_ref, snorm_ref, wout_ref, nmem_ref, wq_ref,
                     x1_ref, q_ref, cat_scr):
    y = ypart_ref[...] + yoff_ref[...] * eacs_ref[...]
    t = y * _silu(z_ref[...])
    gw = D_SSM // SSM_GROUPS
    for g in range(SSM_GROUPS):
        gl = slice(g * gw, (g + 1) * gw)
        tg = t[:, gl]
        ms = jnp.mean(tg * tg, axis=-1, keepdims=True)
        cat_scr[:, gl] = (tg * lax.rsqrt(ms + EPS) * snorm_ref[:, gl]).astype(bf16)
    cat_scr[:, D_SSM:] = pout_ref[...]
    x1 = x_ref[...] + _dot(cat_scr[...], wout_ref[...])
    x1_ref[...] = x1
    h = _rmsnorm(x1, nmem_ref[...]).astype(bf16)
    q_ref[...] = (_dot(h, wq_ref[...]) * (MEM_HEAD_DIM ** -0.5)).astype(bf16)


def _smix_out(x, ypart, yoff, eacs, z, pout, w):
    n = x.shape[0]
    rb = 128
    rows = lambda width: pl.BlockSpec((rb, width), lambda i: (i, 0))
    return pl.pallas_call(
        _smix_out_kernel,
        grid=(n // rb,),
        in_specs=[rows(D_MODEL), rows(D_SSM), rows(D_SSM), rows(D_SSM), rows(D_SSM), rows(D_POOL),
                  _const_spec((1, D_SSM)), _const_spec((D_SSM + D_POOL, D_MODEL)), _const_spec((1, D_MODEL)),
                  _const_spec((D_MODEL, D_MODEL))],
        out_specs=[rows(D_MODEL), rows(D_MODEL)],
        out_shape=[jax.ShapeDtypeStruct((n, D_MODEL), f32), jax.ShapeDtypeStruct((n, D_MODEL), bf16)],
        scratch_shapes=[pltpu.VMEM((rb, D_SSM + D_POOL), bf16)],
        compiler_params=pltpu.CompilerParams(dimension_semantics=("arbitrary",), vmem_limit_bytes=VMEM_LIMIT_BYTES),
        name="smix_out",
    )(x, ypart, yoff, eacs, z, pout, w["snorm"], w["wout"], w["nmem"], w["wq"])


KV_LANE_TILES = MEM_HEAD_DIM // LANES
KV_ROWS_PER_TOKEN = KV_LANE_TILES * MEM_HEADS


def _kv_rows_view(kv):
    ns = kv.shape[0]
    kv = kv.reshape(ns, N_MEM, MEM_HEADS, KV_LANE_TILES, LANES).transpose(0, 1, 3, 2, 4)
    return kv.reshape(ns, N_MEM * KV_ROWS_PER_TOKEN, LANES)


def _kv_head(ref, jj, hd):
    tiles = [ref[jj, pl.ds(dt * MEM_HEADS + hd, N_MEM, stride=KV_ROWS_PER_TOKEN), :] for dt in range(KV_LANE_TILES)]
    return jnp.concatenate(tiles, axis=1).astype(bf16)


def _sattn_kernel(q_ref, k_ref, v_ref, o_ref):
    row_seq = lax.broadcasted_iota(jnp.int32, (ROW_GROUP, MEM_HEAD_DIM), 0) // DEC_SEQ
    for hd in range(MEM_HEADS):
        hl = slice(hd * MEM_HEAD_DIM, (hd + 1) * MEM_HEAD_DIM)
        qh = q_ref[:, hl]
        out = jnp.zeros((ROW_GROUP, MEM_HEAD_DIM), f32)
        for jj in range(S_ATT_BLOCK):
            s = _dot_nt(qh, _kv_head(k_ref, jj, hd))
            p = jnp.exp(s - jnp.max(s, axis=-1, keepdims=True))
            p = (p / jnp.sum(p, axis=-1, keepdims=True)).astype(bf16)
            out = jnp.where(row_seq == jj, _dot(p, _kv_head(v_ref, jj, hd)), out)
        o_ref[:, hl] = out.astype(bf16)


def _sattn(q, mem_k, mem_v):
    ns = mem_k.shape[0]
    nb = S_ATT_BLOCK
    qblk = pl.BlockSpec((nb * DEC_SEQ, D_MODEL), lambda i: (i, 0))
    kvblk = pl.BlockSpec((nb, N_MEM * KV_ROWS_PER_TOKEN, LANES), lambda i: (i, 0, 0))
    return pl.pallas_call(
        _sattn_kernel,
        grid=(ns // nb,),
        in_specs=[qblk, kvblk, kvblk],
        out_specs=qblk,
        out_shape=jax.ShapeDtypeStruct((ns * DEC_SEQ, D_MODEL), bf16),
        compiler_params=pltpu.CompilerParams(dimension_semantics=("arbitrary",), vmem_limit_bytes=VMEM_LIMIT_BYTES),
        name="sattn",
    )(q, mem_k, mem_v)


def _sffn_kernel(x1_ref, ao_ref, sffn_ref, wo_ref, g_ref, wup_ref, cw_ref, cb_ref, wdn_ref, gfin_ref,
                 y_ref, st_ref, h_scr, act_scr):
    nb = x1_ref.shape[0]
    x2 = _time_major(x1_ref, D_MODEL) + _dot(_time_major(ao_ref, D_MODEL), wo_ref[...])
    h_scr[...] = _rmsnorm(x2, g_ref[...]).astype(bf16)
    u = _dot(h_scr[...], wup_ref[...])

    def conv_block(blk):
        cols = slice(blk * FF_CHUNK, (blk + 1) * FF_CHUNK)
        nat = _ff_block_col(blk)
        slots = [sffn_ref[:, i * 2 * D_FF + nat:i * 2 * D_FF + nat + FF_CHUNK] for i in range(FFN_CONV - 1)]
        slots += [u[l * nb:(l + 1) * nb, cols] for l in range(DEC_SEQ)]
        for i in range(FFN_CONV - 1):
            st_ref[:, i * 2 * D_FF + nat:i * 2 * D_FF + nat + FF_CHUNK] = slots[DEC_SEQ + i]
        outs = []
        for l in range(DEC_SEQ):
            acc = cb_ref[:, cols] + cw_ref[0:1, cols] * slots[l]
            for k in range(1, FFN_CONV):
                acc = acc + cw_ref[k:k + 1, cols] * slots[l + k]
            outs.append(acc)
        return jnp.concatenate(outs, axis=0)

    for j in range(N_FF_CHUNKS):
        act = _silu(conv_block(2 * j)) * conv_block(2 * j + 1)
        act_scr[:, j * FF_CHUNK:(j + 1) * FF_CHUNK] = act.astype(bf16)
    y = _rmsnorm(x2 + _dot(act_scr[...], wdn_ref[...]), gfin_ref[...])
    for l in range(DEC_SEQ):
        y_ref[:, l * D_MODEL:(l + 1) * D_MODEL] = y[l * nb:(l + 1) * nb]


def _sffn(x1, ao, sffn, w):
    ns = x1.shape[0]
    nb = S_SEQ_BLOCK
    rows = lambda width: pl.BlockSpec((nb, width), lambda i: (i, 0))
    return pl.pallas_call(
        _sffn_kernel,
        grid=(ns // nb,),
        in_specs=[rows(DEC_SEQ * D_MODEL), rows(DEC_SEQ * D_MODEL), rows((FFN_CONV - 1) * 2 * D_FF),
                  _const_spec((D_MODEL, D_MODEL)), _const_spec((1, D_MODEL)), _const_spec((D_MODEL, 2 * D_FF)),
                  _const_spec((FFN_CONV, 2 * D_FF)), _const_spec((1, 2 * D_FF)),
                  _const_spec((D_FF, D_MODEL)), _const_spec((1, D_MODEL))],
        out_specs=[rows(DEC_SEQ * D_MODEL), rows((FFN_CONV - 1) * 2 * D_FF)],
        out_shape=[jax.ShapeDtypeStruct((ns, DEC_SEQ * D_MODEL), f32),
                   jax.ShapeDtypeStruct((ns, (FFN_CONV - 1) * 2 * D_FF), f32)],
        scratch_shapes=[pltpu.VMEM((DEC_SEQ * nb, D_MODEL), bf16), pltpu.VMEM((DEC_SEQ * nb, D_FF), bf16)],
        compiler_params=pltpu.CompilerParams(dimension_semantics=("arbitrary",), vmem_limit_bytes=VMEM_LIMIT_BYTES),
        name="sffn",
    )(x1, ao, sffn, w["wo"], w["nffn"], w["wup"], w["fcw"], w["fcb"], w["wdn"], w["gfin"])


def _sample_path(x_sample, state_ssm, state_ssm_conv, state_pool, state_ffn_conv, cache_mem_k, cache_mem_v, w):
    ns = x_sample.shape[0]
    x = x_sample.reshape(ns, DEC_SEQ * D_MODEL)
    z, ypart, eacs, xd, cmat, bmat, dec, pout, conv_new, pool_new = _smix_in(
        x, state_ssm_conv[0].reshape(ns, -1), state_pool[0].reshape(ns, -1), w)
    tok = lambda a, width: a.reshape(ns * DEC_SEQ, width)
    yoff, ssm_new = _sssd(dec[:, :SSM_HEADS].reshape(-1), tok(cmat, D_BC), tok(bmat, D_BC), tok(xd, D_SSM),
                          state_ssm[0].reshape(ns, D_SSM, SSM_STATE))
    x1, q = _smix_out(tok(x, D_MODEL), tok(ypart, D_SSM), yoff, tok(eacs, D_SSM), tok(z, D_SSM), tok(pout, D_POOL), w)
    ao = _sattn(q, _kv_rows_view(cache_mem_k[0]), _kv_rows_view(cache_mem_v[0]))
    y, ffn_new = _sffn(x1.reshape(ns, -1), ao.reshape(ns, -1), state_ffn_conv[0].reshape(ns, -1), w)
    return (y.reshape(ns, DEC_SEQ, D_MODEL), ssm_new.reshape(1, ns, SSM_HEADS, SSM_HEAD_DIM, SSM_STATE),
            conv_new.reshape(1, ns, SSM_CONV - 1, D_XBC), pool_new.reshape(1, ns, POOL_HIST, D_POOL),
            ffn_new.reshape(1, ns, FFN_CONV - 1, 2 * D_FF))


def kernel(x_prompt, x_sample, mem_prompt, state_ssm, state_ssm_conv, state_pool, state_ffn_conv, cache_mem_k, cache_mem_v, norm_mix, w_in, ssm_conv_w, ssm_conv_b, ssm_dt_bias, ssm_a_log, ssm_d, ssm_norm, w_pool, pool_scale, w_out, norm_mem, norm_memkv, w_mq, w_mk, w_mv, w_mo, norm_ffn, w_up, ffn_conv_w, ffn_conv_b, w_down, final_norm):
    params = dict(norm_mix=norm_mix, w_in=w_in, ssm_conv_w=ssm_conv_w, ssm_conv_b=ssm_conv_b, ssm_dt_bias=ssm_dt_bias,
                  ssm_a_log=ssm_a_log, ssm_d=ssm_d, ssm_norm=ssm_norm, w_pool=w_pool, pool_scale=pool_scale,
                  w_out=w_out, norm_mem=norm_mem, norm_memkv=norm_memkv, w_mq=w_mq, w_mk=w_mk, w_mv=w_mv, w_mo=w_mo,
                  norm_ffn=norm_ffn, w_up=w_up, ffn_conv_w=ffn_conv_w, ffn_conv_b=ffn_conv_b, w_down=w_down,
                  final_norm=final_norm)
    w = _prep_weights(params)
    yp, ssm_p, conv_p, pool_p, ffn_p, mk_p, mv_p = _prompt_path(x_prompt, mem_prompt, w, PROMPT_TILE)
    ys, ssm_s, conv_s, pool_s, ffn_s = _sample_path(x_sample, state_ssm, state_ssm_conv, state_pool, state_ffn_conv,
                                                    cache_mem_k, cache_mem_v, w)
    return yp, ys, ssm_p, ssm_s, conv_p, conv_s, pool_p, pool_s, ffn_p, ffn_s, mk_p, mv_p
```

```python
import functools

import jax
import jax.numpy as jnp
from jax import lax
from jax.experimental import pallas as pl
from jax.experimental.pallas import tpu as pltpu

f32 = jnp.float32
bf16 = jnp.bfloat16

D_MODEL = 1024
SSM_HEADS = 16
SSM_HEAD_DIM = 64
SSM_STATE = 128
SSM_GROUPS = 2
SSM_CHUNK = 128
D_SSM = 1024
D_BC = SSM_GROUPS * SSM_STATE
D_XBC = D_SSM + 2 * D_BC
SSM_CONV = 4
D_POOL = 1024
POOL_WINDOWS = (2, 4, 8, 16)
POOL_GROUP_DIM = 256
POOL_HIST = 15
N_MEM = 256
MEM_HEADS = 4
MEM_HEAD_DIM = 256
D_FF = 2816
FFN_CONV = 3
EPS = 1e-6
PAST_LEN = 16384

LANES = 128
SUBLANES = 8
MXU_DIM = 256
VMEM_LIMIT_BYTES = 56 * 1024 * 1024

COL_Z = 0
COL_XBC = D_SSM
COL_VP = COL_XBC + D_XBC
COL_DT = COL_VP + D_POOL
D_PROJ = COL_DT + LANES

CONV_HIST_ROWS = SUBLANES
POOL_HIST_ROWS = 2 * SUBLANES
FF_CHUNK = MXU_DIM
N_FF_CHUNKS = D_FF // FF_CHUNK


def _silu(v):
    return v * (1.0 / (1.0 + jnp.exp(-v)))


def _softplus(v):
    return jnp.maximum(v, 0.0) + jnp.log1p(jnp.exp(-jnp.abs(v)))


def _rmsnorm(x, g):
    ms = jnp.mean(x * x, axis=-1, keepdims=True)
    return x * lax.rsqrt(ms + EPS) * g


def _dot(a, b):
    return jnp.dot(a, b, preferred_element_type=f32)


def _dot_nt(a, b):
    return lax.dot_general(a, b, (((1,), (1,)), ((), ())), preferred_element_type=f32)


def _dot_tn(a, b):
    return lax.dot_general(a, b, (((0,), (0,)), ((), ())), preferred_element_type=f32)


def _split3(v):
    p1 = v.astype(bf16)
    r1 = v - p1.astype(f32)
    p2 = r1.astype(bf16)
    r2 = r1 - p2.astype(f32)
    return p1, p2, r2.astype(bf16)


def _const_spec(shape):
    return pl.BlockSpec(shape, lambda *_: (0,) * len(shape))


def _memkv_kernel(mem_ref, g_ref, wk_ref, wv_ref, k_ref, v_ref):
    h = _rmsnorm(mem_ref[0], g_ref[...]).astype(bf16)
    k_ref[0] = _dot(h, wk_ref[...])
    v_ref[0] = _dot(h, wv_ref[...])


def _memkv(mem, g, wk, wv):
    b = mem.shape[0]
    blk = pl.BlockSpec((1, N_MEM, D_MODEL), lambda i: (i, 0, 0))
    return pl.pallas_call(
        _memkv_kernel,
        grid=(b,),
        in_specs=[blk, _const_spec((1, D_MODEL)), _const_spec((D_MODEL, D_MODEL)), _const_spec((D_MODEL, D_MODEL))],
        out_specs=[blk, blk],
        out_shape=[jax.ShapeDtypeStruct((b, N_MEM, D_MODEL), f32)] * 2,
        compiler_params=pltpu.CompilerParams(dimension_semantics=("arbitrary",), vmem_limit_bytes=VMEM_LIMIT_BYTES),
        name="memkv",
    )(mem, g, wk, wv)


def _ssd_chunk(r0, dt_scr, xs_scr, b_scr, c_scr, y_scr, xd_scr, hst_scr, dtb_ref, a_ref, dexp_ref):
    q = SSM_CHUNK
    rows = pl.ds(r0, q)
    row_i = lax.broadcasted_iota(jnp.int32, (q, q), 0)
    col_i = lax.broadcasted_iota(jnp.int32, (q, q), 1)
    causal = col_i <= row_i
    lo = col_i < SSM_HEAD_DIM
    tril = jnp.where(causal, 1.0, 0.0).astype(bf16)

    dt = _softplus(dt_scr[rows, :] + dtb_ref[...])
    da = dt * a_ref[...]
    p1, p2, p3 = _split3(da)
    acs = _dot(tril, p1) + _dot(tril, p2) + _dot(tril, p3)
    acs_t = acs.T

    for g in range(SSM_GROUPS):
        bg = b_scr[rows, g * SSM_STATE:(g + 1) * SSM_STATE]
        cg = c_scr[rows, g * SSM_STATE:(g + 1) * SSM_STATE]
        bg_b = bg.astype(bf16)
        cb = jnp.where(causal, _dot_nt(cg.astype(bf16), bg_b), 0.0)
        cdec_rows = []
        pairs_per_group = SSM_HEADS // SSM_GROUPS // 2
        for pp in range(pairs_per_group):
            pr = g * pairs_per_group + pp
            lanes = slice(pr * LANES, (pr + 1) * LANES)
            lhs, dtb, dend, cdec = [], [], [], []
            for hh in (2 * pr, 2 * pr + 1):
                colb = jnp.broadcast_to(acs[:, hh:hh + 1], (q, q))
                seg = jnp.where(causal, colb - acs_t[hh:hh + 1, :], 0.0)
                lhs.append((jnp.exp(seg) * cb).astype(bf16))
                lhs.append((cg * jnp.exp(colb)).astype(bf16))
                last = colb[q - 1:q, :]
                dend.append(jnp.exp(last - colb))
                cdec.append(jnp.exp(last))
                dtb.append(jnp.broadcast_to(dt[:, hh:hh + 1], (q, q)))
            xs_pair = xs_scr[rows, lanes]
            xdt = xs_pair * jnp.where(lo, dtb[0], dtb[1])
            xd_scr[:, lanes] = (xdt * jnp.where(lo, dend[0], dend[1])).astype(bf16)
            hst_pair = hst_scr[:, lanes]
            rhs = jnp.concatenate([
                jnp.where(lo, xdt, 0.0).astype(bf16), jnp.where(lo, hst_pair, 0.0).astype(bf16),
                jnp.where(lo, 0.0, xdt).astype(bf16), jnp.where(lo, 0.0, hst_pair).astype(bf16)], axis=0)
            y_pair = _dot(jnp.concatenate(lhs, axis=1), rhs)
            y_scr[rows, lanes] = y_pair + xs_pair * dexp_ref[:, lanes]
            cdec_rows.append(jnp.where(lo[:1], cdec[0], cdec[1]))
        gl = slice(g * (D_SSM // SSM_GROUPS), (g + 1) * (D_SSM // SSM_GROUPS))
        upd = _dot_tn(bg_b, xd_scr[:, gl])
        hst_scr[:, gl] = hst_scr[:, gl] * jnp.concatenate(cdec_rows, axis=1) + upd


def _mixer_kernel(x_ref, nmix_ref, win_ref, cw_ref, cb_ref, dtb_ref, a_ref, dexp_ref, snorm_ref, wpool_ref,
                  pscale_ref, wout_ref,
                  x1_ref, ssm_ref, conv_ref, pool_ref,
                  h_scr, z_scr, xbc_scr, vp_scr, dt_scr, xs_scr, b_scr, c_scr, y_scr, xd_scr, hst_scr,
                  pooled_scr, cat_scr, *, tile):
    t = pl.program_id(1)
    last_t = pl.num_programs(1) - 1
    ch = CONV_HIST_ROWS
    ph = POOL_HIST_ROWS

    @pl.when(t == 0)
    def _():
        xbc_scr[:, 0:ch, :] = jnp.zeros((D_XBC // LANES, ch, LANES), f32)
        vp_scr[:, 0:ph, :] = jnp.zeros((D_POOL // LANES, ph, LANES), f32)
        hst_scr[...] = jnp.zeros_like(hst_scr)

    x = x_ref[0]
    h_scr[...] = _rmsnorm(x, nmix_ref[...]).astype(bf16)
    xbc = _dot(h_scr[...], win_ref[:, COL_XBC:COL_VP])
    for j in range(D_XBC // LANES):
        xbc_scr[j, ch:ch + tile, :] = xbc[:, j * LANES:(j + 1) * LANES]
    dt_scr[...] = _dot(h_scr[...], win_ref[:, COL_DT:D_PROJ])
    vp = _dot(h_scr[...], win_ref[:, COL_VP:COL_DT])
    for j in range(D_POOL // LANES):
        vp_scr[j, ph:ph + tile, :] = vp[:, j * LANES:(j + 1) * LANES]
    z_scr[...] = _dot(h_scr[...], win_ref[:, COL_Z:COL_XBC])

    for j in range(D_XBC // LANES):
        cl = slice(j * LANES, (j + 1) * LANES)
        acc = cb_ref[:, cl] + cw_ref[0:1, cl] * xbc_scr[j, pl.ds(ch - 3, tile), :]
        for k in range(1, SSM_CONV):
            acc = acc + cw_ref[k:k + 1, cl] * xbc_scr[j, pl.ds(ch - 3 + k, tile), :]
        act = _silu(acc)
        if j < D_SSM // LANES:
            xs_scr[:, cl] = act
        elif j < (D_SSM + D_BC) // LANES:
            b_scr[:, j * LANES - D_SSM:(j + 1) * LANES - D_SSM] = act
        else:
            c_scr[:, j * LANES - D_SSM - D_BC:(j + 1) * LANES - D_SSM - D_BC] = act
        xbc_scr[j, 0:ch, :] = xbc_scr[j, tile:tile + ch, :]

    pos1 = lax.broadcasted_iota(jnp.int32, (tile, LANES), 0) + (t * tile + 1)
    for j in range(D_POOL // LANES):
        cl = slice(j * LANES, (j + 1) * LANES)
        w = POOL_WINDOWS[j * LANES // POOL_GROUP_DIM]
        cur = vp_scr[j, ph:ph + tile, :]
        s = cur
        for k in range(1, w):
            s = s + vp_scr[j, pl.ds(ph - k, tile), :]
        cnt = jnp.minimum(pos1, w).astype(f32)
        pooled_scr[:, cl] = (s / cnt - cur).astype(bf16)
        vp_scr[j, 0:ph, :] = vp_scr[j, tile:tile + ph, :]
    for gi in range(len(POOL_WINDOWS)):
        gl = slice(gi * POOL_GROUP_DIM, (gi + 1) * POOL_GROUP_DIM)
        pg = _dot(pooled_scr[:, gl], wpool_ref[gi]) * pscale_ref[:, gl]
        cat_scr[:, D_SSM + gi * POOL_GROUP_DIM:D_SSM + (gi + 1) * POOL_GROUP_DIM] = pg.astype(bf16)

    for c in range(tile // SSM_CHUNK):
        _ssd_chunk(c * SSM_CHUNK, dt_scr, xs_scr, b_scr, c_scr, y_scr, xd_scr, hst_scr, dtb_ref, a_ref, dexp_ref)

    gw = D_SSM // SSM_GROUPS
    for g in range(SSM_GROUPS):
        gl = slice(g * gw, (g + 1) * gw)
        tg = y_scr[:, gl] * _silu(z_scr[:, gl])
        ms = jnp.mean(tg * tg, axis=-1, keepdims=True)
        cat_scr[:, gl] = (tg * lax.rsqrt(ms + EPS) * snorm_ref[:, gl]).astype(bf16)

    x1_ref[0] = x + _dot(cat_scr[...], wout_ref[...])

    @pl.when(t == last_t)
    def _():
        for pr in range(D_SSM // LANES):
            ssm_ref[0, pr * LANES:(pr + 1) * LANES, :] = hst_scr[:, pr * LANES:(pr + 1) * LANES].T
        for j in range(D_XBC // LANES):
            conv_ref[0, :, j * LANES:(j + 1) * LANES] = xbc_scr[j, pl.ds(ch - (SSM_CONV - 1), SSM_CONV - 1), :]
        for j in range(D_POOL // LANES):
            pool_ref[0, :, j * LANES:(j + 1) * LANES] = vp_scr[j, pl.ds(ph - POOL_HIST, POOL_HIST), :]


def _mixer_prompt(x, nmix, win, cw, cb, dtb, a_row, dexp, snorm, wpool, pscale, wout, tile):
    b, seq, _ = x.shape
    nt = seq // tile
    xblk = pl.BlockSpec((1, tile, D_MODEL), lambda i, j: (i, j, 0))
    scratch = [
        pltpu.VMEM((tile, D_MODEL), bf16),
        pltpu.VMEM((tile, D_SSM), f32),
        pltpu.VMEM((D_XBC // LANES, CONV_HIST_ROWS + tile, LANES), f32),
        pltpu.VMEM((D_POOL // LANES, POOL_HIST_ROWS + tile, LANES), f32),
        pltpu.VMEM((tile, LANES), f32),
        pltpu.VMEM((tile, D_SSM), f32),
        pltpu.VMEM((tile, D_BC), f32),
        pltpu.VMEM((tile, D_BC), f32),
        pltpu.VMEM((tile, D_SSM), f32),
        pltpu.VMEM((SSM_CHUNK, D_SSM), bf16),
        pltpu.VMEM((SSM_STATE, D_SSM), f32),
        pltpu.VMEM((tile, D_POOL), bf16),
        pltpu.VMEM((tile, D_SSM + D_POOL), bf16),
    ]
    return pl.pallas_call(
        functools.partial(_mixer_kernel, tile=tile),
        grid=(b, nt),
        in_specs=[xblk, _const_spec((1, D_MODEL)), _const_spec((D_MODEL, D_PROJ)), _const_spec((SSM_CONV, D_XBC)),
                  _const_spec((1, D_XBC)), _const_spec((1, LANES)), _const_spec((1, LANES)), _const_spec((1, D_SSM)),
                  _const_spec((1, D_SSM)), _const_spec((len(POOL_WINDOWS), POOL_GROUP_DIM, POOL_GROUP_DIM)),
                  _const_spec((1, D_POOL)), _const_spec((D_SSM + D_POOL, D_MODEL))],
        out_specs=[xblk,
                   pl.BlockSpec((1, D_SSM, SSM_STATE), lambda i, j: (i, 0, 0)),
                   pl.BlockSpec((1, SSM_CONV - 1, D_XBC), lambda i, j: (i, 0, 0)),
                   pl.BlockSpec((1, POOL_HIST, D_POOL), lambda i, j: (i, 0, 0))],
        out_shape=[jax.ShapeDtypeStruct((b, seq, D_MODEL), f32),
                   jax.ShapeDtypeStruct((b, D_SSM, SSM_STATE), f32),
                   jax.ShapeDtypeStruct((b, SSM_CONV - 1, D_XBC), f32),
                   jax.ShapeDtypeStruct((b, POOL_HIST, D_POOL), f32)],
        scratch_shapes=scratch,
        compiler_params=pltpu.CompilerParams(dimension_semantics=("arbitrary", "arbitrary"),
                                             vmem_limit_bytes=VMEM_LIMIT_BYTES),
        name="mixer_prompt",
    )(x, nmix, win, cw, cb, dtb, a_row, dexp, snorm, wpool, pscale, wout)


def _attn_kernel(x_ref, g_ref, wq_ref, k_ref, v_ref, wo_ref, o_ref, q_scr, ao_scr):
    x = x_ref[0]
    h = _rmsnorm(x, g_ref[...]).astype(bf16)
    q_scr[...] = (_dot(h, wq_ref[...]) * (MEM_HEAD_DIM ** -0.5)).astype(bf16)
    for hd in range(MEM_HEADS):
        hl = slice(hd * MEM_HEAD_DIM, (hd + 1) * MEM_HEAD_DIM)
        s = _dot_nt(q_scr[:, hl], k_ref[0, :, hl].astype(bf16))
        p = jnp.exp(s - jnp.max(s, axis=-1, keepdims=True))
        p = (p / jnp.sum(p, axis=-1, keepdims=True)).astype(bf16)
        ao_scr[:, hl] = _dot(p, v_ref[0, :, hl].astype(bf16)).astype(bf16)
    o_ref[0] = x + _dot(ao_scr[...], wo_ref[...])


def _attn_prompt(x, g, wq, mem_k, mem_v, wo, tile):
    b, seq, _ = x.shape
    xblk = pl.BlockSpec((1, tile, D_MODEL), lambda i, j: (i, j, 0))
    kvblk = pl.BlockSpec((1, N_MEM, D_MODEL), lambda i, j: (i, 0, 0))
    return pl.pallas_call(
        _attn_kernel,
        grid=(b, seq // tile),
        in_specs=[xblk, _const_spec((1, D_MODEL)), _const_spec((D_MODEL, D_MODEL)), kvblk, kvblk,
                  _const_spec((D_MODEL, D_MODEL))],
        out_specs=xblk,
        out_shape=jax.ShapeDtypeStruct((b, seq, D_MODEL), f32),
        scratch_shapes=[pltpu.VMEM((tile, D_MODEL), bf16), pltpu.VMEM((tile, D_MODEL), bf16)],
        compiler_params=pltpu.CompilerParams(dimension_semantics=("arbitrary", "arbitrary"),
                                             vmem_limit_bytes=VMEM_LIMIT_BYTES),
        name="attn_prompt",
    )(x, g, wq, mem_k, mem_v, wo)


def _gate_blk(j):
    return j


def _val_blk(j):
    return N_FF_CHUNKS + j


def _ffn_kernel(x_ref, g_ref, wup_ref, cw_ref, cb_ref, wdn_ref, gfin_ref, y_ref, st_ref, h_scr, u_scr, act_scr, *, tile):
    t = pl.program_id(1)
    last_t = pl.num_programs(1) - 1
    ch = CONV_HIST_ROWS
    tpc = FF_CHUNK // LANES
    ntile = 2 * D_FF // LANES

    @pl.when(t == 0)
    def _():
        u_scr[:, 0:ch, :] = jnp.zeros((ntile, ch, LANES), f32)

    x = x_ref[0]
    h_scr[...] = _rmsnorm(x, g_ref[...]).astype(bf16)
    u = _dot(h_scr[...], wup_ref[...])
    for ti in range(ntile):
        u_scr[ti, ch:ch + tile, :] = u[:, ti * LANES:(ti + 1) * LANES]

    def conv(ti):
        cl = slice(ti * LANES, (ti + 1) * LANES)
        acc = cb_ref[:, cl] + cw_ref[0:1, cl] * u_scr[ti, pl.ds(ch - 2, tile), :]
        acc = acc + cw_ref[1:2, cl] * u_scr[ti, pl.ds(ch - 1, tile), :]
        return acc + cw_ref[2:3, cl] * u_scr[ti, ch:ch + tile, :]

    for j in range(N_FF_CHUNKS):
        for i in range(tpc):
            gate = conv(_gate_blk(j) * tpc + i)
            val = conv(_val_blk(j) * tpc + i)
            act_scr[:, (j * tpc + i) * LANES:(j * tpc + i + 1) * LANES] = (_silu(gate) * val).astype(bf16)
    for ti in range(ntile):
        u_scr[ti, 0:ch, :] = u_scr[ti, tile:tile + ch, :]
    y_ref[0] = _rmsnorm(x + _dot(act_scr[...], wdn_ref[...]), gfin_ref[...])

    @pl.when(t == last_t)
    def _():
        for ti in range(ntile):
            st_ref[0, :, ti * LANES:(ti + 1) * LANES] = u_scr[ti, pl.ds(ch - (FFN_CONV - 1), FFN_CONV - 1), :]


def _ffn_prompt(x, g, wup, cw, cb, wdn, gfin, tile):
    b, seq, _ = x.shape
    xblk = pl.BlockSpec((1, tile, D_MODEL), lambda i, j: (i, j, 0))
    return pl.pallas_call(
        functools.partial(_ffn_kernel, tile=tile),
        grid=(b, seq // tile),
        in_specs=[xblk, _const_spec((1, D_MODEL)), _const_spec((D_MODEL, 2 * D_FF)),
                  _const_spec((FFN_CONV, 2 * D_FF)), _const_spec((1, 2 * D_FF)),
                  _const_spec((D_FF, D_MODEL)), _const_spec((1, D_MODEL))],
        out_specs=[xblk, pl.BlockSpec((1, FFN_CONV - 1, 2 * D_FF), lambda i, j: (i, 0, 0))],
        out_shape=[jax.ShapeDtypeStruct((b, seq, D_MODEL), f32),
                   jax.ShapeDtypeStruct((b, FFN_CONV - 1, 2 * D_FF), f32)],
        scratch_shapes=[pltpu.VMEM((tile, D_MODEL), bf16),
                        pltpu.VMEM((2 * D_FF // LANES, CONV_HIST_ROWS + tile, LANES), f32),
                        pltpu.VMEM((tile, D_FF), bf16)],
        compiler_params=pltpu.CompilerParams(dimension_semantics=("arbitrary", "arbitrary"),
                                             vmem_limit_bytes=VMEM_LIMIT_BYTES),
        name="ffn_prompt",
    )(x, g, wup, cw, cb, wdn, gfin)


def _prep_weights(p):
    i = 0
    w_in = p["w_in"][i]
    dt_cols = jnp.pad(w_in[:, D_SSM + D_XBC:D_SSM + D_XBC + SSM_HEADS], ((0, 0), (0, LANES - SSM_HEADS)))
    win = jnp.concatenate([w_in[:, :D_SSM + D_XBC], w_in[:, D_SSM + D_XBC + SSM_HEADS:], dt_cols], axis=1).astype(bf16)
    pad_h = (0, LANES - SSM_HEADS)
    return dict(
        nmix=p["norm_mix"][i][None], win=win, cw=p["ssm_conv_w"][i], cb=p["ssm_conv_b"][i][None],
        dtb=jnp.pad(p["ssm_dt_bias"][i], pad_h)[None],
        a_row=jnp.pad(-jnp.exp(p["ssm_a_log"][i].astype(f32)), pad_h)[None],
        dexp=jnp.repeat(p["ssm_d"][i], SSM_HEAD_DIM)[None], snorm=p["ssm_norm"][i][None],
        wpool=p["w_pool"][i].astype(bf16), pscale=p["pool_scale"][i][None], wout=p["w_out"][i].astype(bf16),
        nmem=p["norm_mem"][i][None], nmemkv=p["norm_memkv"][i][None],
        wq=p["w_mq"][i].astype(bf16), wk=p["w_mk"][i].astype(bf16), wv=p["w_mv"][i].astype(bf16),
        wo=p["w_mo"][i].astype(bf16),
        nffn=p["norm_ffn"][i][None],
        wup=p["w_up"][i].astype(bf16), fcw=p["ffn_conv_w"][i], fcb=p["ffn_conv_b"][i][None],
        wdn=p["w_down"][i].astype(bf16),
        gfin=p["final_norm"][None],
    )


def _prompt_path(x_prompt, mem_prompt, w, tile):
    b = x_prompt.shape[0]
    mem_k, mem_v = _memkv(mem_prompt, w["nmemkv"], w["wk"], w["wv"])
    x1, ssm, conv, pool = _mixer_prompt(x_prompt, w["nmix"], w["win"], w["cw"], w["cb"], w["dtb"], w["a_row"],
                                        w["dexp"], w["snorm"], w["wpool"], w["pscale"], w["wout"], tile)
    x2 = _attn_prompt(x1, w["nmem"], w["wq"], mem_k, mem_v, w["wo"], tile)
    y, ffn = _ffn_prompt(x2, w["nffn"], w["wup"], w["fcw"], w["fcb"], w["wdn"], w["gfin"], tile)
    return (y, ssm.reshape(1, b, SSM_HEADS, SSM_HEAD_DIM, SSM_STATE), conv[None], pool[None], ffn[None],
            mem_k.reshape(1, b, N_MEM, MEM_HEADS, MEM_HEAD_DIM), mem_v.reshape(1, b, N_MEM, MEM_HEADS, MEM_HEAD_DIM))


PROMPT_TILE = 256


DEC_SEQ = 4
S_SEQ_BLOCK = 32
S_SSD_BLOCK = 8
S_ATT_BLOCK = 4
ROW_GROUP = 16


def _expand_heads(v, lo):
    r = v.shape[0]
    tiles = []
    for pr in range(SSM_HEADS // 2):
        a = jnp.broadcast_to(v[:, 2 * pr:2 * pr + 1], (r, LANES))
        b = jnp.broadcast_to(v[:, 2 * pr + 1:2 * pr + 2], (r, LANES))
        tiles.append(jnp.where(lo, a, b))
    return jnp.concatenate(tiles, axis=1)


def _time_major(ref, width):
    return jnp.concatenate([ref[:, l * width:(l + 1) * width] for l in range(DEC_SEQ)], axis=0)


def _smix_in_kernel(x_ref, sconv_ref, spool_ref, nmix_ref, win_ref, cw_ref, cb_ref, dtb_ref, a_ref, dexp_ref,
                    wpool_ref, pscale_ref,
                    z_ref, ypart_ref, eacs_ref, xd_ref, c_ref, b_ref, dec_ref, pout_ref, conv_ref, pool_ref,
                    h_scr, pooled_scr):
    nb = x_ref.shape[0]
    lo = lax.broadcasted_iota(jnp.int32, (nb, LANES), 1) < SSM_HEAD_DIM
    h_scr[...] = _rmsnorm(_time_major(x_ref, D_MODEL), nmix_ref[...]).astype(bf16)
    z = _dot(h_scr[...], win_ref[:, COL_Z:COL_XBC])
    for l in range(DEC_SEQ):
        z_ref[:, l * D_SSM:(l + 1) * D_SSM] = z[l * nb:(l + 1) * nb]
    xbc = _dot(h_scr[...], win_ref[:, COL_XBC:COL_VP])
    vp = _dot(h_scr[...], win_ref[:, COL_VP:COL_DT])
    dtr = _dot(h_scr[...], win_ref[:, COL_DT:D_PROJ])

    def conv_slot(i):
        if i < SSM_CONV - 1:
            return sconv_ref[:, i * D_XBC:(i + 1) * D_XBC]
        return xbc[(i - SSM_CONV + 1) * nb:(i - SSM_CONV + 2) * nb]

    def pool_slot(i, cl):
        if i < POOL_HIST:
            return spool_ref[:, i * D_POOL + cl.start:i * D_POOL + cl.stop]
        return vp[(i - POOL_HIST) * nb:(i - POOL_HIST + 1) * nb, cl]

    xs, bm, cm, dt, acs = [], [], [], [], []
    for l in range(DEC_SEQ):
        acc = cb_ref[...] + cw_ref[0:1, :] * conv_slot(l)
        for k in range(1, SSM_CONV):
            acc = acc + cw_ref[k:k + 1, :] * conv_slot(l + k)
        act = _silu(acc)
        xs.append(act[:, :D_SSM])
        bm.append(act[:, D_SSM:D_SSM + D_BC])
        cm.append(act[:, D_SSM + D_BC:])
        b_ref[:, l * D_BC:(l + 1) * D_BC] = bm[l]
        c_ref[:, l * D_BC:(l + 1) * D_BC] = cm[l]
        dt.append(_softplus(dtr[l * nb:(l + 1) * nb] + dtb_ref[...]))
        da = dt[l] * a_ref[...]
        acs.append(da if l == 0 else acs[l - 1] + da)
        for gi, w in enumerate(POOL_WINDOWS):
            gl = slice(gi * POOL_GROUP_DIM, (gi + 1) * POOL_GROUP_DIM)
            s = pool_slot(POOL_HIST + l, gl)
            for k in range(1, w):
                s = s + pool_slot(POOL_HIST + l - k, gl)
            cnt = float(min(PAST_LEN + l + 1, w))
            pooled_scr[l * nb:(l + 1) * nb, gl] = (s / cnt - pool_slot(POOL_HIST + l, gl)).astype(bf16)
    for i in range(SSM_CONV - 1):
        conv_ref[:, i * D_XBC:(i + 1) * D_XBC] = conv_slot(DEC_SEQ + i)
    for i in range(POOL_HIST):
        pool_ref[:, i * D_POOL:(i + 1) * D_POOL] = pool_slot(DEC_SEQ + i, slice(0, D_POOL))
    for gi in range(len(POOL_WINDOWS)):
        gl = slice(gi * POOL_GROUP_DIM, (gi + 1) * POOL_GROUP_DIM)
        pg = (_dot(pooled_scr[:, gl], wpool_ref[gi]) * pscale_ref[:, gl]).astype(bf16)
        for l in range(DEC_SEQ):
            pout_ref[:, l * D_POOL + gi * POOL_GROUP_DIM:l * D_POOL + (gi + 1) * POOL_GROUP_DIM] = pg[l * nb:(l + 1) * nb]

    xdt = [xs[l] * _expand_heads(dt[l], lo) for l in range(DEC_SEQ)]
    gw = D_SSM // SSM_GROUPS
    for l in range(DEC_SEQ):
        y = xs[l] * dexp_ref[...]
        for s in range(l + 1):
            decay = _expand_heads(jnp.exp(acs[l] - acs[s]), lo)
            cbs = [jnp.sum(cm[l][:, g * SSM_STATE:(g + 1) * SSM_STATE] * bm[s][:, g * SSM_STATE:(g + 1) * SSM_STATE],
                           axis=-1, keepdims=True) for g in range(SSM_GROUPS)]
            coef = jnp.concatenate([decay[:, g * gw:(g + 1) * gw] * cbs[g] for g in range(SSM_GROUPS)], axis=1)
            y = y + coef * xdt[s]
        ypart_ref[:, l * D_SSM:(l + 1) * D_SSM] = y
        eacs_ref[:, l * D_SSM:(l + 1) * D_SSM] = _expand_heads(jnp.exp(acs[l]), lo)
        xd_ref[:, l * D_SSM:(l + 1) * D_SSM] = xdt[l] * _expand_heads(jnp.exp(acs[DEC_SEQ - 1] - acs[l]), lo)
    dec_ref[...] = jnp.exp(acs[DEC_SEQ - 1])


def _smix_in(x, sconv, spool, w):
    ns = x.shape[0]
    nb = S_SEQ_BLOCK
    rows = lambda width: pl.BlockSpec((nb, width), lambda i: (i, 0))
    outs = [(DEC_SEQ * D_SSM, f32), (DEC_SEQ * D_SSM, f32), (DEC_SEQ * D_SSM, f32), (DEC_SEQ * D_SSM, f32),
            (DEC_SEQ * D_BC, f32), (DEC_SEQ * D_BC, f32), (LANES, f32), (DEC_SEQ * D_POOL, bf16),
            ((SSM_CONV - 1) * D_XBC, f32), (POOL_HIST * D_POOL, f32)]
    return pl.pallas_call(
        _smix_in_kernel,
        grid=(ns // nb,),
        in_specs=[rows(DEC_SEQ * D_MODEL), rows((SSM_CONV - 1) * D_XBC), rows(POOL_HIST * D_POOL),
                  _const_spec((1, D_MODEL)), _const_spec((D_MODEL, D_PROJ)), _const_spec((SSM_CONV, D_XBC)),
                  _const_spec((1, D_XBC)), _const_spec((1, LANES)), _const_spec((1, LANES)), _const_spec((1, D_SSM)),
                  _const_spec((len(POOL_WINDOWS), POOL_GROUP_DIM, POOL_GROUP_DIM)), _const_spec((1, D_POOL))],
        out_specs=[rows(wd) for wd, _ in outs],
        out_shape=[jax.ShapeDtypeStruct((ns, wd), dt) for wd, dt in outs],
        scratch_shapes=[pltpu.VMEM((DEC_SEQ * nb, D_MODEL), bf16), pltpu.VMEM((DEC_SEQ * nb, D_POOL), bf16)],
        compiler_params=pltpu.CompilerParams(dimension_semantics=("arbitrary",), vmem_limit_bytes=VMEM_LIMIT_BYTES),
        name="smix_in",
    )(x, sconv, spool, w["nmix"], w["win"], w["cw"], w["cb"], w["dtb"], w["a_row"], w["dexp"], w["wpool"],
      w["pscale"])


def _sssd_kernel(dec_ref, c_ref, b_ref, xd_ref, st_ref, yoff_ref, stn_ref):
    blk = pl.program_id(0)
    gw = D_SSM // SSM_GROUPS
    row_seq = lax.broadcasted_iota(jnp.int32, (ROW_GROUP, gw), 0) // DEC_SEQ
    hpg = SSM_HEADS // SSM_GROUPS
    for m in range(S_SSD_BLOCK * DEC_SEQ // ROW_GROUP):
        rows = pl.ds(m * ROW_GROUP, ROW_GROUP)
        for g in range(SSM_GROUPS):
            gl = slice(g * gw, (g + 1) * gw)
            c16 = c_ref[rows, g * SSM_STATE:(g + 1) * SSM_STATE].astype(bf16)
            b16 = b_ref[rows, g * SSM_STATE:(g + 1) * SSM_STATE].astype(bf16)
            xd16 = xd_ref[rows, gl]
            yo = jnp.zeros((ROW_GROUP, gw), f32)
            for jj in range(ROW_GROUP // DEC_SEQ):
                j = m * (ROW_GROUP // DEC_SEQ) + jj
                mine = row_seq == jj
                h0 = st_ref[j, gl, :]
                yo = jnp.where(mine, _dot_nt(c16, h0.astype(bf16)), yo)
                upd = _dot_tn(jnp.where(mine, xd16, 0.0).astype(bf16), b16)
                for hh in range(hpg):
                    hr = slice(hh * SSM_HEAD_DIM, (hh + 1) * SSM_HEAD_DIM)
                    d = dec_ref[(blk * S_SSD_BLOCK + j) * SSM_HEADS + g * hpg + hh]
                    stn_ref[j, g * gw + hh * SSM_HEAD_DIM:g * gw + (hh + 1) * SSM_HEAD_DIM, :] = h0[hr] * d + upd[hr]
            yoff_ref[rows, gl] = yo


def _sssd(dec_flat, cmat, bmat, xd, state):
    ns = state.shape[0]
    nb = S_SSD_BLOCK
    rows = lambda width: pl.BlockSpec((nb * DEC_SEQ, width), lambda i: (i, 0))
    stblk = pl.BlockSpec((nb, D_SSM, SSM_STATE), lambda i: (i, 0, 0))
    return pl.pallas_call(
        _sssd_kernel,
        grid=(ns // nb,),
        in_specs=[pl.BlockSpec(memory_space=pltpu.SMEM), rows(D_BC), rows(D_BC), rows(D_SSM), stblk],
        out_specs=[rows(D_SSM), stblk],
        out_shape=[jax.ShapeDtypeStruct((ns * DEC_SEQ, D_SSM), f32), jax.ShapeDtypeStruct(state.shape, f32)],
        compiler_params=pltpu.CompilerParams(dimension_semantics=("arbitrary",), vmem_limit_bytes=VMEM_LIMIT_BYTES),
        name="sssd",
    )(dec_flat, cmat, bmat, xd, state)


def _smix_out_kernel(x_ref, ypart_ref, yoff_ref, eacs_ref, z_ref, pout_ref, snorm_ref, wout_ref, nmem_ref, wq_ref,
                     x1_ref, q_ref, cat_scr):
    y = ypart_ref[...] + yoff_ref[...] * eacs_ref[...]
    t = y * _silu(z_ref[...])
    gw = D_SSM // SSM_GROUPS
    for g in range(SSM_GROUPS):
        gl = slice(g * gw, (g + 1) * gw)
        tg = t[:, gl]
        ms = jnp.mean(tg * tg, axis=-1, keepdims=True)
        cat_scr[:, gl] = (tg * lax.rsqrt(ms + EPS) * snorm_ref[:, gl]).astype(bf16)
    cat_scr[:, D_SSM:] = pout_ref[...]
    x1 = x_ref[...] + _dot(cat_scr[...], wout_ref[...])
    x1_ref[...] = x1
    h = _rmsnorm(x1, nmem_ref[...]).astype(bf16)
    q_ref[...] = (_dot(h, wq_ref[...]) * (MEM_HEAD_DIM ** -0.5)).astype(bf16)


def _smix_out(x, ypart, yoff, eacs, z, pout, w):
    n = x.shape[0]
    rb = 128
    rows = lambda width: pl.BlockSpec((rb, width), lambda i: (i, 0))
    return pl.pallas_call(
        _smix_out_kernel,
        grid=(n // rb,),
        in_specs=[rows(D_MODEL), rows(D_SSM), rows(D_SSM), rows(D_SSM), rows(D_SSM), rows(D_POOL),
                  _const_spec((1, D_SSM)), _const_spec((D_SSM + D_POOL, D_MODEL)), _const_spec((1, D_MODEL)),
                  _const_spec((D_MODEL, D_MODEL))],
        out_specs=[rows(D_MODEL), rows(D_MODEL)],
        out_shape=[jax.ShapeDtypeStruct((n, D_MODEL), f32), jax.ShapeDtypeStruct((n, D_MODEL), bf16)],
        scratch_shapes=[pltpu.VMEM((rb, D_SSM + D_POOL), bf16)],
        compiler_params=pltpu.CompilerParams(dimension_semantics=("arbitrary",), vmem_limit_bytes=VMEM_LIMIT_BYTES),
        name="smix_out",
    )(x, ypart, yoff, eacs, z, pout, w["snorm"], w["wout"], w["nmem"], w["wq"])


KV_LANE_TILES = MEM_HEAD_DIM // LANES
KV_ROWS_PER_TOKEN = KV_LANE_TILES * MEM_HEADS


def _kv_rows_view(kv):
    ns = kv.shape[0]
    kv = kv.reshape(ns, N_MEM, MEM_HEADS, KV_LANE_TILES, LANES).transpose(0, 1, 3, 2, 4)
    return kv.reshape(ns, N_MEM * KV_ROWS_PER_TOKEN, LANES)


def _kv_seq(ref, jj):
    tiles = [ref[jj, pl.ds(dt * MEM_HEADS + hd, N_MEM, stride=KV_ROWS_PER_TOKEN), :]
             for hd in range(MEM_HEADS) for dt in range(KV_LANE_TILES)]
    return jnp.concatenate(tiles, axis=1).astype(bf16)


def _sattn_kernel(q_ref, k_ref, v_ref, o_ref):
    nb, rg = S_ATT_BLOCK, ROW_GROUP
    rows = MEM_HEADS * rg

    def row_ids(width):
        r = lax.broadcasted_iota(jnp.int32, (rows, width), 0)
        return r // rg, (r % rg) // DEC_SEQ

    row_h, row_seq = row_ids(D_MODEL)
    col_h = lax.broadcasted_iota(jnp.int32, (rows, D_MODEL), 1) // MEM_HEAD_DIM
    q = q_ref[...].astype(f32)
    qh = jnp.where(row_h == col_h, jnp.concatenate([q] * MEM_HEADS, axis=0), 0.0)
    lhs_s = jnp.concatenate([jnp.where(row_seq == b, qh, 0.0).astype(bf16) for b in range(nb)], axis=1)
    kcat = jnp.concatenate([_kv_seq(k_ref, b) for b in range(nb)], axis=1)
    s = _dot_nt(lhs_s, kcat)
    p = jnp.exp(s - jnp.max(s, axis=-1, keepdims=True))
    p = p / jnp.sum(p, axis=-1, keepdims=True)
    _, row_seq_p = row_ids(N_MEM)
    lhs_p = jnp.concatenate([jnp.where(row_seq_p == b, p, 0.0).astype(bf16) for b in range(nb)], axis=1)
    vcat = jnp.concatenate([_kv_seq(v_ref, b) for b in range(nb)], axis=0)
    res = _dot(lhs_p, vcat)
    for hd in range(MEM_HEADS):
        hl = slice(hd * MEM_HEAD_DIM, (hd + 1) * MEM_HEAD_DIM)
        o_ref[:, hl] = res[hd * rg:(hd + 1) * rg, hl].astype(bf16)


def _sattn(q, mem_k, mem_v):
    ns = mem_k.shape[0]
    nb = S_ATT_BLOCK
    qblk = pl.BlockSpec((nb * DEC_SEQ, D_MODEL), lambda i: (i, 0))
    kvblk = pl.BlockSpec((nb, N_MEM * KV_ROWS_PER_TOKEN, LANES), lambda i: (i, 0, 0))
    return pl.pallas_call(
        _sattn_kernel,
        grid=(ns // nb,),
        in_specs=[qblk, kvblk, kvblk],
        out_specs=qblk,
        out_shape=jax.ShapeDtypeStruct((ns * DEC_SEQ, D_MODEL), bf16),
        compiler_params=pltpu.CompilerParams(dimension_semantics=("arbitrary",), vmem_limit_bytes=VMEM_LIMIT_BYTES),
        name="sattn",
    )(q, mem_k, mem_v)


def _sffn_kernel(x1_ref, ao_ref, sffn_ref, wo_ref, g_ref, wup_ref, cw_ref, cb_ref, wdn_ref, gfin_ref,
                 y_ref, st_ref, h_scr, act_scr):
    nb = x1_ref.shape[0]
    x2 = _time_major(x1_ref, D_MODEL) + _dot(_time_major(ao_ref, D_MODEL), wo_ref[...])
    h_scr[...] = _rmsnorm(x2, g_ref[...]).astype(bf16)
    u = _dot(h_scr[...], wup_ref[...])

    def conv_block(blk):
        cols = slice(blk * FF_CHUNK, (blk + 1) * FF_CHUNK)
        nat = blk * FF_CHUNK
        slots = [sffn_ref[:, i * 2 * D_FF + nat:i * 2 * D_FF + nat + FF_CHUNK] for i in range(FFN_CONV - 1)]
        slots += [u[l * nb:(l + 1) * nb, cols] for l in range(DEC_SEQ)]
        for i in range(FFN_CONV - 1):
            st_ref[:, i * 2 * D_FF + nat:i * 2 * D_FF + nat + FF_CHUNK] = slots[DEC_SEQ + i]
        outs = []
        for l in range(DEC_SEQ):
            acc = cb_ref[:, cols] + cw_ref[0:1, cols] * slots[l]
            for k in range(1, FFN_CONV):
                acc = acc + cw_ref[k:k + 1, cols] * slots[l + k]
            outs.append(acc)
        return jnp.concatenate(outs, axis=0)

    for j in range(N_FF_CHUNKS):
        act = _silu(conv_block(_gate_blk(j))) * conv_block(_val_blk(j))
        act_scr[:, j * FF_CHUNK:(j + 1) * FF_CHUNK] = act.astype(bf16)
    y = _rmsnorm(x2 + _dot(act_scr[...], wdn_ref[...]), gfin_ref[...])
    for l in range(DEC_SEQ):
        y_ref[:, l * D_MODEL:(l + 1) * D_MODEL] = y[l * nb:(l + 1) * nb]


def _sffn(x1, ao, sffn, w):
    ns = x1.shape[0]
    nb = S_SEQ_BLOCK
    rows = lambda width: pl.BlockSpec((nb, width), lambda i: (i, 0))
    return pl.pallas_call(
        _sffn_kernel,
        grid=(ns // nb,),
        in_specs=[rows(DEC_SEQ * D_MODEL), rows(DEC_SEQ * D_MODEL), rows((FFN_CONV - 1) * 2 * D_FF),
                  _const_spec((D_MODEL, D_MODEL)), _const_spec((1, D_MODEL)), _const_spec((D_MODEL, 2 * D_FF)),
                  _const_spec((FFN_CONV, 2 * D_FF)), _const_spec((1, 2 * D_FF)),
                  _const_spec((D_FF, D_MODEL)), _const_spec((1, D_MODEL))],
        out_specs=[rows(DEC_SEQ * D_MODEL), rows((FFN_CONV - 1) * 2 * D_FF)],
        out_shape=[jax.ShapeDtypeStruct((ns, DEC_SEQ * D_MODEL), f32),
                   jax.ShapeDtypeStruct((ns, (FFN_CONV - 1) * 2 * D_FF), f32)],
        scratch_shapes=[pltpu.VMEM((DEC_SEQ * nb, D_MODEL), bf16), pltpu.VMEM((DEC_SEQ * nb, D_FF), bf16)],
        compiler_params=pltpu.CompilerParams(dimension_semantics=("arbitrary",), vmem_limit_bytes=VMEM_LIMIT_BYTES),
        name="sffn",
    )(x1, ao, sffn, w["wo"], w["nffn"], w["wup"], w["fcw"], w["fcb"], w["wdn"], w["gfin"])


def _sample_path(x_sample, state_ssm, state_ssm_conv, state_pool, state_ffn_conv, cache_mem_k, cache_mem_v, w):
    ns = x_sample.shape[0]
    x = x_sample.reshape(ns, DEC_SEQ * D_MODEL)
    z, ypart, eacs, xd, cmat, bmat, dec, pout, conv_new, pool_new = _smix_in(
        x, state_ssm_conv[0].reshape(ns, -1), state_pool[0].reshape(ns, -1), w)
    tok = lambda a, width: a.reshape(ns * DEC_SEQ, width)
    yoff, ssm_new = _sssd(dec[:, :SSM_HEADS].reshape(-1), tok(cmat, D_BC), tok(bmat, D_BC), tok(xd, D_SSM),
                          state_ssm[0].reshape(ns, D_SSM, SSM_STATE))
    x1, q = _smix_out(tok(x, D_MODEL), tok(ypart, D_SSM), yoff, tok(eacs, D_SSM), tok(z, D_SSM), tok(pout, D_POOL), w)
    ao = _sattn(q, _kv_rows_view(cache_mem_k[0]), _kv_rows_view(cache_mem_v[0]))
    y, ffn_new = _sffn(x1.reshape(ns, -1), ao.reshape(ns, -1), state_ffn_conv[0].reshape(ns, -1), w)
    return (y.reshape(ns, DEC_SEQ, D_MODEL), ssm_new.reshape(1, ns, SSM_HEADS, SSM_HEAD_DIM, SSM_STATE),
            conv_new.reshape(1, ns, SSM_CONV - 1, D_XBC), pool_new.reshape(1, ns, POOL_HIST, D_POOL),
            ffn_new.reshape(1, ns, FFN_CONV - 1, 2 * D_FF))


def kernel(x_prompt, x_sample, mem_prompt, state_ssm, state_ssm_conv, state_pool, state_ffn_conv, cache_mem_k, cache_mem_v, norm_mix, w_in, ssm_conv_w, ssm_conv_b, ssm_dt_bias, ssm_a_log, ssm_d, ssm_norm, w_pool, pool_scale, w_out, norm_mem, norm_memkv, w_mq, w_mk, w_mv, w_mo, norm_ffn, w_up, ffn_conv_w, ffn_conv_b, w_down, final_norm):
    params = dict(norm_mix=norm_mix, w_in=w_in, ssm_conv_w=ssm_conv_w, ssm_conv_b=ssm_conv_b, ssm_dt_bias=ssm_dt_bias,
                  ssm_a_log=ssm_a_log, ssm_d=ssm_d, ssm_norm=ssm_norm, w_pool=w_pool, pool_scale=pool_scale,
                  w_out=w_out, norm_mem=norm_mem, norm_memkv=norm_memkv, w_mq=w_mq, w_mk=w_mk, w_mv=w_mv, w_mo=w_mo,
                  norm_ffn=norm_ffn, w_up=w_up, ffn_conv_w=ffn_conv_w, ffn_conv_b=ffn_conv_b, w_down=w_down,
                  final_norm=final_norm)
    w = _prep_weights(params)
    yp, ssm_p, conv_p, pool_p, ffn_p, mk_p, mv_p = _prompt_path(x_prompt, mem_prompt, w, PROMPT_TILE)
    ys, ssm_s, conv_s, pool_s, ffn_s = _sample_path(x_sample, state_ssm, state_ssm_conv, state_pool, state_ffn_conv,
                                                    cache_mem_k, cache_mem_v, w)
    return yp, ys, ssm_p, ssm_s, conv_p, conv_s, pool_p, pool_s, ffn_p, ffn_s, mk_p, mv_p
```

```python
import functools

import jax
import jax.numpy as jnp
from jax import lax
from jax.experimental import pallas as pl
from jax.experimental.pallas import tpu as pltpu

f32 = jnp.float32
bf16 = jnp.bfloat16

D_MODEL = 1024
SSM_HEADS = 16
SSM_HEAD_DIM = 64
SSM_STATE = 128
SSM_GROUPS = 2
SSM_CHUNK = 128
D_SSM = 1024
D_BC = SSM_GROUPS * SSM_STATE
D_XBC = D_SSM + 2 * D_BC
SSM_CONV = 4
D_POOL = 1024
POOL_WINDOWS = (2, 4, 8, 16)
POOL_GROUP_DIM = 256
POOL_HIST = 15
N_MEM = 256
MEM_HEADS = 4
MEM_HEAD_DIM = 256
D_FF = 2816
FFN_CONV = 3
EPS = 1e-6
PAST_LEN = 16384

LANES = 128
SUBLANES = 8
MXU_DIM = 256
VMEM_LIMIT_BYTES = 56 * 1024 * 1024

COL_Z = 0
COL_XBC = D_SSM
COL_VP = COL_XBC + D_XBC
COL_DT = COL_VP + D_POOL
D_PROJ = COL_DT + LANES

CONV_HIST_ROWS = SUBLANES
POOL_HIST_ROWS = 2 * SUBLANES
FF_CHUNK = MXU_DIM
N_FF_CHUNKS = D_FF // FF_CHUNK


def _silu(v):
    return v * (1.0 / (1.0 + jnp.exp(-v)))


def _softplus(v):
    return jnp.maximum(v, 0.0) + jnp.log1p(jnp.exp(-jnp.abs(v)))


def _rmsnorm(x, g):
    ms = jnp.mean(x * x, axis=-1, keepdims=True)
    return x * lax.rsqrt(ms + EPS) * g


def _dot(a, b):
    return jnp.dot(a, b, preferred_element_type=f32)


def _dot_nt(a, b):
    return lax.dot_general(a, b, (((1,), (1,)), ((), ())), preferred_element_type=f32)


def _dot_tn(a, b):
    return lax.dot_general(a, b, (((0,), (0,)), ((), ())), preferred_element_type=f32)


def _split3(v):
    p1 = v.astype(bf16)
    r1 = v - p1.astype(f32)
    p2 = r1.astype(bf16)
    r2 = r1 - p2.astype(f32)
    return p1, p2, r2.astype(bf16)


def _const_spec(shape):
    return pl.BlockSpec(shape, lambda *_: (0,) * len(shape), pipeline_mode=pl.Buffered(1))


KV_LANE_TILES = MEM_HEAD_DIM // LANES
KV_ROWS_PER_TOKEN = KV_LANE_TILES * MEM_HEADS


def _kv_rows_view(kv):
    ns = kv.shape[0]
    kv = kv.reshape(ns, N_MEM, MEM_HEADS, KV_LANE_TILES, LANES).transpose(0, 1, 3, 2, 4)
    return kv.reshape(ns, N_MEM * KV_ROWS_PER_TOKEN, LANES)


def _kv_from_rows(rows):
    ns = rows.shape[0]
    kv = rows.reshape(ns, N_MEM, KV_LANE_TILES, MEM_HEADS, LANES).transpose(0, 1, 3, 2, 4)
    return kv.reshape(ns, N_MEM, MEM_HEADS, MEM_HEAD_DIM)


def _kv_seq(ref, jj):
    tiles = [ref[jj, pl.ds(dt * MEM_HEADS + hd, N_MEM, stride=KV_ROWS_PER_TOKEN), :]
             for hd in range(MEM_HEADS) for dt in range(KV_LANE_TILES)]
    return jnp.concatenate(tiles, axis=1).astype(bf16)


def _memkv_kernel(mem_ref, g_ref, wk_ref, wv_ref, k_ref, v_ref):
    h = _rmsnorm(mem_ref[0], g_ref[...]).astype(bf16)
    for out_ref, w_ref in ((k_ref, wk_ref), (v_ref, wv_ref)):
        kv = _dot(h, w_ref[...])
        for hd in range(MEM_HEADS):
            for dt in range(KV_LANE_TILES):
                col = hd * MEM_HEAD_DIM + dt * LANES
                out_ref[0, pl.ds(dt * MEM_HEADS + hd, N_MEM, stride=KV_ROWS_PER_TOKEN), :] = kv[:, col:col + LANES]


def _memkv(mem, g, wk, wv):
    b = mem.shape[0]
    blk = pl.BlockSpec((1, N_MEM, D_MODEL), lambda i: (i, 0, 0))
    oblk = pl.BlockSpec((1, N_MEM * KV_ROWS_PER_TOKEN, LANES), lambda i: (i, 0, 0))
    return pl.pallas_call(
        _memkv_kernel,
        grid=(b,),
        in_specs=[blk, _const_spec((1, D_MODEL)), _const_spec((D_MODEL, D_MODEL)), _const_spec((D_MODEL, D_MODEL))],
        out_specs=[oblk, oblk],
        out_shape=[jax.ShapeDtypeStruct((b, N_MEM * KV_ROWS_PER_TOKEN, LANES), f32)] * 2,
        compiler_params=pltpu.CompilerParams(dimension_semantics=("arbitrary",), vmem_limit_bytes=VMEM_LIMIT_BYTES),
        name="memkv",
    )(mem, g, wk, wv)


def _ssd_chunk(r0, dt_scr, xs_scr, b_scr, c_scr, y_scr, xd_scr, hst_scr, dtb_ref, a_ref, dexp_ref):
    q = SSM_CHUNK
    rows = pl.ds(r0, q)
    row_i = lax.broadcasted_iota(jnp.int32, (q, q), 0)
    col_i = lax.broadcasted_iota(jnp.int32, (q, q), 1)
    causal = col_i <= row_i
    lo = col_i < SSM_HEAD_DIM
    tril = jnp.where(causal, 1.0, 0.0).astype(bf16)

    dt = _softplus(dt_scr[rows, :] + dtb_ref[...])
    da = dt * a_ref[...]
    p1, p2, p3 = _split3(da)
    acs = _dot(tril, p1) + _dot(tril, p2) + _dot(tril, p3)
    acs_t = acs.T

    for g in range(SSM_GROUPS):
        bg = b_scr[rows, g * SSM_STATE:(g + 1) * SSM_STATE]
        cg = c_scr[rows, g * SSM_STATE:(g + 1) * SSM_STATE]
        bg_b = bg.astype(bf16)
        cb = jnp.where(causal, _dot_nt(cg.astype(bf16), bg_b), 0.0)
        cdec_rows = []
        pairs_per_group = SSM_HEADS // SSM_GROUPS // 2
        for pp in range(pairs_per_group):
            pr = g * pairs_per_group + pp
            lanes = slice(pr * LANES, (pr + 1) * LANES)
            lhs, dtb, dend, cdec = [], [], [], []
            for hh in (2 * pr, 2 * pr + 1):
                colb = jnp.broadcast_to(acs[:, hh:hh + 1], (q, q))
                seg = jnp.where(causal, colb - acs_t[hh:hh + 1, :], 0.0)
                lhs.append((jnp.exp(seg) * cb).astype(bf16))
                lhs.append((cg * jnp.exp(colb)).astype(bf16))
                last = colb[q - 1:q, :]
                dend.append(jnp.exp(last - colb))
                cdec.append(jnp.exp(last))
                dtb.append(jnp.broadcast_to(dt[:, hh:hh + 1], (q, q)))
            xs_pair = xs_scr[rows, lanes]
            xdt = xs_pair * jnp.where(lo, dtb[0], dtb[1])
            xd_scr[:, lanes] = (xdt * jnp.where(lo, dend[0], dend[1])).astype(bf16)
            hst_pair = hst_scr[:, lanes]
            rhs = jnp.concatenate([
                jnp.where(lo, xdt, 0.0).astype(bf16), jnp.where(lo, hst_pair, 0.0).astype(bf16),
                jnp.where(lo, 0.0, xdt).astype(bf16), jnp.where(lo, 0.0, hst_pair).astype(bf16)], axis=0)
            y_pair = _dot(jnp.concatenate(lhs, axis=1), rhs)
            y_scr[rows, lanes] = y_pair + xs_pair * dexp_ref[:, lanes]
            cdec_rows.append(jnp.where(lo[:1], cdec[0], cdec[1]))
        gl = slice(g * (D_SSM // SSM_GROUPS), (g + 1) * (D_SSM // SSM_GROUPS))
        upd = _dot_tn(bg_b, xd_scr[:, gl])
        hst_scr[:, gl] = hst_scr[:, gl] * jnp.concatenate(cdec_rows, axis=1) + upd


def _mixer_kernel(x_ref, nmix_ref, win_ref, cw_ref, cb_ref, dtb_ref, a_ref, dexp_ref, snorm_ref, wpool_ref,
                  pscale_ref, wout_ref,
                  x1_ref, ssm_ref, conv_ref, pool_ref,
                  xd_scr, hst_scr, *sub_scr, tile, sub):
    t = pl.program_id(1)
    last_t = pl.num_programs(1) - 1
    ch = CONV_HIST_ROWS
    ph = POOL_HIST_ROWS
    n_sub = tile // sub
    per = len(sub_scr) // n_sub
    bufs = [sub_scr[i * per:(i + 1) * per] for i in range(n_sub)]
    xbc0, vp0 = bufs[0][2], bufs[0][3]

    @pl.when(t == 0)
    def _():
        xbc0[:, 0:ch, :] = jnp.zeros((D_XBC // LANES, ch, LANES), f32)
        vp0[:, 0:ph, :] = jnp.zeros((D_POOL // LANES, ph, LANES), f32)
        hst_scr[...] = jnp.zeros_like(hst_scr)

    gw = D_SSM // SSM_GROUPS

    def carry_history(src, dst):
        for j in range(D_XBC // LANES):
            bufs[dst][2][j, 0:ch, :] = bufs[src][2][j, sub:sub + ch, :]
        for j in range(D_POOL // LANES):
            bufs[dst][3][j, 0:ph, :] = bufs[src][3][j, sub:sub + ph, :]

    def stage_in(s):
        h_scr, z_scr, xbc_scr, vp_scr, dt_scr = bufs[s][:5]
        h_scr[...] = _rmsnorm(x_ref[0, s * sub:(s + 1) * sub, :], nmix_ref[...]).astype(bf16)
        xbc = _dot(h_scr[...], win_ref[:, COL_XBC:COL_VP])
        for j in range(D_XBC // LANES):
            xbc_scr[j, ch:ch + sub, :] = xbc[:, j * LANES:(j + 1) * LANES]
        dt_scr[...] = _dot(h_scr[...], win_ref[:, COL_DT:D_PROJ])
        vp = _dot(h_scr[...], win_ref[:, COL_VP:COL_DT])
        for j in range(D_POOL // LANES):
            vp_scr[j, ph:ph + sub, :] = vp[:, j * LANES:(j + 1) * LANES]
        z_scr[...] = _dot(h_scr[...], win_ref[:, COL_Z:COL_XBC])
        if s > 0:
            carry_history(s - 1, s)

    def stage_mid(s):
        _, z_scr, xbc_scr, vp_scr, dt_scr, xs_scr, b_scr, c_scr, y_scr, pooled_scr, cat_scr = bufs[s]
        for j in range(D_XBC // LANES):
            cl = slice(j * LANES, (j + 1) * LANES)
            acc = cb_ref[:, cl] + cw_ref[0:1, cl] * xbc_scr[j, pl.ds(ch - 3, sub), :]
            for k in range(1, SSM_CONV):
                acc = acc + cw_ref[k:k + 1, cl] * xbc_scr[j, pl.ds(ch - 3 + k, sub), :]
            act = _silu(acc)
            if j < D_SSM // LANES:
                xs_scr[:, cl] = act
            elif j < (D_SSM + D_BC) // LANES:
                b_scr[:, j * LANES - D_SSM:(j + 1) * LANES - D_SSM] = act
            else:
                c_scr[:, j * LANES - D_SSM - D_BC:(j + 1) * LANES - D_SSM - D_BC] = act

        pos1 = lax.broadcasted_iota(jnp.int32, (sub, LANES), 0) + (t * tile + s * sub + 1)
        for j in range(D_POOL // LANES):
            cl = slice(j * LANES, (j + 1) * LANES)
            w = POOL_WINDOWS[j * LANES // POOL_GROUP_DIM]
            cur = vp_scr[j, ph:ph + sub, :]
            acc = cur
            for k in range(1, w):
                acc = acc + vp_scr[j, pl.ds(ph - k, sub), :]
            cnt = jnp.minimum(pos1, w).astype(f32)
            pooled_scr[:, cl] = (acc / cnt - cur).astype(bf16)
        for gi in range(len(POOL_WINDOWS)):
            gl = slice(gi * POOL_GROUP_DIM, (gi + 1) * POOL_GROUP_DIM)
            pg = _dot(pooled_scr[:, gl], wpool_ref[gi]) * pscale_ref[:, gl]
            cat_scr[:, D_SSM + gi * POOL_GROUP_DIM:D_SSM + (gi + 1) * POOL_GROUP_DIM] = pg.astype(bf16)

        for c in range(sub // SSM_CHUNK):
            _ssd_chunk(c * SSM_CHUNK, dt_scr, xs_scr, b_scr, c_scr, y_scr, xd_scr, hst_scr, dtb_ref, a_ref, dexp_ref)

        for g in range(SSM_GROUPS):
            gl = slice(g * gw, (g + 1) * gw)
            tg = y_scr[:, gl] * _silu(z_scr[:, gl])
            ms = jnp.mean(tg * tg, axis=-1, keepdims=True)
            cat_scr[:, gl] = (tg * lax.rsqrt(ms + EPS) * snorm_ref[:, gl]).astype(bf16)

    def stage_out(s):
        rows = slice(s * sub, (s + 1) * sub)
        x1_ref[0, rows, :] = x_ref[0, rows, :] + _dot(bufs[s][-1][...], wout_ref[...])

    stage_in(0)
    for s in range(n_sub):
        if s + 1 < n_sub:
            stage_in(s + 1)
        stage_mid(s)
        stage_out(s)
    carry_history(n_sub - 1, 0)

    @pl.when(t == last_t)
    def _():
        for pr in range(D_SSM // LANES):
            ssm_ref[0, pr * LANES:(pr + 1) * LANES, :] = hst_scr[:, pr * LANES:(pr + 1) * LANES].T
        for j in range(D_XBC // LANES):
            conv_ref[0, :, j * LANES:(j + 1) * LANES] = xbc0[j, pl.ds(ch - (SSM_CONV - 1), SSM_CONV - 1), :]
        for j in range(D_POOL // LANES):
            pool_ref[0, :, j * LANES:(j + 1) * LANES] = vp0[j, pl.ds(ph - POOL_HIST, POOL_HIST), :]


def _mixer_prompt(x, nmix, win, cw, cb, dtb, a_row, dexp, snorm, wpool, pscale, wout, tile):
    b, seq, _ = x.shape
    nt = seq // tile
    sub = min(tile, MIXER_SUB_TILE)
    xblk = pl.BlockSpec((1, tile, D_MODEL), lambda i, j: (i, j, 0))
    sub_scratch = [
        pltpu.VMEM((sub, D_MODEL), bf16),
        pltpu.VMEM((sub, D_SSM), f32),
        pltpu.VMEM((D_XBC // LANES, CONV_HIST_ROWS + sub, LANES), f32),
        pltpu.VMEM((D_POOL // LANES, POOL_HIST_ROWS + sub, LANES), f32),
        pltpu.VMEM((sub, LANES), f32),
        pltpu.VMEM((sub, D_SSM), f32),
        pltpu.VMEM((sub, D_BC), f32),
        pltpu.VMEM((sub, D_BC), f32),
        pltpu.VMEM((sub, D_SSM), f32),
        pltpu.VMEM((sub, D_POOL), bf16),
        pltpu.VMEM((sub, D_SSM + D_POOL), bf16),
    ]
    scratch = [pltpu.VMEM((SSM_CHUNK, D_SSM), bf16),
               pltpu.VMEM((SSM_STATE, D_SSM), f32)]
    scratch += sub_scratch * (tile // sub)
    return pl.pallas_call(
        functools.partial(_mixer_kernel, tile=tile, sub=sub),
        grid=(b, nt),
        in_specs=[xblk, _const_spec((1, D_MODEL)), _const_spec((D_MODEL, D_PROJ)), _const_spec((SSM_CONV, D_XBC)),
                  _const_spec((1, D_XBC)), _const_spec((1, LANES)), _const_spec((1, LANES)), _const_spec((1, D_SSM)),
                  _const_spec((1, D_SSM)), _const_spec((len(POOL_WINDOWS), POOL_GROUP_DIM, POOL_GROUP_DIM)),
                  _const_spec((1, D_POOL)), _const_spec((D_SSM + D_POOL, D_MODEL))],
        out_specs=[xblk,
                   pl.BlockSpec((1, D_SSM, SSM_STATE), lambda i, j: (i, 0, 0)),
                   pl.BlockSpec((1, SSM_CONV - 1, D_XBC), lambda i, j: (i, 0, 0)),
                   pl.BlockSpec((1, POOL_HIST, D_POOL), lambda i, j: (i, 0, 0))],
        out_shape=[jax.ShapeDtypeStruct((b, seq, D_MODEL), f32),
                   jax.ShapeDtypeStruct((b, D_SSM, SSM_STATE), f32),
                   jax.ShapeDtypeStruct((b, SSM_CONV - 1, D_XBC), f32),
                   jax.ShapeDtypeStruct((b, POOL_HIST, D_POOL), f32)],
        scratch_shapes=scratch,
        compiler_params=pltpu.CompilerParams(dimension_semantics=("arbitrary", "arbitrary"),
                                             vmem_limit_bytes=VMEM_LIMIT_BYTES),
        name="mixer_prompt",
    )(x, nmix, win, cw, cb, dtb, a_row, dexp, snorm, wpool, pscale, wout)


def _attn_kernel(x_ref, g_ref, wq_ref, k_ref, v_ref, wo_ref, o_ref, q_scr, ao_scr):
    x = x_ref[0]
    h = _rmsnorm(x, g_ref[...]).astype(bf16)
    q_scr[...] = (_dot(h, wq_ref[...]) * (MEM_HEAD_DIM ** -0.5)).astype(bf16)
    k = _kv_seq(k_ref, 0)
    v = _kv_seq(v_ref, 0)
    for hd in range(MEM_HEADS):
        hl = slice(hd * MEM_HEAD_DIM, (hd + 1) * MEM_HEAD_DIM)
        s = _dot_nt(q_scr[:, hl], k[:, hl])
        p = jnp.exp(s - jnp.max(s, axis=-1, keepdims=True))
        p = (p / jnp.sum(p, axis=-1, keepdims=True)).astype(bf16)
        ao_scr[:, hl] = _dot(p, v[:, hl]).astype(bf16)
    o_ref[0] = x + _dot(ao_scr[...], wo_ref[...])


def _attn_prompt(x, g, wq, mem_k, mem_v, wo, tile):
    b, seq, _ = x.shape
    xblk = pl.BlockSpec((1, tile, D_MODEL), lambda i, j: (i, j, 0))
    kvblk = pl.BlockSpec((1, N_MEM * KV_ROWS_PER_TOKEN, LANES), lambda i, j: (i, 0, 0))
    return pl.pallas_call(
        _attn_kernel,
        grid=(b, seq // tile),
        in_specs=[xblk, _const_spec((1, D_MODEL)), _const_spec((D_MODEL, D_MODEL)), kvblk, kvblk,
                  _const_spec((D_MODEL, D_MODEL))],
        out_specs=xblk,
        out_shape=jax.ShapeDtypeStruct((b, seq, D_MODEL), f32),
        scratch_shapes=[pltpu.VMEM((tile, D_MODEL), bf16), pltpu.VMEM((tile, D_MODEL), bf16)],
        compiler_params=pltpu.CompilerParams(dimension_semantics=("arbitrary", "arbitrary"),
                                             vmem_limit_bytes=VMEM_LIMIT_BYTES),
        name="attn_prompt",
    )(x, g, wq, mem_k, mem_v, wo)


def _gate_blk(j):
    return j


def _val_blk(j):
    return N_FF_CHUNKS + j


def _ffn_kernel(x_ref, g_ref, wup_ref, cw_ref, cb_ref, wdn_ref, gfin_ref, y_ref, st_ref, h_scr, u_scr, act_scr, *, tile):
    t = pl.program_id(1)
    last_t = pl.num_programs(1) - 1
    ch = CONV_HIST_ROWS
    tpc = FF_CHUNK // LANES
    ntile = 2 * D_FF // LANES

    @pl.when(t == 0)
    def _():
        u_scr[:, 0:ch, :] = jnp.zeros((ntile, ch, LANES), f32)

    x = x_ref[0]
    h_scr[...] = _rmsnorm(x, g_ref[...]).astype(bf16)
    u = _dot(h_scr[...], wup_ref[...])
    for ti in range(ntile):
        u_scr[ti, ch:ch + tile, :] = u[:, ti * LANES:(ti + 1) * LANES]

    def conv(ti):
        cl = slice(ti * LANES, (ti + 1) * LANES)
        acc = cb_ref[:, cl] + cw_ref[0:1, cl] * u_scr[ti, pl.ds(ch - 2, tile), :]
        acc = acc + cw_ref[1:2, cl] * u_scr[ti, pl.ds(ch - 1, tile), :]
        return acc + cw_ref[2:3, cl] * u_scr[ti, ch:ch + tile, :]

    for j in range(N_FF_CHUNKS):
        for i in range(tpc):
            gate = conv(_gate_blk(j) * tpc + i)
            val = conv(_val_blk(j) * tpc + i)
            act_scr[:, (j * tpc + i) * LANES:(j * tpc + i + 1) * LANES] = (_silu(gate) * val).astype(bf16)
    for ti in range(ntile):
        u_scr[ti, 0:ch, :] = u_scr[ti, tile:tile + ch, :]
    y_ref[0] = _rmsnorm(x + _dot(act_scr[...], wdn_ref[...]), gfin_ref[...])

    @pl.when(t == last_t)
    def _():
        for ti in range(ntile):
            st_ref[0, :, ti * LANES:(ti + 1) * LANES] = u_scr[ti, pl.ds(ch - (FFN_CONV - 1), FFN_CONV - 1), :]


def _ffn_prompt(x, g, wup, cw, cb, wdn, gfin, tile):
    b, seq, _ = x.shape
    xblk = pl.BlockSpec((1, tile, D_MODEL), lambda i, j: (i, j, 0))
    return pl.pallas_call(
        functools.partial(_ffn_kernel, tile=tile),
        grid=(b, seq // tile),
        in_specs=[xblk, _const_spec((1, D_MODEL)), _const_spec((D_MODEL, 2 * D_FF)),
                  _const_spec((FFN_CONV, 2 * D_FF)), _const_spec((1, 2 * D_FF)),
                  _const_spec((D_FF, D_MODEL)), _const_spec((1, D_MODEL))],
        out_specs=[xblk, pl.BlockSpec((1, FFN_CONV - 1, 2 * D_FF), lambda i, j: (i, 0, 0))],
        out_shape=[jax.ShapeDtypeStruct((b, seq, D_MODEL), f32),
                   jax.ShapeDtypeStruct((b, FFN_CONV - 1, 2 * D_FF), f32)],
        scratch_shapes=[pltpu.VMEM((tile, D_MODEL), bf16),
                        pltpu.VMEM((2 * D_FF // LANES, CONV_HIST_ROWS + tile, LANES), f32),
                        pltpu.VMEM((tile, D_FF), bf16)],
        compiler_params=pltpu.CompilerParams(dimension_semantics=("arbitrary", "arbitrary"),
                                             vmem_limit_bytes=VMEM_LIMIT_BYTES),
        name="ffn_prompt",
    )(x, g, wup, cw, cb, wdn, gfin)


def _prep_weights(p):
    i = 0
    w_in = p["w_in"][i]
    dt_cols = jnp.pad(w_in[:, D_SSM + D_XBC:D_SSM + D_XBC + SSM_HEADS], ((0, 0), (0, LANES - SSM_HEADS)))
    win = jnp.concatenate([w_in[:, :D_SSM + D_XBC], w_in[:, D_SSM + D_XBC + SSM_HEADS:], dt_cols], axis=1).astype(bf16)
    pad_h = (0, LANES - SSM_HEADS)
    return dict(
        nmix=p["norm_mix"][i][None], win=win, cw=p["ssm_conv_w"][i], cb=p["ssm_conv_b"][i][None],
        dtb=jnp.pad(p["ssm_dt_bias"][i], pad_h)[None],
        a_row=jnp.pad(-jnp.exp(p["ssm_a_log"][i].astype(f32)), pad_h)[None],
        dexp=jnp.repeat(p["ssm_d"][i], SSM_HEAD_DIM)[None], snorm=p["ssm_norm"][i][None],
        wpool=p["w_pool"][i].astype(bf16), pscale=p["pool_scale"][i][None], wout=p["w_out"][i].astype(bf16),
        nmem=p["norm_mem"][i][None], nmemkv=p["norm_memkv"][i][None],
        wq=p["w_mq"][i].astype(bf16), wk=p["w_mk"][i].astype(bf16), wv=p["w_mv"][i].astype(bf16),
        wo=p["w_mo"][i].astype(bf16),
        nffn=p["norm_ffn"][i][None],
        wup=p["w_up"][i].astype(bf16), fcw=p["ffn_conv_w"][i], fcb=p["ffn_conv_b"][i][None],
        wdn=p["w_down"][i].astype(bf16),
        gfin=p["final_norm"][None],
    )


def _prompt_path(x_prompt, mem_prompt, w, tile):
    b = x_prompt.shape[0]
    mem_k, mem_v = _memkv(mem_prompt, w["nmemkv"], w["wk"], w["wv"])
    x1, ssm, conv, pool = _mixer_prompt(x_prompt, w["nmix"], w["win"], w["cw"], w["cb"], w["dtb"], w["a_row"],
                                        w["dexp"], w["snorm"], w["wpool"], w["pscale"], w["wout"], tile)
    x2 = _attn_prompt(x1, w["nmem"], w["wq"], mem_k, mem_v, w["wo"], tile)
    y, ffn = _ffn_prompt(x2, w["nffn"], w["wup"], w["fcw"], w["fcb"], w["wdn"], w["gfin"], tile)
    return (y, ssm.reshape(1, b, SSM_HEADS, SSM_HEAD_DIM, SSM_STATE), conv[None], pool[None], ffn[None],
            _kv_from_rows(mem_k)[None], _kv_from_rows(mem_v)[None])


PROMPT_TILE = 512
MIXER_SUB_TILE = 512


DEC_SEQ = 4
S_SEQ_BLOCK = 32
S_SSD_BLOCK = 8
S_ATT_BLOCK = 8
X_ROWS_PER_SEQ = DEC_SEQ * D_MODEL // LANES
FFN_ROWS_PER_SEQ = (FFN_CONV - 1) * 2 * D_FF // LANES


def _expand_heads(v, lo):
    r = v.shape[0]
    tiles = []
    for pr in range(SSM_HEADS // 2):
        a = jnp.broadcast_to(v[:, 2 * pr:2 * pr + 1], (r, LANES))
        b = jnp.broadcast_to(v[:, 2 * pr + 1:2 * pr + 2], (r, LANES))
        tiles.append(jnp.where(lo, a, b))
    return jnp.concatenate(tiles, axis=1)


def _tiled_rows_view(a):
    ns, r, c = a.shape
    return a.reshape(ns, r, c // LANES, LANES).transpose(0, 2, 1, 3).reshape(ns * (c // LANES) * r, LANES)


def _from_tiled_rows(rows, ns, r, c):
    return rows.reshape(ns, c // LANES, r, LANES).transpose(0, 2, 1, 3).reshape(ns, r, c)


def _rows_view_get(ref, i, r, tiles, nb, per_seq):
    return jnp.concatenate([ref[pl.ds(dt * r + i, nb, stride=per_seq), :] for dt in tiles], axis=1)


def _rows_view_put(ref, i, r, tiles, nb, per_seq, val):
    for n, dt in enumerate(tiles):
        ref[pl.ds(dt * r + i, nb, stride=per_seq), :] = val[:, n * LANES:(n + 1) * LANES]


def _steps(ref):
    return ref[...].reshape(DEC_SEQ * ref.shape[1], ref.shape[2])


def _smix_in_kernel(x_ref, sconv_ref, spool_ref, nmix_ref, win_ref, cw_ref, cb_ref, dtb_ref, a_ref, dexp_ref,
                    wpool_ref, pscale_ref,
                    xtm_ref, z_ref, ypart_ref, eacs_ref, xd_ref, c_ref, b_ref, dec_ref, pout_ref, conv_ref, pool_ref,
                    h_scr, pooled_scr):
    nb = S_SEQ_BLOCK
    lo = lax.broadcasted_iota(jnp.int32, (nb, LANES), 1) < SSM_HEAD_DIM
    x_tiles = range(D_MODEL // LANES)
    x_steps = [_rows_view_get(x_ref, l, DEC_SEQ, x_tiles, nb, X_ROWS_PER_SEQ) for l in range(DEC_SEQ)]
    for l in range(DEC_SEQ):
        xtm_ref[l] = x_steps[l]
    h_scr[...] = _rmsnorm(jnp.concatenate(x_steps, axis=0), nmix_ref[...]).astype(bf16)
    z = _dot(h_scr[...], win_ref[:, COL_Z:COL_XBC])
    for l in range(DEC_SEQ):
        z_ref[l] = z[l * nb:(l + 1) * nb]
    xbc = _dot(h_scr[...], win_ref[:, COL_XBC:COL_VP])
    vp = _dot(h_scr[...], win_ref[:, COL_VP:COL_DT])
    dtr = _dot(h_scr[...], win_ref[:, COL_DT:D_PROJ])

    def conv_slot(i):
        if i < SSM_CONV - 1:
            return sconv_ref[i]
        return xbc[(i - SSM_CONV + 1) * nb:(i - SSM_CONV + 2) * nb]

    def pool_slot(i, cl):
        if i < POOL_HIST:
            return spool_ref[i, :, cl]
        return vp[(i - POOL_HIST) * nb:(i - POOL_HIST + 1) * nb, cl]

    xs, bm, cm, dt, acs = [], [], [], [], []
    for l in range(DEC_SEQ):
        acc = cb_ref[...] + cw_ref[0:1, :] * conv_slot(l)
        for k in range(1, SSM_CONV):
            acc = acc + cw_ref[k:k + 1, :] * conv_slot(l + k)
        act = _silu(acc)
        xs.append(act[:, :D_SSM])
        bm.append(act[:, D_SSM:D_SSM + D_BC])
        cm.append(act[:, D_SSM + D_BC:])
        b_ref[l] = bm[l]
        c_ref[l] = cm[l]
        dt.append(_softplus(dtr[l * nb:(l + 1) * nb] + dtb_ref[...]))
        da = dt[l] * a_ref[...]
        acs.append(da if l == 0 else acs[l - 1] + da)
        for gi, w in enumerate(POOL_WINDOWS):
            gl = slice(gi * POOL_GROUP_DIM, (gi + 1) * POOL_GROUP_DIM)
            s = pool_slot(POOL_HIST + l, gl)
            for k in range(1, w):
                s = s + pool_slot(POOL_HIST + l - k, gl)
            cnt = float(min(PAST_LEN + l + 1, w))
            pooled_scr[l * nb:(l + 1) * nb, gl] = (s / cnt - pool_slot(POOL_HIST + l, gl)).astype(bf16)
    for i in range(SSM_CONV - 1):
        conv_ref[i] = conv_slot(DEC_SEQ + i)
    for i in range(POOL_HIST):
        pool_ref[i] = pool_slot(DEC_SEQ + i, slice(0, D_POOL))
    pout = jnp.concatenate(
        [_dot(pooled_scr[:, gi * POOL_GROUP_DIM:(gi + 1) * POOL_GROUP_DIM], wpool_ref[gi])
         for gi in range(len(POOL_WINDOWS))], axis=1) * pscale_ref[...]
    for l in range(DEC_SEQ):
        pout_ref[l] = pout[l * nb:(l + 1) * nb]

    xdt = [xs[l] * _expand_heads(dt[l], lo) for l in range(DEC_SEQ)]
    gw = D_SSM // SSM_GROUPS
    for l in range(DEC_SEQ):
        y = xs[l] * dexp_ref[...]
        for s in range(l + 1):
            decay = _expand_heads(jnp.exp(acs[l] - acs[s]), lo)
            cbs = [jnp.sum(cm[l][:, g * SSM_STATE:(g + 1) * SSM_STATE] * bm[s][:, g * SSM_STATE:(g + 1) * SSM_STATE],
                           axis=-1, keepdims=True) for g in range(SSM_GROUPS)]
            coef = jnp.concatenate([decay[:, g * gw:(g + 1) * gw] * cbs[g] for g in range(SSM_GROUPS)], axis=1)
            y = y + coef * xdt[s]
        ypart_ref[l] = y
        eacs_ref[l] = _expand_heads(jnp.exp(acs[l]), lo)
        xd_ref[l] = xdt[l] * _expand_heads(jnp.exp(acs[DEC_SEQ - 1] - acs[l]), lo)
    dec_ref[...] = jnp.exp(acs[DEC_SEQ - 1])


def _smix_in(x_rows, sconv, spool, w):
    ns = sconv.shape[1]
    nb = S_SEQ_BLOCK
    tmaj = lambda steps, width: pl.BlockSpec((steps, nb, width), lambda i: (0, i, 0))
    step_outs = [D_MODEL, D_SSM, D_SSM, D_SSM, D_SSM, D_BC, D_BC]
    out_specs = [tmaj(DEC_SEQ, wd) for wd in step_outs] + [pl.BlockSpec((nb, LANES), lambda i: (i, 0)),
                                                          tmaj(DEC_SEQ, D_POOL),
                                                          tmaj(SSM_CONV - 1, D_XBC), tmaj(POOL_HIST, D_POOL)]
    out_shape = [jax.ShapeDtypeStruct((DEC_SEQ, ns, wd), f32) for wd in step_outs] + [
        jax.ShapeDtypeStruct((ns, LANES), f32), jax.ShapeDtypeStruct((DEC_SEQ, ns, D_POOL), f32),
        jax.ShapeDtypeStruct((SSM_CONV - 1, ns, D_XBC), f32), jax.ShapeDtypeStruct((POOL_HIST, ns, D_POOL), f32)]
    return pl.pallas_call(
        _smix_in_kernel,
        grid=(ns // nb,),
        in_specs=[pl.BlockSpec((nb * X_ROWS_PER_SEQ, LANES), lambda i: (i, 0)), tmaj(SSM_CONV - 1, D_XBC),
                  tmaj(POOL_HIST, D_POOL),
                  _const_spec((1, D_MODEL)), _const_spec((D_MODEL, D_PROJ)), _const_spec((SSM_CONV, D_XBC)),
                  _const_spec((1, D_XBC)), _const_spec((1, LANES)), _const_spec((1, LANES)), _const_spec((1, D_SSM)),
                  _const_spec((len(POOL_WINDOWS), POOL_GROUP_DIM, POOL_GROUP_DIM)), _const_spec((1, D_POOL))],
        out_specs=out_specs,
        out_shape=out_shape,
        scratch_shapes=[pltpu.VMEM((DEC_SEQ * nb, D_MODEL), bf16), pltpu.VMEM((DEC_SEQ * nb, D_POOL), bf16)],
        compiler_params=pltpu.CompilerParams(dimension_semantics=("arbitrary",), vmem_limit_bytes=VMEM_LIMIT_BYTES),
        name="smix_in",
    )(x_rows, sconv, spool, w["nmix"], w["win"], w["cw"], w["cb"], w["dtb"], w["a_row"], w["dexp"], w["wpool"],
      w["pscale"])


def _sssd_kernel(dec_ref, c_ref, b_ref, xd_ref, st_ref, yoff_ref, stn_ref):
    blk = pl.program_id(0)
    nb = S_SSD_BLOCK
    gw = D_SSM // SSM_GROUPS
    hpg = SSM_HEADS // SSM_GROUPS
    row_seq = lax.broadcasted_iota(jnp.int32, (DEC_SEQ * nb, gw), 0) % nb
    cmat, bmat, xd = _steps(c_ref), _steps(b_ref), _steps(xd_ref)
    for g in range(SSM_GROUPS):
        gl = slice(g * gw, (g + 1) * gw)
        cg = cmat[:, g * SSM_STATE:(g + 1) * SSM_STATE].astype(bf16)
        bg = bmat[:, g * SSM_STATE:(g + 1) * SSM_STATE].astype(bf16)
        yo = jnp.zeros((DEC_SEQ * nb, gw), f32)
        for j in range(nb):
            mine = row_seq == j
            h0 = st_ref[j, gl, :]
            yo = jnp.where(mine, _dot_nt(cg, h0.astype(bf16)), yo)
            upd = _dot_tn(jnp.where(mine, xd[:, gl], 0.0).astype(bf16), bg)
            for hh in range(hpg):
                hr = slice(hh * SSM_HEAD_DIM, (hh + 1) * SSM_HEAD_DIM)
                d = dec_ref[(blk * nb + j) * SSM_HEADS + g * hpg + hh]
                stn_ref[j, g * gw + hh * SSM_HEAD_DIM:g * gw + (hh + 1) * SSM_HEAD_DIM, :] = h0[hr] * d + upd[hr]
        for l in range(DEC_SEQ):
            yoff_ref[l, :, gl] = yo[l * nb:(l + 1) * nb]


def _sssd(dec_flat, cmat, bmat, xd, state):
    ns = state.shape[0]
    nb = S_SSD_BLOCK
    tmaj = lambda width: pl.BlockSpec((DEC_SEQ, nb, width), lambda i: (0, i, 0))
    stblk = pl.BlockSpec((nb, D_SSM, SSM_STATE), lambda i: (i, 0, 0))
    return pl.pallas_call(
        _sssd_kernel,
        grid=(ns // nb,),
        in_specs=[pl.BlockSpec(memory_space=pltpu.SMEM), tmaj(D_BC), tmaj(D_BC), tmaj(D_SSM), stblk],
        out_specs=[tmaj(D_SSM), stblk],
        out_shape=[jax.ShapeDtypeStruct((DEC_SEQ, ns, D_SSM), f32), jax.ShapeDtypeStruct(state.shape, f32)],
        compiler_params=pltpu.CompilerParams(dimension_semantics=("arbitrary",), vmem_limit_bytes=VMEM_LIMIT_BYTES),
        name="sssd",
    )(dec_flat, cmat, bmat, xd, state)


def _smix_out_kernel(x_ref, ypart_ref, yoff_ref, eacs_ref, z_ref, pout_ref, snorm_ref, wout_ref, nmem_ref, wq_ref,
                     x1_ref, q_ref, cat_scr):
    y = ypart_ref[...] + yoff_ref[...] * eacs_ref[...]
    t = y * _silu(z_ref[...])
    gw = D_SSM // SSM_GROUPS
    for g in range(SSM_GROUPS):
        gl = slice(g * gw, (g + 1) * gw)
        tg = t[:, gl]
        ms = jnp.mean(tg * tg, axis=-1, keepdims=True)
        cat_scr[:, gl] = (tg * lax.rsqrt(ms + EPS) * snorm_ref[:, gl]).astype(bf16)
    cat_scr[:, D_SSM:] = pout_ref[...].astype(bf16)
    x1 = x_ref[...] + _dot(cat_scr[...], wout_ref[...])
    x1_ref[...] = x1
    h = _rmsnorm(x1, nmem_ref[...]).astype(bf16)
    q_ref[...] = _dot(h, wq_ref[...]) * (MEM_HEAD_DIM ** -0.5)


def _smix_out(x, ypart, yoff, eacs, z, pout, w):
    n = x.shape[0]
    rb = 128
    rows = lambda width: pl.BlockSpec((rb, width), lambda i: (i, 0))
    return pl.pallas_call(
        _smix_out_kernel,
        grid=(n // rb,),
        in_specs=[rows(D_MODEL), rows(D_SSM), rows(D_SSM), rows(D_SSM), rows(D_SSM), rows(D_POOL),
                  _const_spec((1, D_SSM)), _const_spec((D_SSM + D_POOL, D_MODEL)), _const_spec((1, D_MODEL)),
                  _const_spec((D_MODEL, D_MODEL))],
        out_specs=[rows(D_MODEL), rows(D_MODEL)],
        out_shape=[jax.ShapeDtypeStruct((n, D_MODEL), f32), jax.ShapeDtypeStruct((n, D_MODEL), f32)],
        scratch_shapes=[pltpu.VMEM((rb, D_SSM + D_POOL), bf16)],
        compiler_params=pltpu.CompilerParams(dimension_semantics=("arbitrary",), vmem_limit_bytes=VMEM_LIMIT_BYTES),
        name="smix_out",
    )(x, ypart, yoff, eacs, z, pout, w["snorm"], w["wout"], w["nmem"], w["wq"])


def _sattn_kernel(q_ref, k_ref, v_ref, o_ref):
    nb = S_ATT_BLOCK
    rg = DEC_SEQ * nb
    rows = MEM_HEADS * rg

    def row_ids(width):
        r = lax.broadcasted_iota(jnp.int32, (rows, width), 0)
        return r // rg, r % nb

    row_h, row_seq = row_ids(D_MODEL)
    col_h = lax.broadcasted_iota(jnp.int32, (rows, D_MODEL), 1) // MEM_HEAD_DIM
    q = _steps(q_ref)
    qh = jnp.where(row_h == col_h, jnp.concatenate([q] * MEM_HEADS, axis=0), 0.0)
    lhs_s = jnp.concatenate([jnp.where(row_seq == b, qh, 0.0).astype(bf16) for b in range(nb)], axis=1)
    kcat = jnp.concatenate([_kv_seq(k_ref, b) for b in range(nb)], axis=1)
    s = _dot_nt(lhs_s, kcat)
    p = jnp.exp(s - jnp.max(s, axis=-1, keepdims=True))
    p = p / jnp.sum(p, axis=-1, keepdims=True)
    _, row_seq_p = row_ids(N_MEM)
    lhs_p = jnp.concatenate([jnp.where(row_seq_p == b, p, 0.0).astype(bf16) for b in range(nb)], axis=1)
    vcat = jnp.concatenate([_kv_seq(v_ref, b) for b in range(nb)], axis=0)
    res = _dot(lhs_p, vcat)
    for hd in range(MEM_HEADS):
        hl = slice(hd * MEM_HEAD_DIM, (hd + 1) * MEM_HEAD_DIM)
        for l in range(DEC_SEQ):
            o_ref[l, :, hl] = res[hd * rg + l * nb:hd * rg + (l + 1) * nb, hl]


def _sattn(q, mem_k, mem_v):
    ns = mem_k.shape[0]
    nb = S_ATT_BLOCK
    qblk = pl.BlockSpec((DEC_SEQ, nb, D_MODEL), lambda i: (0, i, 0))
    kvblk = pl.BlockSpec((nb, N_MEM * KV_ROWS_PER_TOKEN, LANES), lambda i: (i, 0, 0))
    return pl.pallas_call(
        _sattn_kernel,
        grid=(ns // nb,),
        in_specs=[qblk, kvblk, kvblk],
        out_specs=qblk,
        out_shape=jax.ShapeDtypeStruct((DEC_SEQ, ns, D_MODEL), f32),
        compiler_params=pltpu.CompilerParams(dimension_semantics=("arbitrary",), vmem_limit_bytes=VMEM_LIMIT_BYTES),
        name="sattn",
    )(q, mem_k, mem_v)


def _sffn_kernel(x1_ref, ao_ref, sffn_ref, wo_ref, g_ref, wup_ref, cw_ref, cb_ref, wdn_ref, gfin_ref,
                 y_ref, st_ref, h_scr, act_scr):
    nb = S_SEQ_BLOCK
    x2 = _steps(x1_ref) + _dot(_steps(ao_ref).astype(bf16), wo_ref[...])
    h_scr[...] = _rmsnorm(x2, g_ref[...]).astype(bf16)
    u = _dot(h_scr[...], wup_ref[...])
    hist = FFN_CONV - 1

    def conv_block(blk):
        cols = slice(blk * FF_CHUNK, (blk + 1) * FF_CHUNK)
        tiles = range(blk * FF_CHUNK // LANES, (blk + 1) * FF_CHUNK // LANES)
        slots = [_rows_view_get(sffn_ref, i, hist, tiles, nb, FFN_ROWS_PER_SEQ) for i in range(hist)]
        slots += [u[l * nb:(l + 1) * nb, cols] for l in range(DEC_SEQ)]
        for i in range(hist):
            _rows_view_put(st_ref, i, hist, tiles, nb, FFN_ROWS_PER_SEQ, slots[DEC_SEQ + i])
        outs = []
        for l in range(DEC_SEQ):
            acc = cb_ref[:, cols] + cw_ref[0:1, cols] * slots[l]
            for k in range(1, FFN_CONV):
                acc = acc + cw_ref[k:k + 1, cols] * slots[l + k]
            outs.append(acc)
        return jnp.concatenate(outs, axis=0)

    for j in range(N_FF_CHUNKS):
        act = _silu(conv_block(_gate_blk(j))) * conv_block(_val_blk(j))
        act_scr[:, j * FF_CHUNK:(j + 1) * FF_CHUNK] = act.astype(bf16)
    y = _rmsnorm(x2 + _dot(act_scr[...], wdn_ref[...]), gfin_ref[...])
    for l in range(DEC_SEQ):
        _rows_view_put(y_ref, l, DEC_SEQ, range(D_MODEL // LANES), nb, X_ROWS_PER_SEQ, y[l * nb:(l + 1) * nb])


def _sffn(x1, ao, sffn_rows, w):
    ns = x1.shape[1]
    nb = S_SEQ_BLOCK
    tok = pl.BlockSpec((DEC_SEQ, nb, D_MODEL), lambda i: (0, i, 0))
    yblk = pl.BlockSpec((nb * X_ROWS_PER_SEQ, LANES), lambda i: (i, 0))
    stblk = pl.BlockSpec((nb * FFN_ROWS_PER_SEQ, LANES), lambda i: (i, 0))
    return pl.pallas_call(
        _sffn_kernel,
        grid=(ns // nb,),
        in_specs=[tok, tok, stblk,
                  _const_spec((D_MODEL, D_MODEL)), _const_spec((1, D_MODEL)), _const_spec((D_MODEL, 2 * D_FF)),
                  _const_spec((FFN_CONV, 2 * D_FF)), _const_spec((1, 2 * D_FF)),
                  _const_spec((D_FF, D_MODEL)), _const_spec((1, D_MODEL))],
        out_specs=[yblk, stblk],
        out_shape=[jax.ShapeDtypeStruct((ns * X_ROWS_PER_SEQ, LANES), f32),
                   jax.ShapeDtypeStruct((ns * FFN_ROWS_PER_SEQ, LANES), f32)],
        scratch_shapes=[pltpu.VMEM((DEC_SEQ * nb, D_MODEL), bf16), pltpu.VMEM((DEC_SEQ * nb, D_FF), bf16)],
        compiler_params=pltpu.CompilerParams(dimension_semantics=("arbitrary",), vmem_limit_bytes=VMEM_LIMIT_BYTES),
        name="sffn",
    )(x1, ao, sffn_rows, w["wo"], w["nffn"], w["wup"], w["fcw"], w["fcb"], w["wdn"], w["gfin"])


def _sample_path(x_sample, state_ssm, state_ssm_conv, state_pool, state_ffn_conv, cache_mem_k, cache_mem_v, w):
    ns = x_sample.shape[0]
    xtm, z, ypart, eacs, xd, cmat, bmat, dec, pout, conv_new, pool_new = _smix_in(
        _tiled_rows_view(x_sample), state_ssm_conv[0].transpose(1, 0, 2), state_pool[0].transpose(1, 0, 2), w)
    yoff, ssm_new = _sssd(dec[:, :SSM_HEADS].reshape(-1), cmat, bmat, xd, state_ssm[0].reshape(ns, D_SSM, SSM_STATE))
    flat = lambda a: a.reshape(DEC_SEQ * ns, a.shape[-1])
    x1, q = _smix_out(flat(xtm), flat(ypart), flat(yoff), flat(eacs), flat(z), flat(pout), w)
    ao = _sattn(q.reshape(DEC_SEQ, ns, D_MODEL), _kv_rows_view(cache_mem_k[0]), _kv_rows_view(cache_mem_v[0]))
    y_rows, ffn_rows = _sffn(x1.reshape(DEC_SEQ, ns, D_MODEL), ao, _tiled_rows_view(state_ffn_conv[0]), w)
    return (_from_tiled_rows(y_rows, ns, DEC_SEQ, D_MODEL),
            ssm_new.reshape(1, ns, SSM_HEADS, SSM_HEAD_DIM, SSM_STATE),
            conv_new.transpose(1, 0, 2)[None], pool_new.transpose(1, 0, 2)[None],
            _from_tiled_rows(ffn_rows, ns, FFN_CONV - 1, 2 * D_FF)[None])


def kernel(x_prompt, x_sample, mem_prompt, state_ssm, state_ssm_conv, state_pool, state_ffn_conv, cache_mem_k, cache_mem_v, norm_mix, w_in, ssm_conv_w, ssm_conv_b, ssm_dt_bias, ssm_a_log, ssm_d, ssm_norm, w_pool, pool_scale, w_out, norm_mem, norm_memkv, w_mq, w_mk, w_mv, w_mo, norm_ffn, w_up, ffn_conv_w, ffn_conv_b, w_down, final_norm):
    params = dict(norm_mix=norm_mix, w_in=w_in, ssm_conv_w=ssm_conv_w, ssm_conv_b=ssm_conv_b, ssm_dt_bias=ssm_dt_bias,
                  ssm_a_log=ssm_a_log, ssm_d=ssm_d, ssm_norm=ssm_norm, w_pool=w_pool, pool_scale=pool_scale,
                  w_out=w_out, norm_mem=norm_mem, norm_memkv=norm_memkv, w_mq=w_mq, w_mk=w_mk, w_mv=w_mv, w_mo=w_mo,
                  norm_ffn=norm_ffn, w_up=w_up, ffn_conv_w=ffn_conv_w, ffn_conv_b=ffn_conv_b, w_down=w_down,
                  final_norm=final_norm)
    w = _prep_weights(params)
    yp, ssm_p, conv_p, pool_p, ffn_p, mk_p, mv_p = _prompt_path(x_prompt, mem_prompt, w, PROMPT_TILE)
    ys, ssm_s, conv_s, pool_s, ffn_s = _sample_path(x_sample, state_ssm, state_ssm_conv, state_pool, state_ffn_conv,
                                                    cache_mem_k, cache_mem_v, w)
    return yp, ys, ssm_p, ssm_s, conv_p, conv_s, pool_p, pool_s, ffn_p, ffn_s, mk_p, mv_p
```

```python
import functools

import jax
import jax.numpy as jnp
from jax import lax
from jax.experimental import pallas as pl
from jax.experimental.pallas import tpu as pltpu

f32 = jnp.float32
bf16 = jnp.bfloat16

D_MODEL = 1024
SSM_HEADS = 16
SSM_HEAD_DIM = 64
SSM_STATE = 128
SSM_GROUPS = 2
SSM_CHUNK = 128
D_SSM = 1024
D_BC = SSM_GROUPS * SSM_STATE
D_XBC = D_SSM + 2 * D_BC
SSM_CONV = 4
D_POOL = 1024
POOL_WINDOWS = (2, 4, 8, 16)
POOL_GROUP_DIM = 256
POOL_HIST = 15
N_MEM = 256
MEM_HEADS = 4
MEM_HEAD_DIM = 256
D_FF = 2816
FFN_CONV = 3
EPS = 1e-6
PAST_LEN = 16384

LANES = 128
SUBLANES = 8
MXU_DIM = 256
VMEM_LIMIT_BYTES = 56 * 1024 * 1024

COL_Z = 0
COL_XBC = D_SSM
COL_VP = COL_XBC + D_XBC
COL_DT = COL_VP + D_POOL
D_PROJ = COL_DT + LANES

CONV_HIST_ROWS = SUBLANES
POOL_HIST_ROWS = 2 * SUBLANES
FF_CHUNK = MXU_DIM
N_FF_CHUNKS = D_FF // FF_CHUNK


def _silu(v):
    return v * (1.0 / (1.0 + jnp.exp(-v)))


def _softplus(v):
    return jnp.maximum(v, 0.0) + jnp.log1p(jnp.exp(-jnp.abs(v)))


def _rmsnorm(x, g):
    ms = jnp.mean(x * x, axis=-1, keepdims=True)
    return x * lax.rsqrt(ms + EPS) * g


def _dot(a, b):
    return jnp.dot(a, b, preferred_element_type=f32)


def _dot_nt(a, b):
    return lax.dot_general(a, b, (((1,), (1,)), ((), ())), preferred_element_type=f32)


def _dot_tn(a, b):
    return lax.dot_general(a, b, (((0,), (0,)), ((), ())), preferred_element_type=f32)


def _split3(v):
    p1 = v.astype(bf16)
    r1 = v - p1.astype(f32)
    p2 = r1.astype(bf16)
    r2 = r1 - p2.astype(f32)
    return p1, p2, r2.astype(bf16)


def _const_spec(shape):
    return pl.BlockSpec(shape, lambda *_: (0,) * len(shape), pipeline_mode=pl.Buffered(1))


KV_LANE_TILES = MEM_HEAD_DIM // LANES
KV_ROWS_PER_TOKEN = KV_LANE_TILES * MEM_HEADS


def _kv_rows_view(kv):
    ns = kv.shape[0]
    kv = kv.reshape(ns, N_MEM, MEM_HEADS, KV_LANE_TILES, LANES).transpose(0, 1, 3, 2, 4)
    return kv.reshape(ns, N_MEM * KV_ROWS_PER_TOKEN, LANES)


def _kv_from_rows(rows):
    ns = rows.shape[0]
    kv = rows.reshape(ns, N_MEM, KV_LANE_TILES, MEM_HEADS, LANES).transpose(0, 1, 3, 2, 4)
    return kv.reshape(ns, N_MEM, MEM_HEADS, MEM_HEAD_DIM)


def _kv_seq(ref, jj):
    tiles = [ref[jj, pl.ds(dt * MEM_HEADS + hd, N_MEM, stride=KV_ROWS_PER_TOKEN), :]
             for hd in range(MEM_HEADS) for dt in range(KV_LANE_TILES)]
    return jnp.concatenate(tiles, axis=1).astype(bf16)


def _memkv_kernel(mem_ref, g_ref, wk_ref, wv_ref, k_ref, v_ref):
    h = _rmsnorm(mem_ref[0], g_ref[...]).astype(bf16)
    for out_ref, w_ref in ((k_ref, wk_ref), (v_ref, wv_ref)):
        kv = _dot(h, w_ref[...])
        for hd in range(MEM_HEADS):
            for dt in range(KV_LANE_TILES):
                col = hd * MEM_HEAD_DIM + dt * LANES
                out_ref[0, pl.ds(dt * MEM_HEADS + hd, N_MEM, stride=KV_ROWS_PER_TOKEN), :] = kv[:, col:col + LANES]


def _memkv(mem, g, wk, wv):
    b = mem.shape[0]
    blk = pl.BlockSpec((1, N_MEM, D_MODEL), lambda i: (i, 0, 0))
    oblk = pl.BlockSpec((1, N_MEM * KV_ROWS_PER_TOKEN, LANES), lambda i: (i, 0, 0))
    return pl.pallas_call(
        _memkv_kernel,
        grid=(b,),
        in_specs=[blk, _const_spec((1, D_MODEL)), _const_spec((D_MODEL, D_MODEL)), _const_spec((D_MODEL, D_MODEL))],
        out_specs=[oblk, oblk],
        out_shape=[jax.ShapeDtypeStruct((b, N_MEM * KV_ROWS_PER_TOKEN, LANES), f32)] * 2,
        compiler_params=pltpu.CompilerParams(dimension_semantics=("arbitrary",), vmem_limit_bytes=VMEM_LIMIT_BYTES),
        name="memkv",
    )(mem, g, wk, wv)


def _ssd_chunk(r0, dt_scr, xs_scr, b_scr, c_scr, y_scr, xd_scr, hst_scr, a_ref, dexp_ref, between=()):
    between = list(between) + [None] * 3
    q = SSM_CHUNK
    rows = pl.ds(r0, q)
    row_i = lax.broadcasted_iota(jnp.int32, (q, q), 0)
    col_i = lax.broadcasted_iota(jnp.int32, (q, q), 1)
    causal = col_i <= row_i
    lo = col_i < SSM_HEAD_DIM
    tril = jnp.where(causal, 1.0, 0.0).astype(bf16)

    dt = dt_scr[rows, :]
    da = dt * a_ref[...]
    p1, p2, p3 = _split3(da)
    acs = _dot(tril, p1) + _dot(tril, p2) + _dot(tril, p3)
    acs_t = acs.T
    if between[0] is not None:
        between[0]()

    for g in range(SSM_GROUPS):
        bg = b_scr[rows, g * SSM_STATE:(g + 1) * SSM_STATE]
        cg = c_scr[rows, g * SSM_STATE:(g + 1) * SSM_STATE]
        bg_b = bg.astype(bf16)
        cb = jnp.where(causal, _dot_nt(cg.astype(bf16), bg_b), 0.0)
        cdec_rows = []
        pairs_per_group = SSM_HEADS // SSM_GROUPS // 2
        for pp in range(pairs_per_group):
            pr = g * pairs_per_group + pp
            lanes = slice(pr * LANES, (pr + 1) * LANES)
            lhs, dtb, dend, cdec = [], [], [], []
            for hh in (2 * pr, 2 * pr + 1):
                colb = jnp.broadcast_to(acs[:, hh:hh + 1], (q, q))
                seg = jnp.where(causal, colb - acs_t[hh:hh + 1, :], 0.0)
                lhs.append((jnp.exp(seg) * cb).astype(bf16))
                lhs.append((cg * jnp.exp(colb)).astype(bf16))
                last = colb[q - 1:q, :]
                dend.append(jnp.exp(last - colb))
                cdec.append(jnp.exp(last))
                dtb.append(jnp.broadcast_to(dt[:, hh:hh + 1], (q, q)))
            xs_pair = xs_scr[rows, lanes]
            xdt = xs_pair * jnp.where(lo, dtb[0], dtb[1])
            xd_scr[:, lanes] = (xdt * jnp.where(lo, dend[0], dend[1])).astype(bf16)
            hst_pair = hst_scr[:, lanes]
            rhs = jnp.concatenate([
                jnp.where(lo, xdt, 0.0).astype(bf16), jnp.where(lo, hst_pair, 0.0).astype(bf16),
                jnp.where(lo, 0.0, xdt).astype(bf16), jnp.where(lo, 0.0, hst_pair).astype(bf16)], axis=0)
            y_pair = _dot(jnp.concatenate(lhs, axis=1), rhs)
            y_scr[rows, lanes] = y_pair + xs_pair * dexp_ref[:, lanes]
            cdec_rows.append(jnp.where(lo[:1], cdec[0], cdec[1]))
        gl = slice(g * (D_SSM // SSM_GROUPS), (g + 1) * (D_SSM // SSM_GROUPS))
        upd = _dot_tn(bg_b, xd_scr[:, gl])
        hst_scr[:, gl] = hst_scr[:, gl] * jnp.concatenate(cdec_rows, axis=1) + upd
        if between[1 + g] is not None:
            between[1 + g]()


def _mixer_kernel(x_ref, nmix_ref, win_ref, cw_ref, cb_ref, dtb_ref, a_ref, dexp_ref, snorm_ref, wpool_ref,
                  pscale_ref, wout_ref,
                  x1_ref, ssm_ref, conv_ref, pool_ref,
                  xd_scr, hst_scr, *sub_scr, tile, sub):
    t = pl.program_id(1)
    last_t = pl.num_programs(1) - 1
    ch = CONV_HIST_ROWS
    ph = POOL_HIST_ROWS
    n_sub = tile // sub
    per = len(sub_scr) // n_sub
    bufs = [sub_scr[i * per:(i + 1) * per] for i in range(n_sub)]
    xbc0, vp0 = bufs[0][2], bufs[0][3]

    @pl.when(t == 0)
    def _():
        xbc0[:, 0:ch, :] = jnp.zeros((D_XBC // LANES, ch, LANES), f32)
        vp0[:, 0:ph, :] = jnp.zeros((D_POOL // LANES, ph, LANES), f32)
        hst_scr[...] = jnp.zeros_like(hst_scr)

    gw = D_SSM // SSM_GROUPS

    def carry_history(src, dst):
        for j in range(D_XBC // LANES):
            bufs[dst][2][j, 0:ch, :] = bufs[src][2][j, sub:sub + ch, :]
        for j in range(D_POOL // LANES):
            bufs[dst][3][j, 0:ph, :] = bufs[src][3][j, sub:sub + ph, :]

    def in_xbc(s):
        h_scr, _, xbc_scr = bufs[s][:3]
        h_scr[...] = _rmsnorm(x_ref[0, s * sub:(s + 1) * sub, :], nmix_ref[...]).astype(bf16)
        xbc = _dot(h_scr[...], win_ref[:, COL_XBC:COL_VP])
        for j in range(D_XBC // LANES):
            xbc_scr[j, ch:ch + sub, :] = xbc[:, j * LANES:(j + 1) * LANES]

    def in_dtvp(s):
        h_scr, _, _, vp_scr, dt_scr = bufs[s][:5]
        dt_scr[...] = _dot(h_scr[...], win_ref[:, COL_DT:D_PROJ])
        vp = _dot(h_scr[...], win_ref[:, COL_VP:COL_DT])
        for j in range(D_POOL // LANES):
            vp_scr[j, ph:ph + sub, :] = vp[:, j * LANES:(j + 1) * LANES]

    def in_z(s):
        h_scr, z_scr = bufs[s][:2]
        z_scr[...] = _dot(h_scr[...], win_ref[:, COL_Z:COL_XBC])
        if s > 0:
            carry_history(s - 1, s)

    def stage_mid(s, between):
        _, z_scr, xbc_scr, vp_scr, dt_scr, xs_scr, b_scr, c_scr, y_scr, pooled_scr, cat_scr = bufs[s]
        for j in range(D_XBC // LANES):
            cl = slice(j * LANES, (j + 1) * LANES)
            acc = cb_ref[:, cl] + cw_ref[0:1, cl] * xbc_scr[j, pl.ds(ch - 3, sub), :]
            for k in range(1, SSM_CONV):
                acc = acc + cw_ref[k:k + 1, cl] * xbc_scr[j, pl.ds(ch - 3 + k, sub), :]
            act = _silu(acc)
            if j < D_SSM // LANES:
                xs_scr[:, cl] = act
            elif j < (D_SSM + D_BC) // LANES:
                b_scr[:, j * LANES - D_SSM:(j + 1) * LANES - D_SSM] = act
            else:
                c_scr[:, j * LANES - D_SSM - D_BC:(j + 1) * LANES - D_SSM - D_BC] = act

        pos1 = lax.broadcasted_iota(jnp.int32, (sub, LANES), 0) + (t * tile + s * sub + 1)
        for j in range(D_POOL // LANES):
            cl = slice(j * LANES, (j + 1) * LANES)
            w = POOL_WINDOWS[j * LANES // POOL_GROUP_DIM]
            cur = vp_scr[j, ph:ph + sub, :]
            acc = cur
            for k in range(1, w):
                acc = acc + vp_scr[j, pl.ds(ph - k, sub), :]
            cnt = jnp.minimum(pos1, w).astype(f32)
            pooled_scr[:, cl] = (acc / cnt - cur).astype(bf16)
        for gi in range(len(POOL_WINDOWS)):
            gl = slice(gi * POOL_GROUP_DIM, (gi + 1) * POOL_GROUP_DIM)
            pg = _dot(pooled_scr[:, gl], wpool_ref[gi]) * pscale_ref[:, gl]
            cat_scr[:, D_SSM + gi * POOL_GROUP_DIM:D_SSM + (gi + 1) * POOL_GROUP_DIM] = pg.astype(bf16)

        dt_scr[...] = _softplus(dt_scr[...] + dtb_ref[...])
        n_chunks = sub // SSM_CHUNK
        slots = [None] * (3 * n_chunks)
        for i, piece in enumerate(between):
            slots[i * len(slots) // len(between)] = piece
        for c in range(n_chunks):
            _ssd_chunk(c * SSM_CHUNK, dt_scr, xs_scr, b_scr, c_scr, y_scr, xd_scr, hst_scr, a_ref, dexp_ref,
                       slots[3 * c:3 * c + 3])

        for g in range(SSM_GROUPS):
            gl = slice(g * gw, (g + 1) * gw)
            tg = y_scr[:, gl] * _silu(z_scr[:, gl])
            ms = jnp.mean(tg * tg, axis=-1, keepdims=True)
            cat_scr[:, gl] = (tg * lax.rsqrt(ms + EPS) * snorm_ref[:, gl]).astype(bf16)

    def stage_out(s):
        rows = slice(s * sub, (s + 1) * sub)
        x1_ref[0, rows, :] = x_ref[0, rows, :] + _dot(bufs[s][-1][...], wout_ref[...])

    in_xbc(0)
    in_dtvp(0)
    in_z(0)
    for s in range(n_sub):
        nxt = s + 1 < n_sub
        if nxt:
            in_xbc(s + 1)
        pieces = [functools.partial(in_dtvp, s + 1), functools.partial(in_z, s + 1)] if nxt else []
        if s > 0:
            pieces.append(functools.partial(stage_out, s - 1))
        stage_mid(s, pieces)
    stage_out(n_sub - 1)
    carry_history(n_sub - 1, 0)

    @pl.when(t == last_t)
    def _():
        for pr in range(D_SSM // LANES):
            ssm_ref[0, pr * LANES:(pr + 1) * LANES, :] = hst_scr[:, pr * LANES:(pr + 1) * LANES].T
        for j in range(D_XBC // LANES):
            conv_ref[0, :, j * LANES:(j + 1) * LANES] = xbc0[j, pl.ds(ch - (SSM_CONV - 1), SSM_CONV - 1), :]
        for j in range(D_POOL // LANES):
            pool_ref[0, :, j * LANES:(j + 1) * LANES] = vp0[j, pl.ds(ph - POOL_HIST, POOL_HIST), :]


def _mixer_prompt(x, nmix, win, cw, cb, dtb, a_row, dexp, snorm, wpool, pscale, wout, tile):
    b, seq, _ = x.shape
    nt = seq // tile
    sub = min(tile, MIXER_SUB_TILE)
    xblk = pl.BlockSpec((1, tile, D_MODEL), lambda i, j: (i, j, 0))
    sub_scratch = [
        pltpu.VMEM((sub, D_MODEL), bf16),
        pltpu.VMEM((sub, D_SSM), f32),
        pltpu.VMEM((D_XBC // LANES, CONV_HIST_ROWS + sub, LANES), f32),
        pltpu.VMEM((D_POOL // LANES, POOL_HIST_ROWS + sub, LANES), f32),
        pltpu.VMEM((sub, LANES), f32),
        pltpu.VMEM((sub, D_SSM), f32),
        pltpu.VMEM((sub, D_BC), f32),
        pltpu.VMEM((sub, D_BC), f32),
        pltpu.VMEM((sub, D_SSM), f32),
        pltpu.VMEM((sub, D_POOL), bf16),
        pltpu.VMEM((sub, D_SSM + D_POOL), bf16),
    ]
    scratch = [pltpu.VMEM((SSM_CHUNK, D_SSM), bf16),
               pltpu.VMEM((SSM_STATE, D_SSM), f32)]
    scratch += sub_scratch * (tile // sub)
    return pl.pallas_call(
        functools.partial(_mixer_kernel, tile=tile, sub=sub),
        grid=(b, nt),
        in_specs=[xblk, _const_spec((1, D_MODEL)), _const_spec((D_MODEL, D_PROJ)), _const_spec((SSM_CONV, D_XBC)),
                  _const_spec((1, D_XBC)), _const_spec((1, LANES)), _const_spec((1, LANES)), _const_spec((1, D_SSM)),
                  _const_spec((1, D_SSM)), _const_spec((len(POOL_WINDOWS), POOL_GROUP_DIM, POOL_GROUP_DIM)),
                  _const_spec((1, D_POOL)), _const_spec((D_SSM + D_POOL, D_MODEL))],
        out_specs=[xblk,
                   pl.BlockSpec((1, D_SSM, SSM_STATE), lambda i, j: (i, 0, 0)),
                   pl.BlockSpec((1, SSM_CONV - 1, D_XBC), lambda i, j: (i, 0, 0)),
                   pl.BlockSpec((1, POOL_HIST, D_POOL), lambda i, j: (i, 0, 0))],
        out_shape=[jax.ShapeDtypeStruct((b, seq, D_MODEL), f32),
                   jax.ShapeDtypeStruct((b, D_SSM, SSM_STATE), f32),
                   jax.ShapeDtypeStruct((b, SSM_CONV - 1, D_XBC), f32),
                   jax.ShapeDtypeStruct((b, POOL_HIST, D_POOL), f32)],
        scratch_shapes=scratch,
        compiler_params=pltpu.CompilerParams(dimension_semantics=("arbitrary", "arbitrary"),
                                             vmem_limit_bytes=VMEM_LIMIT_BYTES),
        name="mixer_prompt",
    )(x, nmix, win, cw, cb, dtb, a_row, dexp, snorm, wpool, pscale, wout)


def _attn_kernel(x_ref, g_ref, wq_ref, k_ref, v_ref, wo_ref, o_ref, q_scr, ao_scr):
    tile = x_ref.shape[1]
    sub = min(tile, ATTN_SUB_TILE)
    n_sub = tile // sub
    heads = [slice(hd * MEM_HEAD_DIM, (hd + 1) * MEM_HEAD_DIM) for hd in range(MEM_HEADS)]
    k = _kv_seq(k_ref, 0)
    v = _kv_seq(v_ref, 0)

    def q_proj(s):
        rows = slice(s * sub, (s + 1) * sub)
        h = _rmsnorm(x_ref[0, rows, :], g_ref[...]).astype(bf16)
        q_scr[rows, :] = (_dot(h, wq_ref[...]) * (MEM_HEAD_DIM ** -0.5)).astype(bf16)

    def scores(s):
        rows = slice(s * sub, (s + 1) * sub)
        return [_dot_nt(q_scr[rows, hl], k[:, hl]) for hl in heads]

    def values(s, sc):
        rows = slice(s * sub, (s + 1) * sub)
        for hl, s_h in zip(heads, sc):
            p = jnp.exp(s_h - jnp.max(s_h, axis=-1, keepdims=True))
            p = (p / jnp.sum(p, axis=-1, keepdims=True)).astype(bf16)
            ao_scr[rows, hl] = _dot(p, v[:, hl]).astype(bf16)

    def o_proj(s):
        rows = slice(s * sub, (s + 1) * sub)
        o_ref[0, rows, :] = x_ref[0, rows, :] + _dot(ao_scr[rows, :], wo_ref[...])

    q_proj(0)
    for s in range(n_sub):
        sc = scores(s)
        if s + 1 < n_sub:
            q_proj(s + 1)
        if s > 0:
            o_proj(s - 1)
        values(s, sc)
    o_proj(n_sub - 1)


def _attn_prompt(x, g, wq, mem_k, mem_v, wo, tile):
    b, seq, _ = x.shape
    xblk = pl.BlockSpec((1, tile, D_MODEL), lambda i, j: (i, j, 0))
    kvblk = pl.BlockSpec((1, N_MEM * KV_ROWS_PER_TOKEN, LANES), lambda i, j: (i, 0, 0))
    return pl.pallas_call(
        _attn_kernel,
        grid=(b, seq // tile),
        in_specs=[xblk, _const_spec((1, D_MODEL)), _const_spec((D_MODEL, D_MODEL)), kvblk, kvblk,
                  _const_spec((D_MODEL, D_MODEL))],
        out_specs=xblk,
        out_shape=jax.ShapeDtypeStruct((b, seq, D_MODEL), f32),
        scratch_shapes=[pltpu.VMEM((tile, D_MODEL), bf16), pltpu.VMEM((tile, D_MODEL), bf16)],
        compiler_params=pltpu.CompilerParams(dimension_semantics=("arbitrary", "arbitrary"),
                                             vmem_limit_bytes=VMEM_LIMIT_BYTES),
        name="attn_prompt",
    )(x, g, wq, mem_k, mem_v, wo)


def _gate_blk(j):
    return j


def _val_blk(j):
    return N_FF_CHUNKS + j


def _ffn_kernel(x_ref, g_ref, wup_ref, cw_ref, cb_ref, wdn_ref, gfin_ref, y_ref, st_ref, h_scr, u_scr, act_scr, *, tile):
    t = pl.program_id(1)
    last_t = pl.num_programs(1) - 1
    ch = CONV_HIST_ROWS
    tpc = FF_CHUNK // LANES
    ntile = 2 * D_FF // LANES

    @pl.when(t == 0)
    def _():
        u_scr[:, 0:ch, :] = jnp.zeros((ntile, ch, LANES), f32)

    x = x_ref[0]
    h_scr[...] = _rmsnorm(x, g_ref[...]).astype(bf16)
    u = _dot(h_scr[...], wup_ref[...])
    for ti in range(ntile):
        u_scr[ti, ch:ch + tile, :] = u[:, ti * LANES:(ti + 1) * LANES]

    def conv(ti):
        cl = slice(ti * LANES, (ti + 1) * LANES)
        acc = cb_ref[:, cl] + cw_ref[0:1, cl] * u_scr[ti, pl.ds(ch - 2, tile), :]
        acc = acc + cw_ref[1:2, cl] * u_scr[ti, pl.ds(ch - 1, tile), :]
        return acc + cw_ref[2:3, cl] * u_scr[ti, ch:ch + tile, :]

    for j in range(N_FF_CHUNKS):
        for i in range(tpc):
            gate = conv(_gate_blk(j) * tpc + i)
            val = conv(_val_blk(j) * tpc + i)
            act_scr[:, (j * tpc + i) * LANES:(j * tpc + i + 1) * LANES] = (_silu(gate) * val).astype(bf16)
    for ti in range(ntile):
        u_scr[ti, 0:ch, :] = u_scr[ti, tile:tile + ch, :]
    y_ref[0] = _rmsnorm(x + _dot(act_scr[...], wdn_ref[...]), gfin_ref[...])

    @pl.when(t == last_t)
    def _():
        for ti in range(ntile):
            st_ref[0, :, ti * LANES:(ti + 1) * LANES] = u_scr[ti, pl.ds(ch - (FFN_CONV - 1), FFN_CONV - 1), :]


def _ffn_prompt(x, g, wup, cw, cb, wdn, gfin, tile):
    b, seq, _ = x.shape
    xblk = pl.BlockSpec((1, tile, D_MODEL), lambda i, j: (i, j, 0))
    return pl.pallas_call(
        functools.partial(_ffn_kernel, tile=tile),
        grid=(b, seq // tile),
        in_specs=[xblk, _const_spec((1, D_MODEL)), _const_spec((D_MODEL, 2 * D_FF)),
                  _const_spec((FFN_CONV, 2 * D_FF)), _const_spec((1, 2 * D_FF)),
                  _const_spec((D_FF, D_MODEL)), _const_spec((1, D_MODEL))],
        out_specs=[xblk, pl.BlockSpec((1, FFN_CONV - 1, 2 * D_FF), lambda i, j: (i, 0, 0))],
        out_shape=[jax.ShapeDtypeStruct((b, seq, D_MODEL), f32),
                   jax.ShapeDtypeStruct((b, FFN_CONV - 1, 2 * D_FF), f32)],
        scratch_shapes=[pltpu.VMEM((tile, D_MODEL), bf16),
                        pltpu.VMEM((2 * D_FF // LANES, CONV_HIST_ROWS + tile, LANES), f32),
                        pltpu.VMEM((tile, D_FF), bf16)],
        compiler_params=pltpu.CompilerParams(dimension_semantics=("arbitrary", "arbitrary"),
                                             vmem_limit_bytes=VMEM_LIMIT_BYTES),
        name="ffn_prompt",
    )(x, g, wup, cw, cb, wdn, gfin)


def _prep_weights(p):
    i = 0
    w_in = p["w_in"][i]
    dt_cols = jnp.pad(w_in[:, D_SSM + D_XBC:D_SSM + D_XBC + SSM_HEADS], ((0, 0), (0, LANES - SSM_HEADS)))
    win = jnp.concatenate([w_in[:, :D_SSM + D_XBC], w_in[:, D_SSM + D_XBC + SSM_HEADS:], dt_cols], axis=1).astype(bf16)
    pad_h = (0, LANES - SSM_HEADS)
    return dict(
        nmix=p["norm_mix"][i][None], win=win, cw=p["ssm_conv_w"][i], cb=p["ssm_conv_b"][i][None],
        dtb=jnp.pad(p["ssm_dt_bias"][i], pad_h)[None],
        a_row=jnp.pad(-jnp.exp(p["ssm_a_log"][i].astype(f32)), pad_h)[None],
        dexp=jnp.repeat(p["ssm_d"][i], SSM_HEAD_DIM)[None], snorm=p["ssm_norm"][i][None],
        wpool=p["w_pool"][i].astype(bf16), pscale=p["pool_scale"][i][None], wout=p["w_out"][i].astype(bf16),
        nmem=p["norm_mem"][i][None], nmemkv=p["norm_memkv"][i][None],
        wq=p["w_mq"][i].astype(bf16), wk=p["w_mk"][i].astype(bf16), wv=p["w_mv"][i].astype(bf16),
        wo=p["w_mo"][i].astype(bf16),
        nffn=p["norm_ffn"][i][None],
        wup=p["w_up"][i].astype(bf16), fcw=p["ffn_conv_w"][i], fcb=p["ffn_conv_b"][i][None],
        wdn=p["w_down"][i].astype(bf16),
        gfin=p["final_norm"][None],
    )


def _prompt_path(x_prompt, mem_prompt, w, tile):
    b = x_prompt.shape[0]
    mem_k, mem_v = _memkv(mem_prompt, w["nmemkv"], w["wk"], w["wv"])
    x1, ssm, conv, pool = _mixer_prompt(x_prompt, w["nmix"], w["win"], w["cw"], w["cb"], w["dtb"], w["a_row"],
                                        w["dexp"], w["snorm"], w["wpool"], w["pscale"], w["wout"],
                                        min(MIXER_TILE, x_prompt.shape[1]))
    x2 = _attn_prompt(x1, w["nmem"], w["wq"], mem_k, mem_v, w["wo"], tile)
    y, ffn = _ffn_prompt(x2, w["nffn"], w["wup"], w["fcw"], w["fcb"], w["wdn"], w["gfin"], tile)
    return (y, ssm.reshape(1, b, SSM_HEADS, SSM_HEAD_DIM, SSM_STATE), conv[None], pool[None], ffn[None],
            _kv_from_rows(mem_k)[None], _kv_from_rows(mem_v)[None])


PROMPT_TILE = 512
MIXER_TILE = 1024
MIXER_SUB_TILE = 256
ATTN_SUB_TILE = 256


DEC_SEQ = 4
S_SEQ_BLOCK = 32
S_SSD_BLOCK = 8
S_ATT_BLOCK = 8
X_ROWS_PER_SEQ = DEC_SEQ * D_MODEL // LANES
FFN_ROWS_PER_SEQ = (FFN_CONV - 1) * 2 * D_FF // LANES


def _expand_heads(v, lo):
    r = v.shape[0]
    tiles = []
    for pr in range(SSM_HEADS // 2):
        a = jnp.broadcast_to(v[:, 2 * pr:2 * pr + 1], (r, LANES))
        b = jnp.broadcast_to(v[:, 2 * pr + 1:2 * pr + 2], (r, LANES))
        tiles.append(jnp.where(lo, a, b))
    return jnp.concatenate(tiles, axis=1)


def _tiled_rows_view(a):
    ns, r, c = a.shape
    return a.reshape(ns, r, c // LANES, LANES).transpose(0, 2, 1, 3).reshape(ns * (c // LANES) * r, LANES)


def _from_tiled_rows(rows, ns, r, c):
    return rows.reshape(ns, c // LANES, r, LANES).transpose(0, 2, 1, 3).reshape(ns, r, c)


def _rows_view_get(ref, i, r, tiles, nb, per_seq):
    return jnp.concatenate([ref[pl.ds(dt * r + i, nb, stride=per_seq), :] for dt in tiles], axis=1)


def _rows_view_put(ref, i, r, tiles, nb, per_seq, val):
    for n, dt in enumerate(tiles):
        ref[pl.ds(dt * r + i, nb, stride=per_seq), :] = val[:, n * LANES:(n + 1) * LANES]


def _steps(ref):
    return ref[...].reshape(DEC_SEQ * ref.shape[1], ref.shape[2])


def _smix_in_kernel(x_ref, sconv_ref, spool_ref, nmix_ref, win_ref, cw_ref, cb_ref, dtb_ref, a_ref, dexp_ref,
                    wpool_ref, pscale_ref,
                    xtm_ref, z_ref, ypart_ref, eacs_ref, xd_ref, c_ref, b_ref, dec_ref, pout_ref, conv_ref, pool_ref,
                    h_scr, pooled_scr):
    nb = S_SEQ_BLOCK
    lo = lax.broadcasted_iota(jnp.int32, (nb, LANES), 1) < SSM_HEAD_DIM
    x_tiles = range(D_MODEL // LANES)
    x_steps = [_rows_view_get(x_ref, l, DEC_SEQ, x_tiles, nb, X_ROWS_PER_SEQ) for l in range(DEC_SEQ)]
    for l in range(DEC_SEQ):
        xtm_ref[l] = x_steps[l]
    h_scr[...] = _rmsnorm(jnp.concatenate(x_steps, axis=0), nmix_ref[...]).astype(bf16)
    z = _dot(h_scr[...], win_ref[:, COL_Z:COL_XBC])
    for l in range(DEC_SEQ):
        z_ref[l] = z[l * nb:(l + 1) * nb]
    xbc = _dot(h_scr[...], win_ref[:, COL_XBC:COL_VP])
    vp = _dot(h_scr[...], win_ref[:, COL_VP:COL_DT])
    dtr = _dot(h_scr[...], win_ref[:, COL_DT:D_PROJ])

    def conv_slot(i):
        if i < SSM_CONV - 1:
            return sconv_ref[i]
        return xbc[(i - SSM_CONV + 1) * nb:(i - SSM_CONV + 2) * nb]

    def pool_slot(i, cl):
        if i < POOL_HIST:
            return spool_ref[i, :, cl]
        return vp[(i - POOL_HIST) * nb:(i - POOL_HIST + 1) * nb, cl]

    xs, bm, cm, dt, acs = [], [], [], [], []
    for l in range(DEC_SEQ):
        acc = cb_ref[...] + cw_ref[0:1, :] * conv_slot(l)
        for k in range(1, SSM_CONV):
            acc = acc + cw_ref[k:k + 1, :] * conv_slot(l + k)
        act = _silu(acc)
        xs.append(act[:, :D_SSM])
        bm.append(act[:, D_SSM:D_SSM + D_BC])
        cm.append(act[:, D_SSM + D_BC:])
        b_ref[l] = bm[l]
        c_ref[l] = cm[l]
        dt.append(_softplus(dtr[l * nb:(l + 1) * nb] + dtb_ref[...]))
        da = dt[l] * a_ref[...]
        acs.append(da if l == 0 else acs[l - 1] + da)
        for gi, w in enumerate(POOL_WINDOWS):
            gl = slice(gi * POOL_GROUP_DIM, (gi + 1) * POOL_GROUP_DIM)
            s = pool_slot(POOL_HIST + l, gl)
            for k in range(1, w):
                s = s + pool_slot(POOL_HIST + l - k, gl)
            cnt = float(min(PAST_LEN + l + 1, w))
            pooled_scr[l * nb:(l + 1) * nb, gl] = (s / cnt - pool_slot(POOL_HIST + l, gl)).astype(bf16)
    for i in range(SSM_CONV - 1):
        conv_ref[i] = conv_slot(DEC_SEQ + i)
    for i in range(POOL_HIST):
        pool_ref[i] = pool_slot(DEC_SEQ + i, slice(0, D_POOL))
    pout = jnp.concatenate(
        [_dot(pooled_scr[:, gi * POOL_GROUP_DIM:(gi + 1) * POOL_GROUP_DIM], wpool_ref[gi])
         for gi in range(len(POOL_WINDOWS))], axis=1) * pscale_ref[...]
    for l in range(DEC_SEQ):
        pout_ref[l] = pout[l * nb:(l + 1) * nb]

    xdt = [xs[l] * _expand_heads(dt[l], lo) for l in range(DEC_SEQ)]
    gw = D_SSM // SSM_GROUPS
    for l in range(DEC_SEQ):
        y = xs[l] * dexp_ref[...]
        for s in range(l + 1):
            decay = _expand_heads(jnp.exp(acs[l] - acs[s]), lo)
            cbs = [jnp.sum(cm[l][:, g * SSM_STATE:(g + 1) * SSM_STATE] * bm[s][:, g * SSM_STATE:(g + 1) * SSM_STATE],
                           axis=-1, keepdims=True) for g in range(SSM_GROUPS)]
            coef = jnp.concatenate([decay[:, g * gw:(g + 1) * gw] * cbs[g] for g in range(SSM_GROUPS)], axis=1)
            y = y + coef * xdt[s]
        ypart_ref[l] = y
        eacs_ref[l] = _expand_heads(jnp.exp(acs[l]), lo)
        xd_ref[l] = xdt[l] * _expand_heads(jnp.exp(acs[DEC_SEQ - 1] - acs[l]), lo)
    dec_ref[...] = jnp.exp(acs[DEC_SEQ - 1])


def _smix_in(x_rows, sconv, spool, w):
    ns = sconv.shape[1]
    nb = S_SEQ_BLOCK
    tmaj = lambda steps, width: pl.BlockSpec((steps, nb, width), lambda i: (0, i, 0))
    step_outs = [D_MODEL, D_SSM, D_SSM, D_SSM, D_SSM, D_BC, D_BC]
    out_specs = [tmaj(DEC_SEQ, wd) for wd in step_outs] + [pl.BlockSpec((nb, LANES), lambda i: (i, 0)),
                                                          tmaj(DEC_SEQ, D_POOL),
                                                          tmaj(SSM_CONV - 1, D_XBC), tmaj(POOL_HIST, D_POOL)]
    out_shape = [jax.ShapeDtypeStruct((DEC_SEQ, ns, wd), f32) for wd in step_outs] + [
        jax.ShapeDtypeStruct((ns, LANES), f32), jax.ShapeDtypeStruct((DEC_SEQ, ns, D_POOL), f32),
        jax.ShapeDtypeStruct((SSM_CONV - 1, ns, D_XBC), f32), jax.ShapeDtypeStruct((POOL_HIST, ns, D_POOL), f32)]
    return pl.pallas_call(
        _smix_in_kernel,
        grid=(ns // nb,),
        in_specs=[pl.BlockSpec((nb * X_ROWS_PER_SEQ, LANES), lambda i: (i, 0)), tmaj(SSM_CONV - 1, D_XBC),
                  tmaj(POOL_HIST, D_POOL),
                  _const_spec((1, D_MODEL)), _const_spec((D_MODEL, D_PROJ)), _const_spec((SSM_CONV, D_XBC)),
                  _const_spec((1, D_XBC)), _const_spec((1, LANES)), _const_spec((1, LANES)), _const_spec((1, D_SSM)),
                  _const_spec((len(POOL_WINDOWS), POOL_GROUP_DIM, POOL_GROUP_DIM)), _const_spec((1, D_POOL))],
        out_specs=out_specs,
        out_shape=out_shape,
        scratch_shapes=[pltpu.VMEM((DEC_SEQ * nb, D_MODEL), bf16), pltpu.VMEM((DEC_SEQ * nb, D_POOL), bf16)],
        compiler_params=pltpu.CompilerParams(dimension_semantics=("arbitrary",), vmem_limit_bytes=VMEM_LIMIT_BYTES),
        name="smix_in",
    )(x_rows, sconv, spool, w["nmix"], w["win"], w["cw"], w["cb"], w["dtb"], w["a_row"], w["dexp"], w["wpool"],
      w["pscale"])


def _sssd_kernel(dec_ref, c_ref, b_ref, xd_ref, st_ref, yoff_ref, stn_ref):
    blk = pl.program_id(0)
    nb = S_SSD_BLOCK
    gw = D_SSM // SSM_GROUPS
    hpg = SSM_HEADS // SSM_GROUPS
    row_seq = lax.broadcasted_iota(jnp.int32, (DEC_SEQ * nb, gw), 0) % nb
    cmat, bmat, xd = _steps(c_ref), _steps(b_ref), _steps(xd_ref)
    for g in range(SSM_GROUPS):
        gl = slice(g * gw, (g + 1) * gw)
        cg = cmat[:, g * SSM_STATE:(g + 1) * SSM_STATE].astype(bf16)
        bg = bmat[:, g * SSM_STATE:(g + 1) * SSM_STATE].astype(bf16)
        yo = jnp.zeros((DEC_SEQ * nb, gw), f32)
        for j in range(nb):
            mine = row_seq == j
            h0 = st_ref[j, gl, :]
            yo = jnp.where(mine, _dot_nt(cg, h0.astype(bf16)), yo)
            upd = _dot_tn(jnp.where(mine, xd[:, gl], 0.0).astype(bf16), bg)
            for hh in range(hpg):
                hr = slice(hh * SSM_HEAD_DIM, (hh + 1) * SSM_HEAD_DIM)
                d = dec_ref[(blk * nb + j) * SSM_HEADS + g * hpg + hh]
                stn_ref[j, g * gw + hh * SSM_HEAD_DIM:g * gw + (hh + 1) * SSM_HEAD_DIM, :] = h0[hr] * d + upd[hr]
        for l in range(DEC_SEQ):
            yoff_ref[l, :, gl] = yo[l * nb:(l + 1) * nb]


def _sssd(dec_flat, cmat, bmat, xd, state):
    ns = state.shape[0]
    nb = S_SSD_BLOCK
    tmaj = lambda width: pl.BlockSpec((DEC_SEQ, nb, width), lambda i: (0, i, 0))
    stblk = pl.BlockSpec((nb, D_SSM, SSM_STATE), lambda i: (i, 0, 0))
    return pl.pallas_call(
        _sssd_kernel,
        grid=(ns // nb,),
        in_specs=[pl.BlockSpec(memory_space=pltpu.SMEM), tmaj(D_BC), tmaj(D_BC), tmaj(D_SSM), stblk],
        out_specs=[tmaj(D_SSM), stblk],
        out_shape=[jax.ShapeDtypeStruct((DEC_SEQ, ns, D_SSM), f32), jax.ShapeDtypeStruct(state.shape, f32)],
        compiler_params=pltpu.CompilerParams(dimension_semantics=("arbitrary",), vmem_limit_bytes=VMEM_LIMIT_BYTES),
        name="sssd",
    )(dec_flat, cmat, bmat, xd, state)


def _smix_out_kernel(x_ref, ypart_ref, yoff_ref, eacs_ref, z_ref, pout_ref, snorm_ref, wout_ref, nmem_ref, wq_ref,
                     x1_ref, q_ref, cat_scr):
    y = ypart_ref[...] + yoff_ref[...] * eacs_ref[...]
    t = y * _silu(z_ref[...])
    gw = D_SSM // SSM_GROUPS
    for g in range(SSM_GROUPS):
        gl = slice(g * gw, (g + 1) * gw)
        tg = t[:, gl]
        ms = jnp.mean(tg * tg, axis=-1, keepdims=True)
        cat_scr[:, gl] = (tg * lax.rsqrt(ms + EPS) * snorm_ref[:, gl]).astype(bf16)
    cat_scr[:, D_SSM:] = pout_ref[...].astype(bf16)
    x1 = x_ref[...] + _dot(cat_scr[...], wout_ref[...])
    x1_ref[...] = x1
    h = _rmsnorm(x1, nmem_ref[...]).astype(bf16)
    q_ref[...] = _dot(h, wq_ref[...]) * (MEM_HEAD_DIM ** -0.5)


def _smix_out(x, ypart, yoff, eacs, z, pout, w):
    n = x.shape[0]
    rb = 128
    rows = lambda width: pl.BlockSpec((rb, width), lambda i: (i, 0))
    return pl.pallas_call(
        _smix_out_kernel,
        grid=(n // rb,),
        in_specs=[rows(D_MODEL), rows(D_SSM), rows(D_SSM), rows(D_SSM), rows(D_SSM), rows(D_POOL),
                  _const_spec((1, D_SSM)), _const_spec((D_SSM + D_POOL, D_MODEL)), _const_spec((1, D_MODEL)),
                  _const_spec((D_MODEL, D_MODEL))],
        out_specs=[rows(D_MODEL), rows(D_MODEL)],
        out_shape=[jax.ShapeDtypeStruct((n, D_MODEL), f32), jax.ShapeDtypeStruct((n, D_MODEL), f32)],
        scratch_shapes=[pltpu.VMEM((rb, D_SSM + D_POOL), bf16)],
        compiler_params=pltpu.CompilerParams(dimension_semantics=("arbitrary",), vmem_limit_bytes=VMEM_LIMIT_BYTES),
        name="smix_out",
    )(x, ypart, yoff, eacs, z, pout, w["snorm"], w["wout"], w["nmem"], w["wq"])


def _sattn_kernel(q_ref, k_ref, v_ref, o_ref):
    nb = S_ATT_BLOCK
    rg = DEC_SEQ * nb
    rows = MEM_HEADS * rg

    def row_ids(width):
        r = lax.broadcasted_iota(jnp.int32, (rows, width), 0)
        return r // rg, r % nb

    row_h, row_seq = row_ids(D_MODEL)
    col_h = lax.broadcasted_iota(jnp.int32, (rows, D_MODEL), 1) // MEM_HEAD_DIM
    q = _steps(q_ref)
    qh = jnp.where(row_h == col_h, jnp.concatenate([q] * MEM_HEADS, axis=0), 0.0)
    lhs_s = jnp.concatenate([jnp.where(row_seq == b, qh, 0.0).astype(bf16) for b in range(nb)], axis=1)
    kcat = jnp.concatenate([_kv_seq(k_ref, b) for b in range(nb)], axis=1)
    s = _dot_nt(lhs_s, kcat)
    p = jnp.exp(s - jnp.max(s, axis=-1, keepdims=True))
    p = p / jnp.sum(p, axis=-1, keepdims=True)
    _, row_seq_p = row_ids(N_MEM)
    lhs_p = jnp.concatenate([jnp.where(row_seq_p == b, p, 0.0).astype(bf16) for b in range(nb)], axis=1)
    vcat = jnp.concatenate([_kv_seq(v_ref, b) for b in range(nb)], axis=0)
    res = _dot(lhs_p, vcat)
    for hd in range(MEM_HEADS):
        hl = slice(hd * MEM_HEAD_DIM, (hd + 1) * MEM_HEAD_DIM)
        for l in range(DEC_SEQ):
            o_ref[l, :, hl] = res[hd * rg + l * nb:hd * rg + (l + 1) * nb, hl]


def _sattn(q, mem_k, mem_v):
    ns = mem_k.shape[0]
    nb = S_ATT_BLOCK
    qblk = pl.BlockSpec((DEC_SEQ, nb, D_MODEL), lambda i: (0, i, 0))
    kvblk = pl.BlockSpec((nb, N_MEM * KV_ROWS_PER_TOKEN, LANES), lambda i: (i, 0, 0))
    return pl.pallas_call(
        _sattn_kernel,
        grid=(ns // nb,),
        in_specs=[qblk, kvblk, kvblk],
        out_specs=qblk,
        out_shape=jax.ShapeDtypeStruct((DEC_SEQ, ns, D_MODEL), f32),
        compiler_params=pltpu.CompilerParams(dimension_semantics=("arbitrary",), vmem_limit_bytes=VMEM_LIMIT_BYTES),
        name="sattn",
    )(q, mem_k, mem_v)


def _sffn_kernel(x1_ref, ao_ref, sffn_ref, wo_ref, g_ref, wup_ref, cw_ref, cb_ref, wdn_ref, gfin_ref,
                 y_ref, st_ref, h_scr, act_scr):
    nb = S_SEQ_BLOCK
    x2 = _steps(x1_ref) + _dot(_steps(ao_ref).astype(bf16), wo_ref[...])
    h_scr[...] = _rmsnorm(x2, g_ref[...]).astype(bf16)
    u = _dot(h_scr[...], wup_ref[...])
    hist = FFN_CONV - 1

    def conv_block(blk):
        cols = slice(blk * FF_CHUNK, (blk + 1) * FF_CHUNK)
        tiles = range(blk * FF_CHUNK // LANES, (blk + 1) * FF_CHUNK // LANES)
        slots = [_rows_view_get(sffn_ref, i, hist, tiles, nb, FFN_ROWS_PER_SEQ) for i in range(hist)]
        slots += [u[l * nb:(l + 1) * nb, cols] for l in range(DEC_SEQ)]
        for i in range(hist):
            _rows_view_put(st_ref, i, hist, tiles, nb, FFN_ROWS_PER_SEQ, slots[DEC_SEQ + i])
        outs = []
        for l in range(DEC_SEQ):
            acc = cb_ref[:, cols] + cw_ref[0:1, cols] * slots[l]
            for k in range(1, FFN_CONV):
                acc = acc + cw_ref[k:k + 1, cols] * slots[l + k]
            outs.append(acc)
        return jnp.concatenate(outs, axis=0)

    for j in range(N_FF_CHUNKS):
        act = _silu(conv_block(_gate_blk(j))) * conv_block(_val_blk(j))
        act_scr[:, j * FF_CHUNK:(j + 1) * FF_CHUNK] = act.astype(bf16)
    y = _rmsnorm(x2 + _dot(act_scr[...], wdn_ref[...]), gfin_ref[...])
    for l in range(DEC_SEQ):
        _rows_view_put(y_ref, l, DEC_SEQ, range(D_MODEL // LANES), nb, X_ROWS_PER_SEQ, y[l * nb:(l + 1) * nb])


def _sffn(x1, ao, sffn_rows, w):
    ns = x1.shape[1]
    nb = S_SEQ_BLOCK
    tok = pl.BlockSpec((DEC_SEQ, nb, D_MODEL), lambda i: (0, i, 0))
    yblk = pl.BlockSpec((nb * X_ROWS_PER_SEQ, LANES), lambda i: (i, 0))
    stblk = pl.BlockSpec((nb * FFN_ROWS_PER_SEQ, LANES), lambda i: (i, 0))
    return pl.pallas_call(
        _sffn_kernel,
        grid=(ns // nb,),
        in_specs=[tok, tok, stblk,
                  _const_spec((D_MODEL, D_MODEL)), _const_spec((1, D_MODEL)), _const_spec((D_MODEL, 2 * D_FF)),
                  _const_spec((FFN_CONV, 2 * D_FF)), _const_spec((1, 2 * D_FF)),
                  _const_spec((D_FF, D_MODEL)), _const_spec((1, D_MODEL))],
        out_specs=[yblk, stblk],
        out_shape=[jax.ShapeDtypeStruct((ns * X_ROWS_PER_SEQ, LANES), f32),
                   jax.ShapeDtypeStruct((ns * FFN_ROWS_PER_SEQ, LANES), f32)],
        scratch_shapes=[pltpu.VMEM((DEC_SEQ * nb, D_MODEL), bf16), pltpu.VMEM((DEC_SEQ * nb, D_FF), bf16)],
        compiler_params=pltpu.CompilerParams(dimension_semantics=("arbitrary",), vmem_limit_bytes=VMEM_LIMIT_BYTES),
        name="sffn",
    )(x1, ao, sffn_rows, w["wo"], w["nffn"], w["wup"], w["fcw"], w["fcb"], w["wdn"], w["gfin"])


def _sample_path(x_sample, state_ssm, state_ssm_conv, state_pool, state_ffn_conv, cache_mem_k, cache_mem_v, w):
    ns = x_sample.shape[0]
    xtm, z, ypart, eacs, xd, cmat, bmat, dec, pout, conv_new, pool_new = _smix_in(
        _tiled_rows_view(x_sample), state_ssm_conv[0].transpose(1, 0, 2), state_pool[0].transpose(1, 0, 2), w)
    yoff, ssm_new = _sssd(dec[:, :SSM_HEADS].reshape(-1), cmat, bmat, xd, state_ssm[0].reshape(ns, D_SSM, SSM_STATE))
    flat = lambda a: a.reshape(DEC_SEQ * ns, a.shape[-1])
    x1, q = _smix_out(flat(xtm), flat(ypart), flat(yoff), flat(eacs), flat(z), flat(pout), w)
    ao = _sattn(q.reshape(DEC_SEQ, ns, D_MODEL), _kv_rows_view(cache_mem_k[0]), _kv_rows_view(cache_mem_v[0]))
    y_rows, ffn_rows = _sffn(x1.reshape(DEC_SEQ, ns, D_MODEL), ao, _tiled_rows_view(state_ffn_conv[0]), w)
    return (_from_tiled_rows(y_rows, ns, DEC_SEQ, D_MODEL),
            ssm_new.reshape(1, ns, SSM_HEADS, SSM_HEAD_DIM, SSM_STATE),
            conv_new.transpose(1, 0, 2)[None], pool_new.transpose(1, 0, 2)[None],
            _from_tiled_rows(ffn_rows, ns, FFN_CONV - 1, 2 * D_FF)[None])


def kernel(x_prompt, x_sample, mem_prompt, state_ssm, state_ssm_conv, state_pool, state_ffn_conv, cache_mem_k, cache_mem_v, norm_mix, w_in, ssm_conv_w, ssm_conv_b, ssm_dt_bias, ssm_a_log, ssm_d, ssm_norm, w_pool, pool_scale, w_out, norm_mem, norm_memkv, w_mq, w_mk, w_mv, w_mo, norm_ffn, w_up, ffn_conv_w, ffn_conv_b, w_down, final_norm):
    params = dict(norm_mix=norm_mix, w_in=w_in, ssm_conv_w=ssm_conv_w, ssm_conv_b=ssm_conv_b, ssm_dt_bias=ssm_dt_bias,
                  ssm_a_log=ssm_a_log, ssm_d=ssm_d, ssm_norm=ssm_norm, w_pool=w_pool, pool_scale=pool_scale,
                  w_out=w_out, norm_mem=norm_mem, norm_memkv=norm_memkv, w_mq=w_mq, w_mk=w_mk, w_mv=w_mv, w_mo=w_mo,
                  norm_ffn=norm_ffn, w_up=w_up, ffn_conv_w=ffn_conv_w, ffn_conv_b=ffn_conv_b, w_down=w_down,
                  final_norm=final_norm)
    w = _prep_weights(params)
    yp, ssm_p, conv_p, pool_p, ffn_p, mk_p, mv_p = _prompt_path(x_prompt, mem_prompt, w, PROMPT_TILE)
    ys, ssm_s, conv_s, pool_s, ffn_s = _sample_path(x_sample, state_ssm, state_ssm_conv, state_pool, state_ffn_conv,
                                                    cache_mem_k, cache_mem_v, w)
    return yp, ys, ssm_p, ssm_s, conv_p, conv_s, pool_p, pool_s, ffn_p, ffn_s, mk_p, mv_p
```

```python
import functools

import jax
import jax.numpy as jnp
from jax import lax
from jax.experimental import pallas as pl
from jax.experimental.pallas import tpu as pltpu

f32 = jnp.float32
bf16 = jnp.bfloat16

D_MODEL = 1024
SSM_HEADS = 16
SSM_HEAD_DIM = 64
SSM_STATE = 128
SSM_GROUPS = 2
SSM_CHUNK = 128
D_SSM = 1024
D_BC = SSM_GROUPS * SSM_STATE
D_XBC = D_SSM + 2 * D_BC
SSM_CONV = 4
D_POOL = 1024
POOL_WINDOWS = (2, 4, 8, 16)
POOL_GROUP_DIM = 256
POOL_HIST = 15
N_MEM = 256
MEM_HEADS = 4
MEM_HEAD_DIM = 256
D_FF = 2816
FFN_CONV = 3
EPS = 1e-6
PAST_LEN = 16384

LANES = 128
SUBLANES = 8
MXU_DIM = 256
VMEM_LIMIT_BYTES = 56 * 1024 * 1024

CONV_HIST_ROWS = SUBLANES
POOL_HIST_ROWS = 2 * SUBLANES
FF_CHUNK = MXU_DIM
N_FF_CHUNKS = D_FF // FF_CHUNK


def _silu(v):
    return v * (1.0 / (1.0 + jnp.exp(-v)))


def _softplus(v):
    return jnp.maximum(v, 0.0) + jnp.log1p(jnp.exp(-jnp.abs(v)))


def _rmsnorm(x, g):
    ms = jnp.mean(x * x, axis=-1, keepdims=True)
    return x * lax.rsqrt(ms + EPS) * g


def _dot(a, b):
    return jnp.dot(a, b, preferred_element_type=f32)


def _dot_nt(a, b):
    return lax.dot_general(a, b, (((1,), (1,)), ((), ())), preferred_element_type=f32)


def _dot_tn(a, b):
    return lax.dot_general(a, b, (((0,), (0,)), ((), ())), preferred_element_type=f32)


def _split3(v):
    p1 = v.astype(bf16)
    r1 = v - p1.astype(f32)
    p2 = r1.astype(bf16)
    r2 = r1 - p2.astype(f32)
    return p1, p2, r2.astype(bf16)


def _const_spec(shape):
    return pl.BlockSpec(shape, lambda *_: (0,) * len(shape), pipeline_mode=pl.Buffered(1))


KV_LANE_TILES = MEM_HEAD_DIM // LANES
KV_ROWS_PER_TOKEN = KV_LANE_TILES * MEM_HEADS


def _kv_rows_view(kv):
    ns = kv.shape[0]
    kv = kv.reshape(ns, N_MEM, MEM_HEADS, KV_LANE_TILES, LANES).transpose(0, 1, 3, 2, 4)
    return kv.reshape(ns, N_MEM * KV_ROWS_PER_TOKEN, LANES)


def _kv_from_rows(rows):
    ns = rows.shape[0]
    kv = rows.reshape(ns, N_MEM, KV_LANE_TILES, MEM_HEADS, LANES).transpose(0, 1, 3, 2, 4)
    return kv.reshape(ns, N_MEM, MEM_HEADS, MEM_HEAD_DIM)


def _kv_seq(ref, jj):
    tiles = [ref[jj, pl.ds(dt * MEM_HEADS + hd, N_MEM, stride=KV_ROWS_PER_TOKEN), :]
             for hd in range(MEM_HEADS) for dt in range(KV_LANE_TILES)]
    return jnp.concatenate(tiles, axis=1).astype(bf16)


def _memkv_kernel(mem_ref, g_ref, wk_ref, wv_ref, k_ref, v_ref):
    h = _rmsnorm(mem_ref[0], g_ref[...]).astype(bf16)
    for out_ref, w_ref in ((k_ref, wk_ref), (v_ref, wv_ref)):
        kv = _dot(h, w_ref[...])
        for hd in range(MEM_HEADS):
            for dt in range(KV_LANE_TILES):
                col = hd * MEM_HEAD_DIM + dt * LANES
                out_ref[0, pl.ds(dt * MEM_HEADS + hd, N_MEM, stride=KV_ROWS_PER_TOKEN), :] = kv[:, col:col + LANES]


def _memkv(mem, g, wk, wv):
    b = mem.shape[0]
    blk = pl.BlockSpec((1, N_MEM, D_MODEL), lambda i: (i, 0, 0))
    oblk = pl.BlockSpec((1, N_MEM * KV_ROWS_PER_TOKEN, LANES), lambda i: (i, 0, 0))
    return pl.pallas_call(
        _memkv_kernel,
        grid=(b,),
        in_specs=[blk, _const_spec((1, D_MODEL)), _const_spec((D_MODEL, D_MODEL)), _const_spec((D_MODEL, D_MODEL))],
        out_specs=[oblk, oblk],
        out_shape=[jax.ShapeDtypeStruct((b, N_MEM * KV_ROWS_PER_TOKEN, LANES), f32)] * 2,
        compiler_params=pltpu.CompilerParams(dimension_semantics=("arbitrary",), vmem_limit_bytes=VMEM_LIMIT_BYTES),
        name="memkv",
    )(mem, g, wk, wv)


def _ssd_chunk(r0, dt_scr, xs_scr, b_scr, c_scr, y_scr, xd_scr, hst_scr, a_ref, dexp_ref, between=()):
    between = list(between) + [None] * 3
    q = SSM_CHUNK
    rows = pl.ds(r0, q)
    row_i = lax.broadcasted_iota(jnp.int32, (q, q), 0)
    col_i = lax.broadcasted_iota(jnp.int32, (q, q), 1)
    causal = col_i <= row_i
    lo = col_i < SSM_HEAD_DIM
    tril = jnp.where(causal, 1.0, 0.0).astype(bf16)

    dt = dt_scr[rows, :]
    da = dt * a_ref[...]
    p1, p2, p3 = _split3(da)
    acs = _dot(tril, p1) + _dot(tril, p2) + _dot(tril, p3)
    acs_t = acs.T
    if between[0] is not None:
        between[0]()

    for g in range(SSM_GROUPS):
        bg = b_scr[rows, g * SSM_STATE:(g + 1) * SSM_STATE]
        cg = c_scr[rows, g * SSM_STATE:(g + 1) * SSM_STATE]
        bg_b = bg.astype(bf16)
        cb = jnp.where(causal, _dot_nt(cg.astype(bf16), bg_b), 0.0)
        cdec_rows = []
        pairs_per_group = SSM_HEADS // SSM_GROUPS // 2
        for pp in range(pairs_per_group):
            pr = g * pairs_per_group + pp
            lanes = slice(pr * LANES, (pr + 1) * LANES)
            lhs, dtb, dend, cdec = [], [], [], []
            for hh in (2 * pr, 2 * pr + 1):
                colb = jnp.broadcast_to(acs[:, hh:hh + 1], (q, q))
                seg = jnp.where(causal, colb - acs_t[hh:hh + 1, :], 0.0)
                lhs.append((jnp.exp(seg) * cb).astype(bf16))
                lhs.append((cg * jnp.exp(colb)).astype(bf16))
                last = colb[q - 1:q, :]
                dend.append(jnp.exp(last - colb))
                cdec.append(jnp.exp(last))
                dtb.append(jnp.broadcast_to(dt[:, hh:hh + 1], (q, q)))
            xs_pair = xs_scr[rows, lanes]
            xdt = xs_pair * jnp.where(lo, dtb[0], dtb[1])
            xd_scr[:, lanes] = (xdt * jnp.where(lo, dend[0], dend[1])).astype(bf16)
            hst_pair = hst_scr[:, lanes]
            rhs = jnp.concatenate([
                jnp.where(lo, xdt, 0.0).astype(bf16), jnp.where(lo, hst_pair, 0.0).astype(bf16),
                jnp.where(lo, 0.0, xdt).astype(bf16), jnp.where(lo, 0.0, hst_pair).astype(bf16)], axis=0)
            y_pair = _dot(jnp.concatenate(lhs, axis=1), rhs)
            y_scr[rows, lanes] = y_pair + xs_pair * dexp_ref[:, lanes]
            cdec_rows.append(jnp.where(lo[:1], cdec[0], cdec[1]))
        gl = slice(g * (D_SSM // SSM_GROUPS), (g + 1) * (D_SSM // SSM_GROUPS))
        upd = _dot_tn(bg_b, xd_scr[:, gl])
        hst_scr[:, gl] = hst_scr[:, gl] * jnp.concatenate(cdec_rows, axis=1) + upd
        if between[1 + g] is not None:
            between[1 + g]()


def _mixer_kernel(x_ref, nmix_ref, wzx_ref, wvp_ref, wdt_ref, cw_ref, cb_ref, dtb_ref, a_ref, dexp_ref, snorm_ref, wpool_ref,
                  pscale_ref, wout_ref,
                  x1_ref, ssm_ref, conv_ref, pool_ref,
                  xd_scr, hst_scr, *sub_scr, tile, sub):
    t = pl.program_id(1)
    last_t = pl.num_programs(1) - 1
    ch = CONV_HIST_ROWS
    ph = POOL_HIST_ROWS
    n_sub = tile // sub
    per = len(sub_scr) // n_sub
    bufs = [sub_scr[i * per:(i + 1) * per] for i in range(n_sub)]
    xbc0, vp0 = bufs[0][2], bufs[0][3]

    @pl.when(t == 0)
    def _():
        xbc0[:, 0:ch, :] = jnp.zeros((D_XBC // LANES, ch, LANES), f32)
        vp0[:, 0:ph, :] = jnp.zeros((D_POOL // LANES, ph, LANES), f32)
        hst_scr[...] = jnp.zeros_like(hst_scr)

    gw = D_SSM // SSM_GROUPS

    def carry_history(src, dst):
        for j in range(D_XBC // LANES):
            bufs[dst][2][j, 0:ch, :] = bufs[src][2][j, sub:sub + ch, :]
        for j in range(D_POOL // LANES):
            bufs[dst][3][j, 0:ph, :] = bufs[src][3][j, sub:sub + ph, :]

    def in_xbc(s):
        h_scr, _, xbc_scr = bufs[s][:3]
        h_scr[...] = _rmsnorm(x_ref[0, s * sub:(s + 1) * sub, :], nmix_ref[...]).astype(bf16)
        xbc = _dot(h_scr[...], wzx_ref[:, D_SSM:])
        for j in range(D_XBC // LANES):
            xbc_scr[j, ch:ch + sub, :] = xbc[:, j * LANES:(j + 1) * LANES]

    def in_dtvp(s):
        h_scr, _, _, vp_scr, dt_scr = bufs[s][:5]
        dt_scr[...] = _dot(h_scr[...], wdt_ref[...])
        vp = _dot(h_scr[...], wvp_ref[...])
        for j in range(D_POOL // LANES):
            vp_scr[j, ph:ph + sub, :] = vp[:, j * LANES:(j + 1) * LANES]

    def in_z(s):
        h_scr, z_scr = bufs[s][:2]
        z_scr[...] = _dot(h_scr[...], wzx_ref[:, :D_SSM])
        if s > 0:
            carry_history(s - 1, s)

    def stage_mid(s, between):
        _, z_scr, xbc_scr, vp_scr, dt_scr, xs_scr, b_scr, c_scr, y_scr, pooled_scr, cat_scr = bufs[s]
        for j in range(D_XBC // LANES):
            cl = slice(j * LANES, (j + 1) * LANES)
            acc = cb_ref[:, cl] + cw_ref[0:1, cl] * xbc_scr[j, pl.ds(ch - 3, sub), :]
            for k in range(1, SSM_CONV):
                acc = acc + cw_ref[k:k + 1, cl] * xbc_scr[j, pl.ds(ch - 3 + k, sub), :]
            act = _silu(acc)
            if j < D_SSM // LANES:
                xs_scr[:, cl] = act
            elif j < (D_SSM + D_BC) // LANES:
                b_scr[:, j * LANES - D_SSM:(j + 1) * LANES - D_SSM] = act
            else:
                c_scr[:, j * LANES - D_SSM - D_BC:(j + 1) * LANES - D_SSM - D_BC] = act

        pos1 = lax.broadcasted_iota(jnp.int32, (sub, LANES), 0) + (t * tile + s * sub + 1)
        for j in range(D_POOL // LANES):
            cl = slice(j * LANES, (j + 1) * LANES)
            w = POOL_WINDOWS[j * LANES // POOL_GROUP_DIM]
            cur = vp_scr[j, ph:ph + sub, :]
            acc = cur
            for k in range(1, w):
                acc = acc + vp_scr[j, pl.ds(ph - k, sub), :]
            cnt = jnp.minimum(pos1, w).astype(f32)
            pooled_scr[:, cl] = (acc / cnt - cur).astype(bf16)
        for gi in range(len(POOL_WINDOWS)):
            gl = slice(gi * POOL_GROUP_DIM, (gi + 1) * POOL_GROUP_DIM)
            pg = _dot(pooled_scr[:, gl], wpool_ref[gi]) * pscale_ref[:, gl]
            cat_scr[:, D_SSM + gi * POOL_GROUP_DIM:D_SSM + (gi + 1) * POOL_GROUP_DIM] = pg.astype(bf16)

        dt_scr[...] = _softplus(dt_scr[...] + dtb_ref[...])
        n_chunks = sub // SSM_CHUNK
        slots = [None] * (3 * n_chunks)
        for i, piece in enumerate(between):
            slots[i * len(slots) // len(between)] = piece
        for c in range(n_chunks):
            _ssd_chunk(c * SSM_CHUNK, dt_scr, xs_scr, b_scr, c_scr, y_scr, xd_scr, hst_scr, a_ref, dexp_ref,
                       slots[3 * c:3 * c + 3])

        for g in range(SSM_GROUPS):
            gl = slice(g * gw, (g + 1) * gw)
            tg = y_scr[:, gl] * _silu(z_scr[:, gl])
            ms = jnp.mean(tg * tg, axis=-1, keepdims=True)
            cat_scr[:, gl] = (tg * lax.rsqrt(ms + EPS) * snorm_ref[:, gl]).astype(bf16)

    def stage_out(s):
        rows = slice(s * sub, (s + 1) * sub)
        x1_ref[0, rows, :] = x_ref[0, rows, :] + _dot(bufs[s][-1][...], wout_ref[...])

    in_xbc(0)
    in_dtvp(0)
    in_z(0)
    for s in range(n_sub):
        nxt = s + 1 < n_sub
        if nxt:
            in_xbc(s + 1)
        pieces = [functools.partial(in_dtvp, s + 1), functools.partial(in_z, s + 1)] if nxt else []
        if s > 0:
            pieces.append(functools.partial(stage_out, s - 1))
        stage_mid(s, pieces)
    stage_out(n_sub - 1)
    carry_history(n_sub - 1, 0)

    @pl.when(t == last_t)
    def _():
        for pr in range(D_SSM // LANES):
            ssm_ref[0, pr * LANES:(pr + 1) * LANES, :] = hst_scr[:, pr * LANES:(pr + 1) * LANES].T
        for j in range(D_XBC // LANES):
            conv_ref[0, :, j * LANES:(j + 1) * LANES] = xbc0[j, pl.ds(ch - (SSM_CONV - 1), SSM_CONV - 1), :]
        for j in range(D_POOL // LANES):
            pool_ref[0, :, j * LANES:(j + 1) * LANES] = vp0[j, pl.ds(ph - POOL_HIST, POOL_HIST), :]


def _mixer_prompt(x, nmix, wzx, wvp, wdt, cw, cb, dtb, a_row, dexp, snorm, wpool, pscale, wout, tile):
    b, seq, _ = x.shape
    nt = seq // tile
    sub = min(tile, MIXER_SUB_TILE)
    xblk = pl.BlockSpec((1, tile, D_MODEL), lambda i, j: (i, j, 0))
    sub_scratch = [
        pltpu.VMEM((sub, D_MODEL), bf16),
        pltpu.VMEM((sub, D_SSM), f32),
        pltpu.VMEM((D_XBC // LANES, CONV_HIST_ROWS + sub, LANES), f32),
        pltpu.VMEM((D_POOL // LANES, POOL_HIST_ROWS + sub, LANES), f32),
        pltpu.VMEM((sub, LANES), f32),
        pltpu.VMEM((sub, D_SSM), f32),
        pltpu.VMEM((sub, D_BC), f32),
        pltpu.VMEM((sub, D_BC), f32),
        pltpu.VMEM((sub, D_SSM), f32),
        pltpu.VMEM((sub, D_POOL), bf16),
        pltpu.VMEM((sub, D_SSM + D_POOL), bf16),
    ]
    scratch = [pltpu.VMEM((SSM_CHUNK, D_SSM), bf16),
               pltpu.VMEM((SSM_STATE, D_SSM), f32)]
    scratch += sub_scratch * (tile // sub)
    return pl.pallas_call(
        functools.partial(_mixer_kernel, tile=tile, sub=sub),
        grid=(b, nt),
        in_specs=[xblk, _const_spec((1, D_MODEL)), _const_spec((D_MODEL, D_SSM + D_XBC)),
                  _const_spec((D_MODEL, D_POOL)), _const_spec((D_MODEL, LANES)), _const_spec((SSM_CONV, D_XBC)),
                  _const_spec((1, D_XBC)), _const_spec((1, LANES)), _const_spec((1, LANES)), _const_spec((1, D_SSM)),
                  _const_spec((1, D_SSM)), _const_spec((len(POOL_WINDOWS), POOL_GROUP_DIM, POOL_GROUP_DIM)),
                  _const_spec((1, D_POOL)), _const_spec((D_SSM + D_POOL, D_MODEL))],
        out_specs=[xblk,
                   pl.BlockSpec((1, D_SSM, SSM_STATE), lambda i, j: (i, 0, 0)),
                   pl.BlockSpec((1, SSM_CONV - 1, D_XBC), lambda i, j: (i, 0, 0)),
                   pl.BlockSpec((1, POOL_HIST, D_POOL), lambda i, j: (i, 0, 0))],
        out_shape=[jax.ShapeDtypeStruct((b, seq, D_MODEL), f32),
                   jax.ShapeDtypeStruct((b, D_SSM, SSM_STATE), f32),
                   jax.ShapeDtypeStruct((b, SSM_CONV - 1, D_XBC), f32),
                   jax.ShapeDtypeStruct((b, POOL_HIST, D_POOL), f32)],
        scratch_shapes=scratch,
        compiler_params=pltpu.CompilerParams(dimension_semantics=("arbitrary", "arbitrary"),
                                             vmem_limit_bytes=VMEM_LIMIT_BYTES),
        name="mixer_prompt",
    )(x, nmix, wzx, wvp, wdt, cw, cb, dtb, a_row, dexp, snorm, wpool, pscale, wout)


def _attn_kernel(x_ref, g_ref, wq_ref, k_ref, v_ref, wo_ref, o_ref, q_scr, ao_scr):
    tile = x_ref.shape[1]
    sub = min(tile, ATTN_SUB_TILE)
    n_sub = tile // sub
    heads = [slice(hd * MEM_HEAD_DIM, (hd + 1) * MEM_HEAD_DIM) for hd in range(MEM_HEADS)]
    k = _kv_seq(k_ref, 0)
    v = _kv_seq(v_ref, 0)

    def q_proj(s):
        rows = slice(s * sub, (s + 1) * sub)
        h = _rmsnorm(x_ref[0, rows, :], g_ref[...]).astype(bf16)
        q_scr[rows, :] = (_dot(h, wq_ref[...]) * (MEM_HEAD_DIM ** -0.5)).astype(bf16)

    def scores(s):
        rows = slice(s * sub, (s + 1) * sub)
        return [_dot_nt(q_scr[rows, hl], k[:, hl]) for hl in heads]

    def values(s, sc):
        rows = slice(s * sub, (s + 1) * sub)
        for hl, s_h in zip(heads, sc):
            p = jnp.exp(s_h - jnp.max(s_h, axis=-1, keepdims=True))
            p = (p / jnp.sum(p, axis=-1, keepdims=True)).astype(bf16)
            ao_scr[rows, hl] = _dot(p, v[:, hl]).astype(bf16)

    def o_proj(s):
        rows = slice(s * sub, (s + 1) * sub)
        o_ref[0, rows, :] = x_ref[0, rows, :] + _dot(ao_scr[rows, :], wo_ref[...])

    q_proj(0)
    for s in range(n_sub):
        sc = scores(s)
        if s + 1 < n_sub:
            q_proj(s + 1)
        if s > 0:
            o_proj(s - 1)
        values(s, sc)
    o_proj(n_sub - 1)


def _attn_prompt(x, g, wq, mem_k, mem_v, wo, tile):
    b, seq, _ = x.shape
    xblk = pl.BlockSpec((1, tile, D_MODEL), lambda i, j: (i, j, 0))
    kvblk = pl.BlockSpec((1, N_MEM * KV_ROWS_PER_TOKEN, LANES), lambda i, j: (i, 0, 0))
    return pl.pallas_call(
        _attn_kernel,
        grid=(b, seq // tile),
        in_specs=[xblk, _const_spec((1, D_MODEL)), _const_spec((D_MODEL, D_MODEL)), kvblk, kvblk,
                  _const_spec((D_MODEL, D_MODEL))],
        out_specs=xblk,
        out_shape=jax.ShapeDtypeStruct((b, seq, D_MODEL), f32),
        scratch_shapes=[pltpu.VMEM((tile, D_MODEL), bf16), pltpu.VMEM((tile, D_MODEL), bf16)],
        compiler_params=pltpu.CompilerParams(dimension_semantics=("arbitrary", "arbitrary"),
                                             vmem_limit_bytes=VMEM_LIMIT_BYTES),
        name="attn_prompt",
    )(x, g, wq, mem_k, mem_v, wo)


def _gate_blk(j):
    return j


def _val_blk(j):
    return N_FF_CHUNKS + j


def _ffn_kernel(x_ref, g_ref, wup_ref, cw_ref, cb_ref, wdn_ref, gfin_ref, y_ref, st_ref, h_scr, u_scr, act_scr, *, tile):
    t = pl.program_id(1)
    last_t = pl.num_programs(1) - 1
    ch = CONV_HIST_ROWS
    tpc = FF_CHUNK // LANES
    ntile = 2 * D_FF // LANES

    @pl.when(t == 0)
    def _():
        u_scr[:, 0:ch, :] = jnp.zeros((ntile, ch, LANES), f32)

    x = x_ref[0]
    h_scr[...] = _rmsnorm(x, g_ref[...]).astype(bf16)
    u = _dot(h_scr[...], wup_ref[...])
    for ti in range(ntile):
        u_scr[ti, ch:ch + tile, :] = u[:, ti * LANES:(ti + 1) * LANES]

    def conv(ti):
        cl = slice(ti * LANES, (ti + 1) * LANES)
        acc = cb_ref[:, cl] + cw_ref[0:1, cl] * u_scr[ti, pl.ds(ch - 2, tile), :]
        acc = acc + cw_ref[1:2, cl] * u_scr[ti, pl.ds(ch - 1, tile), :]
        return acc + cw_ref[2:3, cl] * u_scr[ti, ch:ch + tile, :]

    for j in range(N_FF_CHUNKS):
        for i in range(tpc):
            gate = conv(_gate_blk(j) * tpc + i)
            val = conv(_val_blk(j) * tpc + i)
            act_scr[:, (j * tpc + i) * LANES:(j * tpc + i + 1) * LANES] = (_silu(gate) * val).astype(bf16)
    for ti in range(ntile):
        u_scr[ti, 0:ch, :] = u_scr[ti, tile:tile + ch, :]
    y_ref[0] = _rmsnorm(x + _dot(act_scr[...], wdn_ref[...]), gfin_ref[...])

    @pl.when(t == last_t)
    def _():
        for ti in range(ntile):
            st_ref[0, :, ti * LANES:(ti + 1) * LANES] = u_scr[ti, pl.ds(ch - (FFN_CONV - 1), FFN_CONV - 1), :]


def _ffn_prompt(x, g, wup, cw, cb, wdn, gfin, tile):
    b, seq, _ = x.shape
    xblk = pl.BlockSpec((1, tile, D_MODEL), lambda i, j: (i, j, 0))
    return pl.pallas_call(
        functools.partial(_ffn_kernel, tile=tile),
        grid=(b, seq // tile),
        in_specs=[xblk, _const_spec((1, D_MODEL)), _const_spec((D_MODEL, 2 * D_FF)),
                  _const_spec((FFN_CONV, 2 * D_FF)), _const_spec((1, 2 * D_FF)),
                  _const_spec((D_FF, D_MODEL)), _const_spec((1, D_MODEL))],
        out_specs=[xblk, pl.BlockSpec((1, FFN_CONV - 1, 2 * D_FF), lambda i, j: (i, 0, 0))],
        out_shape=[jax.ShapeDtypeStruct((b, seq, D_MODEL), f32),
                   jax.ShapeDtypeStruct((b, FFN_CONV - 1, 2 * D_FF), f32)],
        scratch_shapes=[pltpu.VMEM((tile, D_MODEL), bf16),
                        pltpu.VMEM((2 * D_FF // LANES, CONV_HIST_ROWS + tile, LANES), f32),
                        pltpu.VMEM((tile, D_FF), bf16)],
        compiler_params=pltpu.CompilerParams(dimension_semantics=("arbitrary", "arbitrary"),
                                             vmem_limit_bytes=VMEM_LIMIT_BYTES),
        name="ffn_prompt",
    )(x, g, wup, cw, cb, wdn, gfin)


def _prep_weights(p):
    i = 0
    w_in = p["w_in"][i]
    n_zx = D_SSM + D_XBC
    wzx = w_in[:, :n_zx].astype(bf16)
    wdt = jnp.pad(w_in[:, n_zx:n_zx + SSM_HEADS], ((0, 0), (0, LANES - SSM_HEADS))).astype(bf16)
    wvp = w_in[:, n_zx + SSM_HEADS:].astype(bf16)
    pad_h = (0, LANES - SSM_HEADS)
    return dict(
        nmix=p["norm_mix"][i][None], wzx=wzx, wvp=wvp, wdt=wdt, cw=p["ssm_conv_w"][i], cb=p["ssm_conv_b"][i][None],
        dtb=jnp.pad(p["ssm_dt_bias"][i], pad_h)[None],
        a_row=jnp.pad(-jnp.exp(p["ssm_a_log"][i].astype(f32)), pad_h)[None],
        dexp=jnp.repeat(p["ssm_d"][i], SSM_HEAD_DIM)[None], snorm=p["ssm_norm"][i][None],
        wpool=p["w_pool"][i].astype(bf16), pscale=p["pool_scale"][i][None], wout=p["w_out"][i].astype(bf16),
        nmem=p["norm_mem"][i][None], nmemkv=p["norm_memkv"][i][None],
        wq=p["w_mq"][i].astype(bf16), wk=p["w_mk"][i].astype(bf16), wv=p["w_mv"][i].astype(bf16),
        wo=p["w_mo"][i].astype(bf16),
        nffn=p["norm_ffn"][i][None],
        wup=p["w_up"][i].astype(bf16), fcw=p["ffn_conv_w"][i], fcb=p["ffn_conv_b"][i][None],
        wdn=p["w_down"][i].astype(bf16),
        gfin=p["final_norm"][None],
    )


def _prompt_path(x_prompt, mem_prompt, w, tile):
    b = x_prompt.shape[0]
    mem_k, mem_v = _memkv(mem_prompt, w["nmemkv"], w["wk"], w["wv"])
    x1, ssm, conv, pool = _mixer_prompt(x_prompt, w["nmix"], w["wzx"], w["wvp"], w["wdt"], w["cw"], w["cb"], w["dtb"], w["a_row"],
                                        w["dexp"], w["snorm"], w["wpool"], w["pscale"], w["wout"],
                                        min(MIXER_TILE, x_prompt.shape[1]))
    x2 = _attn_prompt(x1, w["nmem"], w["wq"], mem_k, mem_v, w["wo"], tile)
    y, ffn = _ffn_prompt(x2, w["nffn"], w["wup"], w["fcw"], w["fcb"], w["wdn"], w["gfin"], tile)
    return (y, ssm.reshape(1, b, SSM_HEADS, SSM_HEAD_DIM, SSM_STATE), conv[None], pool[None], ffn[None],
            _kv_from_rows(mem_k)[None], _kv_from_rows(mem_v)[None])


PROMPT_TILE = 512
MIXER_TILE = 1024
MIXER_SUB_TILE = 256
ATTN_SUB_TILE = 256


DEC_SEQ = 4
S_SEQ_BLOCK = 64
S_SSD_BLOCK = 8
S_ATT_BLOCK = 8
X_ROWS_PER_SEQ = DEC_SEQ * D_MODEL // LANES
FFN_ROWS_PER_SEQ = (FFN_CONV - 1) * 2 * D_FF // LANES


def _expand_heads(v, lo):
    r = v.shape[0]
    tiles = []
    for pr in range(SSM_HEADS // 2):
        a = jnp.broadcast_to(v[:, 2 * pr:2 * pr + 1], (r, LANES))
        b = jnp.broadcast_to(v[:, 2 * pr + 1:2 * pr + 2], (r, LANES))
        tiles.append(jnp.where(lo, a, b))
    return jnp.concatenate(tiles, axis=1)


def _tiled_rows_view(a):
    ns, r, c = a.shape
    return a.reshape(ns, r, c // LANES, LANES).transpose(0, 2, 1, 3).reshape(ns * (c // LANES) * r, LANES)


def _from_tiled_rows(rows, ns, r, c):
    return rows.reshape(ns, c // LANES, r, LANES).transpose(0, 2, 1, 3).reshape(ns, r, c)


def _rows_view_get(ref, i, r, tiles, nb, per_seq):
    return jnp.concatenate([ref[pl.ds(dt * r + i, nb, stride=per_seq), :] for dt in tiles], axis=1)


def _rows_view_put(ref, i, r, tiles, nb, per_seq, val):
    for n, dt in enumerate(tiles):
        ref[pl.ds(dt * r + i, nb, stride=per_seq), :] = val[:, n * LANES:(n + 1) * LANES]


def _steps(ref):
    return ref[...].reshape(DEC_SEQ * ref.shape[1], ref.shape[2])


def _smix_in_kernel(x_ref, sconv_ref, spool_ref, nmix_ref, wzx_ref, wvp_ref, wdt_ref, cw_ref, cb_ref, dtb_ref, a_ref, dexp_ref,
                    wpool_ref, pscale_ref,
                    xtm_ref, z_ref, ypart_ref, eacs_ref, xd_ref, c_ref, b_ref, dec_ref, pout_ref, conv_ref, pool_ref,
                    h_scr, pooled_scr):
    nb = S_SEQ_BLOCK
    lo = lax.broadcasted_iota(jnp.int32, (nb, LANES), 1) < SSM_HEAD_DIM
    x_tiles = range(D_MODEL // LANES)
    x_steps = [_rows_view_get(x_ref, l, DEC_SEQ, x_tiles, nb, X_ROWS_PER_SEQ) for l in range(DEC_SEQ)]
    for l in range(DEC_SEQ):
        xtm_ref[l] = x_steps[l]
    h_scr[...] = _rmsnorm(jnp.concatenate(x_steps, axis=0), nmix_ref[...]).astype(bf16)
    z = _dot(h_scr[...], wzx_ref[:, :D_SSM])
    for l in range(DEC_SEQ):
        z_ref[l] = z[l * nb:(l + 1) * nb]
    xbc = _dot(h_scr[...], wzx_ref[:, D_SSM:])
    vp = _dot(h_scr[...], wvp_ref[...])
    dtr = _dot(h_scr[...], wdt_ref[...])

    def conv_slot(i):
        if i < SSM_CONV - 1:
            return sconv_ref[i]
        return xbc[(i - SSM_CONV + 1) * nb:(i - SSM_CONV + 2) * nb]

    def pool_slot(i, cl):
        if i < POOL_HIST:
            return spool_ref[i, :, cl]
        return vp[(i - POOL_HIST) * nb:(i - POOL_HIST + 1) * nb, cl]

    xs, bm, cm, dt, acs = [], [], [], [], []
    for l in range(DEC_SEQ):
        acc = cb_ref[...] + cw_ref[0:1, :] * conv_slot(l)
        for k in range(1, SSM_CONV):
            acc = acc + cw_ref[k:k + 1, :] * conv_slot(l + k)
        act = _silu(acc)
        xs.append(act[:, :D_SSM])
        bm.append(act[:, D_SSM:D_SSM + D_BC])
        cm.append(act[:, D_SSM + D_BC:])
        b_ref[l] = bm[l]
        c_ref[l] = cm[l]
        dt.append(_softplus(dtr[l * nb:(l + 1) * nb] + dtb_ref[...]))
        da = dt[l] * a_ref[...]
        acs.append(da if l == 0 else acs[l - 1] + da)
        for gi, w in enumerate(POOL_WINDOWS):
            gl = slice(gi * POOL_GROUP_DIM, (gi + 1) * POOL_GROUP_DIM)
            s = pool_slot(POOL_HIST + l, gl)
            for k in range(1, w):
                s = s + pool_slot(POOL_HIST + l - k, gl)
            cnt = float(min(PAST_LEN + l + 1, w))
            pooled_scr[l * nb:(l + 1) * nb, gl] = (s / cnt - pool_slot(POOL_HIST + l, gl)).astype(bf16)
    for i in range(SSM_CONV - 1):
        conv_ref[i] = conv_slot(DEC_SEQ + i)
    for i in range(POOL_HIST):
        pool_ref[i] = pool_slot(DEC_SEQ + i, slice(0, D_POOL))
    pout = jnp.concatenate(
        [_dot(pooled_scr[:, gi * POOL_GROUP_DIM:(gi + 1) * POOL_GROUP_DIM], wpool_ref[gi])
         for gi in range(len(POOL_WINDOWS))], axis=1) * pscale_ref[...]
    for l in range(DEC_SEQ):
        pout_ref[l] = pout[l * nb:(l + 1) * nb]

    xdt = [xs[l] * _expand_heads(dt[l], lo) for l in range(DEC_SEQ)]
    gw = D_SSM // SSM_GROUPS
    for l in range(DEC_SEQ):
        y = xs[l] * dexp_ref[...]
        for s in range(l + 1):
            decay = _expand_heads(jnp.exp(acs[l] - acs[s]), lo)
            cbs = [jnp.sum(cm[l][:, g * SSM_STATE:(g + 1) * SSM_STATE] * bm[s][:, g * SSM_STATE:(g + 1) * SSM_STATE],
                           axis=-1, keepdims=True) for g in range(SSM_GROUPS)]
            coef = jnp.concatenate([decay[:, g * gw:(g + 1) * gw] * cbs[g] for g in range(SSM_GROUPS)], axis=1)
            y = y + coef * xdt[s]
        ypart_ref[l] = y
        eacs_ref[l] = _expand_heads(jnp.exp(acs[l]), lo)
        xd_ref[l] = xdt[l] * _expand_heads(jnp.exp(acs[DEC_SEQ - 1] - acs[l]), lo)
    dec_ref[...] = jnp.exp(acs[DEC_SEQ - 1])


def _smix_in(x_rows, sconv, spool, w):
    ns = sconv.shape[1]
    nb = S_SEQ_BLOCK
    tmaj = lambda steps, width: pl.BlockSpec((steps, nb, width), lambda i: (0, i, 0))
    step_outs = [D_MODEL, D_SSM, D_SSM, D_SSM, D_SSM, D_BC, D_BC]
    out_specs = [tmaj(DEC_SEQ, wd) for wd in step_outs] + [pl.BlockSpec((nb, LANES), lambda i: (i, 0)),
                                                          tmaj(DEC_SEQ, D_POOL),
                                                          tmaj(SSM_CONV - 1, D_XBC), tmaj(POOL_HIST, D_POOL)]
    out_shape = [jax.ShapeDtypeStruct((DEC_SEQ, ns, wd), f32) for wd in step_outs] + [
        jax.ShapeDtypeStruct((ns, LANES), f32), jax.ShapeDtypeStruct((DEC_SEQ, ns, D_POOL), f32),
        jax.ShapeDtypeStruct((SSM_CONV - 1, ns, D_XBC), f32), jax.ShapeDtypeStruct((POOL_HIST, ns, D_POOL), f32)]
    return pl.pallas_call(
        _smix_in_kernel,
        grid=(ns // nb,),
        in_specs=[pl.BlockSpec((nb * X_ROWS_PER_SEQ, LANES), lambda i: (i, 0)), tmaj(SSM_CONV - 1, D_XBC),
                  tmaj(POOL_HIST, D_POOL),
                  _const_spec((1, D_MODEL)), _const_spec((D_MODEL, D_SSM + D_XBC)), _const_spec((D_MODEL, D_POOL)),
                  _const_spec((D_MODEL, LANES)), _const_spec((SSM_CONV, D_XBC)),
                  _const_spec((1, D_XBC)), _const_spec((1, LANES)), _const_spec((1, LANES)), _const_spec((1, D_SSM)),
                  _const_spec((len(POOL_WINDOWS), POOL_GROUP_DIM, POOL_GROUP_DIM)), _const_spec((1, D_POOL))],
        out_specs=out_specs,
        out_shape=out_shape,
        scratch_shapes=[pltpu.VMEM((DEC_SEQ * nb, D_MODEL), bf16), pltpu.VMEM((DEC_SEQ * nb, D_POOL), bf16)],
        compiler_params=pltpu.CompilerParams(dimension_semantics=("arbitrary",), vmem_limit_bytes=VMEM_LIMIT_BYTES),
        name="smix_in",
    )(x_rows, sconv, spool, w["nmix"], w["wzx"], w["wvp"], w["wdt"], w["cw"], w["cb"], w["dtb"], w["a_row"], w["dexp"], w["wpool"],
      w["pscale"])


def _sssd_kernel(dec_ref, c_ref, b_ref, xd_ref, st_ref, yoff_ref, stn_ref):
    blk = pl.program_id(0)
    nb = S_SSD_BLOCK
    gw = D_SSM // SSM_GROUPS
    hpg = SSM_HEADS // SSM_GROUPS
    row_seq = lax.broadcasted_iota(jnp.int32, (DEC_SEQ * nb, gw), 0) % nb
    cmat, bmat, xd = _steps(c_ref), _steps(b_ref), _steps(xd_ref)
    for g in range(SSM_GROUPS):
        gl = slice(g * gw, (g + 1) * gw)
        cg = cmat[:, g * SSM_STATE:(g + 1) * SSM_STATE].astype(bf16)
        bg = bmat[:, g * SSM_STATE:(g + 1) * SSM_STATE].astype(bf16)
        yo = jnp.zeros((DEC_SEQ * nb, gw), f32)
        for j in range(nb):
            mine = row_seq == j
            h0 = st_ref[j, gl, :]
            yo = jnp.where(mine, _dot_nt(cg, h0.astype(bf16)), yo)
            upd = _dot_tn(jnp.where(mine, xd[:, gl], 0.0).astype(bf16), bg)
            for hh in range(hpg):
                hr = slice(hh * SSM_HEAD_DIM, (hh + 1) * SSM_HEAD_DIM)
                d = dec_ref[(blk * nb + j) * SSM_HEADS + g * hpg + hh]
                stn_ref[j, g * gw + hh * SSM_HEAD_DIM:g * gw + (hh + 1) * SSM_HEAD_DIM, :] = h0[hr] * d + upd[hr]
        for l in range(DEC_SEQ):
            yoff_ref[l, :, gl] = yo[l * nb:(l + 1) * nb]


def _sssd(dec_flat, cmat, bmat, xd, state):
    ns = state.shape[0]
    nb = S_SSD_BLOCK
    tmaj = lambda width: pl.BlockSpec((DEC_SEQ, nb, width), lambda i: (0, i, 0))
    stblk = pl.BlockSpec((nb, D_SSM, SSM_STATE), lambda i: (i, 0, 0))
    return pl.pallas_call(
        _sssd_kernel,
        grid=(ns // nb,),
        in_specs=[pl.BlockSpec(memory_space=pltpu.SMEM), tmaj(D_BC), tmaj(D_BC), tmaj(D_SSM), stblk],
        out_specs=[tmaj(D_SSM), stblk],
        out_shape=[jax.ShapeDtypeStruct((DEC_SEQ, ns, D_SSM), f32), jax.ShapeDtypeStruct(state.shape, f32)],
        compiler_params=pltpu.CompilerParams(dimension_semantics=("arbitrary",), vmem_limit_bytes=VMEM_LIMIT_BYTES),
        name="sssd",
    )(dec_flat, cmat, bmat, xd, state)


def _smix_out_kernel(x_ref, ypart_ref, yoff_ref, eacs_ref, z_ref, pout_ref, snorm_ref, wout_ref, nmem_ref, wq_ref,
                     x1_ref, q_ref, cat_scr):
    y = ypart_ref[...] + yoff_ref[...] * eacs_ref[...]
    t = y * _silu(z_ref[...])
    gw = D_SSM // SSM_GROUPS
    for g in range(SSM_GROUPS):
        gl = slice(g * gw, (g + 1) * gw)
        tg = t[:, gl]
        ms = jnp.mean(tg * tg, axis=-1, keepdims=True)
        cat_scr[:, gl] = (tg * lax.rsqrt(ms + EPS) * snorm_ref[:, gl]).astype(bf16)
    cat_scr[:, D_SSM:] = pout_ref[...].astype(bf16)
    x1 = x_ref[...] + _dot(cat_scr[...], wout_ref[...])
    x1_ref[...] = x1
    h = _rmsnorm(x1, nmem_ref[...]).astype(bf16)
    q_ref[...] = _dot(h, wq_ref[...]) * (MEM_HEAD_DIM ** -0.5)


def _smix_out(x, ypart, yoff, eacs, z, pout, w):
    n = x.shape[0]
    rb = 128
    rows = lambda width: pl.BlockSpec((rb, width), lambda i: (i, 0))
    return pl.pallas_call(
        _smix_out_kernel,
        grid=(n // rb,),
        in_specs=[rows(D_MODEL), rows(D_SSM), rows(D_SSM), rows(D_SSM), rows(D_SSM), rows(D_POOL),
                  _const_spec((1, D_SSM)), _const_spec((D_SSM + D_POOL, D_MODEL)), _const_spec((1, D_MODEL)),
                  _const_spec((D_MODEL, D_MODEL))],
        out_specs=[rows(D_MODEL), rows(D_MODEL)],
        out_shape=[jax.ShapeDtypeStruct((n, D_MODEL), f32), jax.ShapeDtypeStruct((n, D_MODEL), f32)],
        scratch_shapes=[pltpu.VMEM((rb, D_SSM + D_POOL), bf16)],
        compiler_params=pltpu.CompilerParams(dimension_semantics=("arbitrary",), vmem_limit_bytes=VMEM_LIMIT_BYTES),
        name="smix_out",
    )(x, ypart, yoff, eacs, z, pout, w["snorm"], w["wout"], w["nmem"], w["wq"])


def _sattn_kernel(q_ref, k_ref, v_ref, o_ref):
    nb = S_ATT_BLOCK
    rg = DEC_SEQ * nb
    rows = MEM_HEADS * rg

    def row_ids(width):
        r = lax.broadcasted_iota(jnp.int32, (rows, width), 0)
        return r // rg, r % nb

    row_h, row_seq = row_ids(D_MODEL)
    col_h = lax.broadcasted_iota(jnp.int32, (rows, D_MODEL), 1) // MEM_HEAD_DIM
    q = _steps(q_ref)
    qh = jnp.where(row_h == col_h, jnp.concatenate([q] * MEM_HEADS, axis=0), 0.0)
    lhs_s = jnp.concatenate([jnp.where(row_seq == b, qh, 0.0).astype(bf16) for b in range(nb)], axis=1)
    kcat = jnp.concatenate([_kv_seq(k_ref, b) for b in range(nb)], axis=1)
    s = _dot_nt(lhs_s, kcat)
    p = jnp.exp(s - jnp.max(s, axis=-1, keepdims=True))
    p = p / jnp.sum(p, axis=-1, keepdims=True)
    _, row_seq_p = row_ids(N_MEM)
    lhs_p = jnp.concatenate([jnp.where(row_seq_p == b, p, 0.0).astype(bf16) for b in range(nb)], axis=1)
    vcat = jnp.concatenate([_kv_seq(v_ref, b) for b in range(nb)], axis=0)
    res = _dot(lhs_p, vcat)
    for hd in range(MEM_HEADS):
        hl = slice(hd * MEM_HEAD_DIM, (hd + 1) * MEM_HEAD_DIM)
        for l in range(DEC_SEQ):
            o_ref[l, :, hl] = res[hd * rg + l * nb:hd * rg + (l + 1) * nb, hl]


def _sattn(q, mem_k, mem_v):
    ns = mem_k.shape[0]
    nb = S_ATT_BLOCK
    qblk = pl.BlockSpec((DEC_SEQ, nb, D_MODEL), lambda i: (0, i, 0))
    kvblk = pl.BlockSpec((nb, N_MEM * KV_ROWS_PER_TOKEN, LANES), lambda i: (i, 0, 0))
    return pl.pallas_call(
        _sattn_kernel,
        grid=(ns // nb,),
        in_specs=[qblk, kvblk, kvblk],
        out_specs=qblk,
        out_shape=jax.ShapeDtypeStruct((DEC_SEQ, ns, D_MODEL), f32),
        compiler_params=pltpu.CompilerParams(dimension_semantics=("arbitrary",), vmem_limit_bytes=VMEM_LIMIT_BYTES),
        name="sattn",
    )(q, mem_k, mem_v)


def _sffn_kernel(x1_ref, ao_ref, sffn_ref, wo_ref, g_ref, wup_ref, cw_ref, cb_ref, wdn_ref, gfin_ref,
                 y_ref, st_ref, h_scr, act_scr):
    nb = S_SEQ_BLOCK
    x2 = _steps(x1_ref) + _dot(_steps(ao_ref).astype(bf16), wo_ref[...])
    h_scr[...] = _rmsnorm(x2, g_ref[...]).astype(bf16)
    u = _dot(h_scr[...], wup_ref[...])
    hist = FFN_CONV - 1

    def conv_block(blk):
        cols = slice(blk * FF_CHUNK, (blk + 1) * FF_CHUNK)
        tiles = range(blk * FF_CHUNK // LANES, (blk + 1) * FF_CHUNK // LANES)
        slots = [sffn_ref[:, i, cols] for i in range(hist)]
        slots += [u[l * nb:(l + 1) * nb, cols] for l in range(DEC_SEQ)]
        for i in range(hist):
            st_ref[:, i, cols] = slots[DEC_SEQ + i]
        outs = []
        for l in range(DEC_SEQ):
            acc = cb_ref[:, cols] + cw_ref[0:1, cols] * slots[l]
            for k in range(1, FFN_CONV):
                acc = acc + cw_ref[k:k + 1, cols] * slots[l + k]
            outs.append(acc)
        return jnp.concatenate(outs, axis=0)

    for j in range(N_FF_CHUNKS):
        act = _silu(conv_block(_gate_blk(j))) * conv_block(_val_blk(j))
        act_scr[:, j * FF_CHUNK:(j + 1) * FF_CHUNK] = act.astype(bf16)
    y = _rmsnorm(x2 + _dot(act_scr[...], wdn_ref[...]), gfin_ref[...])
    for l in range(DEC_SEQ):
        _rows_view_put(y_ref, l, DEC_SEQ, range(D_MODEL // LANES), nb, X_ROWS_PER_SEQ, y[l * nb:(l + 1) * nb])


def _sffn(x1, ao, sffn_rows, w):
    ns = x1.shape[1]
    nb = S_SEQ_BLOCK
    tok = pl.BlockSpec((DEC_SEQ, nb, D_MODEL), lambda i: (0, i, 0))
    yblk = pl.BlockSpec((nb * X_ROWS_PER_SEQ, LANES), lambda i: (i, 0))
    stblk = pl.BlockSpec((nb, FFN_CONV - 1, 2 * D_FF), lambda i: (i, 0, 0))
    return pl.pallas_call(
        _sffn_kernel,
        grid=(ns // nb,),
        in_specs=[tok, tok, stblk,
                  _const_spec((D_MODEL, D_MODEL)), _const_spec((1, D_MODEL)), _const_spec((D_MODEL, 2 * D_FF)),
                  _const_spec((FFN_CONV, 2 * D_FF)), _const_spec((1, 2 * D_FF)),
                  _const_spec((D_FF, D_MODEL)), _const_spec((1, D_MODEL))],
        out_specs=[yblk, stblk],
        out_shape=[jax.ShapeDtypeStruct((ns * X_ROWS_PER_SEQ, LANES), f32),
                   jax.ShapeDtypeStruct((ns, FFN_CONV - 1, 2 * D_FF), f32)],
        scratch_shapes=[pltpu.VMEM((DEC_SEQ * nb, D_MODEL), bf16), pltpu.VMEM((DEC_SEQ * nb, D_FF), bf16)],
        compiler_params=pltpu.CompilerParams(dimension_semantics=("arbitrary",), vmem_limit_bytes=VMEM_LIMIT_BYTES),
        name="sffn",
    )(x1, ao, sffn_rows, w["wo"], w["nffn"], w["wup"], w["fcw"], w["fcb"], w["wdn"], w["gfin"])


def _sample_path(x_sample, state_ssm, state_ssm_conv, state_pool, state_ffn_conv, cache_mem_k, cache_mem_v, w):
    ns = x_sample.shape[0]
    xtm, z, ypart, eacs, xd, cmat, bmat, dec, pout, conv_new, pool_new = _smix_in(
        _tiled_rows_view(x_sample), state_ssm_conv[0].transpose(1, 0, 2), state_pool[0].transpose(1, 0, 2), w)
    yoff, ssm_new = _sssd(dec[:, :SSM_HEADS].reshape(-1), cmat, bmat, xd, state_ssm[0].reshape(ns, D_SSM, SSM_STATE))
    flat = lambda a: a.reshape(DEC_SEQ * ns, a.shape[-1])
    x1, q = _smix_out(flat(xtm), flat(ypart), flat(yoff), flat(eacs), flat(z), flat(pout), w)
    ao = _sattn(q.reshape(DEC_SEQ, ns, D_MODEL), _kv_rows_view(cache_mem_k[0]), _kv_rows_view(cache_mem_v[0]))
    y_rows, ffn_new = _sffn(x1.reshape(DEC_SEQ, ns, D_MODEL), ao, state_ffn_conv[0], w)
    return (_from_tiled_rows(y_rows, ns, DEC_SEQ, D_MODEL),
            ssm_new.reshape(1, ns, SSM_HEADS, SSM_HEAD_DIM, SSM_STATE),
            conv_new.transpose(1, 0, 2)[None], pool_new.transpose(1, 0, 2)[None],
            ffn_new[None])


def kernel(x_prompt, x_sample, mem_prompt, state_ssm, state_ssm_conv, state_pool, state_ffn_conv, cache_mem_k, cache_mem_v, norm_mix, w_in, ssm_conv_w, ssm_conv_b, ssm_dt_bias, ssm_a_log, ssm_d, ssm_norm, w_pool, pool_scale, w_out, norm_mem, norm_memkv, w_mq, w_mk, w_mv, w_mo, norm_ffn, w_up, ffn_conv_w, ffn_conv_b, w_down, final_norm):
    params = dict(norm_mix=norm_mix, w_in=w_in, ssm_conv_w=ssm_conv_w, ssm_conv_b=ssm_conv_b, ssm_dt_bias=ssm_dt_bias,
                  ssm_a_log=ssm_a_log, ssm_d=ssm_d, ssm_norm=ssm_norm, w_pool=w_pool, pool_scale=pool_scale,
                  w_out=w_out, norm_mem=norm_mem, norm_memkv=norm_memkv, w_mq=w_mq, w_mk=w_mk, w_mv=w_mv, w_mo=w_mo,
                  norm_ffn=norm_ffn, w_up=w_up, ffn_conv_w=ffn_conv_w, ffn_conv_b=ffn_conv_b, w_down=w_down,
                  final_norm=final_norm)
    w = _prep_weights(params)
    yp, ssm_p, conv_p, pool_p, ffn_p, mk_p, mv_p = _prompt_path(x_prompt, mem_prompt, w, PROMPT_TILE)
    ys, ssm_s, conv_s, pool_s, ffn_s = _sample_path(x_sample, state_ssm, state_ssm_conv, state_pool, state_ffn_conv,
                                                    cache_mem_k, cache_mem_v, w)
    return yp, ys, ssm_p, ssm_s, conv_p, conv_s, pool_p, pool_s, ffn_p, ffn_s, mk_p, mv_p
```

```python
import functools

import jax
import jax.numpy as jnp
from jax import lax
from jax.experimental import pallas as pl
from jax.experimental.pallas import tpu as pltpu

f32 = jnp.float32
bf16 = jnp.bfloat16

D_MODEL = 1024
SSM_HEADS = 16
SSM_HEAD_DIM = 64
SSM_STATE = 128
SSM_GROUPS = 2
SSM_CHUNK = 128
D_SSM = 1024
D_BC = SSM_GROUPS * SSM_STATE
D_XBC = D_SSM + 2 * D_BC
SSM_CONV = 4
D_POOL = 1024
POOL_WINDOWS = (2, 4, 8, 16)
POOL_GROUP_DIM = 256
POOL_HIST = 15
N_MEM = 256
MEM_HEADS = 4
MEM_HEAD_DIM = 256
D_FF = 2816
FFN_CONV = 3
EPS = 1e-6
PAST_LEN = 16384

LANES = 128
SUBLANES = 8
MXU_DIM = 256
VMEM_LIMIT_BYTES = 56 * 1024 * 1024

CONV_HIST_ROWS = SUBLANES
POOL_HIST_ROWS = 2 * SUBLANES
FF_CHUNK = MXU_DIM
N_FF_CHUNKS = D_FF // FF_CHUNK


def _silu(v):
    half = 0.5 * v
    return half + half * jnp.tanh(half)


def _softplus(v):
    return jnp.maximum(v, 0.0) + jnp.log1p(jnp.exp(-jnp.abs(v)))


def _rmsnorm(x, g):
    ms = jnp.mean(x * x, axis=-1, keepdims=True)
    return x * lax.rsqrt(ms + EPS) * g


def _dot(a, b):
    return jnp.dot(a, b, preferred_element_type=f32)


def _dot_nt(a, b):
    return lax.dot_general(a, b, (((1,), (1,)), ((), ())), preferred_element_type=f32)


def _dot_tn(a, b):
    return lax.dot_general(a, b, (((0,), (0,)), ((), ())), preferred_element_type=f32)


def _split3(v):
    p1 = v.astype(bf16)
    r1 = v - p1.astype(f32)
    p2 = r1.astype(bf16)
    r2 = r1 - p2.astype(f32)
    return p1, p2, r2.astype(bf16)


def _const_spec(shape):
    return pl.BlockSpec(shape, lambda *_: (0,) * len(shape), pipeline_mode=pl.Buffered(1))


KV_LANE_TILES = MEM_HEAD_DIM // LANES
KV_ROWS_PER_TOKEN = KV_LANE_TILES * MEM_HEADS


def _kv_rows_view(kv):
    ns = kv.shape[0]
    kv = kv.reshape(ns, N_MEM, MEM_HEADS, KV_LANE_TILES, LANES).transpose(0, 1, 3, 2, 4)
    return kv.reshape(ns, N_MEM * KV_ROWS_PER_TOKEN, LANES)


def _kv_from_rows(rows):
    ns = rows.shape[0]
    kv = rows.reshape(ns, N_MEM, KV_LANE_TILES, MEM_HEADS, LANES).transpose(0, 1, 3, 2, 4)
    return kv.reshape(ns, N_MEM, MEM_HEADS, MEM_HEAD_DIM)


def _kv_seq(ref, jj):
    tiles = [ref[jj, pl.ds(dt * MEM_HEADS + hd, N_MEM, stride=KV_ROWS_PER_TOKEN), :]
             for hd in range(MEM_HEADS) for dt in range(KV_LANE_TILES)]
    return jnp.concatenate(tiles, axis=1).astype(bf16)


def _memkv_kernel(mem_ref, g_ref, wk_ref, wv_ref, k_ref, v_ref):
    h = _rmsnorm(mem_ref[0], g_ref[...]).astype(bf16)
    for out_ref, w_ref in ((k_ref, wk_ref), (v_ref, wv_ref)):
        kv = _dot(h, w_ref[...])
        for hd in range(MEM_HEADS):
            for dt in range(KV_LANE_TILES):
                col = hd * MEM_HEAD_DIM + dt * LANES
                out_ref[0, pl.ds(dt * MEM_HEADS + hd, N_MEM, stride=KV_ROWS_PER_TOKEN), :] = kv[:, col:col + LANES]


def _memkv(mem, g, wk, wv):
    b = mem.shape[0]
    blk = pl.BlockSpec((1, N_MEM, D_MODEL), lambda i: (i, 0, 0))
    oblk = pl.BlockSpec((1, N_MEM * KV_ROWS_PER_TOKEN, LANES), lambda i: (i, 0, 0))
    return pl.pallas_call(
        _memkv_kernel,
        grid=(b,),
        in_specs=[blk, _const_spec((1, D_MODEL)), _const_spec((D_MODEL, D_MODEL)), _const_spec((D_MODEL, D_MODEL))],
        out_specs=[oblk, oblk],
        out_shape=[jax.ShapeDtypeStruct((b, N_MEM * KV_ROWS_PER_TOKEN, LANES), f32)] * 2,
        compiler_params=pltpu.CompilerParams(dimension_semantics=("arbitrary",), vmem_limit_bytes=VMEM_LIMIT_BYTES),
        name="memkv",
    )(mem, g, wk, wv)


def _ssd_chunk(r0, dt_scr, xs_scr, b_scr, c_scr, y_scr, xd_scr, hst_scr, a_ref, dexp_ref, between=()):
    between = list(between) + [None] * 3
    q = SSM_CHUNK
    rows = pl.ds(r0, q)
    row_i = lax.broadcasted_iota(jnp.int32, (q, q), 0)
    col_i = lax.broadcasted_iota(jnp.int32, (q, q), 1)
    causal = col_i <= row_i
    lo = col_i < SSM_HEAD_DIM
    tril = jnp.where(causal, 1.0, 0.0).astype(bf16)

    dt = dt_scr[rows, :]
    da = dt * a_ref[...]
    p1, p2, p3 = _split3(da)
    acs = _dot(tril, p1) + _dot(tril, p2) + _dot(tril, p3)
    acs_t = acs.T
    if between[0] is not None:
        between[0]()

    for g in range(SSM_GROUPS):
        bg = b_scr[rows, g * SSM_STATE:(g + 1) * SSM_STATE]
        cg = c_scr[rows, g * SSM_STATE:(g + 1) * SSM_STATE]
        bg_b = bg.astype(bf16)
        cb = jnp.where(causal, _dot_nt(cg.astype(bf16), bg_b), 0.0)
        cdec_rows = []
        pairs_per_group = SSM_HEADS // SSM_GROUPS // 2
        for pp in range(pairs_per_group):
            pr = g * pairs_per_group + pp
            lanes = slice(pr * LANES, (pr + 1) * LANES)
            lhs, dtb, dend, cdec = [], [], [], []
            for hh in (2 * pr, 2 * pr + 1):
                colb = jnp.broadcast_to(acs[:, hh:hh + 1], (q, q))
                seg = jnp.where(causal, colb - acs_t[hh:hh + 1, :], 0.0)
                lhs.append((jnp.exp(seg) * cb).astype(bf16))
                lhs.append((cg * jnp.exp(colb)).astype(bf16))
                last = colb[q - 1:q, :]
                dend.append(jnp.exp(last - colb))
                cdec.append(jnp.exp(last))
                dtb.append(jnp.broadcast_to(dt[:, hh:hh + 1], (q, q)))
            xs_pair = xs_scr[rows, lanes]
            xdt = xs_pair * jnp.where(lo, dtb[0], dtb[1])
            xd_scr[:, lanes] = (xdt * jnp.where(lo, dend[0], dend[1])).astype(bf16)
            hst_pair = hst_scr[:, lanes]
            rhs = jnp.concatenate([
                jnp.where(lo, xdt, 0.0).astype(bf16), jnp.where(lo, hst_pair, 0.0).astype(bf16),
                jnp.where(lo, 0.0, xdt).astype(bf16), jnp.where(lo, 0.0, hst_pair).astype(bf16)], axis=0)
            y_pair = _dot(jnp.concatenate(lhs, axis=1), rhs)
            y_scr[rows, lanes] = y_pair + xs_pair * dexp_ref[:, lanes]
            cdec_rows.append(jnp.where(lo[:1], cdec[0], cdec[1]))
        gl = slice(g * (D_SSM // SSM_GROUPS), (g + 1) * (D_SSM // SSM_GROUPS))
        upd = _dot_tn(bg_b, xd_scr[:, gl])
        hst_scr[:, gl] = hst_scr[:, gl] * jnp.concatenate(cdec_rows, axis=1) + upd
        if between[1 + g] is not None:
            between[1 + g]()


def _mixer_kernel(x_ref, nmix_ref, wzx_ref, wvp_ref, wdt_ref, cw_ref, cb_ref, dtb_ref, a_ref, dexp_ref, snorm_ref, wpool_ref,
                  pscale_ref, wout_ref,
                  x1_ref, ssm_ref, conv_ref, pool_ref,
                  xd_scr, hst_scr, *sub_scr, tile, sub):
    t = pl.program_id(1)
    last_t = pl.num_programs(1) - 1
    ch = CONV_HIST_ROWS
    ph = POOL_HIST_ROWS
    n_sub = tile // sub
    per = len(sub_scr) // n_sub
    bufs = [sub_scr[i * per:(i + 1) * per] for i in range(n_sub)]
    xbc0, vp0 = bufs[0][2], bufs[0][3]

    @pl.when(t == 0)
    def _():
        xbc0[:, 0:ch, :] = jnp.zeros((D_XBC // LANES, ch, LANES), f32)
        vp0[:, 0:ph, :] = jnp.zeros((D_POOL // LANES, ph, LANES), f32)
        hst_scr[...] = jnp.zeros_like(hst_scr)

    gw = D_SSM // SSM_GROUPS

    def carry_history(src, dst):
        for j in range(D_XBC // LANES):
            bufs[dst][2][j, 0:ch, :] = bufs[src][2][j, sub:sub + ch, :]
        for j in range(D_POOL // LANES):
            bufs[dst][3][j, 0:ph, :] = bufs[src][3][j, sub:sub + ph, :]

    def in_xbc(s):
        h_scr, _, xbc_scr = bufs[s][:3]
        h_scr[...] = _rmsnorm(x_ref[0, s * sub:(s + 1) * sub, :], nmix_ref[...]).astype(bf16)
        xbc = _dot(h_scr[...], wzx_ref[:, D_SSM:])
        for j in range(D_XBC // LANES):
            xbc_scr[j, ch:ch + sub, :] = xbc[:, j * LANES:(j + 1) * LANES]

    def in_dtvp(s):
        h_scr, _, _, vp_scr, dt_scr = bufs[s][:5]
        dt_scr[...] = _dot(h_scr[...], wdt_ref[...])
        vp = _dot(h_scr[...], wvp_ref[...])
        for j in range(D_POOL // LANES):
            vp_scr[j, ph:ph + sub, :] = vp[:, j * LANES:(j + 1) * LANES]

    def in_z(s):
        h_scr, z_scr = bufs[s][:2]
        z_scr[...] = _dot(h_scr[...], wzx_ref[:, :D_SSM])
        if s > 0:
            carry_history(s - 1, s)

    def stage_mid(s, between):
        _, z_scr, xbc_scr, vp_scr, dt_scr, xs_scr, b_scr, c_scr, y_scr, pooled_scr, cat_scr = bufs[s]
        for j in range(D_XBC // LANES):
            cl = slice(j * LANES, (j + 1) * LANES)
            acc = cb_ref[:, cl] + cw_ref[0:1, cl] * xbc_scr[j, pl.ds(ch - 3, sub), :]
            for k in range(1, SSM_CONV):
                acc = acc + cw_ref[k:k + 1, cl] * xbc_scr[j, pl.ds(ch - 3 + k, sub), :]
            act = _silu(acc)
            if j < D_SSM // LANES:
                xs_scr[:, cl] = act
            elif j < (D_SSM + D_BC) // LANES:
                b_scr[:, j * LANES - D_SSM:(j + 1) * LANES - D_SSM] = act
            else:
                c_scr[:, j * LANES - D_SSM - D_BC:(j + 1) * LANES - D_SSM - D_BC] = act

        pos1 = lax.broadcasted_iota(jnp.int32, (sub, LANES), 0) + (t * tile + s * sub + 1)
        for j in range(D_POOL // LANES):
            cl = slice(j * LANES, (j + 1) * LANES)
            w = POOL_WINDOWS[j * LANES // POOL_GROUP_DIM]
            cur = vp_scr[j, ph:ph + sub, :]
            acc = cur
            for k in range(1, w):
                acc = acc + vp_scr[j, pl.ds(ph - k, sub), :]
            cnt = jnp.minimum(pos1, w).astype(f32)
            pooled_scr[:, cl] = (acc / cnt - cur).astype(bf16)
        for gi in range(len(POOL_WINDOWS)):
            gl = slice(gi * POOL_GROUP_DIM, (gi + 1) * POOL_GROUP_DIM)
            pg = _dot(pooled_scr[:, gl], wpool_ref[gi]) * pscale_ref[:, gl]
            cat_scr[:, D_SSM + gi * POOL_GROUP_DIM:D_SSM + (gi + 1) * POOL_GROUP_DIM] = pg.astype(bf16)

        dt_scr[...] = _softplus(dt_scr[...] + dtb_ref[...])
        n_chunks = sub // SSM_CHUNK
        slots = [None] * (3 * n_chunks)
        for i, piece in enumerate(between):
            slots[i * len(slots) // len(between)] = piece
        for c in range(n_chunks):
            _ssd_chunk(c * SSM_CHUNK, dt_scr, xs_scr, b_scr, c_scr, y_scr, xd_scr, hst_scr, a_ref, dexp_ref,
                       slots[3 * c:3 * c + 3])

        for g in range(SSM_GROUPS):
            gl = slice(g * gw, (g + 1) * gw)
            tg = y_scr[:, gl] * _silu(z_scr[:, gl])
            ms = jnp.mean(tg * tg, axis=-1, keepdims=True)
            cat_scr[:, gl] = (tg * lax.rsqrt(ms + EPS) * snorm_ref[:, gl]).astype(bf16)

    def stage_out(s):
        rows = slice(s * sub, (s + 1) * sub)
        x1_ref[0, rows, :] = x_ref[0, rows, :] + _dot(bufs[s][-1][...], wout_ref[...])

    in_xbc(0)
    in_dtvp(0)
    in_z(0)
    for s in range(n_sub):
        nxt = s + 1 < n_sub
        if nxt:
            in_xbc(s + 1)
        pieces = [functools.partial(in_dtvp, s + 1), functools.partial(in_z, s + 1)] if nxt else []
        if s > 0:
            pieces.append(functools.partial(stage_out, s - 1))
        stage_mid(s, pieces)
    stage_out(n_sub - 1)
    carry_history(n_sub - 1, 0)

    @pl.when(t == last_t)
    def _():
        for pr in range(D_SSM // LANES):
            ssm_ref[0, pr * LANES:(pr + 1) * LANES, :] = hst_scr[:, pr * LANES:(pr + 1) * LANES].T
        for j in range(D_XBC // LANES):
            conv_ref[0, :, j * LANES:(j + 1) * LANES] = xbc0[j, pl.ds(ch - (SSM_CONV - 1), SSM_CONV - 1), :]
        for j in range(D_POOL // LANES):
            pool_ref[0, :, j * LANES:(j + 1) * LANES] = vp0[j, pl.ds(ph - POOL_HIST, POOL_HIST), :]


def _mixer_prompt(x, nmix, wzx, wvp, wdt, cw, cb, dtb, a_row, dexp, snorm, wpool, pscale, wout, tile):
    b, seq, _ = x.shape
    nt = seq // tile
    sub = min(tile, MIXER_SUB_TILE)
    xblk = pl.BlockSpec((1, tile, D_MODEL), lambda i, j: (i, j, 0))
    sub_scratch = [
        pltpu.VMEM((sub, D_MODEL), bf16),
        pltpu.VMEM((sub, D_SSM), f32),
        pltpu.VMEM((D_XBC // LANES, CONV_HIST_ROWS + sub, LANES), f32),
        pltpu.VMEM((D_POOL // LANES, POOL_HIST_ROWS + sub, LANES), f32),
        pltpu.VMEM((sub, LANES), f32),
        pltpu.VMEM((sub, D_SSM), f32),
        pltpu.VMEM((sub, D_BC), f32),
        pltpu.VMEM((sub, D_BC), f32),
        pltpu.VMEM((sub, D_SSM), f32),
        pltpu.VMEM((sub, D_POOL), bf16),
        pltpu.VMEM((sub, D_SSM + D_POOL), bf16),
    ]
    scratch = [pltpu.VMEM((SSM_CHUNK, D_SSM), bf16),
               pltpu.VMEM((SSM_STATE, D_SSM), f32)]
    scratch += sub_scratch * (tile // sub)
    return pl.pallas_call(
        functools.partial(_mixer_kernel, tile=tile, sub=sub),
        grid=(b, nt),
        in_specs=[xblk, _const_spec((1, D_MODEL)), _const_spec((D_MODEL, D_SSM + D_XBC)),
                  _const_spec((D_MODEL, D_POOL)), _const_spec((D_MODEL, LANES)), _const_spec((SSM_CONV, D_XBC)),
                  _const_spec((1, D_XBC)), _const_spec((1, LANES)), _const_spec((1, LANES)), _const_spec((1, D_SSM)),
                  _const_spec((1, D_SSM)), _const_spec((len(POOL_WINDOWS), POOL_GROUP_DIM, POOL_GROUP_DIM)),
                  _const_spec((1, D_POOL)), _const_spec((D_SSM + D_POOL, D_MODEL))],
        out_specs=[xblk,
                   pl.BlockSpec((1, D_SSM, SSM_STATE), lambda i, j: (i, 0, 0)),
                   pl.BlockSpec((1, SSM_CONV - 1, D_XBC), lambda i, j: (i, 0, 0)),
                   pl.BlockSpec((1, POOL_HIST, D_POOL), lambda i, j: (i, 0, 0))],
        out_shape=[jax.ShapeDtypeStruct((b, seq, D_MODEL), f32),
                   jax.ShapeDtypeStruct((b, D_SSM, SSM_STATE), f32),
                   jax.ShapeDtypeStruct((b, SSM_CONV - 1, D_XBC), f32),
                   jax.ShapeDtypeStruct((b, POOL_HIST, D_POOL), f32)],
        scratch_shapes=scratch,
        compiler_params=pltpu.CompilerParams(dimension_semantics=("arbitrary", "arbitrary"),
                                             vmem_limit_bytes=VMEM_LIMIT_BYTES),
        name="mixer_prompt",
    )(x, nmix, wzx, wvp, wdt, cw, cb, dtb, a_row, dexp, snorm, wpool, pscale, wout)


def _attn_kernel(x_ref, g_ref, wq_ref, k_ref, v_ref, wo_ref, o_ref, q_scr, ao_scr):
    tile = x_ref.shape[1]
    sub = min(tile, ATTN_SUB_TILE)
    n_sub = tile // sub
    heads = [slice(hd * MEM_HEAD_DIM, (hd + 1) * MEM_HEAD_DIM) for hd in range(MEM_HEADS)]
    k = _kv_seq(k_ref, 0)
    v = _kv_seq(v_ref, 0)

    def q_proj(s):
        rows = slice(s * sub, (s + 1) * sub)
        h = _rmsnorm(x_ref[0, rows, :], g_ref[...]).astype(bf16)
        q_scr[rows, :] = (_dot(h, wq_ref[...]) * (MEM_HEAD_DIM ** -0.5)).astype(bf16)

    def scores(s):
        rows = slice(s * sub, (s + 1) * sub)
        return [_dot_nt(q_scr[rows, hl], k[:, hl]) for hl in heads]

    def values(s, sc):
        rows = slice(s * sub, (s + 1) * sub)
        for hl, s_h in zip(heads, sc):
            p = jnp.exp(s_h - jnp.max(s_h, axis=-1, keepdims=True))
            p = (p / jnp.sum(p, axis=-1, keepdims=True)).astype(bf16)
            ao_scr[rows, hl] = _dot(p, v[:, hl]).astype(bf16)

    def o_proj(s):
        rows = slice(s * sub, (s + 1) * sub)
        o_ref[0, rows, :] = x_ref[0, rows, :] + _dot(ao_scr[rows, :], wo_ref[...])

    q_proj(0)
    for s in range(n_sub):
        sc = scores(s)
        if s + 1 < n_sub:
            q_proj(s + 1)
        if s > 0:
            o_proj(s - 1)
        values(s, sc)
    o_proj(n_sub - 1)


def _attn_prompt(x, g, wq, mem_k, mem_v, wo, tile):
    b, seq, _ = x.shape
    xblk = pl.BlockSpec((1, tile, D_MODEL), lambda i, j: (i, j, 0))
    kvblk = pl.BlockSpec((1, N_MEM * KV_ROWS_PER_TOKEN, LANES), lambda i, j: (i, 0, 0))
    return pl.pallas_call(
        _attn_kernel,
        grid=(b, seq // tile),
        in_specs=[xblk, _const_spec((1, D_MODEL)), _const_spec((D_MODEL, D_MODEL)), kvblk, kvblk,
                  _const_spec((D_MODEL, D_MODEL))],
        out_specs=xblk,
        out_shape=jax.ShapeDtypeStruct((b, seq, D_MODEL), f32),
        scratch_shapes=[pltpu.VMEM((tile, D_MODEL), bf16), pltpu.VMEM((tile, D_MODEL), bf16)],
        compiler_params=pltpu.CompilerParams(dimension_semantics=("arbitrary", "arbitrary"),
                                             vmem_limit_bytes=VMEM_LIMIT_BYTES),
        name="attn_prompt",
    )(x, g, wq, mem_k, mem_v, wo)


def _gate_blk(j):
    return j


def _val_blk(j):
    return N_FF_CHUNKS + j


def _ffn_kernel(x_ref, g_ref, wup_ref, cw_ref, cb_ref, wdn_ref, gfin_ref, y_ref, st_ref, h_scr, u_scr, act_scr, *, tile):
    t = pl.program_id(1)
    last_t = pl.num_programs(1) - 1
    ch = CONV_HIST_ROWS
    tpc = FF_CHUNK // LANES
    ntile = 2 * D_FF // LANES

    @pl.when(t == 0)
    def _():
        u_scr[:, 0:ch, :] = jnp.zeros((ntile, ch, LANES), f32)

    sub = min(tile, FFN_SUB_TILE)
    n_sub = tile // sub

    def up_proj(s):
        rows = slice(s * sub, (s + 1) * sub)
        h_scr[rows, :] = _rmsnorm(x_ref[0, rows, :], g_ref[...]).astype(bf16)
        u = _dot(h_scr[rows, :], wup_ref[...])
        for ti in range(ntile):
            u_scr[ti, ch + s * sub:ch + (s + 1) * sub, :] = u[:, ti * LANES:(ti + 1) * LANES]

    def conv(s, ti):
        cl = slice(ti * LANES, (ti + 1) * LANES)
        r0 = ch + s * sub
        acc = cb_ref[:, cl] + cw_ref[0:1, cl] * u_scr[ti, pl.ds(r0 - 2, sub), :]
        acc = acc + cw_ref[1:2, cl] * u_scr[ti, pl.ds(r0 - 1, sub), :]
        return acc + cw_ref[2:3, cl] * u_scr[ti, r0:r0 + sub, :]

    def gate(s):
        rows = slice(s * sub, (s + 1) * sub)
        for j in range(N_FF_CHUNKS):
            for i in range(tpc):
                g_t = conv(s, _gate_blk(j) * tpc + i)
                v_t = conv(s, _val_blk(j) * tpc + i)
                act_scr[rows, (j * tpc + i) * LANES:(j * tpc + i + 1) * LANES] = (_silu(g_t) * v_t).astype(bf16)

    def down_proj(s):
        rows = slice(s * sub, (s + 1) * sub)
        y_ref[0, rows, :] = _rmsnorm(x_ref[0, rows, :] + _dot(act_scr[rows, :], wdn_ref[...]), gfin_ref[...])

    up_proj(0)
    for s in range(n_sub):
        if s + 1 < n_sub:
            up_proj(s + 1)
        gate(s)
        down_proj(s)
    for ti in range(ntile):
        u_scr[ti, 0:ch, :] = u_scr[ti, tile:tile + ch, :]

    @pl.when(t == last_t)
    def _():
        for ti in range(ntile):
            st_ref[0, :, ti * LANES:(ti + 1) * LANES] = u_scr[ti, pl.ds(ch - (FFN_CONV - 1), FFN_CONV - 1), :]


def _ffn_prompt(x, g, wup, cw, cb, wdn, gfin, tile):
    b, seq, _ = x.shape
    xblk = pl.BlockSpec((1, tile, D_MODEL), lambda i, j: (i, j, 0))
    return pl.pallas_call(
        functools.partial(_ffn_kernel, tile=tile),
        grid=(b, seq // tile),
        in_specs=[xblk, _const_spec((1, D_MODEL)), _const_spec((D_MODEL, 2 * D_FF)),
                  _const_spec((FFN_CONV, 2 * D_FF)), _const_spec((1, 2 * D_FF)),
                  _const_spec((D_FF, D_MODEL)), _const_spec((1, D_MODEL))],
        out_specs=[xblk, pl.BlockSpec((1, FFN_CONV - 1, 2 * D_FF), lambda i, j: (i, 0, 0))],
        out_shape=[jax.ShapeDtypeStruct((b, seq, D_MODEL), f32),
                   jax.ShapeDtypeStruct((b, FFN_CONV - 1, 2 * D_FF), f32)],
        scratch_shapes=[pltpu.VMEM((tile, D_MODEL), bf16),
                        pltpu.VMEM((2 * D_FF // LANES, CONV_HIST_ROWS + tile, LANES), f32),
                        pltpu.VMEM((tile, D_FF), bf16)],
        compiler_params=pltpu.CompilerParams(dimension_semantics=("arbitrary", "arbitrary"),
                                             vmem_limit_bytes=VMEM_LIMIT_BYTES),
        name="ffn_prompt",
    )(x, g, wup, cw, cb, wdn, gfin)


def _prep_weights(p):
    i = 0
    w_in = p["w_in"][i]
    n_zx = D_SSM + D_XBC
    wzx = w_in[:, :n_zx].astype(bf16)
    wdt = jnp.pad(w_in[:, n_zx:n_zx + SSM_HEADS], ((0, 0), (0, LANES - SSM_HEADS))).astype(bf16)
    wvp = w_in[:, n_zx + SSM_HEADS:].astype(bf16)
    pad_h = (0, LANES - SSM_HEADS)
    return dict(
        nmix=p["norm_mix"][i][None], wzx=wzx, wvp=wvp, wdt=wdt, cw=p["ssm_conv_w"][i], cb=p["ssm_conv_b"][i][None],
        dtb=jnp.pad(p["ssm_dt_bias"][i], pad_h)[None],
        a_row=jnp.pad(-jnp.exp(p["ssm_a_log"][i].astype(f32)), pad_h)[None],
        dexp=jnp.repeat(p["ssm_d"][i], SSM_HEAD_DIM)[None], snorm=p["ssm_norm"][i][None],
        wpool=p["w_pool"][i].astype(bf16), pscale=p["pool_scale"][i][None], wout=p["w_out"][i].astype(bf16),
        nmem=p["norm_mem"][i][None], nmemkv=p["norm_memkv"][i][None],
        wq=p["w_mq"][i].astype(bf16), wk=p["w_mk"][i].astype(bf16), wv=p["w_mv"][i].astype(bf16),
        wo=p["w_mo"][i].astype(bf16),
        nffn=p["norm_ffn"][i][None],
        wup=p["w_up"][i].astype(bf16), fcw=p["ffn_conv_w"][i], fcb=p["ffn_conv_b"][i][None],
        wdn=p["w_down"][i].astype(bf16),
        gfin=p["final_norm"][None],
    )


def _prompt_path(x_prompt, mem_prompt, w, tile):
    b = x_prompt.shape[0]
    mem_k, mem_v = _memkv(mem_prompt, w["nmemkv"], w["wk"], w["wv"])
    x1, ssm, conv, pool = _mixer_prompt(x_prompt, w["nmix"], w["wzx"], w["wvp"], w["wdt"], w["cw"], w["cb"], w["dtb"], w["a_row"],
                                        w["dexp"], w["snorm"], w["wpool"], w["pscale"], w["wout"],
                                        min(MIXER_TILE, x_prompt.shape[1]))
    x2 = _attn_prompt(x1, w["nmem"], w["wq"], mem_k, mem_v, w["wo"], min(ATTN_TILE, x_prompt.shape[1]))
    y, ffn = _ffn_prompt(x2, w["nffn"], w["wup"], w["fcw"], w["fcb"], w["wdn"], w["gfin"], tile)
    return (y, ssm.reshape(1, b, SSM_HEADS, SSM_HEAD_DIM, SSM_STATE), conv[None], pool[None], ffn[None],
            _kv_from_rows(mem_k)[None], _kv_from_rows(mem_v)[None])


PROMPT_TILE = 512
ATTN_TILE = 1024
MIXER_TILE = 1024
MIXER_SUB_TILE = 256
ATTN_SUB_TILE = 256
FFN_SUB_TILE = 512


DEC_SEQ = 4
S_SEQ_BLOCK = 64
S_SSD_BLOCK = 16
S_ATT_BLOCK = 8
X_ROWS_PER_SEQ = DEC_SEQ * D_MODEL // LANES
FFN_ROWS_PER_SEQ = (FFN_CONV - 1) * 2 * D_FF // LANES


def _expand_heads(v, lo):
    r = v.shape[0]
    tiles = []
    for pr in range(SSM_HEADS // 2):
        a = jnp.broadcast_to(v[:, 2 * pr:2 * pr + 1], (r, LANES))
        b = jnp.broadcast_to(v[:, 2 * pr + 1:2 * pr + 2], (r, LANES))
        tiles.append(jnp.where(lo, a, b))
    return jnp.concatenate(tiles, axis=1)


def _tiled_rows_view(a):
    ns, r, c = a.shape
    return a.reshape(ns, r, c // LANES, LANES).transpose(0, 2, 1, 3).reshape(ns * (c // LANES) * r, LANES)


def _from_tiled_rows(rows, ns, r, c):
    return rows.reshape(ns, c // LANES, r, LANES).transpose(0, 2, 1, 3).reshape(ns, r, c)


def _rows_view_get(ref, i, r, tiles, nb, per_seq):
    return jnp.concatenate([ref[pl.ds(dt * r + i, nb, stride=per_seq), :] for dt in tiles], axis=1)


def _rows_view_put(ref, i, r, tiles, nb, per_seq, val):
    for n, dt in enumerate(tiles):
        ref[pl.ds(dt * r + i, nb, stride=per_seq), :] = val[:, n * LANES:(n + 1) * LANES]


def _steps(ref):
    return ref[...].reshape(DEC_SEQ * ref.shape[1], ref.shape[2])


def _smix_in_kernel(x_ref, sconv_ref, spool_ref, nmix_ref, wzx_ref, wvp_ref, wdt_ref, cw_ref, cb_ref, dtb_ref, a_ref, dexp_ref,
                    wpool_ref, pscale_ref,
                    xtm_ref, z_ref, ypart_ref, eacs_ref, xd_ref, c_ref, b_ref, dec_ref, pout_ref, conv_ref, pool_ref,
                    h_scr, pooled_scr):
    nb = S_SEQ_BLOCK
    lo = lax.broadcasted_iota(jnp.int32, (nb, LANES), 1) < SSM_HEAD_DIM
    x_tiles = range(D_MODEL // LANES)
    x_steps = [_rows_view_get(x_ref, l, DEC_SEQ, x_tiles, nb, X_ROWS_PER_SEQ) for l in range(DEC_SEQ)]
    for l in range(DEC_SEQ):
        xtm_ref[l] = x_steps[l]
    h_scr[...] = _rmsnorm(jnp.concatenate(x_steps, axis=0), nmix_ref[...]).astype(bf16)
    z = _dot(h_scr[...], wzx_ref[:, :D_SSM])
    for l in range(DEC_SEQ):
        z_ref[l] = z[l * nb:(l + 1) * nb]
    xbc = _dot(h_scr[...], wzx_ref[:, D_SSM:])
    vp = _dot(h_scr[...], wvp_ref[...])
    dtr = _dot(h_scr[...], wdt_ref[...])

    def conv_slot(i):
        if i < SSM_CONV - 1:
            return sconv_ref[i]
        return xbc[(i - SSM_CONV + 1) * nb:(i - SSM_CONV + 2) * nb]

    def pool_slot(i, cl):
        if i < POOL_HIST:
            return spool_ref[i, :, cl]
        return vp[(i - POOL_HIST) * nb:(i - POOL_HIST + 1) * nb, cl]

    xs, bm, cm, dt, acs = [], [], [], [], []
    for l in range(DEC_SEQ):
        acc = cb_ref[...] + cw_ref[0:1, :] * conv_slot(l)
        for k in range(1, SSM_CONV):
            acc = acc + cw_ref[k:k + 1, :] * conv_slot(l + k)
        act = _silu(acc)
        xs.append(act[:, :D_SSM])
        bm.append(act[:, D_SSM:D_SSM + D_BC])
        cm.append(act[:, D_SSM + D_BC:])
        b_ref[l] = bm[l]
        c_ref[l] = cm[l]
        dt.append(_softplus(dtr[l * nb:(l + 1) * nb] + dtb_ref[...]))
        da = dt[l] * a_ref[...]
        acs.append(da if l == 0 else acs[l - 1] + da)
        for gi, w in enumerate(POOL_WINDOWS):
            gl = slice(gi * POOL_GROUP_DIM, (gi + 1) * POOL_GROUP_DIM)
            s = pool_slot(POOL_HIST + l, gl)
            for k in range(1, w):
                s = s + pool_slot(POOL_HIST + l - k, gl)
            cnt = float(min(PAST_LEN + l + 1, w))
            pooled_scr[l * nb:(l + 1) * nb, gl] = (s / cnt - pool_slot(POOL_HIST + l, gl)).astype(bf16)
    for i in range(SSM_CONV - 1):
        conv_ref[i] = conv_slot(DEC_SEQ + i)
    for i in range(POOL_HIST):
        pool_ref[i] = pool_slot(DEC_SEQ + i, slice(0, D_POOL))
    pout = jnp.concatenate(
        [_dot(pooled_scr[:, gi * POOL_GROUP_DIM:(gi + 1) * POOL_GROUP_DIM], wpool_ref[gi])
         for gi in range(len(POOL_WINDOWS))], axis=1) * pscale_ref[...]
    for l in range(DEC_SEQ):
        pout_ref[l] = pout[l * nb:(l + 1) * nb]

    xdt = [xs[l] * _expand_heads(dt[l], lo) for l in range(DEC_SEQ)]
    gw = D_SSM // SSM_GROUPS
    for l in range(DEC_SEQ):
        y = xs[l] * dexp_ref[...]
        for s in range(l + 1):
            decay = _expand_heads(jnp.exp(acs[l] - acs[s]), lo)
            cbs = [jnp.sum(cm[l][:, g * SSM_STATE:(g + 1) * SSM_STATE] * bm[s][:, g * SSM_STATE:(g + 1) * SSM_STATE],
                           axis=-1, keepdims=True) for g in range(SSM_GROUPS)]
            coef = jnp.concatenate([decay[:, g * gw:(g + 1) * gw] * cbs[g] for g in range(SSM_GROUPS)], axis=1)
            y = y + coef * xdt[s]
        ypart_ref[l] = y
        eacs_ref[l] = _expand_heads(jnp.exp(acs[l]), lo)
        xd_ref[l] = xdt[l] * _expand_heads(jnp.exp(acs[DEC_SEQ - 1] - acs[l]), lo)
    dec_ref[...] = jnp.exp(acs[DEC_SEQ - 1])


def _smix_in(x_rows, sconv, spool, w):
    ns = sconv.shape[1]
    nb = S_SEQ_BLOCK
    tmaj = lambda steps, width: pl.BlockSpec((steps, nb, width), lambda i: (0, i, 0))
    step_outs = [D_MODEL, D_SSM, D_SSM, D_SSM, D_SSM, D_BC, D_BC]
    out_specs = [tmaj(DEC_SEQ, wd) for wd in step_outs] + [pl.BlockSpec((nb, LANES), lambda i: (i, 0)),
                                                          tmaj(DEC_SEQ, D_POOL),
                                                          tmaj(SSM_CONV - 1, D_XBC), tmaj(POOL_HIST, D_POOL)]
    out_shape = [jax.ShapeDtypeStruct((DEC_SEQ, ns, wd), f32) for wd in step_outs] + [
        jax.ShapeDtypeStruct((ns, LANES), f32), jax.ShapeDtypeStruct((DEC_SEQ, ns, D_POOL), f32),
        jax.ShapeDtypeStruct((SSM_CONV - 1, ns, D_XBC), f32), jax.ShapeDtypeStruct((POOL_HIST, ns, D_POOL), f32)]
    return pl.pallas_call(
        _smix_in_kernel,
        grid=(ns // nb,),
        in_specs=[pl.BlockSpec((nb * X_ROWS_PER_SEQ, LANES), lambda i: (i, 0)), tmaj(SSM_CONV - 1, D_XBC),
                  tmaj(POOL_HIST, D_POOL),
                  _const_spec((1, D_MODEL)), _const_spec((D_MODEL, D_SSM + D_XBC)), _const_spec((D_MODEL, D_POOL)),
                  _const_spec((D_MODEL, LANES)), _const_spec((SSM_CONV, D_XBC)),
                  _const_spec((1, D_XBC)), _const_spec((1, LANES)), _const_spec((1, LANES)), _const_spec((1, D_SSM)),
                  _const_spec((len(POOL_WINDOWS), POOL_GROUP_DIM, POOL_GROUP_DIM)), _const_spec((1, D_POOL))],
        out_specs=out_specs,
        out_shape=out_shape,
        scratch_shapes=[pltpu.VMEM((DEC_SEQ * nb, D_MODEL), bf16), pltpu.VMEM((DEC_SEQ * nb, D_POOL), bf16)],
        compiler_params=pltpu.CompilerParams(dimension_semantics=("arbitrary",), vmem_limit_bytes=VMEM_LIMIT_BYTES),
        name="smix_in",
    )(x_rows, sconv, spool, w["nmix"], w["wzx"], w["wvp"], w["wdt"], w["cw"], w["cb"], w["dtb"], w["a_row"], w["dexp"], w["wpool"],
      w["pscale"])


def _sssd_kernel(dec_ref, c_ref, b_ref, xd_ref, st_ref, yoff_ref, stn_ref):
    blk = pl.program_id(0)
    nb = S_SSD_BLOCK
    gw = D_SSM // SSM_GROUPS
    hpg = SSM_HEADS // SSM_GROUPS
    row_seq = lax.broadcasted_iota(jnp.int32, (DEC_SEQ * nb, gw), 0) % nb
    cmat, bmat, xd = _steps(c_ref), _steps(b_ref), _steps(xd_ref)
    for g in range(SSM_GROUPS):
        gl = slice(g * gw, (g + 1) * gw)
        cg = cmat[:, g * SSM_STATE:(g + 1) * SSM_STATE].astype(bf16)
        bg = bmat[:, g * SSM_STATE:(g + 1) * SSM_STATE].astype(bf16)
        yo = jnp.zeros((DEC_SEQ * nb, gw), f32)
        for j in range(nb):
            mine = row_seq == j
            h0 = st_ref[j, gl, :]
            yo = jnp.where(mine, _dot_nt(cg, h0.astype(bf16)), yo)
            upd = _dot_tn(jnp.where(mine, xd[:, gl], 0.0).astype(bf16), bg)
            for hh in range(hpg):
                hr = slice(hh * SSM_HEAD_DIM, (hh + 1) * SSM_HEAD_DIM)
                d = dec_ref[(blk * nb + j) * SSM_HEADS + g * hpg + hh]
                stn_ref[j, g * gw + hh * SSM_HEAD_DIM:g * gw + (hh + 1) * SSM_HEAD_DIM, :] = h0[hr] * d + upd[hr]
        for l in range(DEC_SEQ):
            yoff_ref[l, :, gl] = yo[l * nb:(l + 1) * nb]


def _sssd(dec_flat, cmat, bmat, xd, state):
    ns = state.shape[0]
    nb = S_SSD_BLOCK
    tmaj = lambda width: pl.BlockSpec((DEC_SEQ, nb, width), lambda i: (0, i, 0))
    stblk = pl.BlockSpec((nb, D_SSM, SSM_STATE), lambda i: (i, 0, 0))
    return pl.pallas_call(
        _sssd_kernel,
        grid=(ns // nb,),
        in_specs=[pl.BlockSpec(memory_space=pltpu.SMEM), tmaj(D_BC), tmaj(D_BC), tmaj(D_SSM), stblk],
        out_specs=[tmaj(D_SSM), stblk],
        out_shape=[jax.ShapeDtypeStruct((DEC_SEQ, ns, D_SSM), f32), jax.ShapeDtypeStruct(state.shape, f32)],
        compiler_params=pltpu.CompilerParams(dimension_semantics=("arbitrary",), vmem_limit_bytes=VMEM_LIMIT_BYTES),
        name="sssd",
    )(dec_flat, cmat, bmat, xd, state)


def _smix_out_kernel(x_ref, ypart_ref, yoff_ref, eacs_ref, z_ref, pout_ref, snorm_ref, wout_ref, nmem_ref, wq_ref,
                     x1_ref, q_ref, cat_scr):
    y = ypart_ref[...] + yoff_ref[...] * eacs_ref[...]
    t = y * _silu(z_ref[...])
    gw = D_SSM // SSM_GROUPS
    for g in range(SSM_GROUPS):
        gl = slice(g * gw, (g + 1) * gw)
        tg = t[:, gl]
        ms = jnp.mean(tg * tg, axis=-1, keepdims=True)
        cat_scr[:, gl] = (tg * lax.rsqrt(ms + EPS) * snorm_ref[:, gl]).astype(bf16)
    cat_scr[:, D_SSM:] = pout_ref[...].astype(bf16)
    x1 = x_ref[...] + _dot(cat_scr[...], wout_ref[...])
    x1_ref[...] = x1
    h = _rmsnorm(x1, nmem_ref[...]).astype(bf16)
    q_ref[...] = _dot(h, wq_ref[...]) * (MEM_HEAD_DIM ** -0.5)


def _smix_out(x, ypart, yoff, eacs, z, pout, w):
    n = x.shape[0]
    rb = 128
    rows = lambda width: pl.BlockSpec((rb, width), lambda i: (i, 0))
    return pl.pallas_call(
        _smix_out_kernel,
        grid=(n // rb,),
        in_specs=[rows(D_MODEL), rows(D_SSM), rows(D_SSM), rows(D_SSM), rows(D_SSM), rows(D_POOL),
                  _const_spec((1, D_SSM)), _const_spec((D_SSM + D_POOL, D_MODEL)), _const_spec((1, D_MODEL)),
                  _const_spec((D_MODEL, D_MODEL))],
        out_specs=[rows(D_MODEL), rows(D_MODEL)],
        out_shape=[jax.ShapeDtypeStruct((n, D_MODEL), f32), jax.ShapeDtypeStruct((n, D_MODEL), f32)],
        scratch_shapes=[pltpu.VMEM((rb, D_SSM + D_POOL), bf16)],
        compiler_params=pltpu.CompilerParams(dimension_semantics=("arbitrary",), vmem_limit_bytes=VMEM_LIMIT_BYTES),
        name="smix_out",
    )(x, ypart, yoff, eacs, z, pout, w["snorm"], w["wout"], w["nmem"], w["wq"])


def _sattn_kernel(q_ref, k_ref, v_ref, o_ref):
    nb = S_ATT_BLOCK
    rg = DEC_SEQ * nb
    rows = MEM_HEADS * rg

    def row_ids(width):
        r = lax.broadcasted_iota(jnp.int32, (rows, width), 0)
        return r // rg, r % nb

    row_h, row_seq = row_ids(D_MODEL)
    col_h = lax.broadcasted_iota(jnp.int32, (rows, D_MODEL), 1) // MEM_HEAD_DIM
    q = _steps(q_ref)
    qh = jnp.where(row_h == col_h, jnp.concatenate([q] * MEM_HEADS, axis=0), 0.0)
    lhs_s = jnp.concatenate([jnp.where(row_seq == b, qh, 0.0).astype(bf16) for b in range(nb)], axis=1)
    kcat = jnp.concatenate([_kv_seq(k_ref, b) for b in range(nb)], axis=1)
    s = _dot_nt(lhs_s, kcat)
    p = jnp.exp(s - jnp.max(s, axis=-1, keepdims=True))
    p = p / jnp.sum(p, axis=-1, keepdims=True)
    _, row_seq_p = row_ids(N_MEM)
    lhs_p = jnp.concatenate([jnp.where(row_seq_p == b, p, 0.0).astype(bf16) for b in range(nb)], axis=1)
    vcat = jnp.concatenate([_kv_seq(v_ref, b) for b in range(nb)], axis=0)
    res = _dot(lhs_p, vcat)
    for hd in range(MEM_HEADS):
        hl = slice(hd * MEM_HEAD_DIM, (hd + 1) * MEM_HEAD_DIM)
        for l in range(DEC_SEQ):
            o_ref[l, :, hl] = res[hd * rg + l * nb:hd * rg + (l + 1) * nb, hl]


def _sattn(q, mem_k, mem_v):
    ns = mem_k.shape[0]
    nb = S_ATT_BLOCK
    qblk = pl.BlockSpec((DEC_SEQ, nb, D_MODEL), lambda i: (0, i, 0))
    kvblk = pl.BlockSpec((nb, N_MEM * KV_ROWS_PER_TOKEN, LANES), lambda i: (i, 0, 0))
    return pl.pallas_call(
        _sattn_kernel,
        grid=(ns // nb,),
        in_specs=[qblk, kvblk, kvblk],
        out_specs=qblk,
        out_shape=jax.ShapeDtypeStruct((DEC_SEQ, ns, D_MODEL), f32),
        compiler_params=pltpu.CompilerParams(dimension_semantics=("arbitrary",), vmem_limit_bytes=VMEM_LIMIT_BYTES),
        name="sattn",
    )(q, mem_k, mem_v)


def _sffn_kernel(x1_ref, ao_ref, sffn_ref, wo_ref, g_ref, wup_ref, cw_ref, cb_ref, wdn_ref, gfin_ref,
                 y_ref, st_ref, h_scr, act_scr):
    nb = S_SEQ_BLOCK
    x2 = _steps(x1_ref) + _dot(_steps(ao_ref).astype(bf16), wo_ref[...])
    h_scr[...] = _rmsnorm(x2, g_ref[...]).astype(bf16)
    u = _dot(h_scr[...], wup_ref[...])
    hist = FFN_CONV - 1

    def conv_block(blk):
        cols = slice(blk * FF_CHUNK, (blk + 1) * FF_CHUNK)
        tiles = range(blk * FF_CHUNK // LANES, (blk + 1) * FF_CHUNK // LANES)
        slots = [sffn_ref[:, i, cols] for i in range(hist)]
        slots += [u[l * nb:(l + 1) * nb, cols] for l in range(DEC_SEQ)]
        for i in range(hist):
            st_ref[:, i, cols] = slots[DEC_SEQ + i]
        outs = []
        for l in range(DEC_SEQ):
            acc = cb_ref[:, cols] + cw_ref[0:1, cols] * slots[l]
            for k in range(1, FFN_CONV):
                acc = acc + cw_ref[k:k + 1, cols] * slots[l + k]
            outs.append(acc)
        return jnp.concatenate(outs, axis=0)

    for j in range(N_FF_CHUNKS):
        act = _silu(conv_block(_gate_blk(j))) * conv_block(_val_blk(j))
        act_scr[:, j * FF_CHUNK:(j + 1) * FF_CHUNK] = act.astype(bf16)
    y = _rmsnorm(x2 + _dot(act_scr[...], wdn_ref[...]), gfin_ref[...])
    for l in range(DEC_SEQ):
        _rows_view_put(y_ref, l, DEC_SEQ, range(D_MODEL // LANES), nb, X_ROWS_PER_SEQ, y[l * nb:(l + 1) * nb])


def _sffn(x1, ao, sffn_rows, w):
    ns = x1.shape[1]
    nb = S_SEQ_BLOCK
    tok = pl.BlockSpec((DEC_SEQ, nb, D_MODEL), lambda i: (0, i, 0))
    yblk = pl.BlockSpec((nb * X_ROWS_PER_SEQ, LANES), lambda i: (i, 0))
    stblk = pl.BlockSpec((nb, FFN_CONV - 1, 2 * D_FF), lambda i: (i, 0, 0))
    return pl.pallas_call(
        _sffn_kernel,
        grid=(ns // nb,),
        in_specs=[tok, tok, stblk,
                  _const_spec((D_MODEL, D_MODEL)), _const_spec((1, D_MODEL)), _const_spec((D_MODEL, 2 * D_FF)),
                  _const_spec((FFN_CONV, 2 * D_FF)), _const_spec((1, 2 * D_FF)),
                  _const_spec((D_FF, D_MODEL)), _const_spec((1, D_MODEL))],
        out_specs=[yblk, stblk],
        out_shape=[jax.ShapeDtypeStruct((ns * X_ROWS_PER_SEQ, LANES), f32),
                   jax.ShapeDtypeStruct((ns, FFN_CONV - 1, 2 * D_FF), f32)],
        scratch_shapes=[pltpu.VMEM((DEC_SEQ * nb, D_MODEL), bf16), pltpu.VMEM((DEC_SEQ * nb, D_FF), bf16)],
        compiler_params=pltpu.CompilerParams(dimension_semantics=("arbitrary",), vmem_limit_bytes=VMEM_LIMIT_BYTES),
        name="sffn",
    )(x1, ao, sffn_rows, w["wo"], w["nffn"], w["wup"], w["fcw"], w["fcb"], w["wdn"], w["gfin"])


def _sample_path(x_sample, state_ssm, state_ssm_conv, state_pool, state_ffn_conv, cache_mem_k, cache_mem_v, w):
    ns = x_sample.shape[0]
    xtm, z, ypart, eacs, xd, cmat, bmat, dec, pout, conv_new, pool_new = _smix_in(
        _tiled_rows_view(x_sample), state_ssm_conv[0].transpose(1, 0, 2), state_pool[0].transpose(1, 0, 2), w)
    yoff, ssm_new = _sssd(dec[:, :SSM_HEADS].reshape(-1), cmat, bmat, xd, state_ssm[0].reshape(ns, D_SSM, SSM_STATE))
    flat = lambda a: a.reshape(DEC_SEQ * ns, a.shape[-1])
    x1, q = _smix_out(flat(xtm), flat(ypart), flat(yoff), flat(eacs), flat(z), flat(pout), w)
    ao = _sattn(q.reshape(DEC_SEQ, ns, D_MODEL), _kv_rows_view(cache_mem_k[0]), _kv_rows_view(cache_mem_v[0]))
    y_rows, ffn_new = _sffn(x1.reshape(DEC_SEQ, ns, D_MODEL), ao, state_ffn_conv[0], w)
    return (_from_tiled_rows(y_rows, ns, DEC_SEQ, D_MODEL),
            ssm_new.reshape(1, ns, SSM_HEADS, SSM_HEAD_DIM, SSM_STATE),
            conv_new.transpose(1, 0, 2)[None], pool_new.transpose(1, 0, 2)[None],
            ffn_new[None])


def kernel(x_prompt, x_sample, mem_prompt, state_ssm, state_ssm_conv, state_pool, state_ffn_conv, cache_mem_k, cache_mem_v, norm_mix, w_in, ssm_conv_w, ssm_conv_b, ssm_dt_bias, ssm_a_log, ssm_d, ssm_norm, w_pool, pool_scale, w_out, norm_mem, norm_memkv, w_mq, w_mk, w_mv, w_mo, norm_ffn, w_up, ffn_conv_w, ffn_conv_b, w_down, final_norm):
    params = dict(norm_mix=norm_mix, w_in=w_in, ssm_conv_w=ssm_conv_w, ssm_conv_b=ssm_conv_b, ssm_dt_bias=ssm_dt_bias,
                  ssm_a_log=ssm_a_log, ssm_d=ssm_d, ssm_norm=ssm_norm, w_pool=w_pool, pool_scale=pool_scale,
                  w_out=w_out, norm_mem=norm_mem, norm_memkv=norm_memkv, w_mq=w_mq, w_mk=w_mk, w_mv=w_mv, w_mo=w_mo,
                  norm_ffn=norm_ffn, w_up=w_up, ffn_conv_w=ffn_conv_w, ffn_conv_b=ffn_conv_b, w_down=w_down,
                  final_norm=final_norm)
    w = _prep_weights(params)
    yp, ssm_p, conv_p, pool_p, ffn_p, mk_p, mv_p = _prompt_path(x_prompt, mem_prompt, w, PROMPT_TILE)
    ys, ssm_s, conv_s, pool_s, ffn_s = _sample_path(x_sample, state_ssm, state_ssm_conv, state_pool, state_ffn_conv,
                                                    cache_mem_k, cache_mem_v, w)
    return yp, ys, ssm_p, ssm_s, conv_p, conv_s, pool_p, pool_s, ffn_p, ffn_s, mk_p, mv_p
```

```python
import functools

import jax
import jax.numpy as jnp
from jax import lax
from jax.experimental import pallas as pl
from jax.experimental.pallas import tpu as pltpu

f32 = jnp.float32
bf16 = jnp.bfloat16

D_MODEL = 1024
SSM_HEADS = 16
SSM_HEAD_DIM = 64
SSM_STATE = 128
SSM_GROUPS = 2
SSM_CHUNK = 128
D_SSM = 1024
D_BC = SSM_GROUPS * SSM_STATE
D_XBC = D_SSM + 2 * D_BC
SSM_CONV = 4
D_POOL = 1024
POOL_WINDOWS = (2, 4, 8, 16)
POOL_GROUP_DIM = 256
POOL_HIST = 15
N_MEM = 256
MEM_HEADS = 4
MEM_HEAD_DIM = 256
D_FF = 2816
FFN_CONV = 3
EPS = 1e-6
PAST_LEN = 16384

LANES = 128
SUBLANES = 8
MXU_DIM = 256
VMEM_LIMIT_BYTES = 56 * 1024 * 1024

CONV_HIST_ROWS = SUBLANES
POOL_HIST_ROWS = 2 * SUBLANES
FF_CHUNK = MXU_DIM
N_FF_CHUNKS = D_FF // FF_CHUNK


def _silu(v):
    return v * (1.0 / (1.0 + jnp.exp(-v)))


def _softplus(v):
    return jnp.maximum(v, 0.0) + jnp.log1p(jnp.exp(-jnp.abs(v)))


def _rmsnorm(x, g):
    ms = jnp.mean(x * x, axis=-1, keepdims=True)
    return x * lax.rsqrt(ms + EPS) * g


def _dot(a, b):
    return jnp.dot(a, b, preferred_element_type=f32)


def _dot_nt(a, b):
    return lax.dot_general(a, b, (((1,), (1,)), ((), ())), preferred_element_type=f32)


def _dot_tn(a, b):
    return lax.dot_general(a, b, (((0,), (0,)), ((), ())), preferred_element_type=f32)


def _split3(v):
    p1 = v.astype(bf16)
    r1 = v - p1.astype(f32)
    p2 = r1.astype(bf16)
    r2 = r1 - p2.astype(f32)
    return p1, p2, r2.astype(bf16)


def _const_spec(shape):
    return pl.BlockSpec(shape, lambda *_: (0,) * len(shape), pipeline_mode=pl.Buffered(1))


KV_LANE_TILES = MEM_HEAD_DIM // LANES
KV_ROWS_PER_TOKEN = KV_LANE_TILES * MEM_HEADS


def _kv_rows_view(kv):
    ns = kv.shape[0]
    kv = kv.reshape(ns, N_MEM, MEM_HEADS, KV_LANE_TILES, LANES).transpose(0, 1, 3, 2, 4)
    return kv.reshape(ns, N_MEM * KV_ROWS_PER_TOKEN, LANES)


def _kv_from_rows(rows):
    ns = rows.shape[0]
    kv = rows.reshape(ns, N_MEM, KV_LANE_TILES, MEM_HEADS, LANES).transpose(0, 1, 3, 2, 4)
    return kv.reshape(ns, N_MEM, MEM_HEADS, MEM_HEAD_DIM)


def _kv_seq(ref, jj):
    tiles = [ref[jj, pl.ds(dt * MEM_HEADS + hd, N_MEM, stride=KV_ROWS_PER_TOKEN), :]
             for hd in range(MEM_HEADS) for dt in range(KV_LANE_TILES)]
    return jnp.concatenate(tiles, axis=1).astype(bf16)


def _memkv_kernel(mem_ref, g_ref, wk_ref, wv_ref, k_ref, v_ref):
    h = _rmsnorm(mem_ref[0], g_ref[...]).astype(bf16)
    for out_ref, w_ref in ((k_ref, wk_ref), (v_ref, wv_ref)):
        kv = _dot(h, w_ref[...])
        for hd in range(MEM_HEADS):
            for dt in range(KV_LANE_TILES):
                col = hd * MEM_HEAD_DIM + dt * LANES
                out_ref[0, pl.ds(dt * MEM_HEADS + hd, N_MEM, stride=KV_ROWS_PER_TOKEN), :] = kv[:, col:col + LANES]


def _memkv(mem, g, wk, wv):
    b = mem.shape[0]
    blk = pl.BlockSpec((1, N_MEM, D_MODEL), lambda i: (i, 0, 0))
    oblk = pl.BlockSpec((1, N_MEM * KV_ROWS_PER_TOKEN, LANES), lambda i: (i, 0, 0))
    return pl.pallas_call(
        _memkv_kernel,
        grid=(b,),
        in_specs=[blk, _const_spec((1, D_MODEL)), _const_spec((D_MODEL, D_MODEL)), _const_spec((D_MODEL, D_MODEL))],
        out_specs=[oblk, oblk],
        out_shape=[jax.ShapeDtypeStruct((b, N_MEM * KV_ROWS_PER_TOKEN, LANES), f32)] * 2,
        compiler_params=pltpu.CompilerParams(dimension_semantics=("arbitrary",), vmem_limit_bytes=VMEM_LIMIT_BYTES),
        name="memkv",
    )(mem, g, wk, wv)


def _ssd_chunk(r0, dt_scr, xs_scr, b_scr, c_scr, y_scr, xd_scr, hst_scr, a_ref, dexp_ref, between=()):
    between = list(between) + [None] * 3
    q = SSM_CHUNK
    rows = pl.ds(r0, q)
    row_i = lax.broadcasted_iota(jnp.int32, (q, q), 0)
    col_i = lax.broadcasted_iota(jnp.int32, (q, q), 1)
    causal = col_i <= row_i
    lo = col_i < SSM_HEAD_DIM
    tril = jnp.where(causal, 1.0, 0.0).astype(bf16)

    dt = dt_scr[rows, :]
    da = dt * a_ref[...]
    p1, p2, p3 = _split3(da)
    acs = _dot(tril, p1) + _dot(tril, p2) + _dot(tril, p3)
    acs_t = acs.T
    if between[0] is not None:
        between[0]()

    for g in range(SSM_GROUPS):
        bg = b_scr[rows, g * SSM_STATE:(g + 1) * SSM_STATE]
        cg = c_scr[rows, g * SSM_STATE:(g + 1) * SSM_STATE]
        bg_b = bg.astype(bf16)
        cb = jnp.where(causal, _dot_nt(cg.astype(bf16), bg_b), 0.0)
        cdec_rows = []
        pairs_per_group = SSM_HEADS // SSM_GROUPS // 2
        for pp in range(pairs_per_group):
            pr = g * pairs_per_group + pp
            lanes = slice(pr * LANES, (pr + 1) * LANES)
            lhs, dtb, dend, cdec = [], [], [], []
            for hh in (2 * pr, 2 * pr + 1):
                colb = jnp.broadcast_to(acs[:, hh:hh + 1], (q, q))
                seg = jnp.where(causal, colb - acs_t[hh:hh + 1, :], 0.0)
                lhs.append((jnp.exp(seg) * cb).astype(bf16))
                lhs.append((cg * jnp.exp(colb)).astype(bf16))
                last = colb[q - 1:q, :]
                dend.append(jnp.exp(last - colb))
                cdec.append(jnp.exp(last))
                dtb.append(jnp.broadcast_to(dt[:, hh:hh + 1], (q, q)))
            xs_pair = xs_scr[rows, lanes]
            xdt = xs_pair * jnp.where(lo, dtb[0], dtb[1])
            xd_scr[:, lanes] = (xdt * jnp.where(lo, dend[0], dend[1])).astype(bf16)
            hst_pair = hst_scr[:, lanes]
            rhs = jnp.concatenate([
                jnp.where(lo, xdt, 0.0).astype(bf16), jnp.where(lo, hst_pair, 0.0).astype(bf16),
                jnp.where(lo, 0.0, xdt).astype(bf16), jnp.where(lo, 0.0, hst_pair).astype(bf16)], axis=0)
            y_pair = _dot(jnp.concatenate(lhs, axis=1), rhs)
            y_scr[rows, lanes] = y_pair + xs_pair * dexp_ref[:, lanes]
            cdec_rows.append(jnp.where(lo[:1], cdec[0], cdec[1]))
        gl = slice(g * (D_SSM // SSM_GROUPS), (g + 1) * (D_SSM // SSM_GROUPS))
        upd = _dot_tn(bg_b, xd_scr[:, gl])
        hst_scr[:, gl] = hst_scr[:, gl] * jnp.concatenate(cdec_rows, axis=1) + upd
        if between[1 + g] is not None:
            between[1 + g]()


def _mixer_kernel(x_ref, nmix_ref, wzx_ref, wvp_ref, wdt_ref, cw_ref, cb_ref, dtb_ref, a_ref, dexp_ref, snorm_ref, wpool_ref,
                  pscale_ref, wout_ref,
                  x1_ref, ssm_ref, conv_ref, pool_ref,
                  xd_scr, hst_scr, *sub_scr, tile, sub):
    t = pl.program_id(1)
    last_t = pl.num_programs(1) - 1
    ch = CONV_HIST_ROWS
    ph = POOL_HIST_ROWS
    n_sub = tile // sub
    per = len(sub_scr) // n_sub
    bufs = [sub_scr[i * per:(i + 1) * per] for i in range(n_sub)]
    xbc0, vp0 = bufs[0][2], bufs[0][3]

    @pl.when(t == 0)
    def _():
        xbc0[:, 0:ch, :] = jnp.zeros((D_XBC // LANES, ch, LANES), f32)
        vp0[:, 0:ph, :] = jnp.zeros((D_POOL // LANES, ph, LANES), f32)
        hst_scr[...] = jnp.zeros_like(hst_scr)

    gw = D_SSM // SSM_GROUPS

    def carry_history(src, dst):
        for j in range(D_XBC // LANES):
            bufs[dst][2][j, 0:ch, :] = bufs[src][2][j, sub:sub + ch, :]
        for j in range(D_POOL // LANES):
            bufs[dst][3][j, 0:ph, :] = bufs[src][3][j, sub:sub + ph, :]

    def in_xbc(s):
        h_scr, _, xbc_scr = bufs[s][:3]
        h_scr[...] = _rmsnorm(x_ref[0, s * sub:(s + 1) * sub, :], nmix_ref[...]).astype(bf16)
        xbc = _dot(h_scr[...], wzx_ref[:, D_SSM:])
        for j in range(D_XBC // LANES):
            xbc_scr[j, ch:ch + sub, :] = xbc[:, j * LANES:(j + 1) * LANES]

    def in_dtvp(s):
        h_scr, _, _, vp_scr, dt_scr = bufs[s][:5]
        dt_scr[...] = _dot(h_scr[...], wdt_ref[...])
        vp = _dot(h_scr[...], wvp_ref[...])
        for j in range(D_POOL // LANES):
            vp_scr[j, ph:ph + sub, :] = vp[:, j * LANES:(j + 1) * LANES]

    def in_z(s):
        h_scr, z_scr = bufs[s][:2]
        z_scr[...] = _dot(h_scr[...], wzx_ref[:, :D_SSM])
        if s > 0:
            carry_history(s - 1, s)

    def stage_mid(s, between):
        _, z_scr, xbc_scr, vp_scr, dt_scr, xs_scr, b_scr, c_scr, y_scr, pooled_scr, cat_scr = bufs[s]
        for j in range(D_XBC // LANES):
            cl = slice(j * LANES, (j + 1) * LANES)
            acc = cb_ref[:, cl] + cw_ref[0:1, cl] * xbc_scr[j, pl.ds(ch - 3, sub), :]
            for k in range(1, SSM_CONV):
                acc = acc + cw_ref[k:k + 1, cl] * xbc_scr[j, pl.ds(ch - 3 + k, sub), :]
            act = _silu(acc)
            if j < D_SSM // LANES:
                xs_scr[:, cl] = act
            elif j < (D_SSM + D_BC) // LANES:
                b_scr[:, j * LANES - D_SSM:(j + 1) * LANES - D_SSM] = act
            else:
                c_scr[:, j * LANES - D_SSM - D_BC:(j + 1) * LANES - D_SSM - D_BC] = act

        pos1 = lax.broadcasted_iota(jnp.int32, (sub, LANES), 0) + (t * tile + s * sub + 1)
        for j in range(D_POOL // LANES):
            cl = slice(j * LANES, (j + 1) * LANES)
            w = POOL_WINDOWS[j * LANES // POOL_GROUP_DIM]
            cur = vp_scr[j, ph:ph + sub, :]
            acc = cur
            for k in range(1, w):
                acc = acc + vp_scr[j, pl.ds(ph - k, sub), :]
            cnt = jnp.minimum(pos1, w).astype(f32)
            pooled_scr[:, cl] = (acc / cnt - cur).astype(bf16)
        for gi in range(len(POOL_WINDOWS)):
            gl = slice(gi * POOL_GROUP_DIM, (gi + 1) * POOL_GROUP_DIM)
            pg = _dot(pooled_scr[:, gl], wpool_ref[gi]) * pscale_ref[:, gl]
            cat_scr[:, D_SSM + gi * POOL_GROUP_DIM:D_SSM + (gi + 1) * POOL_GROUP_DIM] = pg.astype(bf16)

        dt_scr[...] = _softplus(dt_scr[...] + dtb_ref[...])
        n_chunks = sub // SSM_CHUNK
        slots = [None] * (3 * n_chunks)
        for i, piece in enumerate(between):
            slots[i * len(slots) // len(between)] = piece
        for c in range(n_chunks):
            _ssd_chunk(c * SSM_CHUNK, dt_scr, xs_scr, b_scr, c_scr, y_scr, xd_scr, hst_scr, a_ref, dexp_ref,
                       slots[3 * c:3 * c + 3])

        for g in range(SSM_GROUPS):
            gl = slice(g * gw, (g + 1) * gw)
            tg = y_scr[:, gl] * _silu(z_scr[:, gl])
            ms = jnp.mean(tg * tg, axis=-1, keepdims=True)
            cat_scr[:, gl] = (tg * lax.rsqrt(ms + EPS) * snorm_ref[:, gl]).astype(bf16)

    def stage_out(s):
        rows = slice(s * sub, (s + 1) * sub)
        x1_ref[0, rows, :] = x_ref[0, rows, :] + _dot(bufs[s][-1][...], wout_ref[...])

    in_xbc(0)
    in_dtvp(0)
    in_z(0)
    for s in range(n_sub):
        nxt = s + 1 < n_sub
        if nxt:
            in_xbc(s + 1)
        pieces = [functools.partial(in_dtvp, s + 1), functools.partial(in_z, s + 1)] if nxt else []
        if s > 0:
            pieces.append(functools.partial(stage_out, s - 1))
        stage_mid(s, pieces)
    stage_out(n_sub - 1)
    carry_history(n_sub - 1, 0)

    @pl.when(t == last_t)
    def _():
        for pr in range(D_SSM // LANES):
            ssm_ref[0, pr * LANES:(pr + 1) * LANES, :] = hst_scr[:, pr * LANES:(pr + 1) * LANES].T
        for j in range(D_XBC // LANES):
            conv_ref[0, :, j * LANES:(j + 1) * LANES] = xbc0[j, pl.ds(ch - (SSM_CONV - 1), SSM_CONV - 1), :]
        for j in range(D_POOL // LANES):
            pool_ref[0, :, j * LANES:(j + 1) * LANES] = vp0[j, pl.ds(ph - POOL_HIST, POOL_HIST), :]


def _mixer_prompt(x, nmix, wzx, wvp, wdt, cw, cb, dtb, a_row, dexp, snorm, wpool, pscale, wout, tile):
    b, seq, _ = x.shape
    nt = seq // tile
    sub = min(tile, MIXER_SUB_TILE)
    xblk = pl.BlockSpec((1, tile, D_MODEL), lambda i, j: (i, j, 0))
    sub_scratch = [
        pltpu.VMEM((sub, D_MODEL), bf16),
        pltpu.VMEM((sub, D_SSM), f32),
        pltpu.VMEM((D_XBC // LANES, CONV_HIST_ROWS + sub, LANES), f32),
        pltpu.VMEM((D_POOL // LANES, POOL_HIST_ROWS + sub, LANES), f32),
        pltpu.VMEM((sub, LANES), f32),
        pltpu.VMEM((sub, D_SSM), f32),
        pltpu.VMEM((sub, D_BC), f32),
        pltpu.VMEM((sub, D_BC), f32),
        pltpu.VMEM((sub, D_SSM), f32),
        pltpu.VMEM((sub, D_POOL), bf16),
        pltpu.VMEM((sub, D_SSM + D_POOL), bf16),
    ]
    scratch = [pltpu.VMEM((SSM_CHUNK, D_SSM), bf16),
               pltpu.VMEM((SSM_STATE, D_SSM), f32)]
    scratch += sub_scratch * (tile // sub)
    return pl.pallas_call(
        functools.partial(_mixer_kernel, tile=tile, sub=sub),
        grid=(b, nt),
        in_specs=[xblk, _const_spec((1, D_MODEL)), _const_spec((D_MODEL, D_SSM + D_XBC)),
                  _const_spec((D_MODEL, D_POOL)), _const_spec((D_MODEL, LANES)), _const_spec((SSM_CONV, D_XBC)),
                  _const_spec((1, D_XBC)), _const_spec((1, LANES)), _const_spec((1, LANES)), _const_spec((1, D_SSM)),
                  _const_spec((1, D_SSM)), _const_spec((len(POOL_WINDOWS), POOL_GROUP_DIM, POOL_GROUP_DIM)),
                  _const_spec((1, D_POOL)), _const_spec((D_SSM + D_POOL, D_MODEL))],
        out_specs=[xblk,
                   pl.BlockSpec((1, D_SSM, SSM_STATE), lambda i, j: (i, 0, 0)),
                   pl.BlockSpec((1, SSM_CONV - 1, D_XBC), lambda i, j: (i, 0, 0)),
                   pl.BlockSpec((1, POOL_HIST, D_POOL), lambda i, j: (i, 0, 0))],
        out_shape=[jax.ShapeDtypeStruct((b, seq, D_MODEL), f32),
                   jax.ShapeDtypeStruct((b, D_SSM, SSM_STATE), f32),
                   jax.ShapeDtypeStruct((b, SSM_CONV - 1, D_XBC), f32),
                   jax.ShapeDtypeStruct((b, POOL_HIST, D_POOL), f32)],
        scratch_shapes=scratch,
        compiler_params=pltpu.CompilerParams(dimension_semantics=("arbitrary", "arbitrary"),
                                             vmem_limit_bytes=VMEM_LIMIT_BYTES),
        name="mixer_prompt",
    )(x, nmix, wzx, wvp, wdt, cw, cb, dtb, a_row, dexp, snorm, wpool, pscale, wout)


def _attn_kernel(x_ref, g_ref, wq_ref, k_ref, v_ref, wo_ref, o_ref, q_scr, ao_scr):
    tile = x_ref.shape[1]
    sub = min(tile, ATTN_SUB_TILE)
    n_sub = tile // sub
    heads = [slice(hd * MEM_HEAD_DIM, (hd + 1) * MEM_HEAD_DIM) for hd in range(MEM_HEADS)]
    k = _kv_seq(k_ref, 0)
    v = _kv_seq(v_ref, 0)

    def q_proj(s):
        rows = slice(s * sub, (s + 1) * sub)
        h = _rmsnorm(x_ref[0, rows, :], g_ref[...]).astype(bf16)
        q_scr[rows, :] = (_dot(h, wq_ref[...]) * (MEM_HEAD_DIM ** -0.5)).astype(bf16)

    def scores(s):
        rows = slice(s * sub, (s + 1) * sub)
        return [_dot_nt(q_scr[rows, hl], k[:, hl]) for hl in heads]

    def values(s, sc):
        rows = slice(s * sub, (s + 1) * sub)
        for hl, s_h in zip(heads, sc):
            p = jnp.exp(s_h - jnp.max(s_h, axis=-1, keepdims=True))
            p = (p / jnp.sum(p, axis=-1, keepdims=True)).astype(bf16)
            ao_scr[rows, hl] = _dot(p, v[:, hl]).astype(bf16)

    def o_proj(s):
        rows = slice(s * sub, (s + 1) * sub)
        o_ref[0, rows, :] = x_ref[0, rows, :] + _dot(ao_scr[rows, :], wo_ref[...])

    q_proj(0)
    for s in range(n_sub):
        sc = scores(s)
        if s + 1 < n_sub:
            q_proj(s + 1)
        if s > 0:
            o_proj(s - 1)
        values(s, sc)
    o_proj(n_sub - 1)


def _attn_prompt(x, g, wq, mem_k, mem_v, wo, tile):
    b, seq, _ = x.shape
    xblk = pl.BlockSpec((1, tile, D_MODEL), lambda i, j: (i, j, 0))
    kvblk = pl.BlockSpec((1, N_MEM * KV_ROWS_PER_TOKEN, LANES), lambda i, j: (i, 0, 0))
    return pl.pallas_call(
        _attn_kernel,
        grid=(b, seq // tile),
        in_specs=[xblk, _const_spec((1, D_MODEL)), _const_spec((D_MODEL, D_MODEL)), kvblk, kvblk,
                  _const_spec((D_MODEL, D_MODEL))],
        out_specs=xblk,
        out_shape=jax.ShapeDtypeStruct((b, seq, D_MODEL), f32),
        scratch_shapes=[pltpu.VMEM((tile, D_MODEL), bf16), pltpu.VMEM((tile, D_MODEL), bf16)],
        compiler_params=pltpu.CompilerParams(dimension_semantics=("arbitrary", "arbitrary"),
                                             vmem_limit_bytes=VMEM_LIMIT_BYTES),
        name="attn_prompt",
    )(x, g, wq, mem_k, mem_v, wo)


def _gate_blk(j):
    return j


def _val_blk(j):
    return N_FF_CHUNKS + j


def _ffn_kernel(x_ref, g_ref, wup_ref, cw_ref, cb_ref, wdn_ref, gfin_ref, y_ref, st_ref, h_scr, u_scr, act_scr, *, tile):
    t = pl.program_id(1)
    last_t = pl.num_programs(1) - 1
    ch = CONV_HIST_ROWS
    tpc = FF_CHUNK // LANES
    ntile = 2 * D_FF // LANES

    @pl.when(t == 0)
    def _():
        u_scr[:, 0:ch, :] = jnp.zeros((ntile, ch, LANES), f32)

    sub = min(tile, FFN_SUB_TILE)
    n_sub = tile // sub

    def up_proj(s):
        rows = slice(s * sub, (s + 1) * sub)
        h_scr[rows, :] = _rmsnorm(x_ref[0, rows, :], g_ref[...]).astype(bf16)
        u = _dot(h_scr[rows, :], wup_ref[...])
        for ti in range(ntile):
            u_scr[ti, ch + s * sub:ch + (s + 1) * sub, :] = u[:, ti * LANES:(ti + 1) * LANES]

    def conv(s, ti):
        cl = slice(ti * LANES, (ti + 1) * LANES)
        r0 = ch + s * sub
        acc = cb_ref[:, cl] + cw_ref[0:1, cl] * u_scr[ti, pl.ds(r0 - 2, sub), :]
        acc = acc + cw_ref[1:2, cl] * u_scr[ti, pl.ds(r0 - 1, sub), :]
        return acc + cw_ref[2:3, cl] * u_scr[ti, r0:r0 + sub, :]

    def gate(s):
        rows = slice(s * sub, (s + 1) * sub)
        for j in range(N_FF_CHUNKS):
            for i in range(tpc):
                g_t = conv(s, _gate_blk(j) * tpc + i)
                v_t = conv(s, _val_blk(j) * tpc + i)
                act_scr[rows, (j * tpc + i) * LANES:(j * tpc + i + 1) * LANES] = (_silu(g_t) * v_t).astype(bf16)

    def down_proj(s):
        rows = slice(s * sub, (s + 1) * sub)
        y_ref[0, rows, :] = _rmsnorm(x_ref[0, rows, :] + _dot(act_scr[rows, :], wdn_ref[...]), gfin_ref[...])

    up_proj(0)
    for s in range(n_sub):
        if s + 1 < n_sub:
            up_proj(s + 1)
        gate(s)
        down_proj(s)
    for ti in range(ntile):
        u_scr[ti, 0:ch, :] = u_scr[ti, tile:tile + ch, :]

    @pl.when(t == last_t)
    def _():
        for ti in range(ntile):
            st_ref[0, :, ti * LANES:(ti + 1) * LANES] = u_scr[ti, pl.ds(ch - (FFN_CONV - 1), FFN_CONV - 1), :]


def _ffn_prompt(x, g, wup, cw, cb, wdn, gfin, tile):
    b, seq, _ = x.shape
    xblk = pl.BlockSpec((1, tile, D_MODEL), lambda i, j: (i, j, 0))
    return pl.pallas_call(
        functools.partial(_ffn_kernel, tile=tile),
        grid=(b, seq // tile),
        in_specs=[xblk, _const_spec((1, D_MODEL)), _const_spec((D_MODEL, 2 * D_FF)),
                  _const_spec((FFN_CONV, 2 * D_FF)), _const_spec((1, 2 * D_FF)),
                  _const_spec((D_FF, D_MODEL)), _const_spec((1, D_MODEL))],
        out_specs=[xblk, pl.BlockSpec((1, FFN_CONV - 1, 2 * D_FF), lambda i, j: (i, 0, 0))],
        out_shape=[jax.ShapeDtypeStruct((b, seq, D_MODEL), f32),
                   jax.ShapeDtypeStruct((b, FFN_CONV - 1, 2 * D_FF), f32)],
        scratch_shapes=[pltpu.VMEM((tile, D_MODEL), bf16),
                        pltpu.VMEM((2 * D_FF // LANES, CONV_HIST_ROWS + tile, LANES), f32),
                        pltpu.VMEM((tile, D_FF), bf16)],
        compiler_params=pltpu.CompilerParams(dimension_semantics=("arbitrary", "arbitrary"),
                                             vmem_limit_bytes=VMEM_LIMIT_BYTES),
        name="ffn_prompt",
    )(x, g, wup, cw, cb, wdn, gfin)


def _prep_weights(p):
    i = 0
    w_in = p["w_in"][i]
    n_zx = D_SSM + D_XBC
    wzx = w_in[:, :n_zx].astype(bf16)
    wdt = jnp.pad(w_in[:, n_zx:n_zx + SSM_HEADS], ((0, 0), (0, LANES - SSM_HEADS))).astype(bf16)
    wvp = w_in[:, n_zx + SSM_HEADS:].astype(bf16)
    pad_h = (0, LANES - SSM_HEADS)
    return dict(
        nmix=p["norm_mix"][i][None], wzx=wzx, wvp=wvp, wdt=wdt, cw=p["ssm_conv_w"][i], cb=p["ssm_conv_b"][i][None],
        dtb=jnp.pad(p["ssm_dt_bias"][i], pad_h)[None],
        a_row=jnp.pad(-jnp.exp(p["ssm_a_log"][i].astype(f32)), pad_h)[None],
        dexp=jnp.repeat(p["ssm_d"][i], SSM_HEAD_DIM)[None], snorm=p["ssm_norm"][i][None],
        wpool=p["w_pool"][i].astype(bf16), pscale=p["pool_scale"][i][None], wout=p["w_out"][i].astype(bf16),
        nmem=p["norm_mem"][i][None], nmemkv=p["norm_memkv"][i][None],
        wq=p["w_mq"][i].astype(bf16), wk=p["w_mk"][i].astype(bf16), wv=p["w_mv"][i].astype(bf16),
        wo=p["w_mo"][i].astype(bf16),
        nffn=p["norm_ffn"][i][None],
        wup=p["w_up"][i].astype(bf16), fcw=p["ffn_conv_w"][i], fcb=p["ffn_conv_b"][i][None],
        wdn=p["w_down"][i].astype(bf16),
        gfin=p["final_norm"][None],
    )


def _prompt_path(x_prompt, mem_prompt, w, tile):
    b = x_prompt.shape[0]
    mem_k, mem_v = _memkv(mem_prompt, w["nmemkv"], w["wk"], w["wv"])
    x1, ssm, conv, pool = _mixer_prompt(x_prompt, w["nmix"], w["wzx"], w["wvp"], w["wdt"], w["cw"], w["cb"], w["dtb"], w["a_row"],
                                        w["dexp"], w["snorm"], w["wpool"], w["pscale"], w["wout"],
                                        min(MIXER_TILE, x_prompt.shape[1]))
    x2 = _attn_prompt(x1, w["nmem"], w["wq"], mem_k, mem_v, w["wo"], min(ATTN_TILE, x_prompt.shape[1]))
    y, ffn = _ffn_prompt(x2, w["nffn"], w["wup"], w["fcw"], w["fcb"], w["wdn"], w["gfin"], tile)
    return (y, ssm.reshape(1, b, SSM_HEADS, SSM_HEAD_DIM, SSM_STATE), conv[None], pool[None], ffn[None],
            _kv_from_rows(mem_k)[None], _kv_from_rows(mem_v)[None])


PROMPT_TILE = 512
ATTN_TILE = 1024
MIXER_TILE = 1024
MIXER_SUB_TILE = 256
ATTN_SUB_TILE = 256
FFN_SUB_TILE = 512


DEC_SEQ = 4
S_SEQ_BLOCK = 64
S_SSD_BLOCK = 16
S_ATT_BLOCK = 8
X_ROWS_PER_SEQ = DEC_SEQ * D_MODEL // LANES
FFN_ROWS_PER_SEQ = (FFN_CONV - 1) * 2 * D_FF // LANES


def _expand_heads(v, lo):
    r = v.shape[0]
    tiles = []
    for pr in range(SSM_HEADS // 2):
        a = jnp.broadcast_to(v[:, 2 * pr:2 * pr + 1], (r, LANES))
        b = jnp.broadcast_to(v[:, 2 * pr + 1:2 * pr + 2], (r, LANES))
        tiles.append(jnp.where(lo, a, b))
    return jnp.concatenate(tiles, axis=1)


def _tiled_rows_view(a):
    ns, r, c = a.shape
    return a.reshape(ns, r, c // LANES, LANES).transpose(0, 2, 1, 3).reshape(ns * (c // LANES) * r, LANES)


def _from_tiled_rows(rows, ns, r, c):
    return rows.reshape(ns, c // LANES, r, LANES).transpose(0, 2, 1, 3).reshape(ns, r, c)


def _rows_view_get(ref, i, r, tiles, nb, per_seq):
    return jnp.concatenate([ref[pl.ds(dt * r + i, nb, stride=per_seq), :] for dt in tiles], axis=1)


def _rows_view_put(ref, i, r, tiles, nb, per_seq, val):
    for n, dt in enumerate(tiles):
        ref[pl.ds(dt * r + i, nb, stride=per_seq), :] = val[:, n * LANES:(n + 1) * LANES]


def _steps(ref):
    return ref[...].reshape(DEC_SEQ * ref.shape[1], ref.shape[2])


def _smix_in_kernel(x_ref, sconv_ref, spool_ref, nmix_ref, wzx_ref, wvp_ref, wdt_ref, cw_ref, cb_ref, dtb_ref, a_ref, dexp_ref,
                    wpool_ref, pscale_ref,
                    xtm_ref, z_ref, ypart_ref, eacs_ref, xd_ref, c_ref, b_ref, dec_ref, pout_ref, conv_ref, pool_ref,
                    h_scr, pooled_scr):
    nb = S_SEQ_BLOCK
    lo = lax.broadcasted_iota(jnp.int32, (nb, LANES), 1) < SSM_HEAD_DIM
    x_tiles = range(D_MODEL // LANES)
    x_steps = [_rows_view_get(x_ref, l, DEC_SEQ, x_tiles, nb, X_ROWS_PER_SEQ) for l in range(DEC_SEQ)]
    for l in range(DEC_SEQ):
        xtm_ref[l] = x_steps[l]
    h_scr[...] = _rmsnorm(jnp.concatenate(x_steps, axis=0), nmix_ref[...]).astype(bf16)
    z = _dot(h_scr[...], wzx_ref[:, :D_SSM])
    for l in range(DEC_SEQ):
        z_ref[l] = z[l * nb:(l + 1) * nb]
    xbc = _dot(h_scr[...], wzx_ref[:, D_SSM:])
    vp = _dot(h_scr[...], wvp_ref[...])
    dtr = _dot(h_scr[...], wdt_ref[...])

    def conv_slot(i):
        if i < SSM_CONV - 1:
            return sconv_ref[i]
        return xbc[(i - SSM_CONV + 1) * nb:(i - SSM_CONV + 2) * nb]

    def pool_slot(i, cl):
        if i < POOL_HIST:
            return spool_ref[i, :, cl]
        return vp[(i - POOL_HIST) * nb:(i - POOL_HIST + 1) * nb, cl]

    xs, bm, cm, dt, acs = [], [], [], [], []
    for l in range(DEC_SEQ):
        acc = cb_ref[...] + cw_ref[0:1, :] * conv_slot(l)
        for k in range(1, SSM_CONV):
            acc = acc + cw_ref[k:k + 1, :] * conv_slot(l + k)
        act = _silu(acc)
        xs.append(act[:, :D_SSM])
        bm.append(act[:, D_SSM:D_SSM + D_BC])
        cm.append(act[:, D_SSM + D_BC:])
        b_ref[l] = bm[l]
        c_ref[l] = cm[l]
        dt.append(_softplus(dtr[l * nb:(l + 1) * nb] + dtb_ref[...]))
        da = dt[l] * a_ref[...]
        acs.append(da if l == 0 else acs[l - 1] + da)
        for gi, w in enumerate(POOL_WINDOWS):
            gl = slice(gi * POOL_GROUP_DIM, (gi + 1) * POOL_GROUP_DIM)
            s = pool_slot(POOL_HIST + l, gl)
            for k in range(1, w):
                s = s + pool_slot(POOL_HIST + l - k, gl)
            cnt = float(min(PAST_LEN + l + 1, w))
            pooled_scr[l * nb:(l + 1) * nb, gl] = (s / cnt - pool_slot(POOL_HIST + l, gl)).astype(bf16)
    for i in range(SSM_CONV - 1):
        conv_ref[i] = conv_slot(DEC_SEQ + i)
    for i in range(POOL_HIST):
        pool_ref[i] = pool_slot(DEC_SEQ + i, slice(0, D_POOL))
    pout = jnp.concatenate(
        [_dot(pooled_scr[:, gi * POOL_GROUP_DIM:(gi + 1) * POOL_GROUP_DIM], wpool_ref[gi])
         for gi in range(len(POOL_WINDOWS))], axis=1) * pscale_ref[...]
    for l in range(DEC_SEQ):
        pout_ref[l] = pout[l * nb:(l + 1) * nb]

    xdt = [xs[l] * _expand_heads(dt[l], lo) for l in range(DEC_SEQ)]
    gw = D_SSM // SSM_GROUPS
    for l in range(DEC_SEQ):
        y = xs[l] * dexp_ref[...]
        for s in range(l + 1):
            decay = _expand_heads(jnp.exp(acs[l] - acs[s]), lo)
            cbs = [jnp.sum(cm[l][:, g * SSM_STATE:(g + 1) * SSM_STATE] * bm[s][:, g * SSM_STATE:(g + 1) * SSM_STATE],
                           axis=-1, keepdims=True) for g in range(SSM_GROUPS)]
            coef = jnp.concatenate([decay[:, g * gw:(g + 1) * gw] * cbs[g] for g in range(SSM_GROUPS)], axis=1)
            y = y + coef * xdt[s]
        ypart_ref[l] = y
        eacs_ref[l] = _expand_heads(jnp.exp(acs[l]), lo)
        xd_ref[l] = xdt[l] * _expand_heads(jnp.exp(acs[DEC_SEQ - 1] - acs[l]), lo)
    dec_ref[...] = jnp.exp(acs[DEC_SEQ - 1])


def _smix_in(x_rows, sconv, spool, w):
    ns = sconv.shape[1]
    nb = S_SEQ_BLOCK
    tmaj = lambda steps, width: pl.BlockSpec((steps, nb, width), lambda i: (0, i, 0))
    step_outs = [D_MODEL, D_SSM, D_SSM, D_SSM, D_SSM, D_BC, D_BC]
    out_specs = [tmaj(DEC_SEQ, wd) for wd in step_outs] + [pl.BlockSpec((nb, LANES), lambda i: (i, 0)),
                                                          tmaj(DEC_SEQ, D_POOL),
                                                          tmaj(SSM_CONV - 1, D_XBC), tmaj(POOL_HIST, D_POOL)]
    out_shape = [jax.ShapeDtypeStruct((DEC_SEQ, ns, wd), f32) for wd in step_outs] + [
        jax.ShapeDtypeStruct((ns, LANES), f32), jax.ShapeDtypeStruct((DEC_SEQ, ns, D_POOL), f32),
        jax.ShapeDtypeStruct((SSM_CONV - 1, ns, D_XBC), f32), jax.ShapeDtypeStruct((POOL_HIST, ns, D_POOL), f32)]
    return pl.pallas_call(
        _smix_in_kernel,
        grid=(ns // nb,),
        in_specs=[pl.BlockSpec((nb * X_ROWS_PER_SEQ, LANES), lambda i: (i, 0)), tmaj(SSM_CONV - 1, D_XBC),
                  tmaj(POOL_HIST, D_POOL),
                  _const_spec((1, D_MODEL)), _const_spec((D_MODEL, D_SSM + D_XBC)), _const_spec((D_MODEL, D_POOL)),
                  _const_spec((D_MODEL, LANES)), _const_spec((SSM_CONV, D_XBC)),
                  _const_spec((1, D_XBC)), _const_spec((1, LANES)), _const_spec((1, LANES)), _const_spec((1, D_SSM)),
                  _const_spec((len(POOL_WINDOWS), POOL_GROUP_DIM, POOL_GROUP_DIM)), _const_spec((1, D_POOL))],
        out_specs=out_specs,
        out_shape=out_shape,
        scratch_shapes=[pltpu.VMEM((DEC_SEQ * nb, D_MODEL), bf16), pltpu.VMEM((DEC_SEQ * nb, D_POOL), bf16)],
        compiler_params=pltpu.CompilerParams(dimension_semantics=("arbitrary",), vmem_limit_bytes=VMEM_LIMIT_BYTES),
        name="smix_in",
    )(x_rows, sconv, spool, w["nmix"], w["wzx"], w["wvp"], w["wdt"], w["cw"], w["cb"], w["dtb"], w["a_row"], w["dexp"], w["wpool"],
      w["pscale"])


def _sssd_kernel(dec_ref, c_ref, b_ref, xd_ref, st_ref, yoff_ref, stn_ref):
    blk = pl.program_id(0)
    nb = S_SSD_BLOCK
    gw = D_SSM // SSM_GROUPS
    hpg = SSM_HEADS // SSM_GROUPS
    row_seq = lax.broadcasted_iota(jnp.int32, (DEC_SEQ * nb, gw), 0) % nb
    cmat, bmat, xd = _steps(c_ref), _steps(b_ref), _steps(xd_ref)
    for g in range(SSM_GROUPS):
        gl = slice(g * gw, (g + 1) * gw)
        cg = cmat[:, g * SSM_STATE:(g + 1) * SSM_STATE].astype(bf16)
        bg = bmat[:, g * SSM_STATE:(g + 1) * SSM_STATE].astype(bf16)
        yo = jnp.zeros((DEC_SEQ * nb, gw), f32)
        for j in range(nb):
            mine = row_seq == j
            h0 = st_ref[j, gl, :]
            yo = jnp.where(mine, _dot_nt(cg, h0.astype(bf16)), yo)
            upd = _dot_tn(jnp.where(mine, xd[:, gl], 0.0).astype(bf16), bg)
            for hh in range(hpg):
                hr = slice(hh * SSM_HEAD_DIM, (hh + 1) * SSM_HEAD_DIM)
                d = dec_ref[(blk * nb + j) * SSM_HEADS + g * hpg + hh]
                stn_ref[j, g * gw + hh * SSM_HEAD_DIM:g * gw + (hh + 1) * SSM_HEAD_DIM, :] = h0[hr] * d + upd[hr]
        for l in range(DEC_SEQ):
            yoff_ref[l, :, gl] = yo[l * nb:(l + 1) * nb]


def _sssd(dec_flat, cmat, bmat, xd, state):
    ns = state.shape[0]
    nb = S_SSD_BLOCK
    tmaj = lambda width: pl.BlockSpec((DEC_SEQ, nb, width), lambda i: (0, i, 0))
    stblk = pl.BlockSpec((nb, D_SSM, SSM_STATE), lambda i: (i, 0, 0))
    return pl.pallas_call(
        _sssd_kernel,
        grid=(ns // nb,),
        in_specs=[pl.BlockSpec(memory_space=pltpu.SMEM), tmaj(D_BC), tmaj(D_BC), tmaj(D_SSM), stblk],
        out_specs=[tmaj(D_SSM), stblk],
        out_shape=[jax.ShapeDtypeStruct((DEC_SEQ, ns, D_SSM), f32), jax.ShapeDtypeStruct(state.shape, f32)],
        compiler_params=pltpu.CompilerParams(dimension_semantics=("arbitrary",), vmem_limit_bytes=VMEM_LIMIT_BYTES),
        name="sssd",
    )(dec_flat, cmat, bmat, xd, state)


def _smix_out_kernel(x_ref, ypart_ref, yoff_ref, eacs_ref, z_ref, pout_ref, snorm_ref, wout_ref, nmem_ref, wq_ref,
                     x1_ref, q_ref, cat_scr):
    y = ypart_ref[...] + yoff_ref[...] * eacs_ref[...]
    t = y * _silu(z_ref[...])
    gw = D_SSM // SSM_GROUPS
    for g in range(SSM_GROUPS):
        gl = slice(g * gw, (g + 1) * gw)
        tg = t[:, gl]
        ms = jnp.mean(tg * tg, axis=-1, keepdims=True)
        cat_scr[:, gl] = (tg * lax.rsqrt(ms + EPS) * snorm_ref[:, gl]).astype(bf16)
    cat_scr[:, D_SSM:] = pout_ref[...].astype(bf16)
    x1 = x_ref[...] + _dot(cat_scr[...], wout_ref[...])
    x1_ref[...] = x1
    h = _rmsnorm(x1, nmem_ref[...]).astype(bf16)
    q_ref[...] = _dot(h, wq_ref[...]) * (MEM_HEAD_DIM ** -0.5)


def _smix_out(x, ypart, yoff, eacs, z, pout, w):
    n = x.shape[0]
    rb = 128
    rows = lambda width: pl.BlockSpec((rb, width), lambda i: (i, 0))
    return pl.pallas_call(
        _smix_out_kernel,
        grid=(n // rb,),
        in_specs=[rows(D_MODEL), rows(D_SSM), rows(D_SSM), rows(D_SSM), rows(D_SSM), rows(D_POOL),
                  _const_spec((1, D_SSM)), _const_spec((D_SSM + D_POOL, D_MODEL)), _const_spec((1, D_MODEL)),
                  _const_spec((D_MODEL, D_MODEL))],
        out_specs=[rows(D_MODEL), rows(D_MODEL)],
        out_shape=[jax.ShapeDtypeStruct((n, D_MODEL), f32), jax.ShapeDtypeStruct((n, D_MODEL), f32)],
        scratch_shapes=[pltpu.VMEM((rb, D_SSM + D_POOL), bf16)],
        compiler_params=pltpu.CompilerParams(dimension_semantics=("arbitrary",), vmem_limit_bytes=VMEM_LIMIT_BYTES),
        name="smix_out",
    )(x, ypart, yoff, eacs, z, pout, w["snorm"], w["wout"], w["nmem"], w["wq"])


def _sattn_kernel(q_ref, k_ref, v_ref, o_ref):
    nb = S_ATT_BLOCK
    rg = DEC_SEQ * nb
    rows = MEM_HEADS * rg

    def row_ids(width):
        r = lax.broadcasted_iota(jnp.int32, (rows, width), 0)
        return r // rg, r % nb

    row_h, row_seq = row_ids(D_MODEL)
    col_h = lax.broadcasted_iota(jnp.int32, (rows, D_MODEL), 1) // MEM_HEAD_DIM
    q = _steps(q_ref)
    qh = jnp.where(row_h == col_h, jnp.concatenate([q] * MEM_HEADS, axis=0), 0.0)
    lhs_s = jnp.concatenate([jnp.where(row_seq == b, qh, 0.0).astype(bf16) for b in range(nb)], axis=1)
    kcat = jnp.concatenate([_kv_seq(k_ref, b) for b in range(nb)], axis=1)
    s = _dot_nt(lhs_s, kcat)
    p = jnp.exp(s - jnp.max(s, axis=-1, keepdims=True))
    p = p / jnp.sum(p, axis=-1, keepdims=True)
    _, row_seq_p = row_ids(N_MEM)
    lhs_p = jnp.concatenate([jnp.where(row_seq_p == b, p, 0.0).astype(bf16) for b in range(nb)], axis=1)
    vcat = jnp.concatenate([_kv_seq(v_ref, b) for b in range(nb)], axis=0)
    res = _dot(lhs_p, vcat)
    for hd in range(MEM_HEADS):
        hl = slice(hd * MEM_HEAD_DIM, (hd + 1) * MEM_HEAD_DIM)
        for l in range(DEC_SEQ):
            o_ref[l, :, hl] = res[hd * rg + l * nb:hd * rg + (l + 1) * nb, hl]


def _sattn(q, mem_k, mem_v):
    ns = mem_k.shape[0]
    nb = S_ATT_BLOCK
    qblk = pl.BlockSpec((DEC_SEQ, nb, D_MODEL), lambda i: (0, i, 0))
    kvblk = pl.BlockSpec((nb, N_MEM * KV_ROWS_PER_TOKEN, LANES), lambda i: (i, 0, 0))
    return pl.pallas_call(
        _sattn_kernel,
        grid=(ns // nb,),
        in_specs=[qblk, kvblk, kvblk],
        out_specs=qblk,
        out_shape=jax.ShapeDtypeStruct((DEC_SEQ, ns, D_MODEL), f32),
        compiler_params=pltpu.CompilerParams(dimension_semantics=("arbitrary",), vmem_limit_bytes=VMEM_LIMIT_BYTES),
        name="sattn",
    )(q, mem_k, mem_v)


def _sffn_kernel(x1_ref, ao_ref, sffn_ref, wo_ref, g_ref, wup_ref, cw_ref, cb_ref, wdn_ref, gfin_ref,
                 y_ref, st_ref, h_scr, act_scr):
    nb = S_SEQ_BLOCK
    x2 = _steps(x1_ref) + _dot(_steps(ao_ref).astype(bf16), wo_ref[...])
    h_scr[...] = _rmsnorm(x2, g_ref[...]).astype(bf16)
    u = _dot(h_scr[...], wup_ref[...])
    hist = FFN_CONV - 1

    def conv_block(blk):
        cols = slice(blk * FF_CHUNK, (blk + 1) * FF_CHUNK)
        tiles = range(blk * FF_CHUNK // LANES, (blk + 1) * FF_CHUNK // LANES)
        slots = [sffn_ref[:, i, cols] for i in range(hist)]
        slots += [u[l * nb:(l + 1) * nb, cols] for l in range(DEC_SEQ)]
        for i in range(hist):
            st_ref[:, i, cols] = slots[DEC_SEQ + i]
        outs = []
        for l in range(DEC_SEQ):
            acc = cb_ref[:, cols] + cw_ref[0:1, cols] * slots[l]
            for k in range(1, FFN_CONV):
                acc = acc + cw_ref[k:k + 1, cols] * slots[l + k]
            outs.append(acc)
        return jnp.concatenate(outs, axis=0)

    for j in range(N_FF_CHUNKS):
        act = _silu(conv_block(_gate_blk(j))) * conv_block(_val_blk(j))
        act_scr[:, j * FF_CHUNK:(j + 1) * FF_CHUNK] = act.astype(bf16)
    y = _rmsnorm(x2 + _dot(act_scr[...], wdn_ref[...]), gfin_ref[...])
    for l in range(DEC_SEQ):
        _rows_view_put(y_ref, l, DEC_SEQ, range(D_MODEL // LANES), nb, X_ROWS_PER_SEQ, y[l * nb:(l + 1) * nb])


def _sffn(x1, ao, sffn_rows, w):
    ns = x1.shape[1]
    nb = S_SEQ_BLOCK
    tok = pl.BlockSpec((DEC_SEQ, nb, D_MODEL), lambda i: (0, i, 0))
    yblk = pl.BlockSpec((nb * X_ROWS_PER_SEQ, LANES), lambda i: (i, 0))
    stblk = pl.BlockSpec((nb, FFN_CONV - 1, 2 * D_FF), lambda i: (i, 0, 0))
    return pl.pallas_call(
        _sffn_kernel,
        grid=(ns // nb,),
        in_specs=[tok, tok, stblk,
                  _const_spec((D_MODEL, D_MODEL)), _const_spec((1, D_MODEL)), _const_spec((D_MODEL, 2 * D_FF)),
                  _const_spec((FFN_CONV, 2 * D_FF)), _const_spec((1, 2 * D_FF)),
                  _const_spec((D_FF, D_MODEL)), _const_spec((1, D_MODEL))],
        out_specs=[yblk, stblk],
        out_shape=[jax.ShapeDtypeStruct((ns * X_ROWS_PER_SEQ, LANES), f32),
                   jax.ShapeDtypeStruct((ns, FFN_CONV - 1, 2 * D_FF), f32)],
        scratch_shapes=[pltpu.VMEM((DEC_SEQ * nb, D_MODEL), bf16), pltpu.VMEM((DEC_SEQ * nb, D_FF), bf16)],
        compiler_params=pltpu.CompilerParams(dimension_semantics=("arbitrary",), vmem_limit_bytes=VMEM_LIMIT_BYTES),
        name="sffn",
    )(x1, ao, sffn_rows, w["wo"], w["nffn"], w["wup"], w["fcw"], w["fcb"], w["wdn"], w["gfin"])


def _sample_path(x_sample, state_ssm, state_ssm_conv, state_pool, state_ffn_conv, cache_mem_k, cache_mem_v, w):
    ns = x_sample.shape[0]
    xtm, z, ypart, eacs, xd, cmat, bmat, dec, pout, conv_new, pool_new = _smix_in(
        _tiled_rows_view(x_sample), state_ssm_conv[0].transpose(1, 0, 2), state_pool[0].transpose(1, 0, 2), w)
    yoff, ssm_new = _sssd(dec[:, :SSM_HEADS].reshape(-1), cmat, bmat, xd, state_ssm[0].reshape(ns, D_SSM, SSM_STATE))
    flat = lambda a: a.reshape(DEC_SEQ * ns, a.shape[-1])
    x1, q = _smix_out(flat(xtm), flat(ypart), flat(yoff), flat(eacs), flat(z), flat(pout), w)
    ao = _sattn(q.reshape(DEC_SEQ, ns, D_MODEL), _kv_rows_view(cache_mem_k[0]), _kv_rows_view(cache_mem_v[0]))
    y_rows, ffn_new = _sffn(x1.reshape(DEC_SEQ, ns, D_MODEL), ao, state_ffn_conv[0], w)
    return (_from_tiled_rows(y_rows, ns, DEC_SEQ, D_MODEL),
            ssm_new.reshape(1, ns, SSM_HEADS, SSM_HEAD_DIM, SSM_STATE),
            conv_new.transpose(1, 0, 2)[None], pool_new.transpose(1, 0, 2)[None],
            ffn_new[None])


def kernel(x_prompt, x_sample, mem_prompt, state_ssm, state_ssm_conv, state_pool, state_ffn_conv, cache_mem_k, cache_mem_v, norm_mix, w_in, ssm_conv_w, ssm_conv_b, ssm_dt_bias, ssm_a_log, ssm_d, ssm_norm, w_pool, pool_scale, w_out, norm_mem, norm_memkv, w_mq, w_mk, w_mv, w_mo, norm_ffn, w_up, ffn_conv_w, ffn_conv_b, w_down, final_norm):
    params = dict(norm_mix=norm_mix, w_in=w_in, ssm_conv_w=ssm_conv_w, ssm_conv_b=ssm_conv_b, ssm_dt_bias=ssm_dt_bias,
                  ssm_a_log=ssm_a_log, ssm_d=ssm_d, ssm_norm=ssm_norm, w_pool=w_pool, pool_scale=pool_scale,
                  w_out=w_out, norm_mem=norm_mem, norm_memkv=norm_memkv, w_mq=w_mq, w_mk=w_mk, w_mv=w_mv, w_mo=w_mo,
                  norm_ffn=norm_ffn, w_up=w_up, ffn_conv_w=ffn_conv_w, ffn_conv_b=ffn_conv_b, w_down=w_down,
                  final_norm=final_norm)
    w = _prep_weights(params)
    yp, ssm_p, conv_p, pool_p, ffn_p, mk_p, mv_p = _prompt_path(x_prompt, mem_prompt, w, PROMPT_TILE)
    ys, ssm_s, conv_s, pool_s, ffn_s = _sample_path(x_sample, state_ssm, state_ssm_conv, state_pool, state_ffn_conv,
                                                    cache_mem_k, cache_mem_v, w)
    return yp, ys, ssm_p, ssm_s, conv_p, conv_s, pool_p, pool_s, ffn_p, ffn_s, mk_p, mv_p
```

```python
import functools

import jax
import jax.numpy as jnp
from jax import lax
from jax.experimental import pallas as pl
from jax.experimental.pallas import tpu as pltpu

f32 = jnp.float32
bf16 = jnp.bfloat16

D_MODEL = 1024
SSM_HEADS = 16
SSM_HEAD_DIM = 64
SSM_STATE = 128
SSM_GROUPS = 2
SSM_CHUNK = 128
D_SSM = 1024
D_BC = SSM_GROUPS * SSM_STATE
D_XBC = D_SSM + 2 * D_BC
SSM_CONV = 4
D_POOL = 1024
POOL_WINDOWS = (2, 4, 8, 16)
POOL_GROUP_DIM = 256
POOL_HIST = 15
N_MEM = 256
MEM_HEADS = 4
MEM_HEAD_DIM = 256
D_FF = 2816
FFN_CONV = 3
EPS = 1e-6
PAST_LEN = 16384

LANES = 128
SUBLANES = 8
MXU_DIM = 256
VMEM_LIMIT_BYTES = 56 * 1024 * 1024

CONV_HIST_ROWS = SUBLANES
POOL_HIST_ROWS = 2 * SUBLANES
FF_CHUNK = MXU_DIM
N_FF_CHUNKS = D_FF // FF_CHUNK


def _silu(v):
    half = 0.5 * v
    return half + half * jnp.tanh(half)


def _silu_gate(g, v):
    return g * v * (1.0 / (1.0 + jnp.exp(-g)))


def _softplus(v):
    return jnp.maximum(v, 0.0) + jnp.log1p(jnp.exp(-jnp.abs(v)))


def _rmsnorm(x, g):
    ms = jnp.mean(x * x, axis=-1, keepdims=True)
    return x * lax.rsqrt(ms + EPS) * g


def _dot(a, b):
    return jnp.dot(a, b, preferred_element_type=f32)


def _dot_nt(a, b):
    return lax.dot_general(a, b, (((1,), (1,)), ((), ())), preferred_element_type=f32)


def _dot_tn(a, b):
    return lax.dot_general(a, b, (((0,), (0,)), ((), ())), preferred_element_type=f32)


def _split3(v):
    p1 = v.astype(bf16)
    r1 = v - p1.astype(f32)
    p2 = r1.astype(bf16)
    r2 = r1 - p2.astype(f32)
    return p1, p2, r2.astype(bf16)


def _const_spec(shape):
    return pl.BlockSpec(shape, lambda *_: (0,) * len(shape), pipeline_mode=pl.Buffered(1))


KV_LANE_TILES = MEM_HEAD_DIM // LANES
KV_ROWS_PER_TOKEN = KV_LANE_TILES * MEM_HEADS


def _kv_rows_view(kv):
    ns = kv.shape[0]
    kv = kv.reshape(ns, N_MEM, MEM_HEADS, KV_LANE_TILES, LANES).transpose(0, 1, 3, 2, 4)
    return kv.reshape(ns, N_MEM * KV_ROWS_PER_TOKEN, LANES)


def _kv_from_rows(rows):
    ns = rows.shape[0]
    kv = rows.reshape(ns, N_MEM, KV_LANE_TILES, MEM_HEADS, LANES).transpose(0, 1, 3, 2, 4)
    return kv.reshape(ns, N_MEM, MEM_HEADS, MEM_HEAD_DIM)


def _kv_seq(ref, jj):
    tiles = [ref[jj, pl.ds(dt * MEM_HEADS + hd, N_MEM, stride=KV_ROWS_PER_TOKEN), :]
             for hd in range(MEM_HEADS) for dt in range(KV_LANE_TILES)]
    return jnp.concatenate(tiles, axis=1).astype(bf16)


def _memkv_kernel(mem_ref, g_ref, wk_ref, wv_ref, k_ref, v_ref):
    h = _rmsnorm(mem_ref[0], g_ref[...]).astype(bf16)
    for out_ref, w_ref in ((k_ref, wk_ref), (v_ref, wv_ref)):
        kv = _dot(h, w_ref[...])
        for hd in range(MEM_HEADS):
            for dt in range(KV_LANE_TILES):
                col = hd * MEM_HEAD_DIM + dt * LANES
                out_ref[0, pl.ds(dt * MEM_HEADS + hd, N_MEM, stride=KV_ROWS_PER_TOKEN), :] = kv[:, col:col + LANES]


def _memkv(mem, g, wk, wv):
    b = mem.shape[0]
    blk = pl.BlockSpec((1, N_MEM, D_MODEL), lambda i: (i, 0, 0))
    oblk = pl.BlockSpec((1, N_MEM * KV_ROWS_PER_TOKEN, LANES), lambda i: (i, 0, 0))
    return pl.pallas_call(
        _memkv_kernel,
        grid=(b,),
        in_specs=[blk, _const_spec((1, D_MODEL)), _const_spec((D_MODEL, D_MODEL)), _const_spec((D_MODEL, D_MODEL))],
        out_specs=[oblk, oblk],
        out_shape=[jax.ShapeDtypeStruct((b, N_MEM * KV_ROWS_PER_TOKEN, LANES), f32)] * 2,
        compiler_params=pltpu.CompilerParams(dimension_semantics=("arbitrary",), vmem_limit_bytes=VMEM_LIMIT_BYTES),
        name="memkv",
    )(mem, g, wk, wv)


def _ssd_chunk(r0, dt_scr, xs_scr, b_scr, c_scr, y_scr, xd_scr, hst_scr, a_ref, dexp_ref, between=()):
    between = list(between) + [None] * 3
    q = SSM_CHUNK
    rows = pl.ds(r0, q)
    row_i = lax.broadcasted_iota(jnp.int32, (q, q), 0)
    col_i = lax.broadcasted_iota(jnp.int32, (q, q), 1)
    causal = col_i <= row_i
    lo = col_i < SSM_HEAD_DIM
    tril = jnp.where(causal, 1.0, 0.0).astype(bf16)

    dt = dt_scr[rows, :]
    da = dt * a_ref[...]
    p1, p2, p3 = _split3(da)
    acs = _dot(tril, p1) + _dot(tril, p2) + _dot(tril, p3)
    acs_t = acs.T
    if between[0] is not None:
        between[0]()

    for g in range(SSM_GROUPS):
        bg = b_scr[rows, g * SSM_STATE:(g + 1) * SSM_STATE]
        cg = c_scr[rows, g * SSM_STATE:(g + 1) * SSM_STATE]
        bg_b = bg.astype(bf16)
        cb = jnp.where(causal, _dot_nt(cg.astype(bf16), bg_b), 0.0)
        cdec_rows = []
        pairs_per_group = SSM_HEADS // SSM_GROUPS // 2
        for pp in range(pairs_per_group):
            pr = g * pairs_per_group + pp
            lanes = slice(pr * LANES, (pr + 1) * LANES)
            lhs, dtb, dend, cdec = [], [], [], []
            for hh in (2 * pr, 2 * pr + 1):
                colb = jnp.broadcast_to(acs[:, hh:hh + 1], (q, q))
                seg = jnp.where(causal, colb - acs_t[hh:hh + 1, :], 0.0)
                lhs.append((jnp.exp(seg) * cb).astype(bf16))
                lhs.append((cg * jnp.exp(colb)).astype(bf16))
                last = colb[q - 1:q, :]
                dend.append(jnp.exp(last - colb))
                cdec.append(jnp.exp(last))
                dtb.append(jnp.broadcast_to(dt[:, hh:hh + 1], (q, q)))
            xs_pair = xs_scr[rows, lanes]
            xdt = xs_pair * jnp.where(lo, dtb[0], dtb[1])
            xd_scr[:, lanes] = (xdt * jnp.where(lo, dend[0], dend[1])).astype(bf16)
            hst_pair = hst_scr[:, lanes]
            rhs = jnp.concatenate([
                jnp.where(lo, xdt, 0.0).astype(bf16), jnp.where(lo, hst_pair, 0.0).astype(bf16),
                jnp.where(lo, 0.0, xdt).astype(bf16), jnp.where(lo, 0.0, hst_pair).astype(bf16)], axis=0)
            y_pair = _dot(jnp.concatenate(lhs, axis=1), rhs)
            y_scr[rows, lanes] = y_pair + xs_pair * dexp_ref[:, lanes]
            cdec_rows.append(jnp.where(lo[:1], cdec[0], cdec[1]))
        gl = slice(g * (D_SSM // SSM_GROUPS), (g + 1) * (D_SSM // SSM_GROUPS))
        upd = _dot_tn(bg_b, xd_scr[:, gl])
        hst_scr[:, gl] = hst_scr[:, gl] * jnp.concatenate(cdec_rows, axis=1) + upd
        if between[1 + g] is not None:
            between[1 + g]()


def _mixer_kernel(x_ref, nmix_ref, wzx_ref, wvp_ref, wdt_ref, cw_ref, cb_ref, dtb_ref, a_ref, dexp_ref, snorm_ref, wpool_ref,
                  pscale_ref, wout_ref,
                  x1_ref, ssm_ref, conv_ref, pool_ref,
                  xd_scr, hst_scr, *sub_scr, tile, sub):
    t = pl.program_id(1)
    last_t = pl.num_programs(1) - 1
    ch = CONV_HIST_ROWS
    ph = POOL_HIST_ROWS
    n_sub = tile // sub
    per = len(sub_scr) // n_sub
    bufs = [sub_scr[i * per:(i + 1) * per] for i in range(n_sub)]
    xbc0, vp0 = bufs[0][2], bufs[0][3]

    @pl.when(t == 0)
    def _():
        xbc0[:, 0:ch, :] = jnp.zeros((D_XBC // LANES, ch, LANES), f32)
        vp0[:, 0:ph, :] = jnp.zeros((D_POOL // LANES, ph, LANES), f32)
        hst_scr[...] = jnp.zeros_like(hst_scr)

    gw = D_SSM // SSM_GROUPS

    def carry_history(src, dst):
        for j in range(D_XBC // LANES):
            bufs[dst][2][j, 0:ch, :] = bufs[src][2][j, sub:sub + ch, :]
        for j in range(D_POOL // LANES):
            bufs[dst][3][j, 0:ph, :] = bufs[src][3][j, sub:sub + ph, :]

    def in_xbc(s):
        h_scr, _, xbc_scr = bufs[s][:3]
        h_scr[...] = _rmsnorm(x_ref[0, s * sub:(s + 1) * sub, :], nmix_ref[...]).astype(bf16)
        xbc = _dot(h_scr[...], wzx_ref[:, D_SSM:])
        for j in range(D_XBC // LANES):
            xbc_scr[j, ch:ch + sub, :] = xbc[:, j * LANES:(j + 1) * LANES]

    def in_dtvp(s):
        h_scr, _, _, vp_scr, dt_scr = bufs[s][:5]
        dt_scr[...] = _dot(h_scr[...], wdt_ref[...])
        vp = _dot(h_scr[...], wvp_ref[...])
        for j in range(D_POOL // LANES):
            vp_scr[j, ph:ph + sub, :] = vp[:, j * LANES:(j + 1) * LANES]

    def in_z(s):
        h_scr, z_scr = bufs[s][:2]
        z_scr[...] = _dot(h_scr[...], wzx_ref[:, :D_SSM])
        if s > 0:
            carry_history(s - 1, s)

    def stage_mid(s, between):
        _, z_scr, xbc_scr, vp_scr, dt_scr, xs_scr, b_scr, c_scr, y_scr, pooled_scr, cat_scr = bufs[s]
        for j in range(D_XBC // LANES):
            cl = slice(j * LANES, (j + 1) * LANES)
            acc = cb_ref[:, cl] + cw_ref[0:1, cl] * xbc_scr[j, pl.ds(ch - 3, sub), :]
            for k in range(1, SSM_CONV):
                acc = acc + cw_ref[k:k + 1, cl] * xbc_scr[j, pl.ds(ch - 3 + k, sub), :]
            act = _silu(acc)
            if j < D_SSM // LANES:
                xs_scr[:, cl] = act
            elif j < (D_SSM + D_BC) // LANES:
                b_scr[:, j * LANES - D_SSM:(j + 1) * LANES - D_SSM] = act
            else:
                c_scr[:, j * LANES - D_SSM - D_BC:(j + 1) * LANES - D_SSM - D_BC] = act

        pos1 = lax.broadcasted_iota(jnp.int32, (sub, LANES), 0) + (t * tile + s * sub + 1)
        for j in range(D_POOL // LANES):
            cl = slice(j * LANES, (j + 1) * LANES)
            w = POOL_WINDOWS[j * LANES // POOL_GROUP_DIM]
            cur = vp_scr[j, ph:ph + sub, :]
            acc = cur
            for k in range(1, w):
                acc = acc + vp_scr[j, pl.ds(ph - k, sub), :]
            cnt = jnp.minimum(pos1, w).astype(f32)
            pooled_scr[:, cl] = (acc / cnt - cur).astype(bf16)
        for gi in range(len(POOL_WINDOWS)):
            gl = slice(gi * POOL_GROUP_DIM, (gi + 1) * POOL_GROUP_DIM)
            pg = _dot(pooled_scr[:, gl], wpool_ref[gi]) * pscale_ref[:, gl]
            cat_scr[:, D_SSM + gi * POOL_GROUP_DIM:D_SSM + (gi + 1) * POOL_GROUP_DIM] = pg.astype(bf16)

        dt_scr[...] = _softplus(dt_scr[...] + dtb_ref[...])
        n_chunks = sub // SSM_CHUNK
        slots = [None] * (3 * n_chunks)
        for i, piece in enumerate(between):
            slots[i * len(slots) // len(between)] = piece
        for c in range(n_chunks):
            _ssd_chunk(c * SSM_CHUNK, dt_scr, xs_scr, b_scr, c_scr, y_scr, xd_scr, hst_scr, a_ref, dexp_ref,
                       slots[3 * c:3 * c + 3])

        for g in range(SSM_GROUPS):
            gl = slice(g * gw, (g + 1) * gw)
            tg = y_scr[:, gl] * _silu(z_scr[:, gl])
            ms = jnp.mean(tg * tg, axis=-1, keepdims=True)
            cat_scr[:, gl] = (tg * lax.rsqrt(ms + EPS) * snorm_ref[:, gl]).astype(bf16)

    def stage_out(s):
        rows = slice(s * sub, (s + 1) * sub)
        x1_ref[0, rows, :] = x_ref[0, rows, :] + _dot(bufs[s][-1][...], wout_ref[...])

    in_xbc(0)
    in_dtvp(0)
    in_z(0)
    for s in range(n_sub):
        nxt = s + 1 < n_sub
        if nxt:
            in_xbc(s + 1)
        pieces = [functools.partial(in_dtvp, s + 1), functools.partial(in_z, s + 1)] if nxt else []
        if s > 0:
            pieces.append(functools.partial(stage_out, s - 1))
        stage_mid(s, pieces)
    stage_out(n_sub - 1)
    carry_history(n_sub - 1, 0)

    @pl.when(t == last_t)
    def _():
        for pr in range(D_SSM // LANES):
            ssm_ref[0, pr * LANES:(pr + 1) * LANES, :] = hst_scr[:, pr * LANES:(pr + 1) * LANES].T
        for j in range(D_XBC // LANES):
            conv_ref[0, :, j * LANES:(j + 1) * LANES] = xbc0[j, pl.ds(ch - (SSM_CONV - 1), SSM_CONV - 1), :]
        for j in range(D_POOL // LANES):
            pool_ref[0, :, j * LANES:(j + 1) * LANES] = vp0[j, pl.ds(ph - POOL_HIST, POOL_HIST), :]


def _mixer_prompt(x, nmix, wzx, wvp, wdt, cw, cb, dtb, a_row, dexp, snorm, wpool, pscale, wout, tile):
    b, seq, _ = x.shape
    nt = seq // tile
    sub = min(tile, MIXER_SUB_TILE)
    xblk = pl.BlockSpec((1, tile, D_MODEL), lambda i, j: (i, j, 0))
    sub_scratch = [
        pltpu.VMEM((sub, D_MODEL), bf16),
        pltpu.VMEM((sub, D_SSM), f32),
        pltpu.VMEM((D_XBC // LANES, CONV_HIST_ROWS + sub, LANES), f32),
        pltpu.VMEM((D_POOL // LANES, POOL_HIST_ROWS + sub, LANES), f32),
        pltpu.VMEM((sub, LANES), f32),
        pltpu.VMEM((sub, D_SSM), f32),
        pltpu.VMEM((sub, D_BC), f32),
        pltpu.VMEM((sub, D_BC), f32),
        pltpu.VMEM((sub, D_SSM), f32),
        pltpu.VMEM((sub, D_POOL), bf16),
        pltpu.VMEM((sub, D_SSM + D_POOL), bf16),
    ]
    scratch = [pltpu.VMEM((SSM_CHUNK, D_SSM), bf16),
               pltpu.VMEM((SSM_STATE, D_SSM), f32)]
    scratch += sub_scratch * (tile // sub)
    return pl.pallas_call(
        functools.partial(_mixer_kernel, tile=tile, sub=sub),
        grid=(b, nt),
        in_specs=[xblk, _const_spec((1, D_MODEL)), _const_spec((D_MODEL, D_SSM + D_XBC)),
                  _const_spec((D_MODEL, D_POOL)), _const_spec((D_MODEL, LANES)), _const_spec((SSM_CONV, D_XBC)),
                  _const_spec((1, D_XBC)), _const_spec((1, LANES)), _const_spec((1, LANES)), _const_spec((1, D_SSM)),
                  _const_spec((1, D_SSM)), _const_spec((len(POOL_WINDOWS), POOL_GROUP_DIM, POOL_GROUP_DIM)),
                  _const_spec((1, D_POOL)), _const_spec((D_SSM + D_POOL, D_MODEL))],
        out_specs=[xblk,
                   pl.BlockSpec((1, D_SSM, SSM_STATE), lambda i, j: (i, 0, 0)),
                   pl.BlockSpec((1, SSM_CONV - 1, D_XBC), lambda i, j: (i, 0, 0)),
                   pl.BlockSpec((1, POOL_HIST, D_POOL), lambda i, j: (i, 0, 0))],
        out_shape=[jax.ShapeDtypeStruct((b, seq, D_MODEL), f32),
                   jax.ShapeDtypeStruct((b, D_SSM, SSM_STATE), f32),
                   jax.ShapeDtypeStruct((b, SSM_CONV - 1, D_XBC), f32),
                   jax.ShapeDtypeStruct((b, POOL_HIST, D_POOL), f32)],
        scratch_shapes=scratch,
        compiler_params=pltpu.CompilerParams(dimension_semantics=("arbitrary", "arbitrary"),
                                             vmem_limit_bytes=VMEM_LIMIT_BYTES),
        name="mixer_prompt",
    )(x, nmix, wzx, wvp, wdt, cw, cb, dtb, a_row, dexp, snorm, wpool, pscale, wout)


def _attn_kernel(x_ref, g_ref, wq_ref, k_ref, v_ref, wo_ref, o_ref, q_scr, ao_scr):
    tile = x_ref.shape[1]
    sub = min(tile, ATTN_SUB_TILE)
    n_sub = tile // sub
    heads = [slice(hd * MEM_HEAD_DIM, (hd + 1) * MEM_HEAD_DIM) for hd in range(MEM_HEADS)]
    k = _kv_seq(k_ref, 0)
    v = _kv_seq(v_ref, 0)

    def q_proj(s):
        rows = slice(s * sub, (s + 1) * sub)
        h = _rmsnorm(x_ref[0, rows, :], g_ref[...]).astype(bf16)
        q_scr[rows, :] = (_dot(h, wq_ref[...]) * (MEM_HEAD_DIM ** -0.5)).astype(bf16)

    def scores(s):
        rows = slice(s * sub, (s + 1) * sub)
        return [_dot_nt(q_scr[rows, hl], k[:, hl]) for hl in heads]

    def values(s, sc):
        rows = slice(s * sub, (s + 1) * sub)
        for hl, s_h in zip(heads, sc):
            p = jnp.exp(s_h - jnp.max(s_h, axis=-1, keepdims=True))
            p = (p / jnp.sum(p, axis=-1, keepdims=True)).astype(bf16)
            ao_scr[rows, hl] = _dot(p, v[:, hl]).astype(bf16)

    def o_proj(s):
        rows = slice(s * sub, (s + 1) * sub)
        o_ref[0, rows, :] = x_ref[0, rows, :] + _dot(ao_scr[rows, :], wo_ref[...])

    q_proj(0)
    for s in range(n_sub):
        sc = scores(s)
        if s + 1 < n_sub:
            q_proj(s + 1)
        if s > 0:
            o_proj(s - 1)
        values(s, sc)
    o_proj(n_sub - 1)


def _attn_prompt(x, g, wq, mem_k, mem_v, wo, tile):
    b, seq, _ = x.shape
    xblk = pl.BlockSpec((1, tile, D_MODEL), lambda i, j: (i, j, 0))
    kvblk = pl.BlockSpec((1, N_MEM * KV_ROWS_PER_TOKEN, LANES), lambda i, j: (i, 0, 0))
    return pl.pallas_call(
        _attn_kernel,
        grid=(b, seq // tile),
        in_specs=[xblk, _const_spec((1, D_MODEL)), _const_spec((D_MODEL, D_MODEL)), kvblk, kvblk,
                  _const_spec((D_MODEL, D_MODEL))],
        out_specs=xblk,
        out_shape=jax.ShapeDtypeStruct((b, seq, D_MODEL), f32),
        scratch_shapes=[pltpu.VMEM((tile, D_MODEL), bf16), pltpu.VMEM((tile, D_MODEL), bf16)],
        compiler_params=pltpu.CompilerParams(dimension_semantics=("arbitrary", "arbitrary"),
                                             vmem_limit_bytes=VMEM_LIMIT_BYTES),
        name="attn_prompt",
    )(x, g, wq, mem_k, mem_v, wo)


def _gate_blk(j):
    return j


def _val_blk(j):
    return N_FF_CHUNKS + j


def _ffn_kernel(x_ref, g_ref, wup_ref, cw_ref, cb_ref, wdn_ref, gfin_ref, y_ref, st_ref, h_scr, u_scr, act_scr, *, tile):
    t = pl.program_id(1)
    last_t = pl.num_programs(1) - 1
    ch = CONV_HIST_ROWS
    tpc = FF_CHUNK // LANES
    ntile = 2 * D_FF // LANES

    @pl.when(t == 0)
    def _():
        u_scr[:, 0:ch, :] = jnp.zeros((ntile, ch, LANES), f32)

    sub = min(tile, FFN_SUB_TILE)
    n_sub = tile // sub

    def up_proj(s):
        rows = slice(s * sub, (s + 1) * sub)
        h_scr[rows, :] = _rmsnorm(x_ref[0, rows, :], g_ref[...]).astype(bf16)
        u = _dot(h_scr[rows, :], wup_ref[...])
        for ti in range(ntile):
            u_scr[ti, ch + s * sub:ch + (s + 1) * sub, :] = u[:, ti * LANES:(ti + 1) * LANES]

    def conv(s, ti):
        cl = slice(ti * LANES, (ti + 1) * LANES)
        r0 = ch + s * sub
        acc = cb_ref[:, cl] + cw_ref[0:1, cl] * u_scr[ti, pl.ds(r0 - 2, sub), :]
        acc = acc + cw_ref[1:2, cl] * u_scr[ti, pl.ds(r0 - 1, sub), :]
        return acc + cw_ref[2:3, cl] * u_scr[ti, r0:r0 + sub, :]

    def gate(s):
        rows = slice(s * sub, (s + 1) * sub)
        for j in range(N_FF_CHUNKS):
            for i in range(tpc):
                g_t = conv(s, _gate_blk(j) * tpc + i)
                v_t = conv(s, _val_blk(j) * tpc + i)
                act_scr[rows, (j * tpc + i) * LANES:(j * tpc + i + 1) * LANES] = _silu_gate(g_t, v_t).astype(bf16)

    def down_proj(s):
        rows = slice(s * sub, (s + 1) * sub)
        y_ref[0, rows, :] = _rmsnorm(x_ref[0, rows, :] + _dot(act_scr[rows, :], wdn_ref[...]), gfin_ref[...])

    up_proj(0)
    for s in range(n_sub):
        if s + 1 < n_sub:
            up_proj(s + 1)
        gate(s)
        down_proj(s)
    for ti in range(ntile):
        u_scr[ti, 0:ch, :] = u_scr[ti, tile:tile + ch, :]

    @pl.when(t == last_t)
    def _():
        for ti in range(ntile):
            st_ref[0, :, ti * LANES:(ti + 1) * LANES] = u_scr[ti, pl.ds(ch - (FFN_CONV - 1), FFN_CONV - 1), :]


def _ffn_prompt(x, g, wup, cw, cb, wdn, gfin, tile):
    b, seq, _ = x.shape
    xblk = pl.BlockSpec((1, tile, D_MODEL), lambda i, j: (i, j, 0))
    return pl.pallas_call(
        functools.partial(_ffn_kernel, tile=tile),
        grid=(b, seq // tile),
        in_specs=[xblk, _const_spec((1, D_MODEL)), _const_spec((D_MODEL, 2 * D_FF)),
                  _const_spec((FFN_CONV, 2 * D_FF)), _const_spec((1, 2 * D_FF)),
                  _const_spec((D_FF, D_MODEL)), _const_spec((1, D_MODEL))],
        out_specs=[xblk, pl.BlockSpec((1, FFN_CONV - 1, 2 * D_FF), lambda i, j: (i, 0, 0))],
        out_shape=[jax.ShapeDtypeStruct((b, seq, D_MODEL), f32),
                   jax.ShapeDtypeStruct((b, FFN_CONV - 1, 2 * D_FF), f32)],
        scratch_shapes=[pltpu.VMEM((tile, D_MODEL), bf16),
                        pltpu.VMEM((2 * D_FF // LANES, CONV_HIST_ROWS + tile, LANES), f32),
                        pltpu.VMEM((tile, D_FF), bf16)],
        compiler_params=pltpu.CompilerParams(dimension_semantics=("arbitrary", "arbitrary"),
                                             vmem_limit_bytes=VMEM_LIMIT_BYTES),
        name="ffn_prompt",
    )(x, g, wup, cw, cb, wdn, gfin)


def _prep_weights(p):
    i = 0
    w_in = p["w_in"][i]
    n_zx = D_SSM + D_XBC
    wzx = w_in[:, :n_zx].astype(bf16)
    wdt = jnp.pad(w_in[:, n_zx:n_zx + SSM_HEADS], ((0, 0), (0, LANES - SSM_HEADS))).astype(bf16)
    wvp = w_in[:, n_zx + SSM_HEADS:].astype(bf16)
    pad_h = (0, LANES - SSM_HEADS)
    return dict(
        nmix=p["norm_mix"][i][None], wzx=wzx, wvp=wvp, wdt=wdt, cw=p["ssm_conv_w"][i], cb=p["ssm_conv_b"][i][None],
        dtb=jnp.pad(p["ssm_dt_bias"][i], pad_h)[None],
        a_row=jnp.pad(-jnp.exp(p["ssm_a_log"][i].astype(f32)), pad_h)[None],
        dexp=jnp.repeat(p["ssm_d"][i], SSM_HEAD_DIM)[None], snorm=p["ssm_norm"][i][None],
        wpool=p["w_pool"][i].astype(bf16), pscale=p["pool_scale"][i][None], wout=p["w_out"][i].astype(bf16),
        nmem=p["norm_mem"][i][None], nmemkv=p["norm_memkv"][i][None],
        wq=p["w_mq"][i].astype(bf16), wk=p["w_mk"][i].astype(bf16), wv=p["w_mv"][i].astype(bf16),
        wo=p["w_mo"][i].astype(bf16),
        nffn=p["norm_ffn"][i][None],
        wup=p["w_up"][i].astype(bf16), fcw=p["ffn_conv_w"][i], fcb=p["ffn_conv_b"][i][None],
        wdn=p["w_down"][i].astype(bf16),
        gfin=p["final_norm"][None],
    )


def _prompt_path(x_prompt, mem_prompt, w, tile):
    b = x_prompt.shape[0]
    mem_k, mem_v = _memkv(mem_prompt, w["nmemkv"], w["wk"], w["wv"])
    x1, ssm, conv, pool = _mixer_prompt(x_prompt, w["nmix"], w["wzx"], w["wvp"], w["wdt"], w["cw"], w["cb"], w["dtb"], w["a_row"],
                                        w["dexp"], w["snorm"], w["wpool"], w["pscale"], w["wout"],
                                        min(MIXER_TILE, x_prompt.shape[1]))
    x2 = _attn_prompt(x1, w["nmem"], w["wq"], mem_k, mem_v, w["wo"], min(ATTN_TILE, x_prompt.shape[1]))
    y, ffn = _ffn_prompt(x2, w["nffn"], w["wup"], w["fcw"], w["fcb"], w["wdn"], w["gfin"], tile)
    return (y, ssm.reshape(1, b, SSM_HEADS, SSM_HEAD_DIM, SSM_STATE), conv[None], pool[None], ffn[None],
            _kv_from_rows(mem_k)[None], _kv_from_rows(mem_v)[None])


PROMPT_TILE = 512
ATTN_TILE = 1024
MIXER_TILE = 1024
MIXER_SUB_TILE = 256
ATTN_SUB_TILE = 256
FFN_SUB_TILE = 512


DEC_SEQ = 4
S_SEQ_BLOCK = 64
S_SSD_BLOCK = 16
S_ATT_BLOCK = 8
X_ROWS_PER_SEQ = DEC_SEQ * D_MODEL // LANES
FFN_ROWS_PER_SEQ = (FFN_CONV - 1) * 2 * D_FF // LANES


def _expand_heads(v, lo):
    r = v.shape[0]
    tiles = []
    for pr in range(SSM_HEADS // 2):
        a = jnp.broadcast_to(v[:, 2 * pr:2 * pr + 1], (r, LANES))
        b = jnp.broadcast_to(v[:, 2 * pr + 1:2 * pr + 2], (r, LANES))
        tiles.append(jnp.where(lo, a, b))
    return jnp.concatenate(tiles, axis=1)


def _tiled_rows_view(a):
    ns, r, c = a.shape
    return a.reshape(ns, r, c // LANES, LANES).transpose(0, 2, 1, 3).reshape(ns * (c // LANES) * r, LANES)


def _from_tiled_rows(rows, ns, r, c):
    return rows.reshape(ns, c // LANES, r, LANES).transpose(0, 2, 1, 3).reshape(ns, r, c)


def _rows_view_get(ref, i, r, tiles, nb, per_seq):
    return jnp.concatenate([ref[pl.ds(dt * r + i, nb, stride=per_seq), :] for dt in tiles], axis=1)


def _rows_view_put(ref, i, r, tiles, nb, per_seq, val):
    for n, dt in enumerate(tiles):
        ref[pl.ds(dt * r + i, nb, stride=per_seq), :] = val[:, n * LANES:(n + 1) * LANES]


def _steps(ref):
    return ref[...].reshape(DEC_SEQ * ref.shape[1], ref.shape[2])


def _smix_in_kernel(x_ref, sconv_ref, spool_ref, nmix_ref, wzx_ref, wvp_ref, wdt_ref, cw_ref, cb_ref, dtb_ref, a_ref, dexp_ref,
                    wpool_ref, pscale_ref,
                    xtm_ref, z_ref, ypart_ref, eacs_ref, xd_ref, c_ref, b_ref, dec_ref, pout_ref, conv_ref, pool_ref,
                    h_scr, pooled_scr):
    nb = S_SEQ_BLOCK
    lo = lax.broadcasted_iota(jnp.int32, (nb, LANES), 1) < SSM_HEAD_DIM
    x_tiles = range(D_MODEL // LANES)
    x_steps = [_rows_view_get(x_ref, l, DEC_SEQ, x_tiles, nb, X_ROWS_PER_SEQ) for l in range(DEC_SEQ)]
    for l in range(DEC_SEQ):
        xtm_ref[l] = x_steps[l]
    h_scr[...] = _rmsnorm(jnp.concatenate(x_steps, axis=0), nmix_ref[...]).astype(bf16)
    z = _dot(h_scr[...], wzx_ref[:, :D_SSM])
    for l in range(DEC_SEQ):
        z_ref[l] = z[l * nb:(l + 1) * nb]
    xbc = _dot(h_scr[...], wzx_ref[:, D_SSM:])
    vp = _dot(h_scr[...], wvp_ref[...])
    dtr = _dot(h_scr[...], wdt_ref[...])

    def conv_slot(i):
        if i < SSM_CONV - 1:
            return sconv_ref[i]
        return xbc[(i - SSM_CONV + 1) * nb:(i - SSM_CONV + 2) * nb]

    def pool_slot(i, cl):
        if i < POOL_HIST:
            return spool_ref[i, :, cl]
        return vp[(i - POOL_HIST) * nb:(i - POOL_HIST + 1) * nb, cl]

    xs, bm, cm, dt, acs = [], [], [], [], []
    for l in range(DEC_SEQ):
        acc = cb_ref[...] + cw_ref[0:1, :] * conv_slot(l)
        for k in range(1, SSM_CONV):
            acc = acc + cw_ref[k:k + 1, :] * conv_slot(l + k)
        act = _silu(acc)
        xs.append(act[:, :D_SSM])
        bm.append(act[:, D_SSM:D_SSM + D_BC])
        cm.append(act[:, D_SSM + D_BC:])
        b_ref[l] = bm[l]
        c_ref[l] = cm[l]
        dt.append(_softplus(dtr[l * nb:(l + 1) * nb] + dtb_ref[...]))
        da = dt[l] * a_ref[...]
        acs.append(da if l == 0 else acs[l - 1] + da)
        for gi, w in enumerate(POOL_WINDOWS):
            gl = slice(gi * POOL_GROUP_DIM, (gi + 1) * POOL_GROUP_DIM)
            s = pool_slot(POOL_HIST + l, gl)
            for k in range(1, w):
                s = s + pool_slot(POOL_HIST + l - k, gl)
            cnt = float(min(PAST_LEN + l + 1, w))
            pooled_scr[l * nb:(l + 1) * nb, gl] = (s / cnt - pool_slot(POOL_HIST + l, gl)).astype(bf16)
    for i in range(SSM_CONV - 1):
        conv_ref[i] = conv_slot(DEC_SEQ + i)
    for i in range(POOL_HIST):
        pool_ref[i] = pool_slot(DEC_SEQ + i, slice(0, D_POOL))
    pout = jnp.concatenate(
        [_dot(pooled_scr[:, gi * POOL_GROUP_DIM:(gi + 1) * POOL_GROUP_DIM], wpool_ref[gi])
         for gi in range(len(POOL_WINDOWS))], axis=1) * pscale_ref[...]
    for l in range(DEC_SEQ):
        pout_ref[l] = pout[l * nb:(l + 1) * nb]

    xdt = [xs[l] * _expand_heads(dt[l], lo) for l in range(DEC_SEQ)]
    gw = D_SSM // SSM_GROUPS
    for l in range(DEC_SEQ):
        y = xs[l] * dexp_ref[...]
        for s in range(l + 1):
            decay = _expand_heads(jnp.exp(acs[l] - acs[s]), lo)
            cbs = [jnp.sum(cm[l][:, g * SSM_STATE:(g + 1) * SSM_STATE] * bm[s][:, g * SSM_STATE:(g + 1) * SSM_STATE],
                           axis=-1, keepdims=True) for g in range(SSM_GROUPS)]
            coef = jnp.concatenate([decay[:, g * gw:(g + 1) * gw] * cbs[g] for g in range(SSM_GROUPS)], axis=1)
            y = y + coef * xdt[s]
        ypart_ref[l] = y
        eacs_ref[l] = _expand_heads(jnp.exp(acs[l]), lo)
        xd_ref[l] = xdt[l] * _expand_heads(jnp.exp(acs[DEC_SEQ - 1] - acs[l]), lo)
    dec_ref[...] = jnp.exp(acs[DEC_SEQ - 1])


def _smix_in(x_rows, sconv, spool, w):
    ns = sconv.shape[1]
    nb = S_SEQ_BLOCK
    tmaj = lambda steps, width: pl.BlockSpec((steps, nb, width), lambda i: (0, i, 0))
    step_outs = [D_MODEL, D_SSM, D_SSM, D_SSM, D_SSM, D_BC, D_BC]
    out_specs = [tmaj(DEC_SEQ, wd) for wd in step_outs] + [pl.BlockSpec((nb, LANES), lambda i: (i, 0)),
                                                          tmaj(DEC_SEQ, D_POOL),
                                                          tmaj(SSM_CONV - 1, D_XBC), tmaj(POOL_HIST, D_POOL)]
    out_shape = [jax.ShapeDtypeStruct((DEC_SEQ, ns, wd), f32) for wd in step_outs] + [
        jax.ShapeDtypeStruct((ns, LANES), f32), jax.ShapeDtypeStruct((DEC_SEQ, ns, D_POOL), f32),
        jax.ShapeDtypeStruct((SSM_CONV - 1, ns, D_XBC), f32), jax.ShapeDtypeStruct((POOL_HIST, ns, D_POOL), f32)]
    return pl.pallas_call(
        _smix_in_kernel,
        grid=(ns // nb,),
        in_specs=[pl.BlockSpec((nb * X_ROWS_PER_SEQ, LANES), lambda i: (i, 0)), tmaj(SSM_CONV - 1, D_XBC),
                  tmaj(POOL_HIST, D_POOL),
                  _const_spec((1, D_MODEL)), _const_spec((D_MODEL, D_SSM + D_XBC)), _const_spec((D_MODEL, D_POOL)),
                  _const_spec((D_MODEL, LANES)), _const_spec((SSM_CONV, D_XBC)),
                  _const_spec((1, D_XBC)), _const_spec((1, LANES)), _const_spec((1, LANES)), _const_spec((1, D_SSM)),
                  _const_spec((len(POOL_WINDOWS), POOL_GROUP_DIM, POOL_GROUP_DIM)), _const_spec((1, D_POOL))],
        out_specs=out_specs,
        out_shape=out_shape,
        scratch_shapes=[pltpu.VMEM((DEC_SEQ * nb, D_MODEL), bf16), pltpu.VMEM((DEC_SEQ * nb, D_POOL), bf16)],
        compiler_params=pltpu.CompilerParams(dimension_semantics=("arbitrary",), vmem_limit_bytes=VMEM_LIMIT_BYTES),
        name="smix_in",
    )(x_rows, sconv, spool, w["nmix"], w["wzx"], w["wvp"], w["wdt"], w["cw"], w["cb"], w["dtb"], w["a_row"], w["dexp"], w["wpool"],
      w["pscale"])


def _sssd_kernel(dec_ref, c_ref, b_ref, xd_ref, st_ref, yoff_ref, stn_ref):
    blk = pl.program_id(0)
    nb = S_SSD_BLOCK
    gw = D_SSM // SSM_GROUPS
    hpg = SSM_HEADS // SSM_GROUPS
    row_seq = lax.broadcasted_iota(jnp.int32, (DEC_SEQ * nb, gw), 0) % nb
    cmat, bmat, xd = _steps(c_ref), _steps(b_ref), _steps(xd_ref)
    for g in range(SSM_GROUPS):
        gl = slice(g * gw, (g + 1) * gw)
        cg = cmat[:, g * SSM_STATE:(g + 1) * SSM_STATE].astype(bf16)
        bg = bmat[:, g * SSM_STATE:(g + 1) * SSM_STATE].astype(bf16)
        yo = jnp.zeros((DEC_SEQ * nb, gw), f32)
        for j in range(nb):
            mine = row_seq == j
            h0 = st_ref[j, gl, :]
            yo = jnp.where(mine, _dot_nt(cg, h0.astype(bf16)), yo)
            upd = _dot_tn(jnp.where(mine, xd[:, gl], 0.0).astype(bf16), bg)
            for hh in range(hpg):
                hr = slice(hh * SSM_HEAD_DIM, (hh + 1) * SSM_HEAD_DIM)
                d = dec_ref[(blk * nb + j) * SSM_HEADS + g * hpg + hh]
                stn_ref[j, g * gw + hh * SSM_HEAD_DIM:g * gw + (hh + 1) * SSM_HEAD_DIM, :] = h0[hr] * d + upd[hr]
        for l in range(DEC_SEQ):
            yoff_ref[l, :, gl] = yo[l * nb:(l + 1) * nb]


def _sssd(dec_flat, cmat, bmat, xd, state):
    ns = state.shape[0]
    nb = S_SSD_BLOCK
    tmaj = lambda width: pl.BlockSpec((DEC_SEQ, nb, width), lambda i: (0, i, 0))
    stblk = pl.BlockSpec((nb, D_SSM, SSM_STATE), lambda i: (i, 0, 0))
    return pl.pallas_call(
        _sssd_kernel,
        grid=(ns // nb,),
        in_specs=[pl.BlockSpec(memory_space=pltpu.SMEM), tmaj(D_BC), tmaj(D_BC), tmaj(D_SSM), stblk],
        out_specs=[tmaj(D_SSM), stblk],
        out_shape=[jax.ShapeDtypeStruct((DEC_SEQ, ns, D_SSM), f32), jax.ShapeDtypeStruct(state.shape, f32)],
        compiler_params=pltpu.CompilerParams(dimension_semantics=("arbitrary",), vmem_limit_bytes=VMEM_LIMIT_BYTES),
        name="sssd",
    )(dec_flat, cmat, bmat, xd, state)


def _smix_out_kernel(x_ref, ypart_ref, yoff_ref, eacs_ref, z_ref, pout_ref, snorm_ref, wout_ref, nmem_ref, wq_ref,
                     x1_ref, q_ref, cat_scr):
    y = ypart_ref[...] + yoff_ref[...] * eacs_ref[...]
    t = y * _silu(z_ref[...])
    gw = D_SSM // SSM_GROUPS
    for g in range(SSM_GROUPS):
        gl = slice(g * gw, (g + 1) * gw)
        tg = t[:, gl]
        ms = jnp.mean(tg * tg, axis=-1, keepdims=True)
        cat_scr[:, gl] = (tg * lax.rsqrt(ms + EPS) * snorm_ref[:, gl]).astype(bf16)
    cat_scr[:, D_SSM:] = pout_ref[...].astype(bf16)
    x1 = x_ref[...] + _dot(cat_scr[...], wout_ref[...])
    x1_ref[...] = x1
    h = _rmsnorm(x1, nmem_ref[...]).astype(bf16)
    q_ref[...] = _dot(h, wq_ref[...]) * (MEM_HEAD_DIM ** -0.5)


def _smix_out(x, ypart, yoff, eacs, z, pout, w):
    n = x.shape[0]
    rb = 128
    rows = lambda width: pl.BlockSpec((rb, width), lambda i: (i, 0))
    return pl.pallas_call(
        _smix_out_kernel,
        grid=(n // rb,),
        in_specs=[rows(D_MODEL), rows(D_SSM), rows(D_SSM), rows(D_SSM), rows(D_SSM), rows(D_POOL),
                  _const_spec((1, D_SSM)), _const_spec((D_SSM + D_POOL, D_MODEL)), _const_spec((1, D_MODEL)),
                  _const_spec((D_MODEL, D_MODEL))],
        out_specs=[rows(D_MODEL), rows(D_MODEL)],
        out_shape=[jax.ShapeDtypeStruct((n, D_MODEL), f32), jax.ShapeDtypeStruct((n, D_MODEL), f32)],
        scratch_shapes=[pltpu.VMEM((rb, D_SSM + D_POOL), bf16)],
        compiler_params=pltpu.CompilerParams(dimension_semantics=("arbitrary",), vmem_limit_bytes=VMEM_LIMIT_BYTES),
        name="smix_out",
    )(x, ypart, yoff, eacs, z, pout, w["snorm"], w["wout"], w["nmem"], w["wq"])


def _sattn_kernel(q_ref, k_ref, v_ref, o_ref):
    nb = S_ATT_BLOCK
    rg = DEC_SEQ * nb
    rows = MEM_HEADS * rg

    def row_ids(width):
        r = lax.broadcasted_iota(jnp.int32, (rows, width), 0)
        return r // rg, r % nb

    row_h, row_seq = row_ids(D_MODEL)
    col_h = lax.broadcasted_iota(jnp.int32, (rows, D_MODEL), 1) // MEM_HEAD_DIM
    q = _steps(q_ref)
    qh = jnp.where(row_h == col_h, jnp.concatenate([q] * MEM_HEADS, axis=0), 0.0)
    lhs_s = jnp.concatenate([jnp.where(row_seq == b, qh, 0.0).astype(bf16) for b in range(nb)], axis=1)
    kcat = jnp.concatenate([_kv_seq(k_ref, b) for b in range(nb)], axis=1)
    s = _dot_nt(lhs_s, kcat)
    p = jnp.exp(s - jnp.max(s, axis=-1, keepdims=True))
    p = p / jnp.sum(p, axis=-1, keepdims=True)
    _, row_seq_p = row_ids(N_MEM)
    lhs_p = jnp.concatenate([jnp.where(row_seq_p == b, p, 0.0).astype(bf16) for b in range(nb)], axis=1)
    vcat = jnp.concatenate([_kv_seq(v_ref, b) for b in range(nb)], axis=0)
    res = _dot(lhs_p, vcat)
    for hd in range(MEM_HEADS):
        hl = slice(hd * MEM_HEAD_DIM, (hd + 1) * MEM_HEAD_DIM)
        for l in range(DEC_SEQ):
            o_ref[l, :, hl] = res[hd * rg + l * nb:hd * rg + (l + 1) * nb, hl]


def _sattn(q, mem_k, mem_v):
    ns = mem_k.shape[0]
    nb = S_ATT_BLOCK
    qblk = pl.BlockSpec((DEC_SEQ, nb, D_MODEL), lambda i: (0, i, 0))
    kvblk = pl.BlockSpec((nb, N_MEM * KV_ROWS_PER_TOKEN, LANES), lambda i: (i, 0, 0))
    return pl.pallas_call(
        _sattn_kernel,
        grid=(ns // nb,),
        in_specs=[qblk, kvblk, kvblk],
        out_specs=qblk,
        out_shape=jax.ShapeDtypeStruct((DEC_SEQ, ns, D_MODEL), f32),
        compiler_params=pltpu.CompilerParams(dimension_semantics=("arbitrary",), vmem_limit_bytes=VMEM_LIMIT_BYTES),
        name="sattn",
    )(q, mem_k, mem_v)


def _sffn_kernel(x1_ref, ao_ref, sffn_ref, wo_ref, g_ref, wup_ref, cw_ref, cb_ref, wdn_ref, gfin_ref,
                 y_ref, st_ref, h_scr, act_scr):
    nb = S_SEQ_BLOCK
    x2 = _steps(x1_ref) + _dot(_steps(ao_ref).astype(bf16), wo_ref[...])
    h_scr[...] = _rmsnorm(x2, g_ref[...]).astype(bf16)
    u = _dot(h_scr[...], wup_ref[...])
    hist = FFN_CONV - 1

    def conv_block(blk):
        cols = slice(blk * FF_CHUNK, (blk + 1) * FF_CHUNK)
        tiles = range(blk * FF_CHUNK // LANES, (blk + 1) * FF_CHUNK // LANES)
        slots = [sffn_ref[:, i, cols] for i in range(hist)]
        slots += [u[l * nb:(l + 1) * nb, cols] for l in range(DEC_SEQ)]
        for i in range(hist):
            st_ref[:, i, cols] = slots[DEC_SEQ + i]
        outs = []
        for l in range(DEC_SEQ):
            acc = cb_ref[:, cols] + cw_ref[0:1, cols] * slots[l]
            for k in range(1, FFN_CONV):
                acc = acc + cw_ref[k:k + 1, cols] * slots[l + k]
            outs.append(acc)
        return jnp.concatenate(outs, axis=0)

    for j in range(N_FF_CHUNKS):
        act = _silu(conv_block(_gate_blk(j))) * conv_block(_val_blk(j))
        act_scr[:, j * FF_CHUNK:(j + 1) * FF_CHUNK] = act.astype(bf16)
    y = _rmsnorm(x2 + _dot(act_scr[...], wdn_ref[...]), gfin_ref[...])
    for l in range(DEC_SEQ):
        _rows_view_put(y_ref, l, DEC_SEQ, range(D_MODEL // LANES), nb, X_ROWS_PER_SEQ, y[l * nb:(l + 1) * nb])


def _sffn(x1, ao, sffn_rows, w):
    ns = x1.shape[1]
    nb = S_SEQ_BLOCK
    tok = pl.BlockSpec((DEC_SEQ, nb, D_MODEL), lambda i: (0, i, 0))
    yblk = pl.BlockSpec((nb * X_ROWS_PER_SEQ, LANES), lambda i: (i, 0))
    stblk = pl.BlockSpec((nb, FFN_CONV - 1, 2 * D_FF), lambda i: (i, 0, 0))
    return pl.pallas_call(
        _sffn_kernel,
        grid=(ns // nb,),
        in_specs=[tok, tok, stblk,
                  _const_spec((D_MODEL, D_MODEL)), _const_spec((1, D_MODEL)), _const_spec((D_MODEL, 2 * D_FF)),
                  _const_spec((FFN_CONV, 2 * D_FF)), _const_spec((1, 2 * D_FF)),
                  _const_spec((D_FF, D_MODEL)), _const_spec((1, D_MODEL))],
        out_specs=[yblk, stblk],
        out_shape=[jax.ShapeDtypeStruct((ns * X_ROWS_PER_SEQ, LANES), f32),
                   jax.ShapeDtypeStruct((ns, FFN_CONV - 1, 2 * D_FF), f32)],
        scratch_shapes=[pltpu.VMEM((DEC_SEQ * nb, D_MODEL), bf16), pltpu.VMEM((DEC_SEQ * nb, D_FF), bf16)],
        compiler_params=pltpu.CompilerParams(dimension_semantics=("arbitrary",), vmem_limit_bytes=VMEM_LIMIT_BYTES),
        name="sffn",
    )(x1, ao, sffn_rows, w["wo"], w["nffn"], w["wup"], w["fcw"], w["fcb"], w["wdn"], w["gfin"])


def _sample_path(x_sample, state_ssm, state_ssm_conv, state_pool, state_ffn_conv, cache_mem_k, cache_mem_v, w):
    ns = x_sample.shape[0]
    xtm, z, ypart, eacs, xd, cmat, bmat, dec, pout, conv_new, pool_new = _smix_in(
        _tiled_rows_view(x_sample), state_ssm_conv[0].transpose(1, 0, 2), state_pool[0].transpose(1, 0, 2), w)
    yoff, ssm_new = _sssd(dec[:, :SSM_HEADS].reshape(-1), cmat, bmat, xd, state_ssm[0].reshape(ns, D_SSM, SSM_STATE))
    flat = lambda a: a.reshape(DEC_SEQ * ns, a.shape[-1])
    x1, q = _smix_out(flat(xtm), flat(ypart), flat(yoff), flat(eacs), flat(z), flat(pout), w)
    ao = _sattn(q.reshape(DEC_SEQ, ns, D_MODEL), _kv_rows_view(cache_mem_k[0]), _kv_rows_view(cache_mem_v[0]))
    y_rows, ffn_new = _sffn(x1.reshape(DEC_SEQ, ns, D_MODEL), ao, state_ffn_conv[0], w)
    return (_from_tiled_rows(y_rows, ns, DEC_SEQ, D_MODEL),
            ssm_new.reshape(1, ns, SSM_HEADS, SSM_HEAD_DIM, SSM_STATE),
            conv_new.transpose(1, 0, 2)[None], pool_new.transpose(1, 0, 2)[None],
            ffn_new[None])


def kernel(x_prompt, x_sample, mem_prompt, state_ssm, state_ssm_conv, state_pool, state_ffn_conv, cache_mem_k, cache_mem_v, norm_mix, w_in, ssm_conv_w, ssm_conv_b, ssm_dt_bias, ssm_a_log, ssm_d, ssm_norm, w_pool, pool_scale, w_out, norm_mem, norm_memkv, w_mq, w_mk, w_mv, w_mo, norm_ffn, w_up, ffn_conv_w, ffn_conv_b, w_down, final_norm):
    params = dict(norm_mix=norm_mix, w_in=w_in, ssm_conv_w=ssm_conv_w, ssm_conv_b=ssm_conv_b, ssm_dt_bias=ssm_dt_bias,
                  ssm_a_log=ssm_a_log, ssm_d=ssm_d, ssm_norm=ssm_norm, w_pool=w_pool, pool_scale=pool_scale,
                  w_out=w_out, norm_mem=norm_mem, norm_memkv=norm_memkv, w_mq=w_mq, w_mk=w_mk, w_mv=w_mv, w_mo=w_mo,
                  norm_ffn=norm_ffn, w_up=w_up, ffn_conv_w=ffn_conv_w, ffn_conv_b=ffn_conv_b, w_down=w_down,
                  final_norm=final_norm)
    w = _prep_weights(params)
    yp, ssm_p, conv_p, pool_p, ffn_p, mk_p, mv_p = _prompt_path(x_prompt, mem_prompt, w, PROMPT_TILE)
    ys, ssm_s, conv_s, pool_s, ffn_s = _sample_path(x_sample, state_ssm, state_ssm_conv, state_pool, state_ffn_conv,
                                                    cache_mem_k, cache_mem_v, w)
    return yp, ys, ssm_p, ssm_s, conv_p, conv_s, pool_p, pool_s, ffn_p, ffn_s, mk_p, mv_p
```

```python
import functools

import jax
import jax.numpy as jnp
from jax import lax
from jax.experimental import pallas as pl
from jax.experimental.pallas import tpu as pltpu

f32 = jnp.float32
bf16 = jnp.bfloat16

D_MODEL = 1024
SSM_HEADS = 16
SSM_HEAD_DIM = 64
SSM_STATE = 128
SSM_GROUPS = 2
SSM_CHUNK = 128
D_SSM = 1024
D_BC = SSM_GROUPS * SSM_STATE
D_XBC = D_SSM + 2 * D_BC
SSM_CONV = 4
D_POOL = 1024
POOL_WINDOWS = (2, 4, 8, 16)
POOL_GROUP_DIM = 256
POOL_HIST = 15
N_MEM = 256
MEM_HEADS = 4
MEM_HEAD_DIM = 256
D_FF = 2816
FFN_CONV = 3
EPS = 1e-6
PAST_LEN = 16384

LANES = 128
SUBLANES = 8
MXU_DIM = 256
VMEM_LIMIT_BYTES = 56 * 1024 * 1024

CONV_HIST_ROWS = SUBLANES
POOL_HIST_ROWS = 2 * SUBLANES
FF_CHUNK = MXU_DIM
N_FF_CHUNKS = D_FF // FF_CHUNK


def _silu(v):
    half = 0.5 * v
    return half + half * jnp.tanh(half)


def _silu_gate(g, v):
    return g * v * (1.0 / (1.0 + jnp.exp(-g)))


def _softplus(v):
    return jnp.maximum(v, 0.0) + jnp.log1p(jnp.exp(-jnp.abs(v)))


def _rmsnorm(x, g):
    ms = jnp.mean(x * x, axis=-1, keepdims=True)
    return x * lax.rsqrt(ms + EPS) * g


def _dot(a, b):
    return jnp.dot(a, b, preferred_element_type=f32)


def _dot_nt(a, b):
    return lax.dot_general(a, b, (((1,), (1,)), ((), ())), preferred_element_type=f32)


def _dot_tn(a, b):
    return lax.dot_general(a, b, (((0,), (0,)), ((), ())), preferred_element_type=f32)


def _split3(v):
    p1 = v.astype(bf16)
    r1 = v - p1.astype(f32)
    p2 = r1.astype(bf16)
    r2 = r1 - p2.astype(f32)
    return p1, p2, r2.astype(bf16)


def _const_spec(shape):
    return pl.BlockSpec(shape, lambda *_: (0,) * len(shape), pipeline_mode=pl.Buffered(1))


KV_LANE_TILES = MEM_HEAD_DIM // LANES
KV_ROWS_PER_TOKEN = KV_LANE_TILES * MEM_HEADS


def _kv_rows_view(kv):
    ns = kv.shape[0]
    kv = kv.reshape(ns, N_MEM, MEM_HEADS, KV_LANE_TILES, LANES).transpose(0, 1, 3, 2, 4)
    return kv.reshape(ns, N_MEM * KV_ROWS_PER_TOKEN, LANES)


def _kv_from_rows(rows):
    ns = rows.shape[0]
    kv = rows.reshape(ns, N_MEM, KV_LANE_TILES, MEM_HEADS, LANES).transpose(0, 1, 3, 2, 4)
    return kv.reshape(ns, N_MEM, MEM_HEADS, MEM_HEAD_DIM)


def _kv_seq(ref, jj):
    tiles = [ref[jj, pl.ds(dt * MEM_HEADS + hd, N_MEM, stride=KV_ROWS_PER_TOKEN), :]
             for hd in range(MEM_HEADS) for dt in range(KV_LANE_TILES)]
    return jnp.concatenate(tiles, axis=1).astype(bf16)


def _memkv_kernel(mem_ref, g_ref, wk_ref, wv_ref, k_ref, v_ref):
    h = _rmsnorm(mem_ref[0], g_ref[...]).astype(bf16)
    for out_ref, w_ref in ((k_ref, wk_ref), (v_ref, wv_ref)):
        kv = _dot(h, w_ref[...])
        for hd in range(MEM_HEADS):
            for dt in range(KV_LANE_TILES):
                col = hd * MEM_HEAD_DIM + dt * LANES
                out_ref[0, pl.ds(dt * MEM_HEADS + hd, N_MEM, stride=KV_ROWS_PER_TOKEN), :] = kv[:, col:col + LANES]


def _memkv(mem, g, wk, wv):
    b = mem.shape[0]
    blk = pl.BlockSpec((1, N_MEM, D_MODEL), lambda i: (i, 0, 0))
    oblk = pl.BlockSpec((1, N_MEM * KV_ROWS_PER_TOKEN, LANES), lambda i: (i, 0, 0))
    return pl.pallas_call(
        _memkv_kernel,
        grid=(b,),
        in_specs=[blk, _const_spec((1, D_MODEL)), _const_spec((D_MODEL, D_MODEL)), _const_spec((D_MODEL, D_MODEL))],
        out_specs=[oblk, oblk],
        out_shape=[jax.ShapeDtypeStruct((b, N_MEM * KV_ROWS_PER_TOKEN, LANES), f32)] * 2,
        compiler_params=pltpu.CompilerParams(dimension_semantics=("arbitrary",), vmem_limit_bytes=VMEM_LIMIT_BYTES),
        name="memkv",
    )(mem, g, wk, wv)


def _ssd_chunk(r0, dt_scr, xs_scr, b_scr, c_scr, y_scr, xd_scr, hst_scr, a_ref, dexp_ref, between=()):
    between = list(between) + [None] * 3
    q = SSM_CHUNK
    rows = pl.ds(r0, q)
    row_i = lax.broadcasted_iota(jnp.int32, (q, q), 0)
    col_i = lax.broadcasted_iota(jnp.int32, (q, q), 1)
    causal = col_i <= row_i
    lo = col_i < SSM_HEAD_DIM
    tril = jnp.where(causal, 1.0, 0.0).astype(bf16)

    dt = dt_scr[rows, :]
    da = dt * a_ref[...]
    p1, p2, p3 = _split3(da)
    acs = _dot(tril, p1) + _dot(tril, p2) + _dot(tril, p3)
    acs_t = acs.T
    if between[0] is not None:
        between[0]()

    for g in range(SSM_GROUPS):
        bg = b_scr[rows, g * SSM_STATE:(g + 1) * SSM_STATE]
        cg = c_scr[rows, g * SSM_STATE:(g + 1) * SSM_STATE]
        bg_b = bg.astype(bf16)
        cb = jnp.where(causal, _dot_nt(cg.astype(bf16), bg_b), 0.0)
        cdec_rows = []
        pairs_per_group = SSM_HEADS // SSM_GROUPS // 2
        for pp in range(pairs_per_group):
            pr = g * pairs_per_group + pp
            lanes = slice(pr * LANES, (pr + 1) * LANES)
            lhs, dtb, dend, cdec = [], [], [], []
            for hh in (2 * pr, 2 * pr + 1):
                colb = jnp.broadcast_to(acs[:, hh:hh + 1], (q, q))
                seg = jnp.where(causal, colb - acs_t[hh:hh + 1, :], 0.0)
                lhs.append((jnp.exp(seg) * cb).astype(bf16))
                lhs.append((cg * jnp.exp(colb)).astype(bf16))
                last = colb[q - 1:q, :]
                dend.append(jnp.exp(last - colb))
                cdec.append(jnp.exp(last))
                dtb.append(jnp.broadcast_to(dt[:, hh:hh + 1], (q, q)))
            xs_pair = xs_scr[rows, lanes]
            xdt = xs_pair * jnp.where(lo, dtb[0], dtb[1])
            xd_scr[:, lanes] = (xdt * jnp.where(lo, dend[0], dend[1])).astype(bf16)
            hst_pair = hst_scr[:, lanes]
            rhs = jnp.concatenate([
                jnp.where(lo, xdt, 0.0).astype(bf16), jnp.where(lo, hst_pair, 0.0).astype(bf16),
                jnp.where(lo, 0.0, xdt).astype(bf16), jnp.where(lo, 0.0, hst_pair).astype(bf16)], axis=0)
            y_pair = _dot(jnp.concatenate(lhs, axis=1), rhs)
            y_scr[rows, lanes] = y_pair + xs_pair * dexp_ref[:, lanes]
            cdec_rows.append(jnp.where(lo[:1], cdec[0], cdec[1]))
        gl = slice(g * (D_SSM // SSM_GROUPS), (g + 1) * (D_SSM // SSM_GROUPS))
        upd = _dot_tn(bg_b, xd_scr[:, gl])
        hst_scr[:, gl] = hst_scr[:, gl] * jnp.concatenate(cdec_rows, axis=1) + upd
        if between[1 + g] is not None:
            between[1 + g]()


def _mixer_kernel(x_ref, nmix_ref, wzx_ref, wvp_ref, wdt_ref, cw_ref, cb_ref, dtb_ref, a_ref, dexp_ref, snorm_ref, wpool_ref,
                  pscale_ref, wout_ref,
                  x1_ref, ssm_ref, conv_ref, pool_ref,
                  xd_scr, hst_scr, *sub_scr, tile, sub):
    t = pl.program_id(1)
    last_t = pl.num_programs(1) - 1
    ch = CONV_HIST_ROWS
    ph = POOL_HIST_ROWS
    n_sub = tile // sub
    per = len(sub_scr) // n_sub
    bufs = [sub_scr[i * per:(i + 1) * per] for i in range(n_sub)]
    xbc0, vp0 = bufs[0][2], bufs[0][3]

    @pl.when(t == 0)
    def _():
        xbc0[:, 0:ch, :] = jnp.zeros((D_XBC // LANES, ch, LANES), f32)
        vp0[:, 0:ph, :] = jnp.zeros((D_POOL // LANES, ph, LANES), f32)
        hst_scr[...] = jnp.zeros_like(hst_scr)

    gw = D_SSM // SSM_GROUPS

    def carry_history(src, dst):
        for j in range(D_XBC // LANES):
            bufs[dst][2][j, 0:ch, :] = bufs[src][2][j, sub:sub + ch, :]
        for j in range(D_POOL // LANES):
            bufs[dst][3][j, 0:ph, :] = bufs[src][3][j, sub:sub + ph, :]

    def in_xbc(s):
        h_scr, _, xbc_scr = bufs[s][:3]
        h_scr[...] = _rmsnorm(x_ref[0, s * sub:(s + 1) * sub, :], nmix_ref[...]).astype(bf16)
        xbc = _dot(h_scr[...], wzx_ref[:, D_SSM:])
        for j in range(D_XBC // LANES):
            xbc_scr[j, ch:ch + sub, :] = xbc[:, j * LANES:(j + 1) * LANES]

    def in_dtvp(s):
        h_scr, _, _, vp_scr, dt_scr = bufs[s][:5]
        dt_scr[...] = _dot(h_scr[...], wdt_ref[...])
        vp = _dot(h_scr[...], wvp_ref[...])
        for j in range(D_POOL // LANES):
            vp_scr[j, ph:ph + sub, :] = vp[:, j * LANES:(j + 1) * LANES]

    def in_z(s):
        h_scr, z_scr = bufs[s][:2]
        z_scr[...] = _dot(h_scr[...], wzx_ref[:, :D_SSM])
        if s > 0:
            carry_history(s - 1, s)

    def stage_mid(s, between):
        _, z_scr, xbc_scr, vp_scr, dt_scr, xs_scr, b_scr, c_scr, y_scr, pooled_scr, cat_scr = bufs[s]
        for j in range(D_XBC // LANES):
            cl = slice(j * LANES, (j + 1) * LANES)
            acc = cb_ref[:, cl] + cw_ref[0:1, cl] * xbc_scr[j, pl.ds(ch - 3, sub), :]
            for k in range(1, SSM_CONV):
                acc = acc + cw_ref[k:k + 1, cl] * xbc_scr[j, pl.ds(ch - 3 + k, sub), :]
            act = _silu(acc)
            if j < D_SSM // LANES:
                xs_scr[:, cl] = act
            elif j < (D_SSM + D_BC) // LANES:
                b_scr[:, j * LANES - D_SSM:(j + 1) * LANES - D_SSM] = act
            else:
                c_scr[:, j * LANES - D_SSM - D_BC:(j + 1) * LANES - D_SSM - D_BC] = act

        pos1 = lax.broadcasted_iota(jnp.int32, (sub, LANES), 0) + (t * tile + s * sub + 1)
        for j in range(D_POOL // LANES):
            cl = slice(j * LANES, (j + 1) * LANES)
            w = POOL_WINDOWS[j * LANES // POOL_GROUP_DIM]
            cur = vp_scr[j, ph:ph + sub, :]
            acc = cur
            for k in range(1, w):
                acc = acc + vp_scr[j, pl.ds(ph - k, sub), :]
            cnt = jnp.minimum(pos1, w).astype(f32)
            pooled_scr[:, cl] = (acc / cnt - cur).astype(bf16)
        for gi in range(len(POOL_WINDOWS)):
            gl = slice(gi * POOL_GROUP_DIM, (gi + 1) * POOL_GROUP_DIM)
            pg = _dot(pooled_scr[:, gl], wpool_ref[gi]) * pscale_ref[:, gl]
            cat_scr[:, D_SSM + gi * POOL_GROUP_DIM:D_SSM + (gi + 1) * POOL_GROUP_DIM] = pg.astype(bf16)

        dt_scr[...] = _softplus(dt_scr[...] + dtb_ref[...])
        n_chunks = sub // SSM_CHUNK
        slots = [None] * (3 * n_chunks)
        for i, piece in enumerate(between):
            slots[i * len(slots) // len(between)] = piece
        for c in range(n_chunks):
            _ssd_chunk(c * SSM_CHUNK, dt_scr, xs_scr, b_scr, c_scr, y_scr, xd_scr, hst_scr, a_ref, dexp_ref,
                       slots[3 * c:3 * c + 3])

        for g in range(SSM_GROUPS):
            gl = slice(g * gw, (g + 1) * gw)
            tg = y_scr[:, gl] * _silu(z_scr[:, gl])
            ms = jnp.mean(tg * tg, axis=-1, keepdims=True)
            cat_scr[:, gl] = (tg * lax.rsqrt(ms + EPS) * snorm_ref[:, gl]).astype(bf16)

    def stage_out(s):
        rows = slice(s * sub, (s + 1) * sub)
        x1_ref[0, rows, :] = x_ref[0, rows, :] + _dot(bufs[s][-1][...], wout_ref[...])

    in_xbc(0)
    in_dtvp(0)
    in_z(0)
    for s in range(n_sub):
        nxt = s + 1 < n_sub
        if nxt:
            in_xbc(s + 1)
        pieces = [functools.partial(in_dtvp, s + 1), functools.partial(in_z, s + 1)] if nxt else []
        if s > 0:
            pieces.append(functools.partial(stage_out, s - 1))
        stage_mid(s, pieces)
    stage_out(n_sub - 1)
    carry_history(n_sub - 1, 0)

    @pl.when(t == last_t)
    def _():
        for pr in range(D_SSM // LANES):
            ssm_ref[0, pr * LANES:(pr + 1) * LANES, :] = hst_scr[:, pr * LANES:(pr + 1) * LANES].T
        for j in range(D_XBC // LANES):
            conv_ref[0, :, j * LANES:(j + 1) * LANES] = xbc0[j, pl.ds(ch - (SSM_CONV - 1), SSM_CONV - 1), :]
        for j in range(D_POOL // LANES):
            pool_ref[0, :, j * LANES:(j + 1) * LANES] = vp0[j, pl.ds(ph - POOL_HIST, POOL_HIST), :]


def _mixer_prompt(x, nmix, wzx, wvp, wdt, cw, cb, dtb, a_row, dexp, snorm, wpool, pscale, wout, tile):
    b, seq, _ = x.shape
    nt = seq // tile
    sub = min(tile, MIXER_SUB_TILE)
    xblk = pl.BlockSpec((1, tile, D_MODEL), lambda i, j: (i, j, 0))
    sub_scratch = [
        pltpu.VMEM((sub, D_MODEL), bf16),
        pltpu.VMEM((sub, D_SSM), f32),
        pltpu.VMEM((D_XBC // LANES, CONV_HIST_ROWS + sub, LANES), f32),
        pltpu.VMEM((D_POOL // LANES, POOL_HIST_ROWS + sub, LANES), f32),
        pltpu.VMEM((sub, LANES), f32),
        pltpu.VMEM((sub, D_SSM), f32),
        pltpu.VMEM((sub, D_BC), f32),
        pltpu.VMEM((sub, D_BC), f32),
        pltpu.VMEM((sub, D_SSM), f32),
        pltpu.VMEM((sub, D_POOL), bf16),
        pltpu.VMEM((sub, D_SSM + D_POOL), bf16),
    ]
    scratch = [pltpu.VMEM((SSM_CHUNK, D_SSM), bf16),
               pltpu.VMEM((SSM_STATE, D_SSM), f32)]
    scratch += sub_scratch * (tile // sub)
    return pl.pallas_call(
        functools.partial(_mixer_kernel, tile=tile, sub=sub),
        grid=(b, nt),
        in_specs=[xblk, _const_spec((1, D_MODEL)), _const_spec((D_MODEL, D_SSM + D_XBC)),
                  _const_spec((D_MODEL, D_POOL)), _const_spec((D_MODEL, LANES)), _const_spec((SSM_CONV, D_XBC)),
                  _const_spec((1, D_XBC)), _const_spec((1, LANES)), _const_spec((1, LANES)), _const_spec((1, D_SSM)),
                  _const_spec((1, D_SSM)), _const_spec((len(POOL_WINDOWS), POOL_GROUP_DIM, POOL_GROUP_DIM)),
                  _const_spec((1, D_POOL)), _const_spec((D_SSM + D_POOL, D_MODEL))],
        out_specs=[xblk,
                   pl.BlockSpec((1, D_SSM, SSM_STATE), lambda i, j: (i, 0, 0)),
                   pl.BlockSpec((1, SSM_CONV - 1, D_XBC), lambda i, j: (i, 0, 0)),
                   pl.BlockSpec((1, POOL_HIST, D_POOL), lambda i, j: (i, 0, 0))],
        out_shape=[jax.ShapeDtypeStruct((b, seq, D_MODEL), f32),
                   jax.ShapeDtypeStruct((b, D_SSM, SSM_STATE), f32),
                   jax.ShapeDtypeStruct((b, SSM_CONV - 1, D_XBC), f32),
                   jax.ShapeDtypeStruct((b, POOL_HIST, D_POOL), f32)],
        scratch_shapes=scratch,
        compiler_params=pltpu.CompilerParams(dimension_semantics=("arbitrary", "arbitrary"),
                                             vmem_limit_bytes=VMEM_LIMIT_BYTES),
        name="mixer_prompt",
    )(x, nmix, wzx, wvp, wdt, cw, cb, dtb, a_row, dexp, snorm, wpool, pscale, wout)


def _attn_kernel(x_ref, g_ref, wq_ref, k_ref, v_ref, wo_ref, o_ref, q_scr, ao_scr):
    tile = x_ref.shape[1]
    sub = min(tile, ATTN_SUB_TILE)
    n_sub = tile // sub
    heads = [slice(hd * MEM_HEAD_DIM, (hd + 1) * MEM_HEAD_DIM) for hd in range(MEM_HEADS)]
    k = _kv_seq(k_ref, 0)
    v = _kv_seq(v_ref, 0)

    def q_proj(s):
        rows = slice(s * sub, (s + 1) * sub)
        h = _rmsnorm(x_ref[0, rows, :], g_ref[...]).astype(bf16)
        q_scr[rows, :] = (_dot(h, wq_ref[...]) * (MEM_HEAD_DIM ** -0.5)).astype(bf16)

    def scores(s):
        rows = slice(s * sub, (s + 1) * sub)
        return [_dot_nt(q_scr[rows, hl], k[:, hl]) for hl in heads]

    def values(s, sc):
        rows = slice(s * sub, (s + 1) * sub)
        for hl, s_h in zip(heads, sc):
            p = jnp.exp(s_h - jnp.max(s_h, axis=-1, keepdims=True))
            p = (p / jnp.sum(p, axis=-1, keepdims=True)).astype(bf16)
            ao_scr[rows, hl] = _dot(p, v[:, hl]).astype(bf16)

    def o_proj(s):
        rows = slice(s * sub, (s + 1) * sub)
        o_ref[0, rows, :] = x_ref[0, rows, :] + _dot(ao_scr[rows, :], wo_ref[...])

    q_proj(0)
    for s in range(n_sub):
        sc = scores(s)
        if s + 1 < n_sub:
            q_proj(s + 1)
        if s > 0:
            o_proj(s - 1)
        values(s, sc)
    o_proj(n_sub - 1)


def _attn_prompt(x, g, wq, mem_k, mem_v, wo, tile):
    b, seq, _ = x.shape
    xblk = pl.BlockSpec((1, tile, D_MODEL), lambda i, j: (i, j, 0))
    kvblk = pl.BlockSpec((1, N_MEM * KV_ROWS_PER_TOKEN, LANES), lambda i, j: (i, 0, 0))
    return pl.pallas_call(
        _attn_kernel,
        grid=(b, seq // tile),
        in_specs=[xblk, _const_spec((1, D_MODEL)), _const_spec((D_MODEL, D_MODEL)), kvblk, kvblk,
                  _const_spec((D_MODEL, D_MODEL))],
        out_specs=xblk,
        out_shape=jax.ShapeDtypeStruct((b, seq, D_MODEL), f32),
        scratch_shapes=[pltpu.VMEM((tile, D_MODEL), bf16), pltpu.VMEM((tile, D_MODEL), bf16)],
        compiler_params=pltpu.CompilerParams(dimension_semantics=("arbitrary", "arbitrary"),
                                             vmem_limit_bytes=VMEM_LIMIT_BYTES),
        name="attn_prompt",
    )(x, g, wq, mem_k, mem_v, wo)


def _gate_blk(j):
    return j


def _val_blk(j):
    return N_FF_CHUNKS + j


def _ffn_kernel(x_ref, g_ref, wup_ref, cw_ref, cb_ref, wdn_ref, gfin_ref, y_ref, st_ref, h_scr, u_scr, act_scr, *, tile):
    t = pl.program_id(1)
    last_t = pl.num_programs(1) - 1
    ch = CONV_HIST_ROWS
    tpc = FF_CHUNK // LANES
    ntile = 2 * D_FF // LANES

    @pl.when(t == 0)
    def _():
        u_scr[:, 0:ch, :] = jnp.zeros((ntile, ch, LANES), f32)

    sub = min(tile, FFN_SUB_TILE)
    n_sub = tile // sub

    def up_proj(s):
        rows = slice(s * sub, (s + 1) * sub)
        h_scr[rows, :] = _rmsnorm(x_ref[0, rows, :], g_ref[...]).astype(bf16)
        u = _dot(h_scr[rows, :], wup_ref[...])
        for ti in range(ntile):
            u_scr[ti, ch + s * sub:ch + (s + 1) * sub, :] = u[:, ti * LANES:(ti + 1) * LANES]

    def conv(s, ti):
        cl = slice(ti * LANES, (ti + 1) * LANES)
        r0 = ch + s * sub
        acc = cb_ref[:, cl] + cw_ref[0:1, cl] * u_scr[ti, pl.ds(r0 - 2, sub), :]
        acc = acc + cw_ref[1:2, cl] * u_scr[ti, pl.ds(r0 - 1, sub), :]
        return acc + cw_ref[2:3, cl] * u_scr[ti, r0:r0 + sub, :]

    def gate(s):
        rows = slice(s * sub, (s + 1) * sub)
        for j in range(N_FF_CHUNKS):
            for i in range(tpc):
                g_t = conv(s, _gate_blk(j) * tpc + i)
                v_t = conv(s, _val_blk(j) * tpc + i)
                act_scr[rows, (j * tpc + i) * LANES:(j * tpc + i + 1) * LANES] = _silu_gate(g_t, v_t).astype(bf16)

    def down_proj(s):
        rows = slice(s * sub, (s + 1) * sub)
        y_ref[0, rows, :] = _rmsnorm(x_ref[0, rows, :] + _dot(act_scr[rows, :], wdn_ref[...]), gfin_ref[...])

    up_proj(0)
    for s in range(n_sub):
        if s + 1 < n_sub:
            up_proj(s + 1)
        gate(s)
        down_proj(s)
    for ti in range(ntile):
        u_scr[ti, 0:ch, :] = u_scr[ti, tile:tile + ch, :]

    @pl.when(t == last_t)
    def _():
        for ti in range(ntile):
            st_ref[0, :, ti * LANES:(ti + 1) * LANES] = u_scr[ti, pl.ds(ch - (FFN_CONV - 1), FFN_CONV - 1), :]


def _ffn_prompt(x, g, wup, cw, cb, wdn, gfin, tile):
    b, seq, _ = x.shape
    xblk = pl.BlockSpec((1, tile, D_MODEL), lambda i, j: (i, j, 0))
    return pl.pallas_call(
        functools.partial(_ffn_kernel, tile=tile),
        grid=(b, seq // tile),
        in_specs=[xblk, _const_spec((1, D_MODEL)), _const_spec((D_MODEL, 2 * D_FF)),
                  _const_spec((FFN_CONV, 2 * D_FF)), _const_spec((1, 2 * D_FF)),
                  _const_spec((D_FF, D_MODEL)), _const_spec((1, D_MODEL))],
        out_specs=[xblk, pl.BlockSpec((1, FFN_CONV - 1, 2 * D_FF), lambda i, j: (i, 0, 0))],
        out_shape=[jax.ShapeDtypeStruct((b, seq, D_MODEL), f32),
                   jax.ShapeDtypeStruct((b, FFN_CONV - 1, 2 * D_FF), f32)],
        scratch_shapes=[pltpu.VMEM((tile, D_MODEL), bf16),
                        pltpu.VMEM((2 * D_FF // LANES, CONV_HIST_ROWS + tile, LANES), f32),
                        pltpu.VMEM((tile, D_FF), bf16)],
        compiler_params=pltpu.CompilerParams(dimension_semantics=("arbitrary", "arbitrary"),
                                             vmem_limit_bytes=VMEM_LIMIT_BYTES),
        name="ffn_prompt",
    )(x, g, wup, cw, cb, wdn, gfin)


def _split_in_proj_kernel(wt_ref, wzx_ref, wvp_ref, wdt_ref):
    n_zx = D_SSM + D_XBC
    for c0 in range(0, n_zx, D_MODEL):
        c1 = min(c0 + D_MODEL, n_zx)
        wzx_ref[:, c0:c1] = wt_ref[c0:c1, :].T.astype(bf16)
    lane = lax.broadcasted_iota(jnp.int32, (D_MODEL, LANES), 1)
    wdt_ref[...] = jnp.where(lane < SSM_HEADS, wt_ref[n_zx:n_zx + LANES, :].T, 0.0).astype(bf16)
    wvp_ref[...] = wt_ref[n_zx + SSM_HEADS:, :].T.astype(bf16)


def _split_in_proj(w_in):
    wt = w_in.T
    return pl.pallas_call(
        _split_in_proj_kernel,
        in_specs=[pl.BlockSpec(wt.shape, lambda: (0, 0))],
        out_shape=[jax.ShapeDtypeStruct((D_MODEL, D_SSM + D_XBC), bf16), jax.ShapeDtypeStruct((D_MODEL, D_POOL), bf16),
                   jax.ShapeDtypeStruct((D_MODEL, LANES), bf16)],
        compiler_params=pltpu.CompilerParams(vmem_limit_bytes=VMEM_LIMIT_BYTES),
        name="split_in_proj",
    )(wt)


def _prep_weights(p):
    i = 0
    wzx, wvp, wdt = _split_in_proj(p["w_in"][i])
    pad_h = (0, LANES - SSM_HEADS)
    return dict(
        nmix=p["norm_mix"][i][None], wzx=wzx, wvp=wvp, wdt=wdt, cw=p["ssm_conv_w"][i], cb=p["ssm_conv_b"][i][None],
        dtb=jnp.pad(p["ssm_dt_bias"][i], pad_h)[None],
        a_row=jnp.pad(-jnp.exp(p["ssm_a_log"][i].astype(f32)), pad_h)[None],
        dexp=jnp.repeat(p["ssm_d"][i], SSM_HEAD_DIM)[None], snorm=p["ssm_norm"][i][None],
        wpool=p["w_pool"][i].astype(bf16), pscale=p["pool_scale"][i][None], wout=p["w_out"][i].astype(bf16),
        nmem=p["norm_mem"][i][None], nmemkv=p["norm_memkv"][i][None],
        wq=p["w_mq"][i].astype(bf16), wk=p["w_mk"][i].astype(bf16), wv=p["w_mv"][i].astype(bf16),
        wo=p["w_mo"][i].astype(bf16),
        nffn=p["norm_ffn"][i][None],
        wup=p["w_up"][i].astype(bf16), fcw=p["ffn_conv_w"][i], fcb=p["ffn_conv_b"][i][None],
        wdn=p["w_down"][i].astype(bf16),
        gfin=p["final_norm"][None],
    )


def _prompt_path(x_prompt, mem_prompt, w, tile):
    b = x_prompt.shape[0]
    mem_k, mem_v = _memkv(mem_prompt, w["nmemkv"], w["wk"], w["wv"])
    x1, ssm, conv, pool = _mixer_prompt(x_prompt, w["nmix"], w["wzx"], w["wvp"], w["wdt"], w["cw"], w["cb"], w["dtb"], w["a_row"],
                                        w["dexp"], w["snorm"], w["wpool"], w["pscale"], w["wout"],
                                        min(MIXER_TILE, x_prompt.shape[1]))
    x2 = _attn_prompt(x1, w["nmem"], w["wq"], mem_k, mem_v, w["wo"], min(ATTN_TILE, x_prompt.shape[1]))
    y, ffn = _ffn_prompt(x2, w["nffn"], w["wup"], w["fcw"], w["fcb"], w["wdn"], w["gfin"], tile)
    return (y, ssm.reshape(1, b, SSM_HEADS, SSM_HEAD_DIM, SSM_STATE), conv[None], pool[None], ffn[None],
            _kv_from_rows(mem_k)[None], _kv_from_rows(mem_v)[None])


PROMPT_TILE = 512
ATTN_TILE = 1024
MIXER_TILE = 1024
MIXER_SUB_TILE = 256
ATTN_SUB_TILE = 256
FFN_SUB_TILE = 512


DEC_SEQ = 4
S_SEQ_BLOCK = 64
S_SSD_BLOCK = 16
S_ATT_BLOCK = 8
X_ROWS_PER_SEQ = DEC_SEQ * D_MODEL // LANES
FFN_ROWS_PER_SEQ = (FFN_CONV - 1) * 2 * D_FF // LANES


def _expand_heads(v, lo):
    r = v.shape[0]
    tiles = []
    for pr in range(SSM_HEADS // 2):
        a = jnp.broadcast_to(v[:, 2 * pr:2 * pr + 1], (r, LANES))
        b = jnp.broadcast_to(v[:, 2 * pr + 1:2 * pr + 2], (r, LANES))
        tiles.append(jnp.where(lo, a, b))
    return jnp.concatenate(tiles, axis=1)


def _tiled_rows_view(a):
    ns, r, c = a.shape
    return a.reshape(ns, r, c // LANES, LANES).transpose(0, 2, 1, 3).reshape(ns * (c // LANES) * r, LANES)


def _from_tiled_rows(rows, ns, r, c):
    return rows.reshape(ns, c // LANES, r, LANES).transpose(0, 2, 1, 3).reshape(ns, r, c)


def _rows_view_get(ref, i, r, tiles, nb, per_seq):
    return jnp.concatenate([ref[pl.ds(dt * r + i, nb, stride=per_seq), :] for dt in tiles], axis=1)


def _rows_view_put(ref, i, r, tiles, nb, per_seq, val):
    for n, dt in enumerate(tiles):
        ref[pl.ds(dt * r + i, nb, stride=per_seq), :] = val[:, n * LANES:(n + 1) * LANES]


def _steps(ref):
    return ref[...].reshape(DEC_SEQ * ref.shape[1], ref.shape[2])


def _smix_in_kernel(x_ref, sconv_ref, spool_ref, nmix_ref, wzx_ref, wvp_ref, wdt_ref, cw_ref, cb_ref, dtb_ref, a_ref, dexp_ref,
                    wpool_ref, pscale_ref,
                    xtm_ref, z_ref, ypart_ref, eacs_ref, xd_ref, c_ref, b_ref, dec_ref, pout_ref, conv_ref, pool_ref,
                    h_scr, pooled_scr):
    nb = S_SEQ_BLOCK
    lo = lax.broadcasted_iota(jnp.int32, (nb, LANES), 1) < SSM_HEAD_DIM
    x_tiles = range(D_MODEL // LANES)
    x_steps = [_rows_view_get(x_ref, l, DEC_SEQ, x_tiles, nb, X_ROWS_PER_SEQ) for l in range(DEC_SEQ)]
    for l in range(DEC_SEQ):
        xtm_ref[l] = x_steps[l]
    h_scr[...] = _rmsnorm(jnp.concatenate(x_steps, axis=0), nmix_ref[...]).astype(bf16)
    z = _dot(h_scr[...], wzx_ref[:, :D_SSM])
    for l in range(DEC_SEQ):
        z_ref[l] = z[l * nb:(l + 1) * nb]
    xbc = _dot(h_scr[...], wzx_ref[:, D_SSM:])
    vp = _dot(h_scr[...], wvp_ref[...])
    dtr = _dot(h_scr[...], wdt_ref[...])

    def conv_slot(i):
        if i < SSM_CONV - 1:
            return sconv_ref[i]
        return xbc[(i - SSM_CONV + 1) * nb:(i - SSM_CONV + 2) * nb]

    def pool_slot(i, cl):
        if i < POOL_HIST:
            return spool_ref[i, :, cl]
        return vp[(i - POOL_HIST) * nb:(i - POOL_HIST + 1) * nb, cl]

    xs, bm, cm, dt, acs = [], [], [], [], []
    for l in range(DEC_SEQ):
        acc = cb_ref[...] + cw_ref[0:1, :] * conv_slot(l)
        for k in range(1, SSM_CONV):
            acc = acc + cw_ref[k:k + 1, :] * conv_slot(l + k)
        act = _silu(acc)
        xs.append(act[:, :D_SSM])
        bm.append(act[:, D_SSM:D_SSM + D_BC])
        cm.append(act[:, D_SSM + D_BC:])
        b_ref[l] = bm[l]
        c_ref[l] = cm[l]
        dt.append(_softplus(dtr[l * nb:(l + 1) * nb] + dtb_ref[...]))
        da = dt[l] * a_ref[...]
        acs.append(da if l == 0 else acs[l - 1] + da)
        for gi, w in enumerate(POOL_WINDOWS):
            gl = slice(gi * POOL_GROUP_DIM, (gi + 1) * POOL_GROUP_DIM)
            s = pool_slot(POOL_HIST + l, gl)
            for k in range(1, w):
                s = s + pool_slot(POOL_HIST + l - k, gl)
            cnt = float(min(PAST_LEN + l + 1, w))
            pooled_scr[l * nb:(l + 1) * nb, gl] = (s / cnt - pool_slot(POOL_HIST + l, gl)).astype(bf16)
    for i in range(SSM_CONV - 1):
        conv_ref[i] = conv_slot(DEC_SEQ + i)
    for i in range(POOL_HIST):
        pool_ref[i] = pool_slot(DEC_SEQ + i, slice(0, D_POOL))
    pout = jnp.concatenate(
        [_dot(pooled_scr[:, gi * POOL_GROUP_DIM:(gi + 1) * POOL_GROUP_DIM], wpool_ref[gi])
         for gi in range(len(POOL_WINDOWS))], axis=1) * pscale_ref[...]
    for l in range(DEC_SEQ):
        pout_ref[l] = pout[l * nb:(l + 1) * nb]

    xdt = [xs[l] * _expand_heads(dt[l], lo) for l in range(DEC_SEQ)]
    gw = D_SSM // SSM_GROUPS
    for l in range(DEC_SEQ):
        y = xs[l] * dexp_ref[...]
        for s in range(l + 1):
            decay = _expand_heads(jnp.exp(acs[l] - acs[s]), lo)
            cbs = [jnp.sum(cm[l][:, g * SSM_STATE:(g + 1) * SSM_STATE] * bm[s][:, g * SSM_STATE:(g + 1) * SSM_STATE],
                           axis=-1, keepdims=True) for g in range(SSM_GROUPS)]
            coef = jnp.concatenate([decay[:, g * gw:(g + 1) * gw] * cbs[g] for g in range(SSM_GROUPS)], axis=1)
            y = y + coef * xdt[s]
        ypart_ref[l] = y
        eacs_ref[l] = _expand_heads(jnp.exp(acs[l]), lo)
        xd_ref[l] = xdt[l] * _expand_heads(jnp.exp(acs[DEC_SEQ - 1] - acs[l]), lo)
    dec_ref[...] = jnp.exp(acs[DEC_SEQ - 1])


def _smix_in(x_rows, sconv, spool, w):
    ns = sconv.shape[1]
    nb = S_SEQ_BLOCK
    tmaj = lambda steps, width: pl.BlockSpec((steps, nb, width), lambda i: (0, i, 0))
    step_outs = [D_MODEL, D_SSM, D_SSM, D_SSM, D_SSM, D_BC, D_BC]
    out_specs = [tmaj(DEC_SEQ, wd) for wd in step_outs] + [pl.BlockSpec((nb, LANES), lambda i: (i, 0)),
                                                          tmaj(DEC_SEQ, D_POOL),
                                                          tmaj(SSM_CONV - 1, D_XBC), tmaj(POOL_HIST, D_POOL)]
    out_shape = [jax.ShapeDtypeStruct((DEC_SEQ, ns, wd), f32) for wd in step_outs] + [
        jax.ShapeDtypeStruct((ns, LANES), f32), jax.ShapeDtypeStruct((DEC_SEQ, ns, D_POOL), f32),
        jax.ShapeDtypeStruct((SSM_CONV - 1, ns, D_XBC), f32), jax.ShapeDtypeStruct((POOL_HIST, ns, D_POOL), f32)]
    return pl.pallas_call(
        _smix_in_kernel,
        grid=(ns // nb,),
        in_specs=[pl.BlockSpec((nb * X_ROWS_PER_SEQ, LANES), lambda i: (i, 0)), tmaj(SSM_CONV - 1, D_XBC),
                  tmaj(POOL_HIST, D_POOL),
                  _const_spec((1, D_MODEL)), _const_spec((D_MODEL, D_SSM + D_XBC)), _const_spec((D_MODEL, D_POOL)),
                  _const_spec((D_MODEL, LANES)), _const_spec((SSM_CONV, D_XBC)),
                  _const_spec((1, D_XBC)), _const_spec((1, LANES)), _const_spec((1, LANES)), _const_spec((1, D_SSM)),
                  _const_spec((len(POOL_WINDOWS), POOL_GROUP_DIM, POOL_GROUP_DIM)), _const_spec((1, D_POOL))],
        out_specs=out_specs,
        out_shape=out_shape,
        scratch_shapes=[pltpu.VMEM((DEC_SEQ * nb, D_MODEL), bf16), pltpu.VMEM((DEC_SEQ * nb, D_POOL), bf16)],
        compiler_params=pltpu.CompilerParams(dimension_semantics=("arbitrary",), vmem_limit_bytes=VMEM_LIMIT_BYTES),
        name="smix_in",
    )(x_rows, sconv, spool, w["nmix"], w["wzx"], w["wvp"], w["wdt"], w["cw"], w["cb"], w["dtb"], w["a_row"], w["dexp"], w["wpool"],
      w["pscale"])


def _sssd_kernel(dec_ref, c_ref, b_ref, xd_ref, st_ref, yoff_ref, stn_ref):
    blk = pl.program_id(0)
    nb = S_SSD_BLOCK
    gw = D_SSM // SSM_GROUPS
    hpg = SSM_HEADS // SSM_GROUPS
    row_seq = lax.broadcasted_iota(jnp.int32, (DEC_SEQ * nb, gw), 0) % nb
    cmat, bmat, xd = _steps(c_ref), _steps(b_ref), _steps(xd_ref)
    for g in range(SSM_GROUPS):
        gl = slice(g * gw, (g + 1) * gw)
        cg = cmat[:, g * SSM_STATE:(g + 1) * SSM_STATE].astype(bf16)
        bg = bmat[:, g * SSM_STATE:(g + 1) * SSM_STATE].astype(bf16)
        yo = jnp.zeros((DEC_SEQ * nb, gw), f32)
        for j in range(nb):
            mine = row_seq == j
            h0 = st_ref[j, gl, :]
            yo = jnp.where(mine, _dot_nt(cg, h0.astype(bf16)), yo)
            upd = _dot_tn(jnp.where(mine, xd[:, gl], 0.0).astype(bf16), bg)
            for hh in range(hpg):
                hr = slice(hh * SSM_HEAD_DIM, (hh + 1) * SSM_HEAD_DIM)
                d = dec_ref[(blk * nb + j) * SSM_HEADS + g * hpg + hh]
                stn_ref[j, g * gw + hh * SSM_HEAD_DIM:g * gw + (hh + 1) * SSM_HEAD_DIM, :] = h0[hr] * d + upd[hr]
        for l in range(DEC_SEQ):
            yoff_ref[l, :, gl] = yo[l * nb:(l + 1) * nb]


def _sssd(dec_flat, cmat, bmat, xd, state):
    ns = state.shape[0]
    nb = S_SSD_BLOCK
    tmaj = lambda width: pl.BlockSpec((DEC_SEQ, nb, width), lambda i: (0, i, 0))
    stblk = pl.BlockSpec((nb, D_SSM, SSM_STATE), lambda i: (i, 0, 0))
    return pl.pallas_call(
        _sssd_kernel,
        grid=(ns // nb,),
        in_specs=[pl.BlockSpec(memory_space=pltpu.SMEM), tmaj(D_BC), tmaj(D_BC), tmaj(D_SSM), stblk],
        out_specs=[tmaj(D_SSM), stblk],
        out_shape=[jax.ShapeDtypeStruct((DEC_SEQ, ns, D_SSM), f32), jax.ShapeDtypeStruct(state.shape, f32)],
        compiler_params=pltpu.CompilerParams(dimension_semantics=("arbitrary",), vmem_limit_bytes=VMEM_LIMIT_BYTES),
        name="sssd",
    )(dec_flat, cmat, bmat, xd, state)


def _smix_out_kernel(x_ref, ypart_ref, yoff_ref, eacs_ref, z_ref, pout_ref, snorm_ref, wout_ref, nmem_ref, wq_ref,
                     x1_ref, q_ref, cat_scr):
    y = ypart_ref[...] + yoff_ref[...] * eacs_ref[...]
    t = y * _silu(z_ref[...])
    gw = D_SSM // SSM_GROUPS
    for g in range(SSM_GROUPS):
        gl = slice(g * gw, (g + 1) * gw)
        tg = t[:, gl]
        ms = jnp.mean(tg * tg, axis=-1, keepdims=True)
        cat_scr[:, gl] = (tg * lax.rsqrt(ms + EPS) * snorm_ref[:, gl]).astype(bf16)
    cat_scr[:, D_SSM:] = pout_ref[...].astype(bf16)
    x1 = x_ref[...] + _dot(cat_scr[...], wout_ref[...])
    x1_ref[...] = x1
    h = _rmsnorm(x1, nmem_ref[...]).astype(bf16)
    q_ref[...] = _dot(h, wq_ref[...]) * (MEM_HEAD_DIM ** -0.5)


def _smix_out(x, ypart, yoff, eacs, z, pout, w):
    n = x.shape[0]
    rb = 128
    rows = lambda width: pl.BlockSpec((rb, width), lambda i: (i, 0))
    return pl.pallas_call(
        _smix_out_kernel,
        grid=(n // rb,),
        in_specs=[rows(D_MODEL), rows(D_SSM), rows(D_SSM), rows(D_SSM), rows(D_SSM), rows(D_POOL),
                  _const_spec((1, D_SSM)), _const_spec((D_SSM + D_POOL, D_MODEL)), _const_spec((1, D_MODEL)),
                  _const_spec((D_MODEL, D_MODEL))],
        out_specs=[rows(D_MODEL), rows(D_MODEL)],
        out_shape=[jax.ShapeDtypeStruct((n, D_MODEL), f32), jax.ShapeDtypeStruct((n, D_MODEL), f32)],
        scratch_shapes=[pltpu.VMEM((rb, D_SSM + D_POOL), bf16)],
        compiler_params=pltpu.CompilerParams(dimension_semantics=("arbitrary",), vmem_limit_bytes=VMEM_LIMIT_BYTES),
        name="smix_out",
    )(x, ypart, yoff, eacs, z, pout, w["snorm"], w["wout"], w["nmem"], w["wq"])


def _sattn_kernel(q_ref, k_ref, v_ref, o_ref):
    nb = S_ATT_BLOCK
    rg = DEC_SEQ * nb
    rows = MEM_HEADS * rg

    def row_ids(width):
        r = lax.broadcasted_iota(jnp.int32, (rows, width), 0)
        return r // rg, r % nb

    row_h, row_seq = row_ids(D_MODEL)
    col_h = lax.broadcasted_iota(jnp.int32, (rows, D_MODEL), 1) // MEM_HEAD_DIM
    q = _steps(q_ref)
    qh = jnp.where(row_h == col_h, jnp.concatenate([q] * MEM_HEADS, axis=0), 0.0)
    lhs_s = jnp.concatenate([jnp.where(row_seq == b, qh, 0.0).astype(bf16) for b in range(nb)], axis=1)
    kcat = jnp.concatenate([_kv_seq(k_ref, b) for b in range(nb)], axis=1)
    s = _dot_nt(lhs_s, kcat)
    p = jnp.exp(s - jnp.max(s, axis=-1, keepdims=True))
    p = p / jnp.sum(p, axis=-1, keepdims=True)
    _, row_seq_p = row_ids(N_MEM)
    lhs_p = jnp.concatenate([jnp.where(row_seq_p == b, p, 0.0).astype(bf16) for b in range(nb)], axis=1)
    vcat = jnp.concatenate([_kv_seq(v_ref, b) for b in range(nb)], axis=0)
    res = _dot(lhs_p, vcat)
    for hd in range(MEM_HEADS):
        hl = slice(hd * MEM_HEAD_DIM, (hd + 1) * MEM_HEAD_DIM)
        for l in range(DEC_SEQ):
            o_ref[l, :, hl] = res[hd * rg + l * nb:hd * rg + (l + 1) * nb, hl]


def _sattn(q, mem_k, mem_v):
    ns = mem_k.shape[0]
    nb = S_ATT_BLOCK
    qblk = pl.BlockSpec((DEC_SEQ, nb, D_MODEL), lambda i: (0, i, 0))
    kvblk = pl.BlockSpec((nb, N_MEM * KV_ROWS_PER_TOKEN, LANES), lambda i: (i, 0, 0))
    return pl.pallas_call(
        _sattn_kernel,
        grid=(ns // nb,),
        in_specs=[qblk, kvblk, kvblk],
        out_specs=qblk,
        out_shape=jax.ShapeDtypeStruct((DEC_SEQ, ns, D_MODEL), f32),
        compiler_params=pltpu.CompilerParams(dimension_semantics=("arbitrary",), vmem_limit_bytes=VMEM_LIMIT_BYTES),
        name="sattn",
    )(q, mem_k, mem_v)


def _sffn_kernel(x1_ref, ao_ref, sffn_ref, wo_ref, g_ref, wup_ref, cw_ref, cb_ref, wdn_ref, gfin_ref,
                 y_ref, st_ref, h_scr, act_scr):
    nb = S_SEQ_BLOCK
    x2 = _steps(x1_ref) + _dot(_steps(ao_ref).astype(bf16), wo_ref[...])
    h_scr[...] = _rmsnorm(x2, g_ref[...]).astype(bf16)
    u = _dot(h_scr[...], wup_ref[...])
    hist = FFN_CONV - 1

    def conv_block(blk):
        cols = slice(blk * FF_CHUNK, (blk + 1) * FF_CHUNK)
        tiles = range(blk * FF_CHUNK // LANES, (blk + 1) * FF_CHUNK // LANES)
        slots = [sffn_ref[:, i, cols] for i in range(hist)]
        slots += [u[l * nb:(l + 1) * nb, cols] for l in range(DEC_SEQ)]
        for i in range(hist):
            st_ref[:, i, cols] = slots[DEC_SEQ + i]
        outs = []
        for l in range(DEC_SEQ):
            acc = cb_ref[:, cols] + cw_ref[0:1, cols] * slots[l]
            for k in range(1, FFN_CONV):
                acc = acc + cw_ref[k:k + 1, cols] * slots[l + k]
            outs.append(acc)
        return jnp.concatenate(outs, axis=0)

    for j in range(N_FF_CHUNKS):
        act = _silu(conv_block(_gate_blk(j))) * conv_block(_val_blk(j))
        act_scr[:, j * FF_CHUNK:(j + 1) * FF_CHUNK] = act.astype(bf16)
    y = _rmsnorm(x2 + _dot(act_scr[...], wdn_ref[...]), gfin_ref[...])
    for l in range(DEC_SEQ):
        _rows_view_put(y_ref, l, DEC_SEQ, range(D_MODEL // LANES), nb, X_ROWS_PER_SEQ, y[l * nb:(l + 1) * nb])


def _sffn(x1, ao, sffn_rows, w):
    ns = x1.shape[1]
    nb = S_SEQ_BLOCK
    tok = pl.BlockSpec((DEC_SEQ, nb, D_MODEL), lambda i: (0, i, 0))
    yblk = pl.BlockSpec((nb * X_ROWS_PER_SEQ, LANES), lambda i: (i, 0))
    stblk = pl.BlockSpec((nb, FFN_CONV - 1, 2 * D_FF), lambda i: (i, 0, 0))
    return pl.pallas_call(
        _sffn_kernel,
        grid=(ns // nb,),
        in_specs=[tok, tok, stblk,
                  _const_spec((D_MODEL, D_MODEL)), _const_spec((1, D_MODEL)), _const_spec((D_MODEL, 2 * D_FF)),
                  _const_spec((FFN_CONV, 2 * D_FF)), _const_spec((1, 2 * D_FF)),
                  _const_spec((D_FF, D_MODEL)), _const_spec((1, D_MODEL))],
        out_specs=[yblk, stblk],
        out_shape=[jax.ShapeDtypeStruct((ns * X_ROWS_PER_SEQ, LANES), f32),
                   jax.ShapeDtypeStruct((ns, FFN_CONV - 1, 2 * D_FF), f32)],
        scratch_shapes=[pltpu.VMEM((DEC_SEQ * nb, D_MODEL), bf16), pltpu.VMEM((DEC_SEQ * nb, D_FF), bf16)],
        compiler_params=pltpu.CompilerParams(dimension_semantics=("arbitrary",), vmem_limit_bytes=VMEM_LIMIT_BYTES),
        name="sffn",
    )(x1, ao, sffn_rows, w["wo"], w["nffn"], w["wup"], w["fcw"], w["fcb"], w["wdn"], w["gfin"])


def _sample_path(x_sample, state_ssm, state_ssm_conv, state_pool, state_ffn_conv, cache_mem_k, cache_mem_v, w):
    ns = x_sample.shape[0]
    xtm, z, ypart, eacs, xd, cmat, bmat, dec, pout, conv_new, pool_new = _smix_in(
        _tiled_rows_view(x_sample), state_ssm_conv[0].transpose(1, 0, 2), state_pool[0].transpose(1, 0, 2), w)
    yoff, ssm_new = _sssd(dec[:, :SSM_HEADS].reshape(-1), cmat, bmat, xd, state_ssm[0].reshape(ns, D_SSM, SSM_STATE))
    flat = lambda a: a.reshape(DEC_SEQ * ns, a.shape[-1])
    x1, q = _smix_out(flat(xtm), flat(ypart), flat(yoff), flat(eacs), flat(z), flat(pout), w)
    ao = _sattn(q.reshape(DEC_SEQ, ns, D_MODEL), _kv_rows_view(cache_mem_k[0]), _kv_rows_view(cache_mem_v[0]))
    y_rows, ffn_new = _sffn(x1.reshape(DEC_SEQ, ns, D_MODEL), ao, state_ffn_conv[0], w)
    return (_from_tiled_rows(y_rows, ns, DEC_SEQ, D_MODEL),
            ssm_new.reshape(1, ns, SSM_HEADS, SSM_HEAD_DIM, SSM_STATE),
            conv_new.transpose(1, 0, 2)[None], pool_new.transpose(1, 0, 2)[None],
            ffn_new[None])


def kernel(x_prompt, x_sample, mem_prompt, state_ssm, state_ssm_conv, state_pool, state_ffn_conv, cache_mem_k, cache_mem_v, norm_mix, w_in, ssm_conv_w, ssm_conv_b, ssm_dt_bias, ssm_a_log, ssm_d, ssm_norm, w_pool, pool_scale, w_out, norm_mem, norm_memkv, w_mq, w_mk, w_mv, w_mo, norm_ffn, w_up, ffn_conv_w, ffn_conv_b, w_down, final_norm):
    params = dict(norm_mix=norm_mix, w_in=w_in, ssm_conv_w=ssm_conv_w, ssm_conv_b=ssm_conv_b, ssm_dt_bias=ssm_dt_bias,
                  ssm_a_log=ssm_a_log, ssm_d=ssm_d, ssm_norm=ssm_norm, w_pool=w_pool, pool_scale=pool_scale,
                  w_out=w_out, norm_mem=norm_mem, norm_memkv=norm_memkv, w_mq=w_mq, w_mk=w_mk, w_mv=w_mv, w_mo=w_mo,
                  norm_ffn=norm_ffn, w_up=w_up, ffn_conv_w=ffn_conv_w, ffn_conv_b=ffn_conv_b, w_down=w_down,
                  final_norm=final_norm)
    w = _prep_weights(params)
    yp, ssm_p, conv_p, pool_p, ffn_p, mk_p, mv_p = _prompt_path(x_prompt, mem_prompt, w, PROMPT_TILE)
    ys, ssm_s, conv_s, pool_s, ffn_s = _sample_path(x_sample, state_ssm, state_ssm_conv, state_pool, state_ffn_conv,
                                                    cache_mem_k, cache_mem_v, w)
    return yp, ys, ssm_p, ssm_s, conv_p, conv_s, pool_p, pool_s, ffn_p, ffn_s, mk_p, mv_p
```

```python
import functools

import jax
import jax.numpy as jnp
from jax import lax
from jax.experimental import pallas as pl
from jax.experimental.pallas import tpu as pltpu

f32 = jnp.float32
bf16 = jnp.bfloat16

D_MODEL = 1024
SSM_HEADS = 16
SSM_HEAD_DIM = 64
SSM_STATE = 128
SSM_GROUPS = 2
SSM_CHUNK = 128
D_SSM = 1024
D_BC = SSM_GROUPS * SSM_STATE
D_XBC = D_SSM + 2 * D_BC
SSM_CONV = 4
D_POOL = 1024
POOL_WINDOWS = (2, 4, 8, 16)
POOL_GROUP_DIM = 256
POOL_HIST = 15
N_MEM = 256
MEM_HEADS = 4
MEM_HEAD_DIM = 256
D_FF = 2816
FFN_CONV = 3
EPS = 1e-6
PAST_LEN = 16384

LANES = 128
SUBLANES = 8
MXU_DIM = 256
VMEM_LIMIT_BYTES = 56 * 1024 * 1024

CONV_HIST_ROWS = SUBLANES
POOL_HIST_ROWS = 2 * SUBLANES
FF_CHUNK = MXU_DIM
N_FF_CHUNKS = D_FF // FF_CHUNK


def _silu(v):
    half = 0.5 * v
    return half + half * jnp.tanh(half)


def _silu_gate(g, v):
    return g * v * (1.0 / (1.0 + jnp.exp(-g)))


def _softplus(v):
    return jnp.maximum(v, 0.0) + jnp.log1p(jnp.exp(-jnp.abs(v)))


def _rmsnorm(x, g):
    ms = jnp.mean(x * x, axis=-1, keepdims=True)
    return x * lax.rsqrt(ms + EPS) * g


def _dot(a, b):
    return jnp.dot(a, b, preferred_element_type=f32)


def _dot_nt(a, b):
    return lax.dot_general(a, b, (((1,), (1,)), ((), ())), preferred_element_type=f32)


def _dot_tn(a, b):
    return lax.dot_general(a, b, (((0,), (0,)), ((), ())), preferred_element_type=f32)


def _split3(v):
    p1 = v.astype(bf16)
    r1 = v - p1.astype(f32)
    p2 = r1.astype(bf16)
    r2 = r1 - p2.astype(f32)
    return p1, p2, r2.astype(bf16)


def _const_spec(shape):
    return pl.BlockSpec(shape, lambda *_: (0,) * len(shape), pipeline_mode=pl.Buffered(1))


KV_LANE_TILES = MEM_HEAD_DIM // LANES
KV_ROWS_PER_TOKEN = KV_LANE_TILES * MEM_HEADS


def _kv_rows_view(kv):
    ns = kv.shape[0]
    kv = kv.reshape(ns, N_MEM, MEM_HEADS, KV_LANE_TILES, LANES).transpose(0, 1, 3, 2, 4)
    return kv.reshape(ns, N_MEM * KV_ROWS_PER_TOKEN, LANES)


def _kv_from_rows(rows):
    ns = rows.shape[0]
    kv = rows.reshape(ns, N_MEM, KV_LANE_TILES, MEM_HEADS, LANES).transpose(0, 1, 3, 2, 4)
    return kv.reshape(ns, N_MEM, MEM_HEADS, MEM_HEAD_DIM)


def _kv_seq(ref, jj):
    tiles = [ref[jj, pl.ds(dt * MEM_HEADS + hd, N_MEM, stride=KV_ROWS_PER_TOKEN), :]
             for hd in range(MEM_HEADS) for dt in range(KV_LANE_TILES)]
    return jnp.concatenate(tiles, axis=1).astype(bf16)


def _memkv_kernel(mem_ref, g_ref, wk_ref, wv_ref, k_ref, v_ref):
    h = _rmsnorm(mem_ref[0], g_ref[...]).astype(bf16)
    for out_ref, w_ref in ((k_ref, wk_ref), (v_ref, wv_ref)):
        kv = _dot(h, w_ref[...])
        for hd in range(MEM_HEADS):
            for dt in range(KV_LANE_TILES):
                col = hd * MEM_HEAD_DIM + dt * LANES
                out_ref[0, pl.ds(dt * MEM_HEADS + hd, N_MEM, stride=KV_ROWS_PER_TOKEN), :] = kv[:, col:col + LANES]


def _memkv(mem, g, wk, wv):
    b = mem.shape[0]
    blk = pl.BlockSpec((1, N_MEM, D_MODEL), lambda i: (i, 0, 0))
    oblk = pl.BlockSpec((1, N_MEM * KV_ROWS_PER_TOKEN, LANES), lambda i: (i, 0, 0))
    return pl.pallas_call(
        _memkv_kernel,
        grid=(b,),
        in_specs=[blk, _const_spec((1, D_MODEL)), _const_spec((D_MODEL, D_MODEL)), _const_spec((D_MODEL, D_MODEL))],
        out_specs=[oblk, oblk],
        out_shape=[jax.ShapeDtypeStruct((b, N_MEM * KV_ROWS_PER_TOKEN, LANES), f32)] * 2,
        compiler_params=pltpu.CompilerParams(dimension_semantics=("arbitrary",), vmem_limit_bytes=VMEM_LIMIT_BYTES),
        name="memkv",
    )(mem, g, wk, wv)


def _ssd_chunk(r0, dt_scr, xs_scr, b_scr, c_scr, y_scr, xd_scr, hst_scr, a_ref, dexp_ref, between=()):
    between = list(between) + [None] * 3
    q = SSM_CHUNK
    rows = pl.ds(r0, q)
    row_i = lax.broadcasted_iota(jnp.int32, (q, q), 0)
    col_i = lax.broadcasted_iota(jnp.int32, (q, q), 1)
    causal = col_i <= row_i
    lo = col_i < SSM_HEAD_DIM
    tril = jnp.where(causal, 1.0, 0.0).astype(bf16)

    dt = dt_scr[rows, :]
    da = dt * a_ref[...]
    p1, p2, p3 = _split3(da)
    acs = _dot(tril, p1) + _dot(tril, p2) + _dot(tril, p3)
    acs_t = acs.T
    if between[0] is not None:
        between[0]()

    for g in range(SSM_GROUPS):
        bg = b_scr[rows, g * SSM_STATE:(g + 1) * SSM_STATE]
        cg = c_scr[rows, g * SSM_STATE:(g + 1) * SSM_STATE]
        bg_b = bg.astype(bf16)
        cb = jnp.where(causal, _dot_nt(cg.astype(bf16), bg_b), 0.0)
        cdec_rows = []
        pairs_per_group = SSM_HEADS // SSM_GROUPS // 2
        for pp in range(pairs_per_group):
            pr = g * pairs_per_group + pp
            lanes = slice(pr * LANES, (pr + 1) * LANES)
            lhs, dtb, dend, cdec = [], [], [], []
            for hh in (2 * pr, 2 * pr + 1):
                colb = jnp.broadcast_to(acs[:, hh:hh + 1], (q, q))
                seg = jnp.where(causal, colb - acs_t[hh:hh + 1, :], 0.0)
                lhs.append((jnp.exp(seg) * cb).astype(bf16))
                lhs.append((cg * jnp.exp(colb)).astype(bf16))
                last = colb[q - 1:q, :]
                dend.append(jnp.exp(last - colb))
                cdec.append(jnp.exp(last))
                dtb.append(jnp.broadcast_to(dt[:, hh:hh + 1], (q, q)))
            xs_pair = xs_scr[rows, lanes]
            xdt = xs_pair * jnp.where(lo, dtb[0], dtb[1])
            xd_scr[:, lanes] = (xdt * jnp.where(lo, dend[0], dend[1])).astype(bf16)
            hst_pair = hst_scr[:, lanes]
            rhs = jnp.concatenate([
                jnp.where(lo, xdt, 0.0).astype(bf16), jnp.where(lo, hst_pair, 0.0).astype(bf16),
                jnp.where(lo, 0.0, xdt).astype(bf16), jnp.where(lo, 0.0, hst_pair).astype(bf16)], axis=0)
            y_pair = _dot(jnp.concatenate(lhs, axis=1), rhs)
            y_scr[rows, lanes] = y_pair + xs_pair * dexp_ref[:, lanes]
            cdec_rows.append(jnp.where(lo[:1], cdec[0], cdec[1]))
        gl = slice(g * (D_SSM // SSM_GROUPS), (g + 1) * (D_SSM // SSM_GROUPS))
        upd = _dot_tn(bg_b, xd_scr[:, gl])
        hst_scr[:, gl] = hst_scr[:, gl] * jnp.concatenate(cdec_rows, axis=1) + upd
        if between[1 + g] is not None:
            between[1 + g]()


def _mixer_kernel(x_ref, nmix_ref, wzx_ref, wvp_ref, wdt_ref, cw_ref, cb_ref, dtb_ref, a_ref, dexp_ref, snorm_ref, wpool_ref,
                  pscale_ref, wout_ref,
                  x1_ref, ssm_ref, conv_ref, pool_ref,
                  xd_scr, hst_scr, *sub_scr, tile, sub):
    t = pl.program_id(1)
    last_t = pl.num_programs(1) - 1
    ch = CONV_HIST_ROWS
    ph = POOL_HIST_ROWS
    n_sub = tile // sub
    per = len(sub_scr) // n_sub
    bufs = [sub_scr[i * per:(i + 1) * per] for i in range(n_sub)]
    xbc0, vp0 = bufs[0][2], bufs[0][3]

    @pl.when(t == 0)
    def _():
        xbc0[:, 0:ch, :] = jnp.zeros((D_XBC // LANES, ch, LANES), f32)
        vp0[:, 0:ph, :] = jnp.zeros((D_POOL // LANES, ph, LANES), f32)
        hst_scr[...] = jnp.zeros_like(hst_scr)

    gw = D_SSM // SSM_GROUPS

    def carry_history(src, dst):
        for j in range(D_XBC // LANES):
            bufs[dst][2][j, 0:ch, :] = bufs[src][2][j, sub:sub + ch, :]
        for j in range(D_POOL // LANES):
            bufs[dst][3][j, 0:ph, :] = bufs[src][3][j, sub:sub + ph, :]

    def in_xbc(s):
        h_scr, _, xbc_scr = bufs[s][:3]
        h_scr[...] = _rmsnorm(x_ref[0, s * sub:(s + 1) * sub, :], nmix_ref[...]).astype(bf16)
        xbc = _dot(h_scr[...], wzx_ref[:, D_SSM:])
        for j in range(D_XBC // LANES):
            xbc_scr[j, ch:ch + sub, :] = xbc[:, j * LANES:(j + 1) * LANES]

    def in_dtvp(s):
        h_scr, _, _, vp_scr, dt_scr = bufs[s][:5]
        dt_scr[...] = _dot(h_scr[...], wdt_ref[...])
        vp = _dot(h_scr[...], wvp_ref[...])
        for j in range(D_POOL // LANES):
            vp_scr[j, ph:ph + sub, :] = vp[:, j * LANES:(j + 1) * LANES]

    def in_z(s):
        h_scr, z_scr = bufs[s][:2]
        z_scr[...] = _dot(h_scr[...], wzx_ref[:, :D_SSM])
        if s > 0:
            carry_history(s - 1, s)

    def stage_mid(s, between):
        _, z_scr, xbc_scr, vp_scr, dt_scr, xs_scr, b_scr, c_scr, y_scr, pooled_scr, cat_scr = bufs[s]
        for j in range(D_XBC // LANES):
            cl = slice(j * LANES, (j + 1) * LANES)
            acc = cb_ref[:, cl] + cw_ref[0:1, cl] * xbc_scr[j, pl.ds(ch - 3, sub), :]
            for k in range(1, SSM_CONV):
                acc = acc + cw_ref[k:k + 1, cl] * xbc_scr[j, pl.ds(ch - 3 + k, sub), :]
            act = _silu(acc)
            if j < D_SSM // LANES:
                xs_scr[:, cl] = act
            elif j < (D_SSM + D_BC) // LANES:
                b_scr[:, j * LANES - D_SSM:(j + 1) * LANES - D_SSM] = act
            else:
                c_scr[:, j * LANES - D_SSM - D_BC:(j + 1) * LANES - D_SSM - D_BC] = act

        pos1 = lax.broadcasted_iota(jnp.int32, (sub, LANES), 0) + (t * tile + s * sub + 1)
        for j in range(D_POOL // LANES):
            cl = slice(j * LANES, (j + 1) * LANES)
            w = POOL_WINDOWS[j * LANES // POOL_GROUP_DIM]
            cur = vp_scr[j, ph:ph + sub, :]
            acc = cur
            for k in range(1, w):
                acc = acc + vp_scr[j, pl.ds(ph - k, sub), :]
            cnt = jnp.minimum(pos1, w).astype(f32)
            pooled_scr[:, cl] = (acc / cnt - cur).astype(bf16)
        for gi in range(len(POOL_WINDOWS)):
            gl = slice(gi * POOL_GROUP_DIM, (gi + 1) * POOL_GROUP_DIM)
            pg = _dot(pooled_scr[:, gl], wpool_ref[gi]) * pscale_ref[:, gl]
            cat_scr[:, D_SSM + gi * POOL_GROUP_DIM:D_SSM + (gi + 1) * POOL_GROUP_DIM] = pg.astype(bf16)

        dt_scr[...] = _softplus(dt_scr[...] + dtb_ref[...])
        n_chunks = sub // SSM_CHUNK
        slots = [None] * (3 * n_chunks)
        for i, piece in enumerate(between):
            slots[i * len(slots) // len(between)] = piece
        for c in range(n_chunks):
            _ssd_chunk(c * SSM_CHUNK, dt_scr, xs_scr, b_scr, c_scr, y_scr, xd_scr, hst_scr, a_ref, dexp_ref,
                       slots[3 * c:3 * c + 3])

        for g in range(SSM_GROUPS):
            gl = slice(g * gw, (g + 1) * gw)
            tg = y_scr[:, gl] * _silu(z_scr[:, gl])
            ms = jnp.mean(tg * tg, axis=-1, keepdims=True)
            cat_scr[:, gl] = (tg * lax.rsqrt(ms + EPS) * snorm_ref[:, gl]).astype(bf16)

    def stage_out(s):
        rows = slice(s * sub, (s + 1) * sub)
        x1_ref[0, rows, :] = x_ref[0, rows, :] + _dot(bufs[s][-1][...], wout_ref[...])

    in_xbc(0)
    in_dtvp(0)
    in_z(0)
    for s in range(n_sub):
        nxt = s + 1 < n_sub
        if nxt:
            in_xbc(s + 1)
        pieces = [functools.partial(in_dtvp, s + 1), functools.partial(in_z, s + 1)] if nxt else []
        if s > 0:
            pieces.append(functools.partial(stage_out, s - 1))
        stage_mid(s, pieces)
    stage_out(n_sub - 1)
    carry_history(n_sub - 1, 0)

    @pl.when(t == last_t)
    def _():
        for pr in range(D_SSM // LANES):
            ssm_ref[0, pr * LANES:(pr + 1) * LANES, :] = hst_scr[:, pr * LANES:(pr + 1) * LANES].T
        for j in range(D_XBC // LANES):
            conv_ref[0, :, j * LANES:(j + 1) * LANES] = xbc0[j, pl.ds(ch - (SSM_CONV - 1), SSM_CONV - 1), :]
        for j in range(D_POOL // LANES):
            pool_ref[0, :, j * LANES:(j + 1) * LANES] = vp0[j, pl.ds(ph - POOL_HIST, POOL_HIST), :]


def _mixer_prompt(x, nmix, wzx, wvp, wdt, cw, cb, dtb, a_row, dexp, snorm, wpool, pscale, wout, tile):
    b, seq, _ = x.shape
    nt = seq // tile
    sub = min(tile, MIXER_SUB_TILE)
    xblk = pl.BlockSpec((1, tile, D_MODEL), lambda i, j: (i, j, 0))
    sub_scratch = [
        pltpu.VMEM((sub, D_MODEL), bf16),
        pltpu.VMEM((sub, D_SSM), f32),
        pltpu.VMEM((D_XBC // LANES, CONV_HIST_ROWS + sub, LANES), f32),
        pltpu.VMEM((D_POOL // LANES, POOL_HIST_ROWS + sub, LANES), f32),
        pltpu.VMEM((sub, LANES), f32),
        pltpu.VMEM((sub, D_SSM), f32),
        pltpu.VMEM((sub, D_BC), f32),
        pltpu.VMEM((sub, D_BC), f32),
        pltpu.VMEM((sub, D_SSM), f32),
        pltpu.VMEM((sub, D_POOL), bf16),
        pltpu.VMEM((sub, D_SSM + D_POOL), bf16),
    ]
    scratch = [pltpu.VMEM((SSM_CHUNK, D_SSM), bf16),
               pltpu.VMEM((SSM_STATE, D_SSM), f32)]
    scratch += sub_scratch * (tile // sub)
    return pl.pallas_call(
        functools.partial(_mixer_kernel, tile=tile, sub=sub),
        grid=(b, nt),
        in_specs=[xblk, _const_spec((1, D_MODEL)), _const_spec((D_MODEL, D_SSM + D_XBC)),
                  _const_spec((D_MODEL, D_POOL)), _const_spec((D_MODEL, LANES)), _const_spec((SSM_CONV, D_XBC)),
                  _const_spec((1, D_XBC)), _const_spec((1, LANES)), _const_spec((1, LANES)), _const_spec((1, D_SSM)),
                  _const_spec((1, D_SSM)), _const_spec((len(POOL_WINDOWS), POOL_GROUP_DIM, POOL_GROUP_DIM)),
                  _const_spec((1, D_POOL)), _const_spec((D_SSM + D_POOL, D_MODEL))],
        out_specs=[xblk,
                   pl.BlockSpec((1, D_SSM, SSM_STATE), lambda i, j: (i, 0, 0)),
                   pl.BlockSpec((1, SSM_CONV - 1, D_XBC), lambda i, j: (i, 0, 0)),
                   pl.BlockSpec((1, POOL_HIST, D_POOL), lambda i, j: (i, 0, 0))],
        out_shape=[jax.ShapeDtypeStruct((b, seq, D_MODEL), f32),
                   jax.ShapeDtypeStruct((b, D_SSM, SSM_STATE), f32),
                   jax.ShapeDtypeStruct((b, SSM_CONV - 1, D_XBC), f32),
                   jax.ShapeDtypeStruct((b, POOL_HIST, D_POOL), f32)],
        scratch_shapes=scratch,
        compiler_params=pltpu.CompilerParams(dimension_semantics=("arbitrary", "arbitrary"),
                                             vmem_limit_bytes=VMEM_LIMIT_BYTES),
        name="mixer_prompt",
    )(x, nmix, wzx, wvp, wdt, cw, cb, dtb, a_row, dexp, snorm, wpool, pscale, wout)


def _attn_kernel(x_ref, g_ref, wq_ref, k_ref, v_ref, wo_ref, o_ref, q_scr, ao_scr):
    tile = x_ref.shape[1]
    sub = min(tile, ATTN_SUB_TILE)
    n_sub = tile // sub
    heads = [slice(hd * MEM_HEAD_DIM, (hd + 1) * MEM_HEAD_DIM) for hd in range(MEM_HEADS)]
    k = _kv_seq(k_ref, 0)
    v = _kv_seq(v_ref, 0)

    def q_proj(s):
        rows = slice(s * sub, (s + 1) * sub)
        h = _rmsnorm(x_ref[0, rows, :], g_ref[...]).astype(bf16)
        q_scr[rows, :] = (_dot(h, wq_ref[...]) * (MEM_HEAD_DIM ** -0.5)).astype(bf16)

    def scores(s):
        rows = slice(s * sub, (s + 1) * sub)
        return [_dot_nt(q_scr[rows, hl], k[:, hl]) for hl in heads]

    def values(s, sc):
        rows = slice(s * sub, (s + 1) * sub)
        for hl, s_h in zip(heads, sc):
            p = jnp.exp(s_h - jnp.max(s_h, axis=-1, keepdims=True))
            p = (p / jnp.sum(p, axis=-1, keepdims=True)).astype(bf16)
            ao_scr[rows, hl] = _dot(p, v[:, hl]).astype(bf16)

    def o_proj(s):
        rows = slice(s * sub, (s + 1) * sub)
        o_ref[0, rows, :] = x_ref[0, rows, :] + _dot(ao_scr[rows, :], wo_ref[...])

    q_proj(0)
    for s in range(n_sub):
        sc = scores(s)
        if s + 1 < n_sub:
            q_proj(s + 1)
        if s > 0:
            o_proj(s - 1)
        values(s, sc)
    o_proj(n_sub - 1)


def _attn_prompt(x, g, wq, mem_k, mem_v, wo, tile):
    b, seq, _ = x.shape
    xblk = pl.BlockSpec((1, tile, D_MODEL), lambda i, j: (i, j, 0))
    kvblk = pl.BlockSpec((1, N_MEM * KV_ROWS_PER_TOKEN, LANES), lambda i, j: (i, 0, 0))
    return pl.pallas_call(
        _attn_kernel,
        grid=(b, seq // tile),
        in_specs=[xblk, _const_spec((1, D_MODEL)), _const_spec((D_MODEL, D_MODEL)), kvblk, kvblk,
                  _const_spec((D_MODEL, D_MODEL))],
        out_specs=xblk,
        out_shape=jax.ShapeDtypeStruct((b, seq, D_MODEL), f32),
        scratch_shapes=[pltpu.VMEM((tile, D_MODEL), bf16), pltpu.VMEM((tile, D_MODEL), bf16)],
        compiler_params=pltpu.CompilerParams(dimension_semantics=("arbitrary", "arbitrary"),
                                             vmem_limit_bytes=VMEM_LIMIT_BYTES),
        name="attn_prompt",
    )(x, g, wq, mem_k, mem_v, wo)


def _gate_blk(j):
    return j


def _val_blk(j):
    return N_FF_CHUNKS + j


def _ffn_kernel(x_ref, g_ref, wup_ref, cw_ref, cb_ref, wdn_ref, gfin_ref, y_ref, st_ref, h_scr, u_scr, act_scr, *, tile):
    t = pl.program_id(1)
    last_t = pl.num_programs(1) - 1
    ch = CONV_HIST_ROWS
    tpc = FF_CHUNK // LANES
    ntile = 2 * D_FF // LANES

    @pl.when(t == 0)
    def _():
        u_scr[:, 0:ch, :] = jnp.zeros((ntile, ch, LANES), f32)

    sub = min(tile, FFN_SUB_TILE)
    n_sub = tile // sub

    def up_proj(s):
        rows = slice(s * sub, (s + 1) * sub)
        h_scr[rows, :] = _rmsnorm(x_ref[0, rows, :], g_ref[...]).astype(bf16)
        u = _dot(h_scr[rows, :], wup_ref[...])
        for ti in range(ntile):
            u_scr[ti, ch + s * sub:ch + (s + 1) * sub, :] = u[:, ti * LANES:(ti + 1) * LANES]

    def conv(s, ti):
        cl = slice(ti * LANES, (ti + 1) * LANES)
        r0 = ch + s * sub
        acc = cb_ref[:, cl] + cw_ref[0:1, cl] * u_scr[ti, pl.ds(r0 - 2, sub), :]
        acc = acc + cw_ref[1:2, cl] * u_scr[ti, pl.ds(r0 - 1, sub), :]
        return acc + cw_ref[2:3, cl] * u_scr[ti, r0:r0 + sub, :]

    def gate(s):
        rows = slice(s * sub, (s + 1) * sub)
        for j in range(N_FF_CHUNKS):
            for i in range(tpc):
                g_t = conv(s, _gate_blk(j) * tpc + i)
                v_t = conv(s, _val_blk(j) * tpc + i)
                act_scr[rows, (j * tpc + i) * LANES:(j * tpc + i + 1) * LANES] = _silu_gate(g_t, v_t).astype(bf16)

    def down_proj(s):
        rows = slice(s * sub, (s + 1) * sub)
        y_ref[0, rows, :] = _rmsnorm(x_ref[0, rows, :] + _dot(act_scr[rows, :], wdn_ref[...]), gfin_ref[...])

    up_proj(0)
    for s in range(n_sub):
        if s + 1 < n_sub:
            up_proj(s + 1)
        gate(s)
        down_proj(s)
    for ti in range(ntile):
        u_scr[ti, 0:ch, :] = u_scr[ti, tile:tile + ch, :]

    @pl.when(t == last_t)
    def _():
        for ti in range(ntile):
            st_ref[0, :, ti * LANES:(ti + 1) * LANES] = u_scr[ti, pl.ds(ch - (FFN_CONV - 1), FFN_CONV - 1), :]


def _ffn_prompt(x, g, wup, cw, cb, wdn, gfin, tile):
    b, seq, _ = x.shape
    xblk = pl.BlockSpec((1, tile, D_MODEL), lambda i, j: (i, j, 0))
    return pl.pallas_call(
        functools.partial(_ffn_kernel, tile=tile),
        grid=(b, seq // tile),
        in_specs=[xblk, _const_spec((1, D_MODEL)), _const_spec((D_MODEL, 2 * D_FF)),
                  _const_spec((FFN_CONV, 2 * D_FF)), _const_spec((1, 2 * D_FF)),
                  _const_spec((D_FF, D_MODEL)), _const_spec((1, D_MODEL))],
        out_specs=[xblk, pl.BlockSpec((1, FFN_CONV - 1, 2 * D_FF), lambda i, j: (i, 0, 0))],
        out_shape=[jax.ShapeDtypeStruct((b, seq, D_MODEL), f32),
                   jax.ShapeDtypeStruct((b, FFN_CONV - 1, 2 * D_FF), f32)],
        scratch_shapes=[pltpu.VMEM((tile, D_MODEL), bf16),
                        pltpu.VMEM((2 * D_FF // LANES, CONV_HIST_ROWS + tile, LANES), f32),
                        pltpu.VMEM((tile, D_FF), bf16)],
        compiler_params=pltpu.CompilerParams(dimension_semantics=("arbitrary", "arbitrary"),
                                             vmem_limit_bytes=VMEM_LIMIT_BYTES),
        name="ffn_prompt",
    )(x, g, wup, cw, cb, wdn, gfin)


def _split_in_proj_kernel(wt_ref, wzx_ref, wvp_ref, wdt_ref):
    n_zx = D_SSM + D_XBC
    for c0 in range(0, n_zx, D_MODEL):
        c1 = min(c0 + D_MODEL, n_zx)
        wzx_ref[:, c0:c1] = wt_ref[c0:c1, :].T.astype(bf16)
    lane = lax.broadcasted_iota(jnp.int32, (D_MODEL, LANES), 1)
    wdt_ref[...] = jnp.where(lane < SSM_HEADS, wt_ref[n_zx:n_zx + LANES, :].T, 0.0).astype(bf16)
    wvp_ref[...] = wt_ref[n_zx + SSM_HEADS:, :].T.astype(bf16)


def _split_in_proj(w_in):
    wt = w_in.T
    return pl.pallas_call(
        _split_in_proj_kernel,
        in_specs=[pl.BlockSpec(wt.shape, lambda: (0, 0))],
        out_shape=[jax.ShapeDtypeStruct((D_MODEL, D_SSM + D_XBC), bf16), jax.ShapeDtypeStruct((D_MODEL, D_POOL), bf16),
                   jax.ShapeDtypeStruct((D_MODEL, LANES), bf16)],
        compiler_params=pltpu.CompilerParams(vmem_limit_bytes=VMEM_LIMIT_BYTES),
        name="split_in_proj",
    )(wt)


def _prep_weights(p):
    i = 0
    wzx, wvp, wdt = _split_in_proj(p["w_in"][i])
    pad_h = (0, LANES - SSM_HEADS)
    return dict(
        nmix=p["norm_mix"][i][None], wzx=wzx, wvp=wvp, wdt=wdt, cw=p["ssm_conv_w"][i], cb=p["ssm_conv_b"][i][None],
        dtb=jnp.pad(p["ssm_dt_bias"][i], pad_h)[None],
        a_row=jnp.pad(-jnp.exp(p["ssm_a_log"][i].astype(f32)), pad_h)[None],
        dexp=jnp.repeat(p["ssm_d"][i], SSM_HEAD_DIM)[None], snorm=p["ssm_norm"][i][None],
        wpool=p["w_pool"][i].astype(bf16), pscale=p["pool_scale"][i][None], wout=p["w_out"][i].astype(bf16),
        nmem=p["norm_mem"][i][None], nmemkv=p["norm_memkv"][i][None],
        wq=p["w_mq"][i].astype(bf16), wk=p["w_mk"][i].astype(bf16), wv=p["w_mv"][i].astype(bf16),
        wo=p["w_mo"][i].astype(bf16),
        nffn=p["norm_ffn"][i][None],
        wup=p["w_up"][i].astype(bf16), fcw=p["ffn_conv_w"][i], fcb=p["ffn_conv_b"][i][None],
        wdn=p["w_down"][i].astype(bf16),
        gfin=p["final_norm"][None],
    )


def _prompt_path(x_prompt, mem_prompt, w, tile):
    b = x_prompt.shape[0]
    mem_k, mem_v = _memkv(mem_prompt, w["nmemkv"], w["wk"], w["wv"])
    x1, ssm, conv, pool = _mixer_prompt(x_prompt, w["nmix"], w["wzx"], w["wvp"], w["wdt"], w["cw"], w["cb"], w["dtb"], w["a_row"],
                                        w["dexp"], w["snorm"], w["wpool"], w["pscale"], w["wout"],
                                        min(MIXER_TILE, x_prompt.shape[1]))
    x2 = _attn_prompt(x1, w["nmem"], w["wq"], mem_k, mem_v, w["wo"], min(ATTN_TILE, x_prompt.shape[1]))
    y, ffn = _ffn_prompt(x2, w["nffn"], w["wup"], w["fcw"], w["fcb"], w["wdn"], w["gfin"], tile)
    return (y, ssm.reshape(1, b, SSM_HEADS, SSM_HEAD_DIM, SSM_STATE), conv[None], pool[None], ffn[None],
            _kv_from_rows(mem_k)[None], _kv_from_rows(mem_v)[None])


PROMPT_TILE = 512
ATTN_TILE = 1024
MIXER_TILE = 1024
MIXER_SUB_TILE = 256
ATTN_SUB_TILE = 256
FFN_SUB_TILE = 512


DEC_SEQ = 4
S_SEQ_BLOCK = 64
S_SSD_BLOCK = 16
S_ATT_BLOCK = 8
X_ROWS_PER_SEQ = DEC_SEQ * D_MODEL // LANES


def _expand_heads(v, lo):
    r = v.shape[0]
    tiles = []
    for pr in range(SSM_HEADS // 2):
        a = jnp.broadcast_to(v[:, 2 * pr:2 * pr + 1], (r, LANES))
        b = jnp.broadcast_to(v[:, 2 * pr + 1:2 * pr + 2], (r, LANES))
        tiles.append(jnp.where(lo, a, b))
    return jnp.concatenate(tiles, axis=1)


def _tiled_rows_view(a):
    ns, r, c = a.shape
    return a.reshape(ns, r, c // LANES, LANES).transpose(0, 2, 1, 3).reshape(ns * (c // LANES) * r, LANES)


def _from_tiled_rows(rows, ns, r, c):
    return rows.reshape(ns, c // LANES, r, LANES).transpose(0, 2, 1, 3).reshape(ns, r, c)


def _rows_view_get(ref, i, r, tiles, nb, per_seq):
    return jnp.concatenate([ref[pl.ds(dt * r + i, nb, stride=per_seq), :] for dt in tiles], axis=1)


def _rows_view_put(ref, i, r, tiles, nb, per_seq, val):
    for n, dt in enumerate(tiles):
        ref[pl.ds(dt * r + i, nb, stride=per_seq), :] = val[:, n * LANES:(n + 1) * LANES]


def _steps(ref):
    return ref[...].reshape(DEC_SEQ * ref.shape[1], ref.shape[2])


def _smix_in_kernel(x_ref, sconv_ref, spool_ref, nmix_ref, wzx_ref, wvp_ref, wdt_ref, cw_ref, cb_ref, dtb_ref, a_ref, dexp_ref,
                    wpool_ref, pscale_ref,
                    xtm_ref, z_ref, ypart_ref, eacs_ref, xd_ref, c_ref, b_ref, dec_ref, pout_ref, conv_ref, pool_ref,
                    h_scr, pooled_scr):
    nb = S_SEQ_BLOCK
    lo = lax.broadcasted_iota(jnp.int32, (nb, LANES), 1) < SSM_HEAD_DIM
    x_tiles = range(D_MODEL // LANES)
    x_steps = [_rows_view_get(x_ref, l, DEC_SEQ, x_tiles, nb, X_ROWS_PER_SEQ) for l in range(DEC_SEQ)]
    for l in range(DEC_SEQ):
        xtm_ref[l] = x_steps[l]
    h_scr[...] = _rmsnorm(jnp.concatenate(x_steps, axis=0), nmix_ref[...]).astype(bf16)
    z = _dot(h_scr[...], wzx_ref[:, :D_SSM])
    for l in range(DEC_SEQ):
        z_ref[l] = z[l * nb:(l + 1) * nb]
    xbc = _dot(h_scr[...], wzx_ref[:, D_SSM:])
    vp = _dot(h_scr[...], wvp_ref[...])
    dtr = _dot(h_scr[...], wdt_ref[...])

    def conv_slot(i):
        if i < SSM_CONV - 1:
            return sconv_ref[i]
        return xbc[(i - SSM_CONV + 1) * nb:(i - SSM_CONV + 2) * nb]

    def pool_slot(i, cl):
        if i < POOL_HIST:
            return spool_ref[i, :, cl]
        return vp[(i - POOL_HIST) * nb:(i - POOL_HIST + 1) * nb, cl]

    xs, bm, cm, dt, acs = [], [], [], [], []
    for l in range(DEC_SEQ):
        acc = cb_ref[...] + cw_ref[0:1, :] * conv_slot(l)
        for k in range(1, SSM_CONV):
            acc = acc + cw_ref[k:k + 1, :] * conv_slot(l + k)
        act = _silu(acc)
        xs.append(act[:, :D_SSM])
        bm.append(act[:, D_SSM:D_SSM + D_BC])
        cm.append(act[:, D_SSM + D_BC:])
        b_ref[l] = bm[l]
        c_ref[l] = cm[l]
        dt.append(_softplus(dtr[l * nb:(l + 1) * nb] + dtb_ref[...]))
        da = dt[l] * a_ref[...]
        acs.append(da if l == 0 else acs[l - 1] + da)
        for gi, w in enumerate(POOL_WINDOWS):
            gl = slice(gi * POOL_GROUP_DIM, (gi + 1) * POOL_GROUP_DIM)
            s = pool_slot(POOL_HIST + l, gl)
            for k in range(1, w):
                s = s + pool_slot(POOL_HIST + l - k, gl)
            cnt = float(min(PAST_LEN + l + 1, w))
            pooled_scr[l * nb:(l + 1) * nb, gl] = (s / cnt - pool_slot(POOL_HIST + l, gl)).astype(bf16)
    for i in range(SSM_CONV - 1):
        conv_ref[i] = conv_slot(DEC_SEQ + i)
    for i in range(POOL_HIST):
        pool_ref[i] = pool_slot(DEC_SEQ + i, slice(0, D_POOL))
    pout = jnp.concatenate(
        [_dot(pooled_scr[:, gi * POOL_GROUP_DIM:(gi + 1) * POOL_GROUP_DIM], wpool_ref[gi])
         for gi in range(len(POOL_WINDOWS))], axis=1) * pscale_ref[...]
    for l in range(DEC_SEQ):
        pout_ref[l] = pout[l * nb:(l + 1) * nb]

    xdt = [xs[l] * _expand_heads(dt[l], lo) for l in range(DEC_SEQ)]
    gw = D_SSM // SSM_GROUPS
    for l in range(DEC_SEQ):
        y = xs[l] * dexp_ref[...]
        for s in range(l + 1):
            decay = _expand_heads(jnp.exp(acs[l] - acs[s]), lo)
            cbs = [jnp.sum(cm[l][:, g * SSM_STATE:(g + 1) * SSM_STATE] * bm[s][:, g * SSM_STATE:(g + 1) * SSM_STATE],
                           axis=-1, keepdims=True) for g in range(SSM_GROUPS)]
            coef = jnp.concatenate([decay[:, g * gw:(g + 1) * gw] * cbs[g] for g in range(SSM_GROUPS)], axis=1)
            y = y + coef * xdt[s]
        ypart_ref[l] = y
        eacs_ref[l] = _expand_heads(jnp.exp(acs[l]), lo)
        xd_ref[l] = xdt[l] * _expand_heads(jnp.exp(acs[DEC_SEQ - 1] - acs[l]), lo)
    dec_ref[...] = jnp.exp(acs[DEC_SEQ - 1])


def _smix_in(x_rows, sconv, spool, w):
    ns = sconv.shape[1]
    nb = S_SEQ_BLOCK
    tmaj = lambda steps, width: pl.BlockSpec((steps, nb, width), lambda i: (0, i, 0))
    step_outs = [D_MODEL, D_SSM, D_SSM, D_SSM, D_SSM, D_BC, D_BC]
    out_specs = [tmaj(DEC_SEQ, wd) for wd in step_outs] + [pl.BlockSpec((nb, LANES), lambda i: (i, 0)),
                                                          tmaj(DEC_SEQ, D_POOL),
                                                          tmaj(SSM_CONV - 1, D_XBC), tmaj(POOL_HIST, D_POOL)]
    out_shape = [jax.ShapeDtypeStruct((DEC_SEQ, ns, wd), f32) for wd in step_outs] + [
        jax.ShapeDtypeStruct((ns, LANES), f32), jax.ShapeDtypeStruct((DEC_SEQ, ns, D_POOL), f32),
        jax.ShapeDtypeStruct((SSM_CONV - 1, ns, D_XBC), f32), jax.ShapeDtypeStruct((POOL_HIST, ns, D_POOL), f32)]
    return pl.pallas_call(
        _smix_in_kernel,
        grid=(ns // nb,),
        in_specs=[pl.BlockSpec((nb * X_ROWS_PER_SEQ, LANES), lambda i: (i, 0)), tmaj(SSM_CONV - 1, D_XBC),
                  tmaj(POOL_HIST, D_POOL),
                  _const_spec((1, D_MODEL)), _const_spec((D_MODEL, D_SSM + D_XBC)), _const_spec((D_MODEL, D_POOL)),
                  _const_spec((D_MODEL, LANES)), _const_spec((SSM_CONV, D_XBC)),
                  _const_spec((1, D_XBC)), _const_spec((1, LANES)), _const_spec((1, LANES)), _const_spec((1, D_SSM)),
                  _const_spec((len(POOL_WINDOWS), POOL_GROUP_DIM, POOL_GROUP_DIM)), _const_spec((1, D_POOL))],
        out_specs=out_specs,
        out_shape=out_shape,
        scratch_shapes=[pltpu.VMEM((DEC_SEQ * nb, D_MODEL), bf16), pltpu.VMEM((DEC_SEQ * nb, D_POOL), bf16)],
        compiler_params=pltpu.CompilerParams(dimension_semantics=("arbitrary",), vmem_limit_bytes=VMEM_LIMIT_BYTES),
        name="smix_in",
    )(x_rows, sconv, spool, w["nmix"], w["wzx"], w["wvp"], w["wdt"], w["cw"], w["cb"], w["dtb"], w["a_row"], w["dexp"], w["wpool"],
      w["pscale"])


def _sssd_kernel(dec_ref, c_ref, b_ref, xd_ref, st_ref, yoff_ref, stn_ref):
    blk = pl.program_id(0)
    nb = S_SSD_BLOCK
    gw = D_SSM // SSM_GROUPS
    hpg = SSM_HEADS // SSM_GROUPS
    row_seq = lax.broadcasted_iota(jnp.int32, (DEC_SEQ * nb, gw), 0) % nb
    cmat, bmat, xd = _steps(c_ref), _steps(b_ref), _steps(xd_ref)
    for g in range(SSM_GROUPS):
        gl = slice(g * gw, (g + 1) * gw)
        cg = cmat[:, g * SSM_STATE:(g + 1) * SSM_STATE].astype(bf16)
        bg = bmat[:, g * SSM_STATE:(g + 1) * SSM_STATE].astype(bf16)
        yo = jnp.zeros((DEC_SEQ * nb, gw), f32)
        for j in range(nb):
            mine = row_seq == j
            h0 = st_ref[j, gl, :]
            yo = jnp.where(mine, _dot_nt(cg, h0.astype(bf16)), yo)
            upd = _dot_tn(jnp.where(mine, xd[:, gl], 0.0).astype(bf16), bg)
            for hh in range(hpg):
                hr = slice(hh * SSM_HEAD_DIM, (hh + 1) * SSM_HEAD_DIM)
                d = dec_ref[(blk * nb + j) * SSM_HEADS + g * hpg + hh]
                stn_ref[j, g * gw + hh * SSM_HEAD_DIM:g * gw + (hh + 1) * SSM_HEAD_DIM, :] = h0[hr] * d + upd[hr]
        for l in range(DEC_SEQ):
            yoff_ref[l, :, gl] = yo[l * nb:(l + 1) * nb]


def _sssd(dec_flat, cmat, bmat, xd, state):
    ns = state.shape[0]
    nb = S_SSD_BLOCK
    tmaj = lambda width: pl.BlockSpec((DEC_SEQ, nb, width), lambda i: (0, i, 0))
    stblk = pl.BlockSpec((nb, D_SSM, SSM_STATE), lambda i: (i, 0, 0))
    return pl.pallas_call(
        _sssd_kernel,
        grid=(ns // nb,),
        in_specs=[pl.BlockSpec(memory_space=pltpu.SMEM), tmaj(D_BC), tmaj(D_BC), tmaj(D_SSM), stblk],
        out_specs=[tmaj(D_SSM), stblk],
        out_shape=[jax.ShapeDtypeStruct((DEC_SEQ, ns, D_SSM), f32), jax.ShapeDtypeStruct(state.shape, f32)],
        compiler_params=pltpu.CompilerParams(dimension_semantics=("arbitrary",), vmem_limit_bytes=VMEM_LIMIT_BYTES),
        name="sssd",
    )(dec_flat, cmat, bmat, xd, state)


def _smix_out_kernel(x_ref, ypart_ref, yoff_ref, eacs_ref, z_ref, pout_ref, snorm_ref, wout_ref, nmem_ref, wq_ref,
                     x1_ref, q_ref, cat_scr):
    y = ypart_ref[...] + yoff_ref[...] * eacs_ref[...]
    t = y * _silu(z_ref[...])
    gw = D_SSM // SSM_GROUPS
    for g in range(SSM_GROUPS):
        gl = slice(g * gw, (g + 1) * gw)
        tg = t[:, gl]
        ms = jnp.mean(tg * tg, axis=-1, keepdims=True)
        cat_scr[:, gl] = (tg * lax.rsqrt(ms + EPS) * snorm_ref[:, gl]).astype(bf16)
    cat_scr[:, D_SSM:] = pout_ref[...].astype(bf16)
    x1 = x_ref[...] + _dot(cat_scr[...], wout_ref[...])
    x1_ref[...] = x1
    h = _rmsnorm(x1, nmem_ref[...]).astype(bf16)
    q_ref[...] = _dot(h, wq_ref[...]) * (MEM_HEAD_DIM ** -0.5)


def _smix_out(x, ypart, yoff, eacs, z, pout, w):
    n = x.shape[0]
    rb = 128
    rows = lambda width: pl.BlockSpec((rb, width), lambda i: (i, 0))
    return pl.pallas_call(
        _smix_out_kernel,
        grid=(n // rb,),
        in_specs=[rows(D_MODEL), rows(D_SSM), rows(D_SSM), rows(D_SSM), rows(D_SSM), rows(D_POOL),
                  _const_spec((1, D_SSM)), _const_spec((D_SSM + D_POOL, D_MODEL)), _const_spec((1, D_MODEL)),
                  _const_spec((D_MODEL, D_MODEL))],
        out_specs=[rows(D_MODEL), rows(D_MODEL)],
        out_shape=[jax.ShapeDtypeStruct((n, D_MODEL), f32), jax.ShapeDtypeStruct((n, D_MODEL), f32)],
        scratch_shapes=[pltpu.VMEM((rb, D_SSM + D_POOL), bf16)],
        compiler_params=pltpu.CompilerParams(dimension_semantics=("arbitrary",), vmem_limit_bytes=VMEM_LIMIT_BYTES),
        name="smix_out",
    )(x, ypart, yoff, eacs, z, pout, w["snorm"], w["wout"], w["nmem"], w["wq"])


def _sattn_kernel(q_ref, k_ref, v_ref, o_ref):
    nb = S_ATT_BLOCK
    rg = DEC_SEQ * nb
    rows = MEM_HEADS * rg

    def row_ids(width):
        r = lax.broadcasted_iota(jnp.int32, (rows, width), 0)
        return r // rg, r % nb

    row_h, row_seq = row_ids(D_MODEL)
    col_h = lax.broadcasted_iota(jnp.int32, (rows, D_MODEL), 1) // MEM_HEAD_DIM
    q = _steps(q_ref)
    qh = jnp.where(row_h == col_h, jnp.concatenate([q] * MEM_HEADS, axis=0), 0.0)
    lhs_s = jnp.concatenate([jnp.where(row_seq == b, qh, 0.0).astype(bf16) for b in range(nb)], axis=1)
    kcat = jnp.concatenate([_kv_seq(k_ref, b) for b in range(nb)], axis=1)
    s = _dot_nt(lhs_s, kcat)
    p = jnp.exp(s - jnp.max(s, axis=-1, keepdims=True))
    p = p / jnp.sum(p, axis=-1, keepdims=True)
    _, row_seq_p = row_ids(N_MEM)
    lhs_p = jnp.concatenate([jnp.where(row_seq_p == b, p, 0.0).astype(bf16) for b in range(nb)], axis=1)
    vcat = jnp.concatenate([_kv_seq(v_ref, b) for b in range(nb)], axis=0)
    res = _dot(lhs_p, vcat)
    for hd in range(MEM_HEADS):
        hl = slice(hd * MEM_HEAD_DIM, (hd + 1) * MEM_HEAD_DIM)
        for l in range(DEC_SEQ):
            o_ref[l, :, hl] = res[hd * rg + l * nb:hd * rg + (l + 1) * nb, hl]


def _sattn(q, mem_k, mem_v):
    ns = mem_k.shape[0]
    nb = S_ATT_BLOCK
    qblk = pl.BlockSpec((DEC_SEQ, nb, D_MODEL), lambda i: (0, i, 0))
    kvblk = pl.BlockSpec((nb, N_MEM * KV_ROWS_PER_TOKEN, LANES), lambda i: (i, 0, 0))
    return pl.pallas_call(
        _sattn_kernel,
        grid=(ns // nb,),
        in_specs=[qblk, kvblk, kvblk],
        out_specs=qblk,
        out_shape=jax.ShapeDtypeStruct((DEC_SEQ, ns, D_MODEL), f32),
        compiler_params=pltpu.CompilerParams(dimension_semantics=("arbitrary",), vmem_limit_bytes=VMEM_LIMIT_BYTES),
        name="sattn",
    )(q, mem_k, mem_v)


def _sffn_kernel(x1_ref, ao_ref, sffn_ref, wo_ref, g_ref, wup_ref, cw_ref, cb_ref, wdn_ref, gfin_ref,
                 y_ref, st_ref, h_scr, act_scr):
    nb = S_SEQ_BLOCK
    x2 = _steps(x1_ref) + _dot(_steps(ao_ref).astype(bf16), wo_ref[...])
    h_scr[...] = _rmsnorm(x2, g_ref[...]).astype(bf16)
    u = _dot(h_scr[...], wup_ref[...])
    hist = FFN_CONV - 1

    def conv_block(blk):
        cols = slice(blk * FF_CHUNK, (blk + 1) * FF_CHUNK)
        tiles = range(blk * FF_CHUNK // LANES, (blk + 1) * FF_CHUNK // LANES)
        slots = [sffn_ref[:, i, cols] for i in range(hist)]
        slots += [u[l * nb:(l + 1) * nb, cols] for l in range(DEC_SEQ)]
        for i in range(hist):
            st_ref[:, i, cols] = slots[DEC_SEQ + i]
        outs = []
        for l in range(DEC_SEQ):
            acc = cb_ref[:, cols] + cw_ref[0:1, cols] * slots[l]
            for k in range(1, FFN_CONV):
                acc = acc + cw_ref[k:k + 1, cols] * slots[l + k]
            outs.append(acc)
        return jnp.concatenate(outs, axis=0)

    for j in range(N_FF_CHUNKS):
        act = _silu(conv_block(_gate_blk(j))) * conv_block(_val_blk(j))
        act_scr[:, j * FF_CHUNK:(j + 1) * FF_CHUNK] = act.astype(bf16)
    y = _rmsnorm(x2 + _dot(act_scr[...], wdn_ref[...]), gfin_ref[...])
    for l in range(DEC_SEQ):
        _rows_view_put(y_ref, l, DEC_SEQ, range(D_MODEL // LANES), nb, X_ROWS_PER_SEQ, y[l * nb:(l + 1) * nb])


def _sffn(x1, ao, sffn_rows, w):
    ns = x1.shape[1]
    nb = S_SEQ_BLOCK
    tok = pl.BlockSpec((DEC_SEQ, nb, D_MODEL), lambda i: (0, i, 0))
    yblk = pl.BlockSpec((nb * X_ROWS_PER_SEQ, LANES), lambda i: (i, 0))
    stblk = pl.BlockSpec((nb, FFN_CONV - 1, 2 * D_FF), lambda i: (i, 0, 0))
    return pl.pallas_call(
        _sffn_kernel,
        grid=(ns // nb,),
        in_specs=[tok, tok, stblk,
                  _const_spec((D_MODEL, D_MODEL)), _const_spec((1, D_MODEL)), _const_spec((D_MODEL, 2 * D_FF)),
                  _const_spec((FFN_CONV, 2 * D_FF)), _const_spec((1, 2 * D_FF)),
                  _const_spec((D_FF, D_MODEL)), _const_spec((1, D_MODEL))],
        out_specs=[yblk, stblk],
        out_shape=[jax.ShapeDtypeStruct((ns * X_ROWS_PER_SEQ, LANES), f32),
                   jax.ShapeDtypeStruct((ns, FFN_CONV - 1, 2 * D_FF), f32)],
        scratch_shapes=[pltpu.VMEM((DEC_SEQ * nb, D_MODEL), bf16), pltpu.VMEM((DEC_SEQ * nb, D_FF), bf16)],
        compiler_params=pltpu.CompilerParams(dimension_semantics=("arbitrary",), vmem_limit_bytes=VMEM_LIMIT_BYTES),
        name="sffn",
    )(x1, ao, sffn_rows, w["wo"], w["nffn"], w["wup"], w["fcw"], w["fcb"], w["wdn"], w["gfin"])


def _sample_path(x_sample, state_ssm, state_ssm_conv, state_pool, state_ffn_conv, cache_mem_k, cache_mem_v, w):
    ns = x_sample.shape[0]
    xtm, z, ypart, eacs, xd, cmat, bmat, dec, pout, conv_new, pool_new = _smix_in(
        _tiled_rows_view(x_sample), state_ssm_conv[0].transpose(1, 0, 2), state_pool[0].transpose(1, 0, 2), w)
    yoff, ssm_new = _sssd(dec[:, :SSM_HEADS].reshape(-1), cmat, bmat, xd, state_ssm[0].reshape(ns, D_SSM, SSM_STATE))
    flat = lambda a: a.reshape(DEC_SEQ * ns, a.shape[-1])
    x1, q = _smix_out(flat(xtm), flat(ypart), flat(yoff), flat(eacs), flat(z), flat(pout), w)
    ao = _sattn(q.reshape(DEC_SEQ, ns, D_MODEL), _kv_rows_view(cache_mem_k[0]), _kv_rows_view(cache_mem_v[0]))
    y_rows, ffn_new = _sffn(x1.reshape(DEC_SEQ, ns, D_MODEL), ao, state_ffn_conv[0], w)
    return (_from_tiled_rows(y_rows, ns, DEC_SEQ, D_MODEL),
            ssm_new.reshape(1, ns, SSM_HEADS, SSM_HEAD_DIM, SSM_STATE),
            conv_new.transpose(1, 0, 2)[None], pool_new.transpose(1, 0, 2)[None],
            ffn_new[None])


def kernel(x_prompt, x_sample, mem_prompt, state_ssm, state_ssm_conv, state_pool, state_ffn_conv, cache_mem_k, cache_mem_v, norm_mix, w_in, ssm_conv_w, ssm_conv_b, ssm_dt_bias, ssm_a_log, ssm_d, ssm_norm, w_pool, pool_scale, w_out, norm_mem, norm_memkv, w_mq, w_mk, w_mv, w_mo, norm_ffn, w_up, ffn_conv_w, ffn_conv_b, w_down, final_norm):
    params = dict(norm_mix=norm_mix, w_in=w_in, ssm_conv_w=ssm_conv_w, ssm_conv_b=ssm_conv_b, ssm_dt_bias=ssm_dt_bias,
                  ssm_a_log=ssm_a_log, ssm_d=ssm_d, ssm_norm=ssm_norm, w_pool=w_pool, pool_scale=pool_scale,
                  w_out=w_out, norm_mem=norm_mem, norm_memkv=norm_memkv, w_mq=w_mq, w_mk=w_mk, w_mv=w_mv, w_mo=w_mo,
                  norm_ffn=norm_ffn, w_up=w_up, ffn_conv_w=ffn_conv_w, ffn_conv_b=ffn_conv_b, w_down=w_down,
                  final_norm=final_norm)
    w = _prep_weights(params)
    yp, ssm_p, conv_p, pool_p, ffn_p, mk_p, mv_p = _prompt_path(x_prompt, mem_prompt, w, PROMPT_TILE)
    ys, ssm_s, conv_s, pool_s, ffn_s = _sample_path(x_sample, state_ssm, state_ssm_conv, state_pool, state_ffn_conv,
                                                    cache_mem_k, cache_mem_v, w)
    return yp, ys, ssm_p, ssm_s, conv_p, conv_s, pool_p, pool_s, ffn_p, ffn_s, mk_p, mv_p
```

```python
import functools

import jax
import jax.numpy as jnp
from jax import lax
from jax.experimental import pallas as pl
from jax.experimental.pallas import tpu as pltpu

f32 = jnp.float32
bf16 = jnp.bfloat16

D_MODEL = 1024
SSM_HEADS = 16
SSM_HEAD_DIM = 64
SSM_STATE = 128
SSM_GROUPS = 2
SSM_CHUNK = 128
D_SSM = 1024
D_BC = SSM_GROUPS * SSM_STATE
D_XBC = D_SSM + 2 * D_BC
SSM_CONV = 4
D_POOL = 1024
POOL_WINDOWS = (2, 4, 8, 16)
POOL_GROUP_DIM = 256
POOL_HIST = 15
N_MEM = 256
MEM_HEADS = 4
MEM_HEAD_DIM = 256
D_FF = 2816
FFN_CONV = 3
EPS = 1e-6
PAST_LEN = 16384

LANES = 128
SUBLANES = 8
MXU_DIM = 256
VMEM_LIMIT_BYTES = 56 * 1024 * 1024

CONV_HIST_ROWS = SUBLANES
POOL_HIST_ROWS = 2 * SUBLANES
FF_CHUNK = MXU_DIM
N_FF_CHUNKS = D_FF // FF_CHUNK


def _silu(v):
    half = 0.5 * v
    return half + half * jnp.tanh(half)


def _silu_gate(g, v):
    return g * v * (1.0 / (1.0 + jnp.exp(-g)))


def _softplus(v):
    return jnp.maximum(v, 0.0) + jnp.log1p(jnp.exp(-jnp.abs(v)))


def _rmsnorm(x, g):
    ms = jnp.mean(x * x, axis=-1, keepdims=True)
    return x * lax.rsqrt(ms + EPS) * g


def _dot(a, b):
    return jnp.dot(a, b, preferred_element_type=f32)


def _dot_nt(a, b):
    return lax.dot_general(a, b, (((1,), (1,)), ((), ())), preferred_element_type=f32)


def _dot_tn(a, b):
    return lax.dot_general(a, b, (((0,), (0,)), ((), ())), preferred_element_type=f32)


def _split3(v):
    p1 = v.astype(bf16)
    r1 = v - p1.astype(f32)
    p2 = r1.astype(bf16)
    r2 = r1 - p2.astype(f32)
    return p1, p2, r2.astype(bf16)


def _const_spec(shape):
    return pl.BlockSpec(shape, lambda *_: (0,) * len(shape), pipeline_mode=pl.Buffered(1))


KV_LANE_TILES = MEM_HEAD_DIM // LANES
KV_ROWS_PER_TOKEN = KV_LANE_TILES * MEM_HEADS


def _kv_rows_view(kv):
    ns = kv.shape[0]
    kv = kv.reshape(ns, N_MEM, MEM_HEADS, KV_LANE_TILES, LANES).transpose(0, 1, 3, 2, 4)
    return kv.reshape(ns, N_MEM * KV_ROWS_PER_TOKEN, LANES)


def _kv_from_rows(rows):
    ns = rows.shape[0]
    kv = rows.reshape(ns, N_MEM, KV_LANE_TILES, MEM_HEADS, LANES).transpose(0, 1, 3, 2, 4)
    return kv.reshape(ns, N_MEM, MEM_HEADS, MEM_HEAD_DIM)


def _kv_seq(ref, jj):
    tiles = [ref[jj, pl.ds(dt * MEM_HEADS + hd, N_MEM, stride=KV_ROWS_PER_TOKEN), :]
             for hd in range(MEM_HEADS) for dt in range(KV_LANE_TILES)]
    return jnp.concatenate(tiles, axis=1).astype(bf16)


def _memkv_kernel(mem_ref, g_ref, wk_ref, wv_ref, k_ref, v_ref):
    h = _rmsnorm(mem_ref[0], g_ref[...]).astype(bf16)
    for out_ref, w_ref in ((k_ref, wk_ref), (v_ref, wv_ref)):
        kv = _dot(h, w_ref[...])
        for hd in range(MEM_HEADS):
            for dt in range(KV_LANE_TILES):
                col = hd * MEM_HEAD_DIM + dt * LANES
                out_ref[0, pl.ds(dt * MEM_HEADS + hd, N_MEM, stride=KV_ROWS_PER_TOKEN), :] = kv[:, col:col + LANES]


def _memkv(mem, g, wk, wv):
    b = mem.shape[0]
    blk = pl.BlockSpec((1, N_MEM, D_MODEL), lambda i: (i, 0, 0))
    oblk = pl.BlockSpec((1, N_MEM * KV_ROWS_PER_TOKEN, LANES), lambda i: (i, 0, 0))
    return pl.pallas_call(
        _memkv_kernel,
        grid=(b,),
        in_specs=[blk, _const_spec((1, D_MODEL)), _const_spec((D_MODEL, D_MODEL)), _const_spec((D_MODEL, D_MODEL))],
        out_specs=[oblk, oblk],
        out_shape=[jax.ShapeDtypeStruct((b, N_MEM * KV_ROWS_PER_TOKEN, LANES), f32)] * 2,
        compiler_params=pltpu.CompilerParams(dimension_semantics=("arbitrary",), vmem_limit_bytes=VMEM_LIMIT_BYTES),
        name="memkv",
    )(mem, g, wk, wv)


def _ssd_chunk(r0, dt_scr, xs_scr, b_scr, c_scr, y_scr, xd_scr, hst_scr, a_ref, dexp_ref, between=()):
    between = list(between) + [None] * 3
    q = SSM_CHUNK
    rows = pl.ds(r0, q)
    row_i = lax.broadcasted_iota(jnp.int32, (q, q), 0)
    col_i = lax.broadcasted_iota(jnp.int32, (q, q), 1)
    causal = col_i <= row_i
    lo = col_i < SSM_HEAD_DIM
    tril = jnp.where(causal, 1.0, 0.0).astype(bf16)

    dt = dt_scr[rows, :]
    da = dt * a_ref[...]
    p1, p2, p3 = _split3(da)
    acs = _dot(tril, p1) + _dot(tril, p2) + _dot(tril, p3)
    acs_t = acs.T
    if between[0] is not None:
        between[0]()

    for g in range(SSM_GROUPS):
        bg = b_scr[rows, g * SSM_STATE:(g + 1) * SSM_STATE]
        cg = c_scr[rows, g * SSM_STATE:(g + 1) * SSM_STATE]
        bg_b = bg.astype(bf16)
        cb = jnp.where(causal, _dot_nt(cg.astype(bf16), bg_b), 0.0)
        cdec_rows = []
        pairs_per_group = SSM_HEADS // SSM_GROUPS // 2
        for pp in range(pairs_per_group):
            pr = g * pairs_per_group + pp
            lanes = slice(pr * LANES, (pr + 1) * LANES)
            lhs, dtb, dend, cdec = [], [], [], []
            for hh in (2 * pr, 2 * pr + 1):
                colb = jnp.broadcast_to(acs[:, hh:hh + 1], (q, q))
                seg = jnp.where(causal, colb - acs_t[hh:hh + 1, :], 0.0)
                lhs.append((jnp.exp(seg) * cb).astype(bf16))
                lhs.append((cg * jnp.exp(colb)).astype(bf16))
                last = colb[q - 1:q, :]
                dend.append(jnp.exp(last - colb))
                cdec.append(jnp.exp(last))
                dtb.append(jnp.broadcast_to(dt[:, hh:hh + 1], (q, q)))
            xs_pair = xs_scr[rows, lanes]
            xdt = xs_pair * jnp.where(lo, dtb[0], dtb[1])
            xd_scr[:, lanes] = (xdt * jnp.where(lo, dend[0], dend[1])).astype(bf16)
            hst_pair = hst_scr[:, lanes]
            rhs = jnp.concatenate([
                jnp.where(lo, xdt, 0.0).astype(bf16), jnp.where(lo, hst_pair, 0.0).astype(bf16),
                jnp.where(lo, 0.0, xdt).astype(bf16), jnp.where(lo, 0.0, hst_pair).astype(bf16)], axis=0)
            y_pair = _dot(jnp.concatenate(lhs, axis=1), rhs)
            y_scr[rows, lanes] = y_pair + xs_pair * dexp_ref[:, lanes]
            cdec_rows.append(jnp.where(lo[:1], cdec[0], cdec[1]))
        gl = slice(g * (D_SSM // SSM_GROUPS), (g + 1) * (D_SSM // SSM_GROUPS))
        upd = _dot_tn(bg_b, xd_scr[:, gl])
        hst_scr[:, gl] = hst_scr[:, gl] * jnp.concatenate(cdec_rows, axis=1) + upd
        if between[1 + g] is not None:
            between[1 + g]()


def _mixer_kernel(x_ref, nmix_ref, wzx_ref, wvp_ref, wdt_ref, cw_ref, cb_ref, dtb_ref, a_ref, dexp_ref, snorm_ref, wpool_ref,
                  pscale_ref, wout_ref,
                  x1_ref, ssm_ref, conv_ref, pool_ref,
                  xd_scr, hst_scr, *sub_scr, tile, sub):
    t = pl.program_id(1)
    last_t = pl.num_programs(1) - 1
    ch = CONV_HIST_ROWS
    ph = POOL_HIST_ROWS
    n_sub = tile // sub
    per = len(sub_scr) // n_sub
    bufs = [sub_scr[i * per:(i + 1) * per] for i in range(n_sub)]
    xbc0, vp0 = bufs[0][2], bufs[0][3]

    @pl.when(t == 0)
    def _():
        xbc0[:, 0:ch, :] = jnp.zeros((D_XBC // LANES, ch, LANES), f32)
        vp0[:, 0:ph, :] = jnp.zeros((D_POOL // LANES, ph, LANES), f32)
        hst_scr[...] = jnp.zeros_like(hst_scr)

    gw = D_SSM // SSM_GROUPS

    def carry_history(src, dst):
        for j in range(D_XBC // LANES):
            bufs[dst][2][j, 0:ch, :] = bufs[src][2][j, sub:sub + ch, :]
        for j in range(D_POOL // LANES):
            bufs[dst][3][j, 0:ph, :] = bufs[src][3][j, sub:sub + ph, :]

    def in_xbc(s):
        h_scr, _, xbc_scr = bufs[s][:3]
        h_scr[...] = _rmsnorm(x_ref[0, s * sub:(s + 1) * sub, :], nmix_ref[...]).astype(bf16)
        xbc = _dot(h_scr[...], wzx_ref[:, D_SSM:])
        for j in range(D_XBC // LANES):
            xbc_scr[j, ch:ch + sub, :] = xbc[:, j * LANES:(j + 1) * LANES]

    def in_dtvp(s):
        h_scr, _, _, vp_scr, dt_scr = bufs[s][:5]
        dt_scr[...] = _dot(h_scr[...], wdt_ref[...])
        vp = _dot(h_scr[...], wvp_ref[...])
        for j in range(D_POOL // LANES):
            vp_scr[j, ph:ph + sub, :] = vp[:, j * LANES:(j + 1) * LANES]

    def in_z(s):
        h_scr, z_scr = bufs[s][:2]
        z_scr[...] = _dot(h_scr[...], wzx_ref[:, :D_SSM])
        if s > 0:
            carry_history(s - 1, s)

    def stage_mid(s, between):
        _, z_scr, xbc_scr, vp_scr, dt_scr, xs_scr, b_scr, c_scr, y_scr, pooled_scr, cat_scr = bufs[s]
        for j in range(D_XBC // LANES):
            cl = slice(j * LANES, (j + 1) * LANES)
            acc = cb_ref[:, cl] + cw_ref[0:1, cl] * xbc_scr[j, pl.ds(ch - 3, sub), :]
            for k in range(1, SSM_CONV):
                acc = acc + cw_ref[k:k + 1, cl] * xbc_scr[j, pl.ds(ch - 3 + k, sub), :]
            act = _silu(acc)
            if j < D_SSM // LANES:
                xs_scr[:, cl] = act
            elif j < (D_SSM + D_BC) // LANES:
                b_scr[:, j * LANES - D_SSM:(j + 1) * LANES - D_SSM] = act
            else:
                c_scr[:, j * LANES - D_SSM - D_BC:(j + 1) * LANES - D_SSM - D_BC] = act

        pos1 = lax.broadcasted_iota(jnp.int32, (sub, LANES), 0) + (t * tile + s * sub + 1)
        for j in range(D_POOL // LANES):
            cl = slice(j * LANES, (j + 1) * LANES)
            w = POOL_WINDOWS[j * LANES // POOL_GROUP_DIM]
            cur = vp_scr[j, ph:ph + sub, :]
            acc = cur
            for k in range(1, w):
                acc = acc + vp_scr[j, pl.ds(ph - k, sub), :]
            cnt = jnp.minimum(pos1, w).astype(f32)
            pooled_scr[:, cl] = (acc / cnt - cur).astype(bf16)
        for gi in range(len(POOL_WINDOWS)):
            gl = slice(gi * POOL_GROUP_DIM, (gi + 1) * POOL_GROUP_DIM)
            pg = _dot(pooled_scr[:, gl], wpool_ref[gi]) * pscale_ref[:, gl]
            cat_scr[:, D_SSM + gi * POOL_GROUP_DIM:D_SSM + (gi + 1) * POOL_GROUP_DIM] = pg.astype(bf16)

        dt_scr[...] = _softplus(dt_scr[...] + dtb_ref[...])
        n_chunks = sub // SSM_CHUNK
        slots = [None] * (3 * n_chunks)
        for i, piece in enumerate(between):
            slots[i * len(slots) // len(between)] = piece
        for c in range(n_chunks):
            _ssd_chunk(c * SSM_CHUNK, dt_scr, xs_scr, b_scr, c_scr, y_scr, xd_scr, hst_scr, a_ref, dexp_ref,
                       slots[3 * c:3 * c + 3])

        for g in range(SSM_GROUPS):
            gl = slice(g * gw, (g + 1) * gw)
            tg = y_scr[:, gl] * _silu(z_scr[:, gl])
            ms = jnp.mean(tg * tg, axis=-1, keepdims=True)
            cat_scr[:, gl] = (tg * lax.rsqrt(ms + EPS) * snorm_ref[:, gl]).astype(bf16)

    def stage_out(s):
        rows = slice(s * sub, (s + 1) * sub)
        x1_ref[0, rows, :] = x_ref[0, rows, :] + _dot(bufs[s][-1][...], wout_ref[...])

    in_xbc(0)
    in_dtvp(0)
    in_z(0)
    for s in range(n_sub):
        nxt = s + 1 < n_sub
        if nxt:
            in_xbc(s + 1)
        pieces = [functools.partial(stage_out, s - 1)] if s > 0 else []
        if nxt:
            pieces += [functools.partial(in_dtvp, s + 1), functools.partial(in_z, s + 1)]
        stage_mid(s, pieces)
    stage_out(n_sub - 1)
    carry_history(n_sub - 1, 0)

    @pl.when(t == last_t)
    def _():
        for pr in range(D_SSM // LANES):
            ssm_ref[0, pr * LANES:(pr + 1) * LANES, :] = hst_scr[:, pr * LANES:(pr + 1) * LANES].T
        for j in range(D_XBC // LANES):
            conv_ref[0, :, j * LANES:(j + 1) * LANES] = xbc0[j, pl.ds(ch - (SSM_CONV - 1), SSM_CONV - 1), :]
        for j in range(D_POOL // LANES):
            pool_ref[0, :, j * LANES:(j + 1) * LANES] = vp0[j, pl.ds(ph - POOL_HIST, POOL_HIST), :]


def _mixer_prompt(x, nmix, wzx, wvp, wdt, cw, cb, dtb, a_row, dexp, snorm, wpool, pscale, wout, tile):
    b, seq, _ = x.shape
    nt = seq // tile
    sub = min(tile, MIXER_SUB_TILE)
    xblk = pl.BlockSpec((1, tile, D_MODEL), lambda i, j: (i, j, 0))
    sub_scratch = [
        pltpu.VMEM((sub, D_MODEL), bf16),
        pltpu.VMEM((sub, D_SSM), f32),
        pltpu.VMEM((D_XBC // LANES, CONV_HIST_ROWS + sub, LANES), f32),
        pltpu.VMEM((D_POOL // LANES, POOL_HIST_ROWS + sub, LANES), f32),
        pltpu.VMEM((sub, LANES), f32),
        pltpu.VMEM((sub, D_SSM), f32),
        pltpu.VMEM((sub, D_BC), f32),
        pltpu.VMEM((sub, D_BC), f32),
        pltpu.VMEM((sub, D_SSM), f32),
        pltpu.VMEM((sub, D_POOL), bf16),
        pltpu.VMEM((sub, D_SSM + D_POOL), bf16),
    ]
    scratch = [pltpu.VMEM((SSM_CHUNK, D_SSM), bf16),
               pltpu.VMEM((SSM_STATE, D_SSM), f32)]
    scratch += sub_scratch * (tile // sub)
    return pl.pallas_call(
        functools.partial(_mixer_kernel, tile=tile, sub=sub),
        grid=(b, nt),
        in_specs=[xblk, _const_spec((1, D_MODEL)), _const_spec((D_MODEL, D_SSM + D_XBC)),
                  _const_spec((D_MODEL, D_POOL)), _const_spec((D_MODEL, LANES)), _const_spec((SSM_CONV, D_XBC)),
                  _const_spec((1, D_XBC)), _const_spec((1, LANES)), _const_spec((1, LANES)), _const_spec((1, D_SSM)),
                  _const_spec((1, D_SSM)), _const_spec((len(POOL_WINDOWS), POOL_GROUP_DIM, POOL_GROUP_DIM)),
                  _const_spec((1, D_POOL)), _const_spec((D_SSM + D_POOL, D_MODEL))],
        out_specs=[xblk,
                   pl.BlockSpec((1, D_SSM, SSM_STATE), lambda i, j: (i, 0, 0)),
                   pl.BlockSpec((1, SSM_CONV - 1, D_XBC), lambda i, j: (i, 0, 0)),
                   pl.BlockSpec((1, POOL_HIST, D_POOL), lambda i, j: (i, 0, 0))],
        out_shape=[jax.ShapeDtypeStruct((b, seq, D_MODEL), f32),
                   jax.ShapeDtypeStruct((b, D_SSM, SSM_STATE), f32),
                   jax.ShapeDtypeStruct((b, SSM_CONV - 1, D_XBC), f32),
                   jax.ShapeDtypeStruct((b, POOL_HIST, D_POOL), f32)],
        scratch_shapes=scratch,
        compiler_params=pltpu.CompilerParams(dimension_semantics=("arbitrary", "arbitrary"),
                                             vmem_limit_bytes=VMEM_LIMIT_BYTES),
        name="mixer_prompt",
    )(x, nmix, wzx, wvp, wdt, cw, cb, dtb, a_row, dexp, snorm, wpool, pscale, wout)


def _attn_kernel(x_ref, g_ref, wq_ref, k_ref, v_ref, wo_ref, o_ref, q_scr, ao_scr):
    tile = x_ref.shape[1]
    sub = min(tile, ATTN_SUB_TILE)
    n_sub = tile // sub
    heads = [slice(hd * MEM_HEAD_DIM, (hd + 1) * MEM_HEAD_DIM) for hd in range(MEM_HEADS)]
    k = _kv_seq(k_ref, 0)
    v = _kv_seq(v_ref, 0)

    def q_proj(s):
        rows = slice(s * sub, (s + 1) * sub)
        h = _rmsnorm(x_ref[0, rows, :], g_ref[...]).astype(bf16)
        q_scr[rows, :] = (_dot(h, wq_ref[...]) * (MEM_HEAD_DIM ** -0.5)).astype(bf16)

    def scores(s):
        rows = slice(s * sub, (s + 1) * sub)
        return [_dot_nt(q_scr[rows, hl], k[:, hl]) for hl in heads]

    def values(s, sc):
        rows = slice(s * sub, (s + 1) * sub)
        for hl, s_h in zip(heads, sc):
            p = jnp.exp(s_h - jnp.max(s_h, axis=-1, keepdims=True))
            p = (p / jnp.sum(p, axis=-1, keepdims=True)).astype(bf16)
            ao_scr[rows, hl] = _dot(p, v[:, hl]).astype(bf16)

    def o_proj(s):
        rows = slice(s * sub, (s + 1) * sub)
        o_ref[0, rows, :] = x_ref[0, rows, :] + _dot(ao_scr[rows, :], wo_ref[...])

    q_proj(0)
    for s in range(n_sub):
        sc = scores(s)
        if s + 1 < n_sub:
            q_proj(s + 1)
        if s > 0:
            o_proj(s - 1)
        values(s, sc)
    o_proj(n_sub - 1)


def _attn_prompt(x, g, wq, mem_k, mem_v, wo, tile):
    b, seq, _ = x.shape
    xblk = pl.BlockSpec((1, tile, D_MODEL), lambda i, j: (i, j, 0))
    kvblk = pl.BlockSpec((1, N_MEM * KV_ROWS_PER_TOKEN, LANES), lambda i, j: (i, 0, 0))
    return pl.pallas_call(
        _attn_kernel,
        grid=(b, seq // tile),
        in_specs=[xblk, _const_spec((1, D_MODEL)), _const_spec((D_MODEL, D_MODEL)), kvblk, kvblk,
                  _const_spec((D_MODEL, D_MODEL))],
        out_specs=xblk,
        out_shape=jax.ShapeDtypeStruct((b, seq, D_MODEL), f32),
        scratch_shapes=[pltpu.VMEM((tile, D_MODEL), bf16), pltpu.VMEM((tile, D_MODEL), bf16)],
        compiler_params=pltpu.CompilerParams(dimension_semantics=("arbitrary", "arbitrary"),
                                             vmem_limit_bytes=VMEM_LIMIT_BYTES),
        name="attn_prompt",
    )(x, g, wq, mem_k, mem_v, wo)


def _gate_blk(j):
    return j


def _val_blk(j):
    return N_FF_CHUNKS + j


def _ffn_kernel(x_ref, g_ref, wup_ref, cw_ref, cb_ref, wdn_ref, gfin_ref, y_ref, st_ref, h_scr, u_scr, act_scr, *, tile):
    t = pl.program_id(1)
    last_t = pl.num_programs(1) - 1
    ch = CONV_HIST_ROWS
    tpc = FF_CHUNK // LANES
    ntile = 2 * D_FF // LANES

    @pl.when(t == 0)
    def _():
        u_scr[:, 0:ch, :] = jnp.zeros((ntile, ch, LANES), f32)

    sub = min(tile, FFN_SUB_TILE)
    n_sub = tile // sub

    def up_proj(s):
        rows = slice(s * sub, (s + 1) * sub)
        h_scr[rows, :] = _rmsnorm(x_ref[0, rows, :], g_ref[...]).astype(bf16)
        u = _dot(h_scr[rows, :], wup_ref[...])
        for ti in range(ntile):
            u_scr[ti, ch + s * sub:ch + (s + 1) * sub, :] = u[:, ti * LANES:(ti + 1) * LANES]

    def conv(s, ti):
        cl = slice(ti * LANES, (ti + 1) * LANES)
        r0 = ch + s * sub
        acc = cb_ref[:, cl] + cw_ref[0:1, cl] * u_scr[ti, pl.ds(r0 - 2, sub), :]
        acc = acc + cw_ref[1:2, cl] * u_scr[ti, pl.ds(r0 - 1, sub), :]
        return acc + cw_ref[2:3, cl] * u_scr[ti, r0:r0 + sub, :]

    def gate(s):
        rows = slice(s * sub, (s + 1) * sub)
        for j in range(N_FF_CHUNKS):
            for i in range(tpc):
                g_t = conv(s, _gate_blk(j) * tpc + i)
                v_t = conv(s, _val_blk(j) * tpc + i)
                act_scr[rows, (j * tpc + i) * LANES:(j * tpc + i + 1) * LANES] = _silu_gate(g_t, v_t).astype(bf16)

    def down_proj(s):
        rows = slice(s * sub, (s + 1) * sub)
        y_ref[0, rows, :] = _rmsnorm(x_ref[0, rows, :] + _dot(act_scr[rows, :], wdn_ref[...]), gfin_ref[...])

    up_proj(0)
    for s in range(n_sub):
        if s + 1 < n_sub:
            up_proj(s + 1)
        gate(s)
        down_proj(s)
    for ti in range(ntile):
        u_scr[ti, 0:ch, :] = u_scr[ti, tile:tile + ch, :]

    @pl.when(t == last_t)
    def _():
        for ti in range(ntile):
            st_ref[0, :, ti * LANES:(ti + 1) * LANES] = u_scr[ti, pl.ds(ch - (FFN_CONV - 1), FFN_CONV - 1), :]


def _ffn_prompt(x, g, wup, cw, cb, wdn, gfin, tile):
    b, seq, _ = x.shape
    xblk = pl.BlockSpec((1, tile, D_MODEL), lambda i, j: (i, j, 0))
    return pl.pallas_call(
        functools.partial(_ffn_kernel, tile=tile),
        grid=(b, seq // tile),
        in_specs=[xblk, _const_spec((1, D_MODEL)), _const_spec((D_MODEL, 2 * D_FF)),
                  _const_spec((FFN_CONV, 2 * D_FF)), _const_spec((1, 2 * D_FF)),
                  _const_spec((D_FF, D_MODEL)), _const_spec((1, D_MODEL))],
        out_specs=[xblk, pl.BlockSpec((1, FFN_CONV - 1, 2 * D_FF), lambda i, j: (i, 0, 0))],
        out_shape=[jax.ShapeDtypeStruct((b, seq, D_MODEL), f32),
                   jax.ShapeDtypeStruct((b, FFN_CONV - 1, 2 * D_FF), f32)],
        scratch_shapes=[pltpu.VMEM((tile, D_MODEL), bf16),
                        pltpu.VMEM((2 * D_FF // LANES, CONV_HIST_ROWS + tile, LANES), f32),
                        pltpu.VMEM((tile, D_FF), bf16)],
        compiler_params=pltpu.CompilerParams(dimension_semantics=("arbitrary", "arbitrary"),
                                             vmem_limit_bytes=VMEM_LIMIT_BYTES),
        name="ffn_prompt",
    )(x, g, wup, cw, cb, wdn, gfin)


def _split_in_proj_kernel(wt_ref, wzx_ref, wvp_ref, wdt_ref):
    n_zx = D_SSM + D_XBC
    for c0 in range(0, n_zx, D_MODEL):
        c1 = min(c0 + D_MODEL, n_zx)
        wzx_ref[:, c0:c1] = wt_ref[c0:c1, :].T.astype(bf16)
    lane = lax.broadcasted_iota(jnp.int32, (D_MODEL, LANES), 1)
    wdt_ref[...] = jnp.where(lane < SSM_HEADS, wt_ref[n_zx:n_zx + LANES, :].T, 0.0).astype(bf16)
    wvp_ref[...] = wt_ref[n_zx + SSM_HEADS:, :].T.astype(bf16)


def _split_in_proj(w_in):
    wt = w_in.T
    return pl.pallas_call(
        _split_in_proj_kernel,
        in_specs=[pl.BlockSpec(wt.shape, lambda: (0, 0))],
        out_shape=[jax.ShapeDtypeStruct((D_MODEL, D_SSM + D_XBC), bf16), jax.ShapeDtypeStruct((D_MODEL, D_POOL), bf16),
                   jax.ShapeDtypeStruct((D_MODEL, LANES), bf16)],
        compiler_params=pltpu.CompilerParams(vmem_limit_bytes=VMEM_LIMIT_BYTES),
        name="split_in_proj",
    )(wt)


def _prep_weights(p):
    i = 0
    wzx, wvp, wdt = _split_in_proj(p["w_in"][i])
    pad_h = (0, LANES - SSM_HEADS)
    return dict(
        nmix=p["norm_mix"][i][None], wzx=wzx, wvp=wvp, wdt=wdt, cw=p["ssm_conv_w"][i], cb=p["ssm_conv_b"][i][None],
        dtb=jnp.pad(p["ssm_dt_bias"][i], pad_h)[None],
        a_row=jnp.pad(-jnp.exp(p["ssm_a_log"][i].astype(f32)), pad_h)[None],
        dexp=jnp.repeat(p["ssm_d"][i], SSM_HEAD_DIM)[None], snorm=p["ssm_norm"][i][None],
        wpool=p["w_pool"][i].astype(bf16), pscale=p["pool_scale"][i][None], wout=p["w_out"][i].astype(bf16),
        nmem=p["norm_mem"][i][None], nmemkv=p["norm_memkv"][i][None],
        wq=p["w_mq"][i].astype(bf16), wk=p["w_mk"][i].astype(bf16), wv=p["w_mv"][i].astype(bf16),
        wo=p["w_mo"][i].astype(bf16),
        nffn=p["norm_ffn"][i][None],
        wup=p["w_up"][i].astype(bf16), fcw=p["ffn_conv_w"][i], fcb=p["ffn_conv_b"][i][None],
        wdn=p["w_down"][i].astype(bf16),
        gfin=p["final_norm"][None],
    )


def _prompt_path(x_prompt, mem_prompt, w, tile):
    b = x_prompt.shape[0]
    mem_k, mem_v = _memkv(mem_prompt, w["nmemkv"], w["wk"], w["wv"])
    x1, ssm, conv, pool = _mixer_prompt(x_prompt, w["nmix"], w["wzx"], w["wvp"], w["wdt"], w["cw"], w["cb"], w["dtb"], w["a_row"],
                                        w["dexp"], w["snorm"], w["wpool"], w["pscale"], w["wout"],
                                        min(MIXER_TILE, x_prompt.shape[1]))
    x2 = _attn_prompt(x1, w["nmem"], w["wq"], mem_k, mem_v, w["wo"], min(ATTN_TILE, x_prompt.shape[1]))
    y, ffn = _ffn_prompt(x2, w["nffn"], w["wup"], w["fcw"], w["fcb"], w["wdn"], w["gfin"], tile)
    return (y, ssm.reshape(1, b, SSM_HEADS, SSM_HEAD_DIM, SSM_STATE), conv[None], pool[None], ffn[None],
            _kv_from_rows(mem_k)[None], _kv_from_rows(mem_v)[None])


PROMPT_TILE = 512
ATTN_TILE = 1024
MIXER_TILE = 1024
MIXER_SUB_TILE = 256
ATTN_SUB_TILE = 256
FFN_SUB_TILE = 512


DEC_SEQ = 4
S_SEQ_BLOCK = 64
S_SSD_BLOCK = 16
S_ATT_BLOCK = 8
X_ROWS_PER_SEQ = DEC_SEQ * D_MODEL // LANES


def _expand_heads(v, lo):
    r = v.shape[0]
    tiles = []
    for pr in range(SSM_HEADS // 2):
        a = jnp.broadcast_to(v[:, 2 * pr:2 * pr + 1], (r, LANES))
        b = jnp.broadcast_to(v[:, 2 * pr + 1:2 * pr + 2], (r, LANES))
        tiles.append(jnp.where(lo, a, b))
    return jnp.concatenate(tiles, axis=1)


def _tiled_rows_view(a):
    ns, r, c = a.shape
    return a.reshape(ns, r, c // LANES, LANES).transpose(0, 2, 1, 3).reshape(ns * (c // LANES) * r, LANES)


def _from_tiled_rows(rows, ns, r, c):
    return rows.reshape(ns, c // LANES, r, LANES).transpose(0, 2, 1, 3).reshape(ns, r, c)


def _rows_view_get(ref, i, r, tiles, nb, per_seq):
    return jnp.concatenate([ref[pl.ds(dt * r + i, nb, stride=per_seq), :] for dt in tiles], axis=1)


def _rows_view_put(ref, i, r, tiles, nb, per_seq, val):
    for n, dt in enumerate(tiles):
        ref[pl.ds(dt * r + i, nb, stride=per_seq), :] = val[:, n * LANES:(n + 1) * LANES]


def _steps(ref):
    return ref[...].reshape(DEC_SEQ * ref.shape[1], ref.shape[2])


def _smix_in_kernel(x_ref, sconv_ref, spool_ref, nmix_ref, wzx_ref, wvp_ref, wdt_ref, cw_ref, cb_ref, dtb_ref, a_ref, dexp_ref,
                    wpool_ref, pscale_ref,
                    xtm_ref, z_ref, ypart_ref, eacs_ref, xd_ref, c_ref, b_ref, dec_ref, pout_ref, conv_ref, pool_ref,
                    h_scr, pooled_scr):
    nb = S_SEQ_BLOCK
    lo = lax.broadcasted_iota(jnp.int32, (nb, LANES), 1) < SSM_HEAD_DIM
    x_tiles = range(D_MODEL // LANES)
    x_steps = [_rows_view_get(x_ref, l, DEC_SEQ, x_tiles, nb, X_ROWS_PER_SEQ) for l in range(DEC_SEQ)]
    for l in range(DEC_SEQ):
        xtm_ref[l] = x_steps[l]
    h_scr[...] = _rmsnorm(jnp.concatenate(x_steps, axis=0), nmix_ref[...]).astype(bf16)
    z = _dot(h_scr[...], wzx_ref[:, :D_SSM])
    for l in range(DEC_SEQ):
        z_ref[l] = z[l * nb:(l + 1) * nb]
    xbc = _dot(h_scr[...], wzx_ref[:, D_SSM:])
    vp = _dot(h_scr[...], wvp_ref[...])
    dtr = _dot(h_scr[...], wdt_ref[...])

    def conv_slot(i):
        if i < SSM_CONV - 1:
            return sconv_ref[i]
        return xbc[(i - SSM_CONV + 1) * nb:(i - SSM_CONV + 2) * nb]

    def pool_slot(i, cl):
        if i < POOL_HIST:
            return spool_ref[i, :, cl]
        return vp[(i - POOL_HIST) * nb:(i - POOL_HIST + 1) * nb, cl]

    xs, bm, cm, dt, acs = [], [], [], [], []
    for l in range(DEC_SEQ):
        acc = cb_ref[...] + cw_ref[0:1, :] * conv_slot(l)
        for k in range(1, SSM_CONV):
            acc = acc + cw_ref[k:k + 1, :] * conv_slot(l + k)
        act = _silu(acc)
        xs.append(act[:, :D_SSM])
        bm.append(act[:, D_SSM:D_SSM + D_BC])
        cm.append(act[:, D_SSM + D_BC:])
        b_ref[l] = bm[l]
        c_ref[l] = cm[l]
        dt.append(_softplus(dtr[l * nb:(l + 1) * nb] + dtb_ref[...]))
        da = dt[l] * a_ref[...]
        acs.append(da if l == 0 else acs[l - 1] + da)
        for gi, w in enumerate(POOL_WINDOWS):
            gl = slice(gi * POOL_GROUP_DIM, (gi + 1) * POOL_GROUP_DIM)
            s = pool_slot(POOL_HIST + l, gl)
            for k in range(1, w):
                s = s + pool_slot(POOL_HIST + l - k, gl)
            cnt = float(min(PAST_LEN + l + 1, w))
            pooled_scr[l * nb:(l + 1) * nb, gl] = (s / cnt - pool_slot(POOL_HIST + l, gl)).astype(bf16)
    for i in range(SSM_CONV - 1):
        conv_ref[i] = conv_slot(DEC_SEQ + i)
    for i in range(POOL_HIST):
        pool_ref[i] = pool_slot(DEC_SEQ + i, slice(0, D_POOL))
    pout = jnp.concatenate(
        [_dot(pooled_scr[:, gi * POOL_GROUP_DIM:(gi + 1) * POOL_GROUP_DIM], wpool_ref[gi])
         for gi in range(len(POOL_WINDOWS))], axis=1) * pscale_ref[...]
    for l in range(DEC_SEQ):
        pout_ref[l] = pout[l * nb:(l + 1) * nb]

    xdt = [xs[l] * _expand_heads(dt[l], lo) for l in range(DEC_SEQ)]
    gw = D_SSM // SSM_GROUPS
    for l in range(DEC_SEQ):
        y = xs[l] * dexp_ref[...]
        for s in range(l + 1):
            decay = _expand_heads(jnp.exp(acs[l] - acs[s]), lo)
            cbs = [jnp.sum(cm[l][:, g * SSM_STATE:(g + 1) * SSM_STATE] * bm[s][:, g * SSM_STATE:(g + 1) * SSM_STATE],
                           axis=-1, keepdims=True) for g in range(SSM_GROUPS)]
            coef = jnp.concatenate([decay[:, g * gw:(g + 1) * gw] * cbs[g] for g in range(SSM_GROUPS)], axis=1)
            y = y + coef * xdt[s]
        ypart_ref[l] = y
        eacs_ref[l] = _expand_heads(jnp.exp(acs[l]), lo)
        xd_ref[l] = xdt[l] * _expand_heads(jnp.exp(acs[DEC_SEQ - 1] - acs[l]), lo)
    dec_ref[...] = jnp.exp(acs[DEC_SEQ - 1])


def _smix_in(x_rows, sconv, spool, w):
    ns = sconv.shape[1]
    nb = S_SEQ_BLOCK
    tmaj = lambda steps, width: pl.BlockSpec((steps, nb, width), lambda i: (0, i, 0))
    step_outs = [D_MODEL, D_SSM, D_SSM, D_SSM, D_SSM, D_BC, D_BC]
    out_specs = [tmaj(DEC_SEQ, wd) for wd in step_outs] + [pl.BlockSpec((nb, LANES), lambda i: (i, 0)),
                                                          tmaj(DEC_SEQ, D_POOL),
                                                          tmaj(SSM_CONV - 1, D_XBC), tmaj(POOL_HIST, D_POOL)]
    out_shape = [jax.ShapeDtypeStruct((DEC_SEQ, ns, wd), f32) for wd in step_outs] + [
        jax.ShapeDtypeStruct((ns, LANES), f32), jax.ShapeDtypeStruct((DEC_SEQ, ns, D_POOL), f32),
        jax.ShapeDtypeStruct((SSM_CONV - 1, ns, D_XBC), f32), jax.ShapeDtypeStruct((POOL_HIST, ns, D_POOL), f32)]
    return pl.pallas_call(
        _smix_in_kernel,
        grid=(ns // nb,),
        in_specs=[pl.BlockSpec((nb * X_ROWS_PER_SEQ, LANES), lambda i: (i, 0)), tmaj(SSM_CONV - 1, D_XBC),
                  tmaj(POOL_HIST, D_POOL),
                  _const_spec((1, D_MODEL)), _const_spec((D_MODEL, D_SSM + D_XBC)), _const_spec((D_MODEL, D_POOL)),
                  _const_spec((D_MODEL, LANES)), _const_spec((SSM_CONV, D_XBC)),
                  _const_spec((1, D_XBC)), _const_spec((1, LANES)), _const_spec((1, LANES)), _const_spec((1, D_SSM)),
                  _const_spec((len(POOL_WINDOWS), POOL_GROUP_DIM, POOL_GROUP_DIM)), _const_spec((1, D_POOL))],
        out_specs=out_specs,
        out_shape=out_shape,
        scratch_shapes=[pltpu.VMEM((DEC_SEQ * nb, D_MODEL), bf16), pltpu.VMEM((DEC_SEQ * nb, D_POOL), bf16)],
        compiler_params=pltpu.CompilerParams(dimension_semantics=("arbitrary",), vmem_limit_bytes=VMEM_LIMIT_BYTES),
        name="smix_in",
    )(x_rows, sconv, spool, w["nmix"], w["wzx"], w["wvp"], w["wdt"], w["cw"], w["cb"], w["dtb"], w["a_row"], w["dexp"], w["wpool"],
      w["pscale"])


def _sssd_kernel(dec_ref, c_ref, b_ref, xd_ref, st_ref, yoff_ref, stn_ref):
    blk = pl.program_id(0)
    nb = S_SSD_BLOCK
    gw = D_SSM // SSM_GROUPS
    hpg = SSM_HEADS // SSM_GROUPS
    row_seq = lax.broadcasted_iota(jnp.int32, (DEC_SEQ * nb, gw), 0) % nb
    cmat, bmat, xd = _steps(c_ref), _steps(b_ref), _steps(xd_ref)
    for g in range(SSM_GROUPS):
        gl = slice(g * gw, (g + 1) * gw)
        cg = cmat[:, g * SSM_STATE:(g + 1) * SSM_STATE].astype(bf16)
        bg = bmat[:, g * SSM_STATE:(g + 1) * SSM_STATE].astype(bf16)
        yo = jnp.zeros((DEC_SEQ * nb, gw), f32)
        for j in range(nb):
            mine = row_seq == j
            h0 = st_ref[j, gl, :]
            yo = jnp.where(mine, _dot_nt(cg, h0.astype(bf16)), yo)
            upd = _dot_tn(jnp.where(mine, xd[:, gl], 0.0).astype(bf16), bg)
            for hh in range(hpg):
                hr = slice(hh * SSM_HEAD_DIM, (hh + 1) * SSM_HEAD_DIM)
                d = dec_ref[(blk * nb + j) * SSM_HEADS + g * hpg + hh]
                stn_ref[j, g * gw + hh * SSM_HEAD_DIM:g * gw + (hh + 1) * SSM_HEAD_DIM, :] = h0[hr] * d + upd[hr]
        for l in range(DEC_SEQ):
            yoff_ref[l, :, gl] = yo[l * nb:(l + 1) * nb]


def _sssd(dec_flat, cmat, bmat, xd, state):
    ns = state.shape[0]
    nb = S_SSD_BLOCK
    tmaj = lambda width: pl.BlockSpec((DEC_SEQ, nb, width), lambda i: (0, i, 0))
    stblk = pl.BlockSpec((nb, D_SSM, SSM_STATE), lambda i: (i, 0, 0))
    return pl.pallas_call(
        _sssd_kernel,
        grid=(ns // nb,),
        in_specs=[pl.BlockSpec(memory_space=pltpu.SMEM), tmaj(D_BC), tmaj(D_BC), tmaj(D_SSM), stblk],
        out_specs=[tmaj(D_SSM), stblk],
        out_shape=[jax.ShapeDtypeStruct((DEC_SEQ, ns, D_SSM), f32), jax.ShapeDtypeStruct(state.shape, f32)],
        compiler_params=pltpu.CompilerParams(dimension_semantics=("arbitrary",), vmem_limit_bytes=VMEM_LIMIT_BYTES),
        name="sssd",
    )(dec_flat, cmat, bmat, xd, state)


def _smix_out_kernel(x_ref, ypart_ref, yoff_ref, eacs_ref, z_ref, pout_ref, snorm_ref, wout_ref, nmem_ref, wq_ref,
                     x1_ref, q_ref, cat_scr):
    y = ypart_ref[...] + yoff_ref[...] * eacs_ref[...]
    t = y * _silu(z_ref[...])
    gw = D_SSM // SSM_GROUPS
    for g in range(SSM_GROUPS):
        gl = slice(g * gw, (g + 1) * gw)
        tg = t[:, gl]
        ms = jnp.mean(tg * tg, axis=-1, keepdims=True)
        cat_scr[:, gl] = (tg * lax.rsqrt(ms + EPS) * snorm_ref[:, gl]).astype(bf16)
    cat_scr[:, D_SSM:] = pout_ref[...].astype(bf16)
    x1 = x_ref[...] + _dot(cat_scr[...], wout_ref[...])
    x1_ref[...] = x1
    h = _rmsnorm(x1, nmem_ref[...]).astype(bf16)
    q_ref[...] = _dot(h, wq_ref[...]) * (MEM_HEAD_DIM ** -0.5)


def _smix_out(x, ypart, yoff, eacs, z, pout, w):
    n = x.shape[0]
    rb = 128
    rows = lambda width: pl.BlockSpec((rb, width), lambda i: (i, 0))
    return pl.pallas_call(
        _smix_out_kernel,
        grid=(n // rb,),
        in_specs=[rows(D_MODEL), rows(D_SSM), rows(D_SSM), rows(D_SSM), rows(D_SSM), rows(D_POOL),
                  _const_spec((1, D_SSM)), _const_spec((D_SSM + D_POOL, D_MODEL)), _const_spec((1, D_MODEL)),
                  _const_spec((D_MODEL, D_MODEL))],
        out_specs=[rows(D_MODEL), rows(D_MODEL)],
        out_shape=[jax.ShapeDtypeStruct((n, D_MODEL), f32), jax.ShapeDtypeStruct((n, D_MODEL), f32)],
        scratch_shapes=[pltpu.VMEM((rb, D_SSM + D_POOL), bf16)],
        compiler_params=pltpu.CompilerParams(dimension_semantics=("arbitrary",), vmem_limit_bytes=VMEM_LIMIT_BYTES),
        name="smix_out",
    )(x, ypart, yoff, eacs, z, pout, w["snorm"], w["wout"], w["nmem"], w["wq"])


def _sattn_kernel(q_ref, k_ref, v_ref, o_ref):
    nb = S_ATT_BLOCK
    rg = DEC_SEQ * nb
    rows = MEM_HEADS * rg

    def row_ids(width):
        r = lax.broadcasted_iota(jnp.int32, (rows, width), 0)
        return r // rg, r % nb

    row_h, row_seq = row_ids(D_MODEL)
    col_h = lax.broadcasted_iota(jnp.int32, (rows, D_MODEL), 1) // MEM_HEAD_DIM
    q = _steps(q_ref)
    qh = jnp.where(row_h == col_h, jnp.concatenate([q] * MEM_HEADS, axis=0), 0.0)
    lhs_s = jnp.concatenate([jnp.where(row_seq == b, qh, 0.0).astype(bf16) for b in range(nb)], axis=1)
    kcat = jnp.concatenate([_kv_seq(k_ref, b) for b in range(nb)], axis=1)
    s = _dot_nt(lhs_s, kcat)
    p = jnp.exp(s - jnp.max(s, axis=-1, keepdims=True))
    p = p / jnp.sum(p, axis=-1, keepdims=True)
    _, row_seq_p = row_ids(N_MEM)
    lhs_p = jnp.concatenate([jnp.where(row_seq_p == b, p, 0.0).astype(bf16) for b in range(nb)], axis=1)
    vcat = jnp.concatenate([_kv_seq(v_ref, b) for b in range(nb)], axis=0)
    res = _dot(lhs_p, vcat)
    for hd in range(MEM_HEADS):
        hl = slice(hd * MEM_HEAD_DIM, (hd + 1) * MEM_HEAD_DIM)
        for l in range(DEC_SEQ):
            o_ref[l, :, hl] = res[hd * rg + l * nb:hd * rg + (l + 1) * nb, hl]


def _sattn(q, mem_k, mem_v):
    ns = mem_k.shape[0]
    nb = S_ATT_BLOCK
    qblk = pl.BlockSpec((DEC_SEQ, nb, D_MODEL), lambda i: (0, i, 0))
    kvblk = pl.BlockSpec((nb, N_MEM * KV_ROWS_PER_TOKEN, LANES), lambda i: (i, 0, 0))
    return pl.pallas_call(
        _sattn_kernel,
        grid=(ns // nb,),
        in_specs=[qblk, kvblk, kvblk],
        out_specs=qblk,
        out_shape=jax.ShapeDtypeStruct((DEC_SEQ, ns, D_MODEL), f32),
        compiler_params=pltpu.CompilerParams(dimension_semantics=("arbitrary",), vmem_limit_bytes=VMEM_LIMIT_BYTES),
        name="sattn",
    )(q, mem_k, mem_v)


def _sffn_kernel(x1_ref, ao_ref, sffn_ref, wo_ref, g_ref, wup_ref, cw_ref, cb_ref, wdn_ref, gfin_ref,
                 y_ref, st_ref, h_scr, act_scr):
    nb = S_SEQ_BLOCK
    x2 = _steps(x1_ref) + _dot(_steps(ao_ref).astype(bf16), wo_ref[...])
    h_scr[...] = _rmsnorm(x2, g_ref[...]).astype(bf16)
    u = _dot(h_scr[...], wup_ref[...])
    hist = FFN_CONV - 1

    def conv_block(blk):
        cols = slice(blk * FF_CHUNK, (blk + 1) * FF_CHUNK)
        tiles = range(blk * FF_CHUNK // LANES, (blk + 1) * FF_CHUNK // LANES)
        slots = [sffn_ref[:, i, cols] for i in range(hist)]
        slots += [u[l * nb:(l + 1) * nb, cols] for l in range(DEC_SEQ)]
        for i in range(hist):
            st_ref[:, i, cols] = slots[DEC_SEQ + i]
        outs = []
        for l in range(DEC_SEQ):
            acc = cb_ref[:, cols] + cw_ref[0:1, cols] * slots[l]
            for k in range(1, FFN_CONV):
                acc = acc + cw_ref[k:k + 1, cols] * slots[l + k]
            outs.append(acc)
        return jnp.concatenate(outs, axis=0)

    for j in range(N_FF_CHUNKS):
        act = _silu(conv_block(_gate_blk(j))) * conv_block(_val_blk(j))
        act_scr[:, j * FF_CHUNK:(j + 1) * FF_CHUNK] = act.astype(bf16)
    y = _rmsnorm(x2 + _dot(act_scr[...], wdn_ref[...]), gfin_ref[...])
    for l in range(DEC_SEQ):
        _rows_view_put(y_ref, l, DEC_SEQ, range(D_MODEL // LANES), nb, X_ROWS_PER_SEQ, y[l * nb:(l + 1) * nb])


def _sffn(x1, ao, sffn_rows, w):
    ns = x1.shape[1]
    nb = S_SEQ_BLOCK
    tok = pl.BlockSpec((DEC_SEQ, nb, D_MODEL), lambda i: (0, i, 0))
    yblk = pl.BlockSpec((nb * X_ROWS_PER_SEQ, LANES), lambda i: (i, 0))
    stblk = pl.BlockSpec((nb, FFN_CONV - 1, 2 * D_FF), lambda i: (i, 0, 0))
    return pl.pallas_call(
        _sffn_kernel,
        grid=(ns // nb,),
        in_specs=[tok, tok, stblk,
                  _const_spec((D_MODEL, D_MODEL)), _const_spec((1, D_MODEL)), _const_spec((D_MODEL, 2 * D_FF)),
                  _const_spec((FFN_CONV, 2 * D_FF)), _const_spec((1, 2 * D_FF)),
                  _const_spec((D_FF, D_MODEL)), _const_spec((1, D_MODEL))],
        out_specs=[yblk, stblk],
        out_shape=[jax.ShapeDtypeStruct((ns * X_ROWS_PER_SEQ, LANES), f32),
                   jax.ShapeDtypeStruct((ns, FFN_CONV - 1, 2 * D_FF), f32)],
        scratch_shapes=[pltpu.VMEM((DEC_SEQ * nb, D_MODEL), bf16), pltpu.VMEM((DEC_SEQ * nb, D_FF), bf16)],
        compiler_params=pltpu.CompilerParams(dimension_semantics=("arbitrary",), vmem_limit_bytes=VMEM_LIMIT_BYTES),
        name="sffn",
    )(x1, ao, sffn_rows, w["wo"], w["nffn"], w["wup"], w["fcw"], w["fcb"], w["wdn"], w["gfin"])


def _sample_path(x_sample, state_ssm, state_ssm_conv, state_pool, state_ffn_conv, cache_mem_k, cache_mem_v, w):
    ns = x_sample.shape[0]
    xtm, z, ypart, eacs, xd, cmat, bmat, dec, pout, conv_new, pool_new = _smix_in(
        _tiled_rows_view(x_sample), state_ssm_conv[0].transpose(1, 0, 2), state_pool[0].transpose(1, 0, 2), w)
    yoff, ssm_new = _sssd(dec[:, :SSM_HEADS].reshape(-1), cmat, bmat, xd, state_ssm[0].reshape(ns, D_SSM, SSM_STATE))
    flat = lambda a: a.reshape(DEC_SEQ * ns, a.shape[-1])
    x1, q = _smix_out(flat(xtm), flat(ypart), flat(yoff), flat(eacs), flat(z), flat(pout), w)
    ao = _sattn(q.reshape(DEC_SEQ, ns, D_MODEL), _kv_rows_view(cache_mem_k[0]), _kv_rows_view(cache_mem_v[0]))
    y_rows, ffn_new = _sffn(x1.reshape(DEC_SEQ, ns, D_MODEL), ao, state_ffn_conv[0], w)
    return (_from_tiled_rows(y_rows, ns, DEC_SEQ, D_MODEL),
            ssm_new.reshape(1, ns, SSM_HEADS, SSM_HEAD_DIM, SSM_STATE),
            conv_new.transpose(1, 0, 2)[None], pool_new.transpose(1, 0, 2)[None],
            ffn_new[None])


def kernel(x_prompt, x_sample, mem_prompt, state_ssm, state_ssm_conv, state_pool, state_ffn_conv, cache_mem_k, cache_mem_v, norm_mix, w_in, ssm_conv_w, ssm_conv_b, ssm_dt_bias, ssm_a_log, ssm_d, ssm_norm, w_pool, pool_scale, w_out, norm_mem, norm_memkv, w_mq, w_mk, w_mv, w_mo, norm_ffn, w_up, ffn_conv_w, ffn_conv_b, w_down, final_norm):
    params = dict(norm_mix=norm_mix, w_in=w_in, ssm_conv_w=ssm_conv_w, ssm_conv_b=ssm_conv_b, ssm_dt_bias=ssm_dt_bias,
                  ssm_a_log=ssm_a_log, ssm_d=ssm_d, ssm_norm=ssm_norm, w_pool=w_pool, pool_scale=pool_scale,
                  w_out=w_out, norm_mem=norm_mem, norm_memkv=norm_memkv, w_mq=w_mq, w_mk=w_mk, w_mv=w_mv, w_mo=w_mo,
                  norm_ffn=norm_ffn, w_up=w_up, ffn_conv_w=ffn_conv_w, ffn_conv_b=ffn_conv_b, w_down=w_down,
                  final_norm=final_norm)
    w = _prep_weights(params)
    yp, ssm_p, conv_p, pool_p, ffn_p, mk_p, mv_p = _prompt_path(x_prompt, mem_prompt, w, PROMPT_TILE)
    ys, ssm_s, conv_s, pool_s, ffn_s = _sample_path(x_sample, state_ssm, state_ssm_conv, state_pool, state_ffn_conv,
                                                    cache_mem_k, cache_mem_v, w)
    return yp, ys, ssm_p, ssm_s, conv_p, conv_s, pool_p, pool_s, ffn_p, ffn_s, mk_p, mv_p
```
